```python
import jax, jax.numpy as jnp
from jax import lax
import numpy as np

D_MODEL = 2048
BATCH = 4
SEQ = 2048
DEPTH = 1
DEC_BATCH = 128
DEC_SEQ = 4
PAST_LEN = 16384
PAGE_SIZE = 128

D_MIX = D_MODEL
D_CONV = D_MIX // 2
D_POOL = D_MIX - D_CONV
CONV_HEADS = 8
CONV_WIDTH = 31
POOL_WINDOWS = (2, 4, 8, 16)
N_POOL_GROUPS = len(POOL_WINDOWS)
POOL_GROUP_DIM = D_POOL // N_POOL_GROUPS
POOL_MAX_W = max(POOL_WINDOWS)
N_EXPERT_GROUPS = 4
EXPERTS_PER_GROUP = 8
N_EXPERTS = N_EXPERT_GROUPS * EXPERTS_PER_GROUP
TOP_K_IN_GROUP = 2
D_EXPERT = 512
PROJ_COLS = 2 * D_CONV + D_POOL
EPS = 1e-6

kernel_name = "hymba_conv_pool_hiermoe_step"


def rmsnorm(x, g):
    xf = x.astype(jnp.float32)
    y = xf * lax.rsqrt(jnp.mean(xf * xf, axis=-1, keepdims=True) + EPS)
    return (y * g.astype(jnp.float32)).astype(x.dtype)


def layernorm(x, g, b):
    xf = x.astype(jnp.float32)
    mu = jnp.mean(xf, axis=-1, keepdims=True)
    var = jnp.mean(jnp.square(xf - mu), axis=-1, keepdims=True)
    y = (xf - mu) * lax.rsqrt(var + EPS)
    return (y * g.astype(jnp.float32) + b.astype(jnp.float32)).astype(x.dtype)


def conv_mixer(v, gate, buf, w_dw, b_dw, ln_g, ln_b):
    a = v * jax.nn.sigmoid(gate)
    ext = jnp.concatenate([buf.astype(a.dtype), a], axis=1)
    y = lax.conv_general_dilated(
        ext, w_dw[:, None, :], window_strides=(1,), padding='VALID',
        dimension_numbers=('NWC', 'WIO', 'NWC'), feature_group_count=D_CONV)
    y = layernorm(y + b_dw, ln_g, ln_b)
    return jax.nn.silu(y), ext[:, -(CONV_WIDTH - 1):]


def pool_mixer(u, buf, pos0, w_pool, pool_scale):
    B, T, _ = u.shape
    P = POOL_MAX_W - 1
    ext = jnp.concatenate([buf.astype(u.dtype), u], axis=1)
    c = jnp.cumsum(ext.astype(jnp.float32), axis=1)
    c = jnp.pad(c, ((0, 0), (1, 0), (0, 0)))
    pos = pos0 + jnp.arange(T, dtype=jnp.float32)
    means = []
    for g, w in enumerate(POOL_WINDOWS):
        sl = slice(g * POOL_GROUP_DIM, (g + 1) * POOL_GROUP_DIM)
        s = c[:, P + 1:P + 1 + T, sl] - c[:, P + 1 - w:P + 1 - w + T, sl]
        cnt = jnp.minimum(pos + 1.0, float(w))
        means.append(s / cnt[None, :, None])
    mean = jnp.concatenate(means, axis=-1)
    s = (mean - u.astype(jnp.float32)).astype(u.dtype)
    s = s.reshape(B, T, N_POOL_GROUPS, POOL_GROUP_DIM)
    out = jnp.einsum('btgc,gcd->btgd', s, w_pool).reshape(B, T, D_POOL) * pool_scale
    return out, ext[:, -P:]


def hier_moe(h, w_rg, b_rg, w_re, b_re, w_gate, w_up, w_down):
    B, T, D = h.shape
    hf = h.reshape(-1, D)
    N = hf.shape[0]
    glog = (hf @ w_rg).astype(jnp.float32) + b_rg.astype(jnp.float32)
    gprob = jax.nn.softmax(glog, axis=-1)
    g_sel = jnp.argmax(glog, axis=-1)
    p_g = jnp.take_along_axis(gprob, g_sel[:, None], axis=-1)
    elog = ((hf @ w_re).astype(jnp.float32) + b_re.astype(jnp.float32)).reshape(N, N_EXPERT_GROUPS, EXPERTS_PER_GROUP)
    elog_sel = jnp.take_along_axis(elog, g_sel[:, None, None], axis=1)[:, 0]
    top_v, top_i = lax.top_k(elog_sel, TOP_K_IN_GROUP)
    w_top = jax.nn.softmax(top_v, axis=-1) * p_g
    expert_id = g_sel[:, None] * EXPERTS_PER_GROUP + top_i
    gates = jnp.sum(jax.nn.one_hot(expert_id, N_EXPERTS, dtype=jnp.float32) * w_top[..., None], axis=1)
    y = jnp.zeros((N, D), jnp.float32)
    for e in range(N_EXPERTS):
        a = jax.nn.silu(hf @ w_gate[e]) * (hf @ w_up[e])
        y = y + gates[:, e:e + 1] * (a @ w_down[e]).astype(jnp.float32)
    return y.astype(h.dtype).reshape(B, T, D)


def layer(x, pos0, conv_buf, pool_buf, g_mix, w_in, w_dw, b_dw, ln_g, ln_b, w_pool, pool_scale, w_out,
          g_ffn, w_rg, b_rg, w_re, b_re, w_gate, w_up, w_down):
    h = rmsnorm(x, g_mix)
    p = h @ w_in
    v, gate, u = p[..., :D_CONV], p[..., D_CONV:2 * D_CONV], p[..., 2 * D_CONV:]
    yc, new_conv = conv_mixer(v, gate, conv_buf, w_dw, b_dw, ln_g, ln_b)
    yp, new_pool = pool_mixer(u, pool_buf, pos0, w_pool, pool_scale)
    x = x + jnp.concatenate([yc, yp], axis=-1) @ w_out
    x = x + hier_moe(rmsnorm(x, g_ffn), w_rg, b_rg, w_re, b_re, w_gate, w_up, w_down)
    return x, new_conv, new_pool


def setup_inputs(seed: int = 0) -> dict:
    key = jax.random.key(seed)
    ks = jax.random.split(key, 24)
    f = jnp.float32
    nrm = lambda k, shape, s: jax.random.normal(k, shape, f) * s
    return {
        "x_prompt": nrm(ks[0], (BATCH, SEQ, D_MODEL), 1.0),
        "x_sample": nrm(ks[1], (DEC_BATCH, DEC_SEQ, D_MODEL), 1.0),
        "state_conv": nrm(ks[2], (DEPTH, DEC_BATCH, CONV_WIDTH - 1, D_CONV), 0.5),
        "state_pool": nrm(ks[3], (DEPTH, DEC_BATCH, POOL_MAX_W - 1, D_POOL), 1.0),
        "g_mix": 1.0 + nrm(ks[4], (DEPTH, D_MODEL), 0.02),
        "w_in": nrm(ks[5], (DEPTH, D_MODEL, PROJ_COLS), D_MODEL ** -0.5),
        "w_dw": nrm(ks[6], (DEPTH, CONV_WIDTH, D_CONV), CONV_WIDTH ** -0.5),
        "b_dw": nrm(ks[7], (DEPTH, D_CONV), 0.02),
        "ln_g": 1.0 + nrm(ks[8], (DEPTH, D_CONV), 0.02),
        "ln_b": nrm(ks[9], (DEPTH, D_CONV), 0.02),
        "w_pool": nrm(ks[10], (DEPTH, N_POOL_GROUPS, POOL_GROUP_DIM, POOL_GROUP_DIM), POOL_GROUP_DIM ** -0.5),
        "pool_scale": 1.0 + nrm(ks[11], (DEPTH, D_POOL), 0.1),
        "w_out": nrm(ks[12], (DEPTH, D_MIX, D_MODEL), 0.5 * D_MIX ** -0.5),
        "g_ffn": 1.0 + nrm(ks[13], (DEPTH, D_MODEL), 0.02),
        "w_rg": nrm(ks[14], (DEPTH, D_MODEL, N_EXPERT_GROUPS), D_MODEL ** -0.5),
        "b_rg": nrm(ks[15], (DEPTH, N_EXPERT_GROUPS), 0.01),
        "w_re": nrm(ks[16], (DEPTH, D_MODEL, N_EXPERTS), D_MODEL ** -0.5),
        "b_re": nrm(ks[17], (DEPTH, N_EXPERTS), 0.01),
        "w_gate": nrm(ks[18], (DEPTH, N_EXPERTS, D_MODEL, D_EXPERT), D_MODEL ** -0.5),
        "w_up": nrm(ks[19], (DEPTH, N_EXPERTS, D_MODEL, D_EXPERT), D_MODEL ** -0.5),
        "w_down": nrm(ks[20], (DEPTH, N_EXPERTS, D_EXPERT, D_MODEL), D_EXPERT ** -0.5),
        "g_final": 1.0 + nrm(ks[21], (D_MODEL,), 0.02),
    }


def reference(x_prompt, x_sample, state_conv, state_pool, g_mix, w_in, w_dw, b_dw, ln_g, ln_b, w_pool,
              pool_scale, w_out, g_ffn, w_rg, b_rg, w_re, b_re, w_gate, w_up, w_down, g_final):
    xp, xs = x_prompt, x_sample
    conv_p, conv_s, pool_p, pool_s = [], [], [], []
    for l in range(DEPTH):
        wl = (g_mix[l], w_in[l], w_dw[l], b_dw[l], ln_g[l], ln_b[l], w_pool[l], pool_scale[l], w_out[l],
              g_ffn[l], w_rg[l], b_rg[l], w_re[l], b_re[l], w_gate[l], w_up[l], w_down[l])
        zc = jnp.zeros((xp.shape[0], CONV_WIDTH - 1, D_CONV), xp.dtype)
        zp = jnp.zeros((xp.shape[0], POOL_MAX_W - 1, D_POOL), xp.dtype)
        xp, ncp, npp = layer(xp, 0, zc, zp, *wl)
        xs, ncs, nps = layer(xs, PAST_LEN, state_conv[l], state_pool[l], *wl)
        conv_p.append(ncp); conv_s.append(ncs); pool_p.append(npp); pool_s.append(nps)
    y_prompt = rmsnorm(xp, g_final)
    y_sample = rmsnorm(xs, g_final)
    return (y_prompt, y_sample, jnp.stack(conv_p), jnp.stack(conv_s), jnp.stack(pool_p), jnp.stack(pool_s))
```

```python
import functools

import jax
import jax.numpy as jnp
from jax import lax
from jax.experimental import pallas as pl
from jax.experimental.pallas import tpu as pltpu

F32 = jnp.float32
BF16 = jnp.bfloat16
EPS = 1e-6

LANES = 128
SUBLANES = 8
VMEM_LIMIT = 56 * 1024 * 1024

CONV_W = 31
POOL_WINDOWS = (2, 4, 8, 16)
POOL_MAX_W = 16
N_GROUPS = 4
PER_GROUP = 8
N_EXPERTS = N_GROUPS * PER_GROUP

ROUTER_GROUP_LANE0 = N_EXPERTS
TAIL = 32
PTAIL = 16


def _rows(i, r):
    return pl.ds(pl.multiple_of(i * r, r), r)


def _sigmoid(x):
    return 1.0 / (1.0 + jnp.exp(-x))


def _rmsnorm_to_bf16(src_ref, g_ref, dst_ref, T):
    R = 16

    def body(i, c):
        rs = _rows(i, R)
        x = src_ref[rs, :]
        ms = jnp.mean(x * x, axis=-1, keepdims=True)
        dst_ref[rs, :] = (x * lax.rsqrt(ms + EPS) * g_ref[...]).astype(BF16)
        return c

    lax.fori_loop(0, T // R, body, 0)


def _layernorm_swish(y_ref, lng_ref, lnb_ref, mix_ref, T, dc):
    R = 16

    def body(i, c):
        rs = _rows(i, R)
        y = y_ref[rs, :]
        mu = jnp.mean(y, axis=-1, keepdims=True)
        d = y - mu
        var = jnp.mean(d * d, axis=-1, keepdims=True)
        z = d * lax.rsqrt(var + EPS) * lng_ref[...] + lnb_ref[...]
        mix_ref[rs, 0:dc] = (z * _sigmoid(z)).astype(BF16)
        return c

    lax.fori_loop(0, T // R, body, 0)


def _pool_project(sd_ref, wpool_ref, pscale_ref, mix_ref, dc, gd):
    for g in range(len(POOL_WINDOWS)):
        sl = slice(g * gd, (g + 1) * gd)
        o = jnp.dot(sd_ref[:, sl], wpool_ref[g], preferred_element_type=F32)
        mix_ref[:, dc + g * gd: dc + (g + 1) * gd] = (o * pscale_ref[:, sl]).astype(BF16)


def _out_proj_residual(x_ref, mix_ref, wout_ref, x2_ref):
    x2_ref[...] = x_ref[...] + jnp.dot(mix_ref[...], wout_ref[...], preferred_element_type=F32)


def _router(x2_ref, gffn_ref, wrh_ref, wrl_ref, br_ref, hp_ref, ids_ref, w0_ref, w1_ref,
            hhi_scr, hlo_scr, T, d):
    R = 16
    half = d // 2

    def body(i, c):
        rs = _rows(i, R)
        x = x2_ref[rs, :]
        ms = jnp.mean(x * x, axis=-1, keepdims=True)
        h = x * lax.rsqrt(ms + EPS) * gffn_ref[...]
        hi = h.astype(BF16)
        hif = hi.astype(F32)
        hhi_scr[rs, :] = hi
        hlo_scr[rs, :] = (h - hif).astype(BF16)
        bits = lax.bitcast_convert_type(hif, jnp.uint32)
        hp_ref[rs, :] = bits[:, 0:half] | (bits[:, half:d] >> 16)
        return c

    lax.fori_loop(0, T // R, body, 0)

    lg = (jnp.dot(hhi_scr[...], wrh_ref[...], preferred_element_type=F32)
          + jnp.dot(hlo_scr[...], wrh_ref[...], preferred_element_type=F32)
          + jnp.dot(hhi_scr[...], wrl_ref[...], preferred_element_type=F32)
          + br_ref[...])

    lane = lax.broadcasted_iota(jnp.int32, lg.shape, 1).astype(F32)
    neg = jnp.float32(-jnp.inf)
    big = jnp.float32(1e9)
    g_lo = jnp.float32(ROUTER_GROUP_LANE0)
    gmask = (lane >= g_lo) & (lane < g_lo + N_GROUPS)
    lgg = jnp.where(gmask, lg, neg)
    gmax = jnp.max(lgg, axis=-1, keepdims=True)
    gsel = jnp.min(jnp.where(lgg == gmax, lane, big), axis=-1, keepdims=True) - g_lo
    gsum = jnp.sum(jnp.where(gmask, jnp.exp(lg - gmax), 0.0), axis=-1, keepdims=True)
    p_g = 1.0 / gsum

    e_lo = gsel * PER_GROUP
    emask = (lane >= e_lo) & (lane < e_lo + PER_GROUP)
    le = jnp.where(emask, lg, neg)
    v0 = jnp.max(le, axis=-1, keepdims=True)
    i0 = jnp.min(jnp.where(le == v0, lane, big), axis=-1, keepdims=True)
    le2 = jnp.where(lane == i0, neg, le)
    v1 = jnp.max(le2, axis=-1, keepdims=True)
    i1 = jnp.min(jnp.where(le2 == v1, lane, big), axis=-1, keepdims=True)
    ex = jnp.exp(v1 - v0)
    den = 1.0 / (1.0 + ex)
    w0 = den * p_g
    w1 = ex * den * p_g

    ids_ref[...] = jnp.where(lane == 0.0, i0, jnp.where(lane == 1.0, i1, 0.0)).astype(jnp.int32)
    w0_ref[...] = jnp.broadcast_to(w0, lg.shape)
    w1_ref[...] = jnp.broadcast_to(w1, lg.shape)


def _mixer_prompt_kernel(x_ref, gmix_ref, win_ref, wdw_ref, bdw_ref, lng_ref, lnb_ref, wpool_ref,
                         pscale_ref, wout_ref, gffn_ref, wrh_ref, wrl_ref, br_ref,
                         x2_ref, hp_ref, ids_ref, w0_ref, w1_ref, nconv_ref, npool_ref,
                         h_scr, pg_scr, ext_scr, extu_scr, yc_scr, sd_scr, mix_scr, hhi_scr, hlo_scr,
                         *, T, d, dc, dp):
    s = pl.program_id(1)
    n_s = pl.num_programs(1)
    gd = dp // len(POOL_WINDOWS)

    @pl.when(s == 0)
    def _():
        ext_scr[0:TAIL, :] = jnp.zeros((TAIL, dc), F32)
        extu_scr[0:PTAIL, :] = jnp.zeros((PTAIL, dp), F32)

    _rmsnorm_to_bf16(x_ref, gmix_ref, h_scr, T)

    ext_scr[TAIL:TAIL + T, :] = jnp.dot(h_scr[...], win_ref[:, 0:dc], preferred_element_type=F32)
    pg_scr[...] = jnp.dot(h_scr[...], win_ref[:, dc:2 * dc], preferred_element_type=F32)
    extu_scr[PTAIL:PTAIL + T, :] = jnp.dot(h_scr[...], win_ref[:, 2 * dc:2 * dc + dp],
                                           preferred_element_type=F32)

    R = 16

    def glu(i, c):
        rs = _rows(i, R)
        es = pl.ds(pl.multiple_of(i * R, R) + TAIL, R)
        ext_scr[es, :] = ext_scr[es, :] * _sigmoid(pg_scr[rs, :])
        return c

    lax.fori_loop(0, T // R, glu, 0)

    RC, LC = 32, 256
    BR = RC + TAIL
    shift0 = TAIL - (CONV_W - 1)

    def conv(i, c):
        r0 = pl.multiple_of(i * RC, RC)
        for lc in range(dc // LC):
            ls = slice(lc * LC, (lc + 1) * LC)
            blk = ext_scr[pl.ds(r0, BR), ls]
            acc = jnp.zeros((RC, LC), F32)
            for sft in range(SUBLANES):
                taps = [k for k in range(CONV_W) if (k + shift0) % SUBLANES == sft]
                if not taps:
                    continue
                rolled = blk if sft == 0 else pltpu.roll(blk, BR - sft, 0)
                for k in taps:
                    q = (k + shift0) // SUBLANES
                    acc = acc + wdw_ref[k:k + 1, ls] * rolled[q * SUBLANES:q * SUBLANES + RC, :]
            yc_scr[pl.ds(r0, RC), ls] = acc + bdw_ref[:, ls]
        return c

    lax.fori_loop(0, T // RC, conv, 0)

    _layernorm_swish(yc_scr, lng_ref, lnb_ref, mix_scr, T, dc)

    RP = 32
    BP = RP + PTAIL
    pos_base = s * T

    def pool(i, c):
        r0 = pl.multiple_of(i * RP, RP)
        pos = (pos_base + r0 + lax.broadcasted_iota(jnp.int32, (RP, gd), 0)).astype(F32)
        for g, w in enumerate(POOL_WINDOWS):
            ls = slice(g * gd, (g + 1) * gd)
            blk = extu_scr[pl.ds(r0, BP), ls]
            run = blk
            span = 1
            while span < w:
                run = run + pltpu.roll(run, span, 0)
                span *= 2
            cnt = jnp.minimum(pos + 1.0, jnp.float32(w))
            mean = run[PTAIL:PTAIL + RP, :] / cnt
            sd_scr[pl.ds(r0, RP), ls] = (mean - blk[PTAIL:PTAIL + RP, :]).astype(BF16)
        return c

    lax.fori_loop(0, T // RP, pool, 0)

    _pool_project(sd_scr, wpool_ref, pscale_ref, mix_scr, dc, gd)

    @pl.when(s == n_s - 1)
    def _():
        nconv_ref[...] = ext_scr[TAIL + T - (CONV_W - 1):TAIL + T, :]
        npool_ref[...] = extu_scr[PTAIL + T - (POOL_MAX_W - 1):PTAIL + T, :]

    ext_scr[0:TAIL, :] = ext_scr[T:T + TAIL, :]
    extu_scr[0:PTAIL, :] = extu_scr[T:T + PTAIL, :]

    _out_proj_residual(x_ref, mix_scr, wout_ref, x2_ref)
    _router(x2_ref, gffn_ref, wrh_ref, wrl_ref, br_ref, hp_ref, ids_ref, w0_ref, w1_ref,
            hhi_scr, hlo_scr, T, d)


def _mixer_sample_kernel(x_ref, sconv_ref, spool_ref, gmix_ref, win_ref, wdw_ref, bdw_ref, lng_ref,
                         lnb_ref, wpool_ref, pscale_ref, wout_ref, gffn_ref, wrh_ref, wrl_ref, br_ref,
                         x2_ref, hp_ref, ids_ref, w0_ref, w1_ref, nconv_ref, npool_ref,
                         xt_scr, h_scr, a_scr, pg_scr, u_scr, yc_scr, sd_scr, mix_scr, hhi_scr, hlo_scr,
                         *, TS, BS, d, dc, dp):
    T = TS * BS
    gd = dp // len(POOL_WINDOWS)
    NH = CONV_W - 1
    NP = POOL_MAX_W - 1

    for t in range(TS):
        xt_scr[t * BS:(t + 1) * BS, :] = x_ref[t]

    _rmsnorm_to_bf16(xt_scr, gmix_ref, h_scr, T)
    a_scr[...] = jnp.dot(h_scr[...], win_ref[:, 0:dc], preferred_element_type=F32)
    pg_scr[...] = jnp.dot(h_scr[...], win_ref[:, dc:2 * dc], preferred_element_type=F32)
    u_scr[...] = jnp.dot(h_scr[...], win_ref[:, 2 * dc:2 * dc + dp], preferred_element_type=F32)

    R = 16

    def glu(i, c):
        rs = _rows(i, R)
        a_scr[rs, :] = a_scr[rs, :] * _sigmoid(pg_scr[rs, :])
        return c

    lax.fori_loop(0, T // R, glu, 0)

    def ext_conv(j, rs, ls):
        if j < NH:
            return sconv_ref[j, rs, ls]
        return a_scr[pl.ds((j - NH) * BS + rs.start, rs.size), ls]

    def ext_pool(j, rs, ls):
        if j < NP:
            return spool_ref[j, rs, ls]
        return u_scr[pl.ds((j - NP) * BS + rs.start, rs.size), ls]

    RC, LC = 32, 256

    def conv(i, c):
        rs = _rows(i, RC)
        for t in range(TS):
            for lc in range(dc // LC):
                ls = slice(lc * LC, (lc + 1) * LC)
                acc = jnp.zeros((RC, LC), F32)
                for k in range(CONV_W):
                    acc = acc + wdw_ref[k:k + 1, ls] * ext_conv(t + k, rs, ls)
                yc_scr[pl.ds(t * BS + rs.start, RC), ls] = acc + bdw_ref[:, ls]
        return c

    lax.fori_loop(0, BS // RC, conv, 0)

    _layernorm_swish(yc_scr, lng_ref, lnb_ref, mix_scr, T, dc)

    def pool(i, c):
        rs = _rows(i, RC)
        for t in range(TS):
            for g, w in enumerate(POOL_WINDOWS):
                ls = slice(g * gd, (g + 1) * gd)
                tot = ext_pool(NP + t, rs, ls)
                cur = tot
                for back in range(1, w):
                    tot = tot + ext_pool(NP + t - back, rs, ls)
                sd_scr[pl.ds(t * BS + rs.start, RC), ls] = (tot / jnp.float32(w) - cur).astype(BF16)
        return c

    lax.fori_loop(0, BS // RC, pool, 0)

    _pool_project(sd_scr, wpool_ref, pscale_ref, mix_scr, dc, gd)

    for j in range(NH):
        src = j + TS
        nconv_ref[j] = sconv_ref[src] if src < NH else a_scr[(src - NH) * BS:(src - NH + 1) * BS, :]
    for j in range(NP):
        src = j + TS
        npool_ref[j] = spool_ref[src] if src < NP else u_scr[(src - NP) * BS:(src - NP + 1) * BS, :]

    _out_proj_residual(xt_scr, mix_scr, wout_ref, x2_ref)
    _router(x2_ref, gffn_ref, wrh_ref, wrl_ref, br_ref, hp_ref, ids_ref, w0_ref, w1_ref,
            hhi_scr, hlo_scr, T, d)


def _const_spec(shape):
    nd = len(shape)
    return pl.BlockSpec(shape, lambda *a: (0,) * nd, pipeline_mode=pl.Buffered(1))


def _mixer_weight_specs(d, dc, dp, cols):
    gd = dp // len(POOL_WINDOWS)
    return [
        _const_spec((1, d)),
        _const_spec((d, cols)),
        _const_spec((CONV_W, dc)),
        _const_spec((1, dc)),
        _const_spec((1, dc)),
        _const_spec((1, dc)),
        _const_spec((len(POOL_WINDOWS), gd, gd)),
        _const_spec((1, dp)),
        _const_spec((dc + dp, d)),
        _const_spec((1, d)),
        _const_spec((d, LANES)),
        _const_spec((d, LANES)),
        _const_spec((1, LANES)),
    ]


def _mixer_prompt(x, wts, *, T):
    B, S, d = x.shape
    dc = wts[2].shape[1]
    dp = wts[7].shape[1]
    cols = wts[1].shape[1]
    n_s = S // T
    N = B * S
    tok = lambda b, s: (b * n_s + s, 0)
    out_shape = (
        jax.ShapeDtypeStruct((N, d), F32),
        jax.ShapeDtypeStruct((N, d // 2), jnp.uint32),
        jax.ShapeDtypeStruct((N, LANES), jnp.int32),
        jax.ShapeDtypeStruct((N, LANES), F32),
        jax.ShapeDtypeStruct((N, LANES), F32),
        jax.ShapeDtypeStruct((1, B, CONV_W - 1, dc), F32),
        jax.ShapeDtypeStruct((1, B, POOL_MAX_W - 1, dp), F32),
    )
    out_specs = (
        pl.BlockSpec((T, d), tok),
        pl.BlockSpec((T, d // 2), tok),
        pl.BlockSpec((T, LANES), tok),
        pl.BlockSpec((T, LANES), tok),
        pl.BlockSpec((T, LANES), tok),
        pl.BlockSpec((None, None, CONV_W - 1, dc), lambda b, s: (0, b, 0, 0)),
        pl.BlockSpec((None, None, POOL_MAX_W - 1, dp), lambda b, s: (0, b, 0, 0)),
    )
    scratch = [
        pltpu.VMEM((T, d), BF16),
        pltpu.VMEM((T, dc), F32),
        pltpu.VMEM((T + TAIL, dc), F32),
        pltpu.VMEM((T + PTAIL, dp), F32),
        pltpu.VMEM((T, dc), F32),
        pltpu.VMEM((T, dp), BF16),
        pltpu.VMEM((T, dc + dp), BF16),
        pltpu.VMEM((T, d), BF16),
        pltpu.VMEM((T, d), BF16),
    ]
    return pl.pallas_call(
        functools.partial(_mixer_prompt_kernel, T=T, d=d, dc=dc, dp=dp),
        grid=(B, n_s),
        in_specs=[pl.BlockSpec((None, T, d), lambda b, s: (b, s, 0))] + _mixer_weight_specs(d, dc, dp, cols),
        out_specs=out_specs,
        out_shape=out_shape,
        scratch_shapes=scratch,
        compiler_params=pltpu.CompilerParams(
            dimension_semantics=("arbitrary", "arbitrary"), vmem_limit_bytes=VMEM_LIMIT),
        name="mixer_prompt",
    )(x, *wts)


def _mixer_sample(x_t, sconv_t, spool_t, wts, *, BS):
    TS, Bd, d = x_t.shape
    dc = wts[2].shape[1]
    dp = wts[7].shape[1]
    cols = wts[1].shape[1]
    T = TS * BS
    n_b = Bd // BS
    N = TS * Bd
    tok = lambda i: (i, 0)
    out_shape = (
        jax.ShapeDtypeStruct((N, d), F32),
        jax.ShapeDtypeStruct((N, d // 2), jnp.uint32),
        jax.ShapeDtypeStruct((N, LANES), jnp.int32),
        jax.ShapeDtypeStruct((N, LANES), F32),
        jax.ShapeDtypeStruct((N, LANES), F32),
        jax.ShapeDtypeStruct((CONV_W - 1, Bd, dc), F32),
        jax.ShapeDtypeStruct((POOL_MAX_W - 1, Bd, dp), F32),
    )
    out_specs = (
        pl.BlockSpec((T, d), tok),
        pl.BlockSpec((T, d // 2), tok),
        pl.BlockSpec((T, LANES), tok),
        pl.BlockSpec((T, LANES), tok),
        pl.BlockSpec((T, LANES), tok),
        pl.BlockSpec((CONV_W - 1, BS, dc), lambda i: (0, i, 0)),
        pl.BlockSpec((POOL_MAX_W - 1, BS, dp), lambda i: (0, i, 0)),
    )
    scratch = [
        pltpu.VMEM((T, d), F32),
        pltpu.VMEM((T, d), BF16),
        pltpu.VMEM((T, dc), F32),
        pltpu.VMEM((T, dc), F32),
        pltpu.VMEM((T, dp), F32),
        pltpu.VMEM((T, dc), F32),
        pltpu.VMEM((T, dp), BF16),
        pltpu.VMEM((T, dc + dp), BF16),
        pltpu.VMEM((T, d), BF16),
        pltpu.VMEM((T, d), BF16),
    ]
    return pl.pallas_call(
        functools.partial(_mixer_sample_kernel, TS=TS, BS=BS, d=d, dc=dc, dp=dp),
        grid=(n_b,),
        in_specs=[pl.BlockSpec((TS, BS, d), lambda i: (0, i, 0)),
                  pl.BlockSpec((CONV_W - 1, BS, dc), lambda i: (0, i, 0), pipeline_mode=pl.Buffered(1)),
                  pl.BlockSpec((POOL_MAX_W - 1, BS, dp), lambda i: (0, i, 0), pipeline_mode=pl.Buffered(1))]
                 + _mixer_weight_specs(d, dc, dp, cols),
        out_specs=out_specs,
        out_shape=out_shape,
        scratch_shapes=scratch,
        compiler_params=pltpu.CompilerParams(
            dimension_semantics=("arbitrary",), vmem_limit_bytes=VMEM_LIMIT),
        name="mixer_sample",
    )(x_t, sconv_t, spool_t, *wts)


def _dispatch_kernel(pos_ref, hp_ref, xs_in_ref, xs_ref, sem, *, R, n_steps, tok0):
    del xs_in_ref
    s = pl.program_id(0)

    def row_copy(n, k, slot):
        dst = pos_ref[2 * (tok0 + n) + k]
        return pltpu.make_async_copy(hp_ref.at[pl.ds(n, 1)], xs_ref.at[pl.ds(dst, 1)], sem.at[slot])

    def start_step(step, slot):
        def body(r, c):
            n = step * R + r
            row_copy(n, 0, slot).start()
            row_copy(n, 1, slot).start()
            return c
        lax.fori_loop(0, R, body, 0)

    def wait_step(step, slot):
        def body(r, c):
            n = step * R + r
            row_copy(n, 0, slot).wait()
            row_copy(n, 1, slot).wait()
            return c
        lax.fori_loop(0, R, body, 0)

    slot = s % 2
    start_step(s, slot)

    @pl.when(s > 0)
    def _():
        wait_step(s - 1, 1 - slot)

    @pl.when(s == n_steps - 1)
    def _():
        wait_step(s, slot)


def _dispatch(pos, hp, xs_zero, *, R, tok0):
    N, hw = hp.shape
    n_steps = N // R
    return pl.pallas_call(
        functools.partial(_dispatch_kernel, R=R, n_steps=n_steps, tok0=tok0),
        grid_spec=pltpu.PrefetchScalarGridSpec(
            num_scalar_prefetch=1,
            grid=(n_steps,),
            in_specs=[pl.BlockSpec(memory_space=pl.ANY), pl.BlockSpec(memory_space=pl.ANY)],
            out_specs=pl.BlockSpec(memory_space=pl.ANY),
            scratch_shapes=[pltpu.SemaphoreType.DMA((2,))],
        ),
        out_shape=jax.ShapeDtypeStruct(xs_zero.shape, xs_zero.dtype),
        input_output_aliases={2: 0},
        compiler_params=pltpu.CompilerParams(dimension_semantics=("arbitrary",)),
        name="dispatch",
    )(pos, hp, xs_zero)


def _expert_mlp_kernel(te_ref, na_ref, xs_ref, wg_ref, wu_ref, wd_ref, ys_ref,
                       wg_scr, wu_scr, wd_scr, *, half):
    i = pl.program_id(0)
    active = i < na_ref[0]
    prev = te_ref[jnp.maximum(i - 1, 0)]
    new_expert = (i == 0) | (te_ref[i] != prev)

    @pl.when(active & new_expert)
    def _():
        wg_scr[...] = wg_ref[...].astype(BF16)
        wu_scr[...] = wu_ref[...].astype(BF16)
        wd_scr[...] = wd_ref[...].astype(BF16)

    @pl.when(active)
    def _():
        w = xs_ref[...]
        xa = lax.bitcast_convert_type(w & jnp.uint32(0xFFFF0000), F32).astype(BF16)
        xb = lax.bitcast_convert_type(w << 16, F32).astype(BF16)
        g = (jnp.dot(xa, wg_scr[0:half, :], preferred_element_type=F32)
             + jnp.dot(xb, wg_scr[half:2 * half, :], preferred_element_type=F32))
        u = (jnp.dot(xa, wu_scr[0:half, :], preferred_element_type=F32)
             + jnp.dot(xb, wu_scr[half:2 * half, :], preferred_element_type=F32))
        act = (g * _sigmoid(g) * u).astype(BF16)
        ys_ref[...] = jnp.dot(act, wd_scr[...], preferred_element_type=F32)

    @pl.when(jnp.logical_not(active))
    def _():
        ys_ref[...] = jnp.zeros(ys_ref.shape, F32)


def _expert_mlp(tile_expert, n_active, xs, w_gate, w_up, w_down, *, tm):
    P, half = xs.shape
    E, d, de = w_gate.shape
    n_tiles = P // tm
    return pl.pallas_call(
        functools.partial(_expert_mlp_kernel, half=half),
        grid_spec=pltpu.PrefetchScalarGridSpec(
            num_scalar_prefetch=2,
            grid=(n_tiles,),
            in_specs=[
                pl.BlockSpec((tm, half), lambda i, te, na: (i, 0)),
                pl.BlockSpec((None, d, de), lambda i, te, na: (te[i], 0, 0)),
                pl.BlockSpec((None, d, de), lambda i, te, na: (te[i], 0, 0)),
                pl.BlockSpec((None, de, d), lambda i, te, na: (te[i], 0, 0)),
            ],
            out_specs=pl.BlockSpec((tm, d), lambda i, te, na: (i, 0)),
            scratch_shapes=[pltpu.VMEM((d, de), BF16), pltpu.VMEM((d, de), BF16), pltpu.VMEM((de, d), BF16)],
        ),
        out_shape=jax.ShapeDtypeStruct((P, d), F32),
        compiler_params=pltpu.CompilerParams(
            dimension_semantics=("arbitrary",), vmem_limit_bytes=VMEM_LIMIT),
        name="expert_mlp",
    )(tile_expert, n_active, xs, w_gate, w_up, w_down)


def _combine_kernel(pos_ref, x2_ref, w0_ref, w1_ref, gfin_ref, ys_ref, y_ref, buf, sem,
                    *, R, n_steps, tok0, d):
    s = pl.program_id(0)

    def row_copy(step, r, k, slot):
        n = tok0 + step * R + r
        return pltpu.make_async_copy(ys_ref.at[pl.ds(pos_ref[2 * n + k], 1)],
                                     buf.at[slot, k, pl.ds(r, 1)], sem.at[slot])

    def start_step(step, slot):
        def body(r, c):
            row_copy(step, r, 0, slot).start()
            row_copy(step, r, 1, slot).start()
            return c
        lax.fori_loop(0, R, body, 0)

    def wait_step(step, slot):
        def body(r, c):
            row_copy(step, r, 0, slot).wait()
            row_copy(step, r, 1, slot).wait()
            return c
        lax.fori_loop(0, R, body, 0)

    slot = s % 2

    @pl.when(s == 0)
    def _():
        start_step(s, slot)

    @pl.when(s + 1 < n_steps)
    def _():
        start_step(s + 1, 1 - slot)

    wait_step(s, slot)

    C = 16

    def body(i, c):
        rs = _rows(i, C)
        w0 = w0_ref[rs, :]
        w1 = w1_ref[rs, :]
        parts = []
        ssq = jnp.zeros((C, LANES), F32)
        for j in range(d // LANES):
            ls = slice(j * LANES, (j + 1) * LANES)
            m = w0 * buf[slot, 0, rs, ls] + w1 * buf[slot, 1, rs, ls]
            v = x2_ref[rs, ls] + m
            ssq = ssq + v * v
            parts.append(v)
        ms = jnp.sum(ssq, axis=-1, keepdims=True) * jnp.float32(1.0 / d)
        inv = lax.rsqrt(ms + EPS)
        for j in range(d // LANES):
            ls = slice(j * LANES, (j + 1) * LANES)
            y_ref[rs, ls] = parts[j] * inv * gfin_ref[:, ls]
        return c

    lax.fori_loop(0, R // C, body, 0)


def _combine(pos, x2, w0, w1, g_final, ys, *, R, tok0):
    n_tok, d = x2.shape
    n_steps = n_tok // R
    tok = lambda s, pos: (s, 0)
    return pl.pallas_call(
        functools.partial(_combine_kernel, R=R, n_steps=n_steps, tok0=tok0, d=d),
        grid_spec=pltpu.PrefetchScalarGridSpec(
            num_scalar_prefetch=1,
            grid=(n_steps,),
            in_specs=[
                pl.BlockSpec((R, d), tok),
                pl.BlockSpec((R, LANES), tok),
                pl.BlockSpec((R, LANES), tok),
                pl.BlockSpec((1, d), lambda s, pos: (0, 0)),
                pl.BlockSpec(memory_space=pl.ANY),
            ],
            out_specs=pl.BlockSpec((R, d), lambda s, pos: (s, 0)),
            scratch_shapes=[pltpu.VMEM((2, 2, R, d), F32), pltpu.SemaphoreType.DMA((2,))],
        ),
        out_shape=jax.ShapeDtypeStruct((n_tok, d), F32),
        compiler_params=pltpu.CompilerParams(
            dimension_semantics=("arbitrary",), vmem_limit_bytes=VMEM_LIMIT),
        name="combine",
    )(pos, x2, w0, w1, g_final, ys)


def _routing_plan(ids, tm, n_tiles):
    e0 = ids[:, 0]
    e1 = ids[:, 1]
    ar = jnp.arange(N_EXPERTS, dtype=jnp.int32)
    m = ((e0[:, None] == ar) | (e1[:, None] == ar)).astype(jnp.int32)
    incl = jnp.cumsum(m, axis=0)
    excl = incl - m
    counts = incl[-1]
    tiles_e = (counts + tm - 1) // tm
    tile_end = jnp.cumsum(tiles_e)
    offs = (tile_end - tiles_e) * tm
    base = offs[None, :] + excl
    pos0 = jnp.take_along_axis(base, e0[:, None], axis=1)[:, 0]
    pos1 = jnp.take_along_axis(base, e1[:, None], axis=1)[:, 0]
    pos = jnp.stack([pos0, pos1], axis=1).reshape(-1).astype(jnp.int32)
    n_active = tile_end[-1].astype(jnp.int32)
    t = jnp.arange(n_tiles, dtype=jnp.int32)
    te = jnp.searchsorted(tile_end, jnp.minimum(t, n_active - 1), side="right").astype(jnp.int32)
    te = jnp.minimum(te, N_EXPERTS - 1)
    return pos, te, n_active.reshape(1)


T_PROMPT = 256
BS_SAMPLE = 32
TM_EXPERT = 256
R_DISPATCH = 128
R_COMBINE = 128


def kernel(x_prompt, x_sample, state_conv, state_pool, g_mix, w_in, w_dw, b_dw, ln_g, ln_b, w_pool, pool_scale, w_out, g_ffn, w_rg, b_rg, w_re, b_re, w_gate, w_up, w_down, g_final):
    depth = g_mix.shape[0]
    assert depth == 1
    B, S, d = x_prompt.shape
    Bd, TS, _ = x_sample.shape
    dc = w_dw.shape[2]
    dp = pool_scale.shape[1]
    n_p = B * S
    n_s = Bd * TS
    N = n_p + n_s

    w_r = jnp.zeros((d, LANES), F32)
    w_r = w_r.at[:, 0:N_EXPERTS].set(w_re[0]).at[:, ROUTER_GROUP_LANE0:ROUTER_GROUP_LANE0 + N_GROUPS].set(w_rg[0])
    w_r_hi = w_r.astype(BF16)
    w_r_lo = (w_r - w_r_hi.astype(F32)).astype(BF16)
    b_r = jnp.zeros((1, LANES), F32)
    b_r = b_r.at[0, 0:N_EXPERTS].set(b_re[0]).at[0, ROUTER_GROUP_LANE0:ROUTER_GROUP_LANE0 + N_GROUPS].set(b_rg[0])
    wts = (g_mix[0][None], w_in[0].astype(BF16), w_dw[0], b_dw[0][None], ln_g[0][None], ln_b[0][None],
           w_pool[0].astype(BF16), pool_scale[0][None], w_out[0].astype(BF16), g_ffn[0][None],
           w_r_hi, w_r_lo, b_r)

    x2p, hpp, idp, w0p, w1p, nconv_p, npool_p = _mixer_prompt(x_prompt, wts, T=T_PROMPT)

    x_t = jnp.transpose(x_sample, (1, 0, 2))
    sconv_t = jnp.transpose(state_conv[0], (1, 0, 2))
    spool_t = jnp.transpose(state_pool[0], (1, 0, 2))
    x2s, hps, ids_s, w0s, w1s, nconv_t, npool_t = _mixer_sample(x_t, sconv_t, spool_t, wts, BS=BS_SAMPLE)

    ids = jnp.concatenate([idp[:, 0:2], ids_s[:, 0:2]], axis=0)

    tm = TM_EXPERT
    n_tiles = (2 * N + N_EXPERTS * (tm - 1) + tm - 1) // tm
    pos, tile_expert, n_active = _routing_plan(ids, tm, n_tiles)

    xs = jnp.zeros((n_tiles * tm, d // 2), jnp.uint32)
    xs = _dispatch(pos, hpp, xs, R=R_DISPATCH, tok0=0)
    xs = _dispatch(pos, hps, xs, R=R_DISPATCH, tok0=n_p)
    ys = _expert_mlp(tile_expert, n_active, xs, w_gate[0], w_up[0], w_down[0], tm=tm)

    gfin = g_final[None]
    y_p = _combine(pos, x2p, w0p, w1p, gfin, ys, R=R_COMBINE, tok0=0)
    y_s = _combine(pos, x2s, w0s, w1s, gfin, ys, R=R_COMBINE, tok0=n_p)

    y_prompt = y_p.reshape(B, S, d)
    y_sample = y_s.reshape(Bd // BS_SAMPLE, TS, BS_SAMPLE, d).transpose(0, 2, 1, 3).reshape(Bd, TS, d)
    new_conv_s = jnp.transpose(nconv_t, (1, 0, 2))[None]
    new_pool_s = jnp.transpose(npool_t, (1, 0, 2))[None]
    return (y_prompt, y_sample, nconv_p, new_conv_s, npool_p, new_pool_s)
```

```python
import functools

import jax
import jax.numpy as jnp
from jax import lax
from jax.experimental import pallas as pl
from jax.experimental.pallas import tpu as pltpu

F32 = jnp.float32
BF16 = jnp.bfloat16
EPS = 1e-6

LANES = 128
SUBLANES = 8
VMEM_LIMIT = 56 * 1024 * 1024

CONV_W = 31
POOL_WINDOWS = (2, 4, 8, 16)
POOL_MAX_W = 16
N_GROUPS = 4
PER_GROUP = 8
N_EXPERTS = N_GROUPS * PER_GROUP

ROUTER_GROUP_LANE0 = N_EXPERTS
TAIL = 32
PTAIL = 16


def _rows(i, r):
    return pl.ds(pl.multiple_of(i * r, r), r)


def _sigmoid(x):
    return 1.0 / (1.0 + jnp.exp(-x))


def _rmsnorm_to_bf16(src_ref, g_ref, dst_ref, T):
    R = 16

    def body(i, c):
        rs = _rows(i, R)
        x = src_ref[rs, :]
        ms = jnp.mean(x * x, axis=-1, keepdims=True)
        dst_ref[rs, :] = (x * lax.rsqrt(ms + EPS) * g_ref[...]).astype(BF16)
        return c

    lax.fori_loop(0, T // R, body, 0)


def _layernorm_swish(y_ref, lng_ref, lnb_ref, mix_ref, T, dc):
    R = 16

    def body(i, c):
        rs = _rows(i, R)
        y = y_ref[rs, :]
        mu = jnp.mean(y, axis=-1, keepdims=True)
        d = y - mu
        var = jnp.mean(d * d, axis=-1, keepdims=True)
        z = d * lax.rsqrt(var + EPS) * lng_ref[...] + lnb_ref[...]
        mix_ref[rs, 0:dc] = (z * _sigmoid(z)).astype(BF16)
        return c

    lax.fori_loop(0, T // R, body, 0)


def _pool_project(sd_ref, wpool_ref, pscale_ref, mix_ref, dc, gd):
    for g in range(len(POOL_WINDOWS)):
        sl = slice(g * gd, (g + 1) * gd)
        o = jnp.dot(sd_ref[:, sl], wpool_ref[g], preferred_element_type=F32)
        mix_ref[:, dc + g * gd: dc + (g + 1) * gd] = (o * pscale_ref[:, sl]).astype(BF16)


def _out_proj_residual(x_ref, mix_ref, wout_ref, x2_ref):
    x2_ref[...] = x_ref[...] + jnp.dot(mix_ref[...], wout_ref[...], preferred_element_type=F32)


def _router(x2_ref, gffn_ref, wrh_ref, wrl_ref, br_ref, hf_ref, ids_ref, w0_ref, w1_ref,
            hhi_scr, hlo_scr, T, d):
    R = 16

    def body(i, c):
        rs = _rows(i, R)
        x = x2_ref[rs, :]
        ms = jnp.mean(x * x, axis=-1, keepdims=True)
        h = x * lax.rsqrt(ms + EPS) * gffn_ref[...]
        hi = h.astype(BF16)
        hf_ref[rs, :] = h
        hhi_scr[rs, :] = hi
        hlo_scr[rs, :] = (h - hi.astype(F32)).astype(BF16)
        return c

    lax.fori_loop(0, T // R, body, 0)

    lg = (jnp.dot(hhi_scr[...], wrh_ref[...], preferred_element_type=F32)
          + jnp.dot(hlo_scr[...], wrh_ref[...], preferred_element_type=F32)
          + jnp.dot(hhi_scr[...], wrl_ref[...], preferred_element_type=F32)
          + br_ref[...])

    lane = lax.broadcasted_iota(jnp.int32, lg.shape, 1).astype(F32)
    neg = jnp.float32(-jnp.inf)
    big = jnp.float32(1e9)
    g_lo = jnp.float32(ROUTER_GROUP_LANE0)
    gmask = (lane >= g_lo) & (lane < g_lo + N_GROUPS)
    lgg = jnp.where(gmask, lg, neg)
    gmax = jnp.max(lgg, axis=-1, keepdims=True)
    gsel = jnp.min(jnp.where(lgg == gmax, lane, big), axis=-1, keepdims=True) - g_lo
    gsum = jnp.sum(jnp.where(gmask, jnp.exp(lg - gmax), 0.0), axis=-1, keepdims=True)
    p_g = 1.0 / gsum

    e_lo = gsel * PER_GROUP
    emask = (lane >= e_lo) & (lane < e_lo + PER_GROUP)
    le = jnp.where(emask, lg, neg)
    v0 = jnp.max(le, axis=-1, keepdims=True)
    i0 = jnp.min(jnp.where(le == v0, lane, big), axis=-1, keepdims=True)
    le2 = jnp.where(lane == i0, neg, le)
    v1 = jnp.max(le2, axis=-1, keepdims=True)
    i1 = jnp.min(jnp.where(le2 == v1, lane, big), axis=-1, keepdims=True)
    ex = jnp.exp(v1 - v0)
    den = 1.0 / (1.0 + ex)
    w0 = den * p_g
    w1 = ex * den * p_g

    ids_ref[...] = jnp.where(lane == 0.0, i0, jnp.where(lane == 1.0, i1, 0.0)).astype(jnp.int32)
    w0_ref[...] = jnp.broadcast_to(w0, lg.shape)
    w1_ref[...] = jnp.broadcast_to(w1, lg.shape)


def _mixer_prompt_kernel(x_ref, gmix_ref, win_ref, wdw_ref, bdw_ref, lng_ref, lnb_ref, wpool_ref,
                         pscale_ref, wout_ref, gffn_ref, wrh_ref, wrl_ref, br_ref,
                         x2_ref, hf_ref, ids_ref, w0_ref, w1_ref, nconv_ref, npool_ref,
                         h_scr, pg_scr, ext_scr, extu_scr, yc_scr, sd_scr, mix_scr, hhi_scr, hlo_scr,
                         *, T, d, dc, dp, n_batch):
    @pl.when(pl.program_id(0) == n_batch)
    def _():
        x2_ref[...] = jnp.zeros(x2_ref.shape, F32)
        hf_ref[...] = jnp.zeros(hf_ref.shape, F32)
        ids_ref[...] = jnp.zeros(ids_ref.shape, jnp.int32)
        w0_ref[...] = jnp.zeros(w0_ref.shape, F32)
        w1_ref[...] = jnp.zeros(w1_ref.shape, F32)

    @pl.when(pl.program_id(0) < n_batch)
    def _():
        _mixer_prompt_tile(x_ref, gmix_ref, win_ref, wdw_ref, bdw_ref, lng_ref, lnb_ref, wpool_ref,
                           pscale_ref, wout_ref, gffn_ref, wrh_ref, wrl_ref, br_ref,
                           x2_ref, hf_ref, ids_ref, w0_ref, w1_ref, nconv_ref, npool_ref,
                           h_scr, pg_scr, ext_scr, extu_scr, yc_scr, sd_scr, mix_scr, hhi_scr, hlo_scr,
                           T=T, d=d, dc=dc, dp=dp)


def _mixer_prompt_tile(x_ref, gmix_ref, win_ref, wdw_ref, bdw_ref, lng_ref, lnb_ref, wpool_ref,
                       pscale_ref, wout_ref, gffn_ref, wrh_ref, wrl_ref, br_ref,
                       x2_ref, hf_ref, ids_ref, w0_ref, w1_ref, nconv_ref, npool_ref,
                       h_scr, pg_scr, ext_scr, extu_scr, yc_scr, sd_scr, mix_scr, hhi_scr, hlo_scr,
                       *, T, d, dc, dp):
    s = pl.program_id(1)
    n_s = pl.num_programs(1)
    gd = dp // len(POOL_WINDOWS)

    @pl.when(s == 0)
    def _():
        ext_scr[0:TAIL, :] = jnp.zeros((TAIL, dc), F32)
        extu_scr[0:PTAIL, :] = jnp.zeros((PTAIL, dp), F32)

    _rmsnorm_to_bf16(x_ref, gmix_ref, h_scr, T)

    ext_scr[TAIL:TAIL + T, :] = jnp.dot(h_scr[...], win_ref[:, 0:dc], preferred_element_type=F32)
    pg_scr[...] = jnp.dot(h_scr[...], win_ref[:, dc:2 * dc], preferred_element_type=F32)
    extu_scr[PTAIL:PTAIL + T, :] = jnp.dot(h_scr[...], win_ref[:, 2 * dc:2 * dc + dp],
                                           preferred_element_type=F32)

    R = 16

    def glu(i, c):
        rs = _rows(i, R)
        es = pl.ds(pl.multiple_of(i * R, R) + TAIL, R)
        ext_scr[es, :] = ext_scr[es, :] * _sigmoid(pg_scr[rs, :])
        return c

    lax.fori_loop(0, T // R, glu, 0)

    RC, LC = 32, 256
    BR = RC + TAIL
    shift0 = TAIL - (CONV_W - 1)

    def conv(i, c):
        r0 = pl.multiple_of(i * RC, RC)
        for lc in range(dc // LC):
            ls = slice(lc * LC, (lc + 1) * LC)
            blk = ext_scr[pl.ds(r0, BR), ls]
            acc = jnp.zeros((RC, LC), F32)
            for sft in range(SUBLANES):
                taps = [k for k in range(CONV_W) if (k + shift0) % SUBLANES == sft]
                if not taps:
                    continue
                rolled = blk if sft == 0 else pltpu.roll(blk, BR - sft, 0)
                for k in taps:
                    q = (k + shift0) // SUBLANES
                    acc = acc + wdw_ref[k:k + 1, ls] * rolled[q * SUBLANES:q * SUBLANES + RC, :]
            yc_scr[pl.ds(r0, RC), ls] = acc + bdw_ref[:, ls]
        return c

    lax.fori_loop(0, T // RC, conv, 0)

    _layernorm_swish(yc_scr, lng_ref, lnb_ref, mix_scr, T, dc)

    RP = 32
    BP = RP + PTAIL
    pos_base = s * T

    def pool(i, c):
        r0 = pl.multiple_of(i * RP, RP)
        pos = (pos_base + r0 + lax.broadcasted_iota(jnp.int32, (RP, gd), 0)).astype(F32)
        for g, w in enumerate(POOL_WINDOWS):
            ls = slice(g * gd, (g + 1) * gd)
            blk = extu_scr[pl.ds(r0, BP), ls]
            run = blk
            span = 1
            while span < w:
                run = run + pltpu.roll(run, span, 0)
                span *= 2
            cnt = jnp.minimum(pos + 1.0, jnp.float32(w))
            mean = run[PTAIL:PTAIL + RP, :] / cnt
            sd_scr[pl.ds(r0, RP), ls] = (mean - blk[PTAIL:PTAIL + RP, :]).astype(BF16)
        return c

    lax.fori_loop(0, T // RP, pool, 0)

    _pool_project(sd_scr, wpool_ref, pscale_ref, mix_scr, dc, gd)

    @pl.when(s == n_s - 1)
    def _():
        nconv_ref[...] = ext_scr[TAIL + T - (CONV_W - 1):TAIL + T, :]
        npool_ref[...] = extu_scr[PTAIL + T - (POOL_MAX_W - 1):PTAIL + T, :]

    ext_scr[0:TAIL, :] = ext_scr[T:T + TAIL, :]
    extu_scr[0:PTAIL, :] = extu_scr[T:T + PTAIL, :]

    _out_proj_residual(x_ref, mix_scr, wout_ref, x2_ref)
    _router(x2_ref, gffn_ref, wrh_ref, wrl_ref, br_ref, hf_ref, ids_ref, w0_ref, w1_ref,
            hhi_scr, hlo_scr, T, d)


def _mixer_sample_kernel(x_ref, sconv_ref, spool_ref, gmix_ref, win_ref, wdw_ref, bdw_ref, lng_ref,
                         lnb_ref, wpool_ref, pscale_ref, wout_ref, gffn_ref, wrh_ref, wrl_ref, br_ref,
                         x2_in, hf_in, ids_in, w0_in, w1_in,
                         x2_ref, hf_ref, ids_ref, w0_ref, w1_ref, nconv_ref, npool_ref,
                         xt_scr, h_scr, a_scr, pg_scr, u_scr, yc_scr, sd_scr, mix_scr, hhi_scr, hlo_scr,
                         *, TS, BS, d, dc, dp):
    del x2_in, hf_in, ids_in, w0_in, w1_in
    T = TS * BS
    gd = dp // len(POOL_WINDOWS)
    NH = CONV_W - 1
    NP = POOL_MAX_W - 1

    for t in range(TS):
        xt_scr[t * BS:(t + 1) * BS, :] = x_ref[t]

    _rmsnorm_to_bf16(xt_scr, gmix_ref, h_scr, T)
    a_scr[...] = jnp.dot(h_scr[...], win_ref[:, 0:dc], preferred_element_type=F32)
    pg_scr[...] = jnp.dot(h_scr[...], win_ref[:, dc:2 * dc], preferred_element_type=F32)
    u_scr[...] = jnp.dot(h_scr[...], win_ref[:, 2 * dc:2 * dc + dp], preferred_element_type=F32)

    R = 16

    def glu(i, c):
        rs = _rows(i, R)
        a_scr[rs, :] = a_scr[rs, :] * _sigmoid(pg_scr[rs, :])
        return c

    lax.fori_loop(0, T // R, glu, 0)

    def ext_conv(j, rs, ls):
        if j < NH:
            return sconv_ref[j, rs, ls]
        return a_scr[pl.ds((j - NH) * BS + rs.start, rs.size), ls]

    def ext_pool(j, rs, ls):
        if j < NP:
            return spool_ref[j, rs, ls]
        return u_scr[pl.ds((j - NP) * BS + rs.start, rs.size), ls]

    RC, LC = 32, 256

    def conv(i, c):
        rs = _rows(i, RC)
        for t in range(TS):
            for lc in range(dc // LC):
                ls = slice(lc * LC, (lc + 1) * LC)
                acc = jnp.zeros((RC, LC), F32)
                for k in range(CONV_W):
                    acc = acc + wdw_ref[k:k + 1, ls] * ext_conv(t + k, rs, ls)
                yc_scr[pl.ds(t * BS + rs.start, RC), ls] = acc + bdw_ref[:, ls]
        return c

    lax.fori_loop(0, BS // RC, conv, 0)

    _layernorm_swish(yc_scr, lng_ref, lnb_ref, mix_scr, T, dc)

    def pool(i, c):
        rs = _rows(i, RC)
        for t in range(TS):
            for g, w in enumerate(POOL_WINDOWS):
                ls = slice(g * gd, (g + 1) * gd)
                tot = ext_pool(NP + t, rs, ls)
                cur = tot
                for back in range(1, w):
                    tot = tot + ext_pool(NP + t - back, rs, ls)
                sd_scr[pl.ds(t * BS + rs.start, RC), ls] = (tot / jnp.float32(w) - cur).astype(BF16)
        return c

    lax.fori_loop(0, BS // RC, pool, 0)

    _pool_project(sd_scr, wpool_ref, pscale_ref, mix_scr, dc, gd)

    for j in range(NH):
        src = j + TS
        nconv_ref[j] = sconv_ref[src] if src < NH else a_scr[(src - NH) * BS:(src - NH + 1) * BS, :]
    for j in range(NP):
        src = j + TS
        npool_ref[j] = spool_ref[src] if src < NP else u_scr[(src - NP) * BS:(src - NP + 1) * BS, :]

    _out_proj_residual(xt_scr, mix_scr, wout_ref, x2_ref)
    _router(x2_ref, gffn_ref, wrh_ref, wrl_ref, br_ref, hf_ref, ids_ref, w0_ref, w1_ref,
            hhi_scr, hlo_scr, T, d)


def _const_spec(shape):
    nd = len(shape)
    return pl.BlockSpec(shape, lambda *a: (0,) * nd, pipeline_mode=pl.Buffered(1))


def _mixer_weight_specs(d, dc, dp, cols):
    gd = dp // len(POOL_WINDOWS)
    return [
        _const_spec((1, d)),
        _const_spec((d, cols)),
        _const_spec((CONV_W, dc)),
        _const_spec((1, dc)),
        _const_spec((1, dc)),
        _const_spec((1, dc)),
        _const_spec((len(POOL_WINDOWS), gd, gd)),
        _const_spec((1, dp)),
        _const_spec((dc + dp, d)),
        _const_spec((1, d)),
        _const_spec((d, LANES)),
        _const_spec((d, LANES)),
        _const_spec((1, LANES)),
    ]


def _mixer_prompt(x, wts, *, T, n_total):
    B, S, d = x.shape
    dc = wts[2].shape[1]
    dp = wts[7].shape[1]
    cols = wts[1].shape[1]
    n_s = S // T
    N = n_total
    n_blk = N // T
    assert N % T == 0 and N - B * S <= S
    tok = lambda b, s: (jnp.minimum(b * n_s + s, n_blk - 1), 0)
    bclamp = lambda b: jnp.minimum(b, B - 1)
    out_shape = (
        jax.ShapeDtypeStruct((N, d), F32),
        jax.ShapeDtypeStruct((N, d), F32),
        jax.ShapeDtypeStruct((N, LANES), jnp.int32),
        jax.ShapeDtypeStruct((N, LANES), F32),
        jax.ShapeDtypeStruct((N, LANES), F32),
        jax.ShapeDtypeStruct((1, B, CONV_W - 1, dc), F32),
        jax.ShapeDtypeStruct((1, B, POOL_MAX_W - 1, dp), F32),
    )
    out_specs = (
        pl.BlockSpec((T, d), tok),
        pl.BlockSpec((T, d), tok),
        pl.BlockSpec((T, LANES), tok),
        pl.BlockSpec((T, LANES), tok),
        pl.BlockSpec((T, LANES), tok),
        pl.BlockSpec((None, None, CONV_W - 1, dc), lambda b, s: (0, bclamp(b), 0, 0)),
        pl.BlockSpec((None, None, POOL_MAX_W - 1, dp), lambda b, s: (0, bclamp(b), 0, 0)),
    )
    scratch = [
        pltpu.VMEM((T, d), BF16),
        pltpu.VMEM((T, dc), F32),
        pltpu.VMEM((T + TAIL, dc), F32),
        pltpu.VMEM((T + PTAIL, dp), F32),
        pltpu.VMEM((T, dc), F32),
        pltpu.VMEM((T, dp), BF16),
        pltpu.VMEM((T, dc + dp), BF16),
        pltpu.VMEM((T, d), BF16),
        pltpu.VMEM((T, d), BF16),
    ]
    return pl.pallas_call(
        functools.partial(_mixer_prompt_kernel, T=T, d=d, dc=dc, dp=dp, n_batch=B),
        grid=(B + 1, n_s),
        in_specs=[pl.BlockSpec((None, T, d), lambda b, s: (bclamp(b), s, 0))]
                 + _mixer_weight_specs(d, dc, dp, cols),
        out_specs=out_specs,
        out_shape=out_shape,
        scratch_shapes=scratch,
        compiler_params=pltpu.CompilerParams(
            dimension_semantics=("arbitrary", "arbitrary"), vmem_limit_bytes=VMEM_LIMIT),
        name="mixer_prompt",
    )(x, *wts)


def _mixer_sample(x_t, sconv_t, spool_t, wts, tok_arrays, *, BS, tok0):
    TS, Bd, d = x_t.shape
    dc = wts[2].shape[1]
    dp = wts[7].shape[1]
    cols = wts[1].shape[1]
    T = TS * BS
    n_b = Bd // BS
    b0 = tok0 // T
    tok = lambda i: (b0 + i, 0)
    out_shape = tuple(jax.ShapeDtypeStruct(a.shape, a.dtype) for a in tok_arrays) + (
        jax.ShapeDtypeStruct((CONV_W - 1, Bd, dc), F32),
        jax.ShapeDtypeStruct((POOL_MAX_W - 1, Bd, dp), F32),
    )
    out_specs = (
        pl.BlockSpec((T, d), tok),
        pl.BlockSpec((T, d), tok),
        pl.BlockSpec((T, LANES), tok),
        pl.BlockSpec((T, LANES), tok),
        pl.BlockSpec((T, LANES), tok),
        pl.BlockSpec((CONV_W - 1, BS, dc), lambda i: (0, i, 0)),
        pl.BlockSpec((POOL_MAX_W - 1, BS, dp), lambda i: (0, i, 0)),
    )
    scratch = [
        pltpu.VMEM((T, d), F32),
        pltpu.VMEM((T, d), BF16),
        pltpu.VMEM((T, dc), F32),
        pltpu.VMEM((T, dc), F32),
        pltpu.VMEM((T, dp), F32),
        pltpu.VMEM((T, dc), F32),
        pltpu.VMEM((T, dp), BF16),
        pltpu.VMEM((T, dc + dp), BF16),
        pltpu.VMEM((T, d), BF16),
        pltpu.VMEM((T, d), BF16),
    ]
    return pl.pallas_call(
        functools.partial(_mixer_sample_kernel, TS=TS, BS=BS, d=d, dc=dc, dp=dp),
        grid=(n_b,),
        in_specs=[pl.BlockSpec((TS, BS, d), lambda i: (0, i, 0)),
                  pl.BlockSpec((CONV_W - 1, BS, dc), lambda i: (0, i, 0), pipeline_mode=pl.Buffered(1)),
                  pl.BlockSpec((POOL_MAX_W - 1, BS, dp), lambda i: (0, i, 0), pipeline_mode=pl.Buffered(1))]
                 + _mixer_weight_specs(d, dc, dp, cols)
                 + [pl.BlockSpec(memory_space=pl.ANY)] * len(tok_arrays),
        out_specs=out_specs,
        out_shape=out_shape,
        scratch_shapes=scratch,
        input_output_aliases={3 + len(wts) + j: j for j in range(len(tok_arrays))},
        compiler_params=pltpu.CompilerParams(
            dimension_semantics=("arbitrary",), vmem_limit_bytes=VMEM_LIMIT),
        name="mixer_sample",
    )(x_t, sconv_t, spool_t, *wts, *tok_arrays)


def _invert_kernel(pos_ref, src_ref, dst_ref, *, n_pairs, n_tiles, tm, n_tok):
    def fill_tile(t, c):
        trash = 2 * n_tok + (t % 2) * tm

        def fill(r, c2):
            src_ref[t * tm + r] = 0
            dst_ref[t * tm + r] = trash + r
            return c2

        return lax.fori_loop(0, tm, fill, c, unroll=8)

    lax.fori_loop(0, n_tiles, fill_tile, 0)

    def fill_phantom(r, c):
        dst_ref[n_tiles * tm + r] = 2 * n_tok + tm + r
        return c

    lax.fori_loop(0, tm, fill_phantom, 0, unroll=8)

    def put(n, c):
        for k in range(2):
            p = pos_ref[2 * n + k]
            src_ref[p] = n
            dst_ref[p] = k * n_tok + n
        return c

    lax.fori_loop(0, n_pairs // 2, put, 0, unroll=4)


def _invert(pos, *, n_tiles, tm, n_tok):
    n_pairs = pos.shape[0]
    src_tab = jax.ShapeDtypeStruct((n_tiles * tm,), jnp.int32)
    dst_tab = jax.ShapeDtypeStruct(((n_tiles + 1) * tm,), jnp.int32)
    return pl.pallas_call(
        functools.partial(_invert_kernel, n_pairs=n_pairs, n_tiles=n_tiles, tm=tm, n_tok=n_tok),
        in_specs=[pl.BlockSpec(memory_space=pltpu.SMEM)],
        out_specs=(pl.BlockSpec(memory_space=pltpu.SMEM), pl.BlockSpec(memory_space=pltpu.SMEM)),
        out_shape=(src_tab, dst_tab),
        name="invert",
    )(pos)


def _expert_mlp_kernel(te_ref, na_ref, src_next_ref, src_cur_ref, dst_prev_ref, dst_cur_ref, hf_ref,
                       wg_ref, wu_ref, wd_ref, o_ref, xbuf0, xbuf1, ybuf0, ybuf1, wg_scr, wu_scr, wd_scr,
                       gsem, ssem, *, tm, n_tiles):
    i = pl.program_id(0)
    n_act = na_ref[0]
    active = i < n_act
    slot = i % 2
    last = n_tiles - 1
    prev = te_ref[jnp.maximum(i - 1, 0)]
    new_expert = (i == 0) | (te_ref[i] != prev)
    xbuf = (xbuf0, xbuf1)
    ybuf = (ybuf0, ybuf1)

    def gather_start(src_ref, b):
        for r in range(tm):
            pltpu.make_async_copy(hf_ref.at[pl.ds(src_ref[0, 0, r], 1)], xbuf[b].at[pl.ds(r, 1)],
                                  gsem.at[b]).start(priority=r % 2)

    def gather_wait(b):
        pltpu.make_async_copy(hf_ref.at[pl.ds(0, tm)], xbuf[b], gsem.at[b]).wait()

    def scatter_start(dst_ref, b):
        for r in range(tm):
            pltpu.make_async_copy(ybuf[b].at[pl.ds(r, 1)], o_ref.at[pl.ds(dst_ref[0, 0, r], 1)],
                                  ssem.at[b]).start(priority=r % 2)

    def scatter_wait(b):
        pltpu.make_async_copy(ybuf[b], o_ref.at[pl.ds(0, tm)], ssem.at[b]).wait()

    @pl.when(i == 0)
    def _():
        gather_start(src_cur_ref, 0)
        ybuf1[...] = jnp.zeros(ybuf1.shape, F32)
        trash0 = pltpu.make_async_copy(ybuf1, o_ref.at[pl.ds(o_ref.shape[0] - 2 * tm, tm)], ssem.at[0])
        trash0.start()
        trash0.wait()

    @pl.when(active & new_expert)
    def _():
        wg_scr[...] = wg_ref[...].astype(BF16)
        wu_scr[...] = wu_ref[...].astype(BF16)
        wd_scr[...] = wd_ref[...].astype(BF16)

    for par in range(2):
        is_par = slot == par

        @pl.when(is_par & (i <= n_act))
        def _():
            gather_wait(par)

        @pl.when(is_par & (i >= 1) & (i - 2 < n_act))
        def _():
            scatter_wait(par)

        @pl.when(is_par & active)
        def _():
            x = xbuf[par][...].astype(BF16)
            g = jnp.dot(x, wg_scr[...], preferred_element_type=F32)
            u = jnp.dot(x, wu_scr[...], preferred_element_type=F32)
            act = (g * _sigmoid(g) * u).astype(BF16)
            ybuf[par][...] = jnp.dot(act, wd_scr[...], preferred_element_type=F32)
            gather_start(src_next_ref, 1 - par)
            scatter_start(dst_prev_ref, 1 - par)

        @pl.when(is_par & (i == n_act))
        def _():
            scatter_start(dst_prev_ref, 1 - par)

        @pl.when(is_par & (i == last))
        def _():
            @pl.when(last - 1 < n_act)
            def _():
                scatter_wait(1 - par)

            @pl.when(last < n_act)
            def _():
                scatter_start(dst_cur_ref, par)
                scatter_wait(par)
                gather_wait(1 - par)


def _expert_mlp(tile_expert, n_active, src, dst, hf, w_gate, w_up, w_down, *, tm):
    n_tok, d = hf.shape
    E, _, de = w_gate.shape
    n_tiles = src.shape[0]
    nxt = lambda i, te, na: (jnp.minimum(i + 1, n_tiles - 1), 0, 0)
    cur = lambda i, te, na: (i, 0, 0)
    prv = lambda i, te, na: (jnp.where(i == 0, n_tiles, i - 1), 0, 0)
    smem_tile = lambda imap: pl.BlockSpec((1, 1, tm), imap, memory_space=pltpu.SMEM)
    return pl.pallas_call(
        functools.partial(_expert_mlp_kernel, tm=tm, n_tiles=n_tiles),
        grid_spec=pltpu.PrefetchScalarGridSpec(
            num_scalar_prefetch=2,
            grid=(n_tiles,),
            in_specs=[
                smem_tile(nxt), smem_tile(cur), smem_tile(prv), smem_tile(cur),
                pl.BlockSpec(memory_space=pl.ANY),
                pl.BlockSpec((None, d, de), lambda i, te, na: (te[i], 0, 0)),
                pl.BlockSpec((None, d, de), lambda i, te, na: (te[i], 0, 0)),
                pl.BlockSpec((None, de, d), lambda i, te, na: (te[i], 0, 0)),
            ],
            out_specs=pl.BlockSpec(memory_space=pl.ANY),
            scratch_shapes=[pltpu.VMEM((tm, d), F32), pltpu.VMEM((tm, d), F32),
                            pltpu.VMEM((tm, d), F32), pltpu.VMEM((tm, d), F32),
                            pltpu.VMEM((d, de), BF16), pltpu.VMEM((d, de), BF16), pltpu.VMEM((de, d), BF16),
                            pltpu.SemaphoreType.DMA((2,)), pltpu.SemaphoreType.DMA((2,))],
        ),
        out_shape=jax.ShapeDtypeStruct((2 * n_tok + 2 * tm, d), F32),
        compiler_params=pltpu.CompilerParams(
            dimension_semantics=("arbitrary",), vmem_limit_bytes=VMEM_LIMIT),
        name="expert_mlp",
    )(tile_expert, n_active, src, src, dst, dst, hf, w_gate, w_up, w_down)


def _combine_kernel(x2_ref, o0_ref, o1_ref, w0_ref, w1_ref, gfin_ref, y_ref, *, R, d):
    C = 16

    def body(i, c):
        rs = _rows(i, C)
        w0 = w0_ref[rs, :]
        w1 = w1_ref[rs, :]
        parts = []
        ssq = jnp.zeros((C, LANES), F32)
        for j in range(d // LANES):
            ls = slice(j * LANES, (j + 1) * LANES)
            m = w0 * o0_ref[rs, ls] + w1 * o1_ref[rs, ls]
            v = x2_ref[rs, ls] + m
            ssq = ssq + v * v
            parts.append(v)
        ms = jnp.sum(ssq, axis=-1, keepdims=True) * jnp.float32(1.0 / d)
        inv = lax.rsqrt(ms + EPS)
        for j in range(d // LANES):
            ls = slice(j * LANES, (j + 1) * LANES)
            y_ref[rs, ls] = parts[j] * inv * gfin_ref[:, ls]
        return c

    lax.fori_loop(0, R // C, body, 0)


def _combine(x2, o, w0, w1, g_final, *, R, tok0, n_out):
    n_tok, d = x2.shape
    n_steps = n_out // R
    b0 = tok0 // R
    b1 = n_tok // R
    tok = lambda s: (b0 + s, 0)
    return pl.pallas_call(
        functools.partial(_combine_kernel, R=R, d=d),
        grid=(n_steps,),
        in_specs=[
            pl.BlockSpec((R, d), tok),
            pl.BlockSpec((R, d), tok),
            pl.BlockSpec((R, d), lambda s: (b1 + b0 + s, 0)),
            pl.BlockSpec((R, LANES), tok),
            pl.BlockSpec((R, LANES), tok),
            pl.BlockSpec((1, d), lambda s: (0, 0)),
        ],
        out_specs=pl.BlockSpec((R, d), lambda s: (s, 0)),
        out_shape=jax.ShapeDtypeStruct((n_out, d), F32),
        compiler_params=pltpu.CompilerParams(
            dimension_semantics=("arbitrary",), vmem_limit_bytes=VMEM_LIMIT),
        name="combine",
    )(x2, o, o, w0, w1, g_final)


def _routing_plan(ids, tm, n_tiles):
    e0 = ids[:, 0]
    e1 = ids[:, 1]
    ar = jnp.arange(N_EXPERTS, dtype=jnp.int32)
    m = ((e0[:, None] == ar) | (e1[:, None] == ar)).astype(jnp.int32)
    incl = jnp.cumsum(m, axis=0)
    excl = incl - m
    counts = incl[-1]
    tiles_e = (counts + tm - 1) // tm
    tile_end = jnp.cumsum(tiles_e)
    offs = (tile_end - tiles_e) * tm
    base = offs[None, :] + excl
    pos0 = jnp.take_along_axis(base, e0[:, None], axis=1)[:, 0]
    pos1 = jnp.take_along_axis(base, e1[:, None], axis=1)[:, 0]
    pos = jnp.stack([pos0, pos1], axis=1).reshape(-1).astype(jnp.int32)
    n_active = tile_end[-1].astype(jnp.int32)
    t = jnp.arange(n_tiles, dtype=jnp.int32)
    tq = jnp.minimum(t, n_active - 1)
    te = jnp.sum((tile_end[None, :] <= tq[:, None]).astype(jnp.int32), axis=1)
    te = jnp.minimum(te, N_EXPERTS - 1)
    return pos, te, n_active.reshape(1)


T_PROMPT = 256
BS_SAMPLE = 32
TM_EXPERT = 256
R_COMBINE = 128


def kernel(x_prompt, x_sample, state_conv, state_pool, g_mix, w_in, w_dw, b_dw, ln_g, ln_b, w_pool, pool_scale, w_out, g_ffn, w_rg, b_rg, w_re, b_re, w_gate, w_up, w_down, g_final):
    depth = g_mix.shape[0]
    assert depth == 1
    B, S, d = x_prompt.shape
    Bd, TS, _ = x_sample.shape
    n_p = B * S
    n_s = Bd * TS
    N = n_p + n_s

    w_r = jnp.zeros((d, LANES), F32)
    w_r = w_r.at[:, 0:N_EXPERTS].set(w_re[0]).at[:, ROUTER_GROUP_LANE0:ROUTER_GROUP_LANE0 + N_GROUPS].set(w_rg[0])
    w_r_hi = w_r.astype(BF16)
    w_r_lo = (w_r - w_r_hi.astype(F32)).astype(BF16)
    b_r = jnp.zeros((1, LANES), F32)
    b_r = b_r.at[0, 0:N_EXPERTS].set(b_re[0]).at[0, ROUTER_GROUP_LANE0:ROUTER_GROUP_LANE0 + N_GROUPS].set(b_rg[0])
    wts = (g_mix[0][None], w_in[0].astype(BF16), w_dw[0], b_dw[0][None], ln_g[0][None], ln_b[0][None],
           w_pool[0].astype(BF16), pool_scale[0][None], w_out[0].astype(BF16), g_ffn[0][None],
           w_r_hi, w_r_lo, b_r)

    *tok_arrays, nconv_p, npool_p = _mixer_prompt(x_prompt, wts, T=T_PROMPT, n_total=N)

    x_t = jnp.transpose(x_sample, (1, 0, 2))
    sconv_t = jnp.transpose(state_conv[0], (1, 0, 2))
    spool_t = jnp.transpose(state_pool[0], (1, 0, 2))
    x2, hf, ids, w0, w1, nconv_t, npool_t = _mixer_sample(x_t, sconv_t, spool_t, wts, tok_arrays,
                                                          BS=BS_SAMPLE, tok0=n_p)

    tm = TM_EXPERT
    n_tiles = (2 * N + N_EXPERTS * (tm - 1) + tm - 1) // tm
    pos, tile_expert, n_active = _routing_plan(ids[:, 0:2], tm, n_tiles)
    src, dst = _invert(pos, n_tiles=n_tiles, tm=tm, n_tok=N)
    src = src.reshape(n_tiles, 1, tm)
    dst = dst.reshape(n_tiles + 1, 1, tm)

    o = _expert_mlp(tile_expert, n_active, src, dst, hf, w_gate[0], w_up[0], w_down[0], tm=tm)

    gfin = g_final[None]
    y_p = _combine(x2, o, w0, w1, gfin, R=R_COMBINE, tok0=0, n_out=n_p)
    y_s = _combine(x2, o, w0, w1, gfin, R=R_COMBINE, tok0=n_p, n_out=n_s)

    y_prompt = y_p.reshape(B, S, d)
    y_sample = y_s.reshape(Bd // BS_SAMPLE, TS, BS_SAMPLE, d).transpose(0, 2, 1, 3).reshape(Bd, TS, d)
    new_conv_s = jnp.transpose(nconv_t, (1, 0, 2))[None]
    new_pool_s = jnp.transpose(npool_t, (1, 0, 2))[None]
    return (y_prompt, y_sample, nconv_p, new_conv_s, npool_p, new_pool_s)
```

```python
import functools

import jax
import jax.numpy as jnp
from jax import lax
from jax.experimental import pallas as pl
from jax.experimental.pallas import tpu as pltpu

F32 = jnp.float32
BF16 = jnp.bfloat16
EPS = 1e-6

LANES = 128
SUBLANES = 8
VMEM_LIMIT = 56 * 1024 * 1024

CONV_W = 31
POOL_WINDOWS = (2, 4, 8, 16)
POOL_MAX_W = 16
N_GROUPS = 4
PER_GROUP = 8
N_EXPERTS = N_GROUPS * PER_GROUP

ROUTER_GROUP_LANE0 = N_EXPERTS
TOK_CHUNK = 16
ROW_LOOP_UNROLL = 4
TAIL = 32
PTAIL = 16


def _rows(i, r):
    return pl.ds(pl.multiple_of(i * r, r), r)


def _sigmoid(x):
    return 1.0 / (1.0 + jnp.exp(-x))


def _rmsnorm_to_bf16(src_ref, g_ref, dst_ref, T):
    R = 16

    def body(i, c):
        rs = _rows(i, R)
        x = src_ref[rs, :]
        ms = jnp.mean(x * x, axis=-1, keepdims=True)
        dst_ref[rs, :] = (x * lax.rsqrt(ms + EPS) * g_ref[...]).astype(BF16)
        return c

    lax.fori_loop(0, T // R, body, 0, unroll=ROW_LOOP_UNROLL)


def _layernorm_swish(y_ref, lng_ref, lnb_ref, mix_ref, T, dc):
    R = 16

    def body(i, c):
        rs = _rows(i, R)
        y = y_ref[rs, :]
        mu = jnp.mean(y, axis=-1, keepdims=True)
        d = y - mu
        var = jnp.mean(d * d, axis=-1, keepdims=True)
        z = d * lax.rsqrt(var + EPS) * lng_ref[...] + lnb_ref[...]
        mix_ref[rs, 0:dc] = (z * _sigmoid(z)).astype(BF16)
        return c

    lax.fori_loop(0, T // R, body, 0, unroll=ROW_LOOP_UNROLL)


def _pool_project(sd_ref, wpool_ref, pscale_ref, mix_ref, dc, gd):
    for g in range(len(POOL_WINDOWS)):
        sl = slice(g * gd, (g + 1) * gd)
        o = jnp.dot(sd_ref[:, sl], wpool_ref[g], preferred_element_type=F32)
        mix_ref[:, dc + g * gd: dc + (g + 1) * gd] = (o * pscale_ref[:, sl]).astype(BF16)


def _out_proj_residual(x_ref, mix_ref, wout_ref, x2_ref):
    x2_ref[...] = x_ref[...] + jnp.dot(mix_ref[...], wout_ref[...], preferred_element_type=F32)


def _router(x2_ref, gffn_ref, wrh_ref, wrl_ref, br_ref, hf_ref, ids_ref, w0_ref, w1_ref,
            hhi_scr, hlo_scr, T, d):
    R = TOK_CHUNK

    def body(i, c):
        rs = _rows(i, R)
        x = x2_ref[rs, :]
        ms = jnp.mean(x * x, axis=-1, keepdims=True)
        h = x * lax.rsqrt(ms + EPS) * gffn_ref[...]
        hi = h.astype(BF16)
        hf_ref[i] = hi.reshape(R, d // LANES, LANES)
        hhi_scr[rs, :] = hi
        hlo_scr[rs, :] = (h - hi.astype(F32)).astype(BF16)
        return c

    lax.fori_loop(0, T // R, body, 0, unroll=ROW_LOOP_UNROLL)

    lg = (jnp.dot(hhi_scr[...], wrh_ref[...], preferred_element_type=F32)
          + jnp.dot(hlo_scr[...], wrh_ref[...], preferred_element_type=F32)
          + jnp.dot(hhi_scr[...], wrl_ref[...], preferred_element_type=F32)
          + br_ref[...])

    lane = lax.broadcasted_iota(jnp.int32, lg.shape, 1).astype(F32)
    neg = jnp.float32(-jnp.inf)
    big = jnp.float32(1e9)
    g_lo = jnp.float32(ROUTER_GROUP_LANE0)
    gmask = (lane >= g_lo) & (lane < g_lo + N_GROUPS)
    lgg = jnp.where(gmask, lg, neg)
    gmax = jnp.max(lgg, axis=-1, keepdims=True)
    gsel = jnp.min(jnp.where(lgg == gmax, lane, big), axis=-1, keepdims=True) - g_lo
    gsum = jnp.sum(jnp.where(gmask, jnp.exp(lg - gmax), 0.0), axis=-1, keepdims=True)
    p_g = 1.0 / gsum

    e_lo = gsel * PER_GROUP
    emask = (lane >= e_lo) & (lane < e_lo + PER_GROUP)
    le = jnp.where(emask, lg, neg)
    v0 = jnp.max(le, axis=-1, keepdims=True)
    i0 = jnp.min(jnp.where(le == v0, lane, big), axis=-1, keepdims=True)
    le2 = jnp.where(lane == i0, neg, le)
    v1 = jnp.max(le2, axis=-1, keepdims=True)
    i1 = jnp.min(jnp.where(le2 == v1, lane, big), axis=-1, keepdims=True)
    ex = jnp.exp(v1 - v0)
    den = 1.0 / (1.0 + ex)
    w0 = den * p_g
    w1 = ex * den * p_g

    ids_ref[...] = jnp.where(lane == 0.0, i0, jnp.where(lane == 1.0, i1, 0.0)).astype(jnp.int32)
    w0_ref[...] = jnp.broadcast_to(w0, lg.shape)
    w1_ref[...] = jnp.broadcast_to(w1, lg.shape)


def _mixer_prompt_kernel(x_ref, gmix_ref, win_ref, wdw_ref, bdw_ref, lng_ref, lnb_ref, wpool_ref,
                         pscale_ref, wout_ref, gffn_ref, wrh_ref, wrl_ref, br_ref,
                         x2_ref, hf_ref, ids_ref, w0_ref, w1_ref, nconv_ref, npool_ref,
                         h_scr, pg_scr, ext_scr, extu_scr, yc_scr, sd_scr, mix_scr, hhi_scr, hlo_scr,
                         *, T, d, dc, dp, n_batch):
    @pl.when(pl.program_id(0) == n_batch)
    def _():
        x2_ref[...] = jnp.zeros(x2_ref.shape, F32)
        hf_ref[...] = jnp.zeros(hf_ref.shape, BF16)
        ids_ref[...] = jnp.zeros(ids_ref.shape, jnp.int32)
        w0_ref[...] = jnp.zeros(w0_ref.shape, F32)
        w1_ref[...] = jnp.zeros(w1_ref.shape, F32)

    @pl.when(pl.program_id(0) < n_batch)
    def _():
        _mixer_prompt_tile(x_ref, gmix_ref, win_ref, wdw_ref, bdw_ref, lng_ref, lnb_ref, wpool_ref,
                           pscale_ref, wout_ref, gffn_ref, wrh_ref, wrl_ref, br_ref,
                           x2_ref, hf_ref, ids_ref, w0_ref, w1_ref, nconv_ref, npool_ref,
                           h_scr, pg_scr, ext_scr, extu_scr, yc_scr, sd_scr, mix_scr, hhi_scr, hlo_scr,
                           T=T, d=d, dc=dc, dp=dp)


def _mixer_prompt_tile(x_ref, gmix_ref, win_ref, wdw_ref, bdw_ref, lng_ref, lnb_ref, wpool_ref,
                       pscale_ref, wout_ref, gffn_ref, wrh_ref, wrl_ref, br_ref,
                       x2_ref, hf_ref, ids_ref, w0_ref, w1_ref, nconv_ref, npool_ref,
                       h_scr, pg_scr, ext_scr, extu_scr, yc_scr, sd_scr, mix_scr, hhi_scr, hlo_scr,
                       *, T, d, dc, dp):
    s = pl.program_id(1)
    n_s = pl.num_programs(1)
    gd = dp // len(POOL_WINDOWS)

    @pl.when(s == 0)
    def _():
        ext_scr[0:TAIL, :] = jnp.zeros((TAIL, dc), F32)
        extu_scr[0:PTAIL, :] = jnp.zeros((PTAIL, dp), F32)

    _rmsnorm_to_bf16(x_ref, gmix_ref, h_scr, T)

    ext_scr[TAIL:TAIL + T, :] = jnp.dot(h_scr[...], win_ref[:, 0:dc], preferred_element_type=F32)
    pg_scr[...] = jnp.dot(h_scr[...], win_ref[:, dc:2 * dc], preferred_element_type=F32)
    extu_scr[PTAIL:PTAIL + T, :] = jnp.dot(h_scr[...], win_ref[:, 2 * dc:2 * dc + dp],
                                           preferred_element_type=F32)

    R = 16

    def glu(i, c):
        rs = _rows(i, R)
        es = pl.ds(pl.multiple_of(i * R, R) + TAIL, R)
        ext_scr[es, :] = ext_scr[es, :] * _sigmoid(pg_scr[rs, :])
        return c

    lax.fori_loop(0, T // R, glu, 0, unroll=ROW_LOOP_UNROLL)

    RC, LC = 32, 256
    BR = RC + TAIL
    shift0 = TAIL - (CONV_W - 1)

    def conv(i, c):
        r0 = pl.multiple_of(i * RC, RC)
        for lc in range(dc // LC):
            ls = slice(lc * LC, (lc + 1) * LC)
            blk = ext_scr[pl.ds(r0, BR), ls]
            acc = jnp.zeros((RC, LC), F32)
            for sft in range(SUBLANES):
                taps = [k for k in range(CONV_W) if (k + shift0) % SUBLANES == sft]
                if not taps:
                    continue
                rolled = blk if sft == 0 else pltpu.roll(blk, BR - sft, 0)
                for k in taps:
                    q = (k + shift0) // SUBLANES
                    acc = acc + wdw_ref[k:k + 1, ls] * rolled[q * SUBLANES:q * SUBLANES + RC, :]
            yc_scr[pl.ds(r0, RC), ls] = acc + bdw_ref[:, ls]
        return c

    lax.fori_loop(0, T // RC, conv, 0)

    _layernorm_swish(yc_scr, lng_ref, lnb_ref, mix_scr, T, dc)

    RP = 32
    BP = RP + PTAIL
    pos_base = s * T

    def pool(i, c):
        r0 = pl.multiple_of(i * RP, RP)
        pos = (pos_base + r0 + lax.broadcasted_iota(jnp.int32, (RP, gd), 0)).astype(F32)
        for g, w in enumerate(POOL_WINDOWS):
            ls = slice(g * gd, (g + 1) * gd)
            blk = extu_scr[pl.ds(r0, BP), ls]
            run = blk
            span = 1
            while span < w:
                run = run + pltpu.roll(run, span, 0)
                span *= 2
            cnt = jnp.minimum(pos + 1.0, jnp.float32(w))
            mean = run[PTAIL:PTAIL + RP, :] / cnt
            sd_scr[pl.ds(r0, RP), ls] = (mean - blk[PTAIL:PTAIL + RP, :]).astype(BF16)
        return c

    lax.fori_loop(0, T // RP, pool, 0)

    _pool_project(sd_scr, wpool_ref, pscale_ref, mix_scr, dc, gd)

    @pl.when(s == n_s - 1)
    def _():
        nconv_ref[...] = ext_scr[TAIL + T - (CONV_W - 1):TAIL + T, :]
        npool_ref[...] = extu_scr[PTAIL + T - (POOL_MAX_W - 1):PTAIL + T, :]

    ext_scr[0:TAIL, :] = ext_scr[T:T + TAIL, :]
    extu_scr[0:PTAIL, :] = extu_scr[T:T + PTAIL, :]

    _out_proj_residual(x_ref, mix_scr, wout_ref, x2_ref)
    _router(x2_ref, gffn_ref, wrh_ref, wrl_ref, br_ref, hf_ref, ids_ref, w0_ref, w1_ref,
            hhi_scr, hlo_scr, T, d)


def _mixer_sample_kernel(x_ref, sconv_ref, spool_ref, gmix_ref, win_ref, wdw_ref, bdw_ref, lng_ref,
                         lnb_ref, wpool_ref, pscale_ref, wout_ref, gffn_ref, wrh_ref, wrl_ref, br_ref,
                         x2_in, hf_in, ids_in, w0_in, w1_in,
                         x2_ref, hf_ref, ids_ref, w0_ref, w1_ref, nconv_ref, npool_ref,
                         xt_scr, h_scr, a_scr, pg_scr, u_scr, yc_scr, sd_scr, mix_scr, hhi_scr, hlo_scr,
                         *, TS, BS, d, dc, dp):
    del x2_in, hf_in, ids_in, w0_in, w1_in
    T = TS * BS
    gd = dp // len(POOL_WINDOWS)
    NH = CONV_W - 1
    NP = POOL_MAX_W - 1

    for t in range(TS):
        xt_scr[t * BS:(t + 1) * BS, :] = x_ref[t]

    _rmsnorm_to_bf16(xt_scr, gmix_ref, h_scr, T)
    a_scr[...] = jnp.dot(h_scr[...], win_ref[:, 0:dc], preferred_element_type=F32)
    pg_scr[...] = jnp.dot(h_scr[...], win_ref[:, dc:2 * dc], preferred_element_type=F32)
    u_scr[...] = jnp.dot(h_scr[...], win_ref[:, 2 * dc:2 * dc + dp], preferred_element_type=F32)

    R = 16

    def glu(i, c):
        rs = _rows(i, R)
        a_scr[rs, :] = a_scr[rs, :] * _sigmoid(pg_scr[rs, :])
        return c

    lax.fori_loop(0, T // R, glu, 0, unroll=ROW_LOOP_UNROLL)

    def ext_conv(j, rs, ls):
        if j < NH:
            return sconv_ref[j, rs, ls]
        return a_scr[pl.ds((j - NH) * BS + rs.start, rs.size), ls]

    def ext_pool(j, rs, ls):
        if j < NP:
            return spool_ref[j, rs, ls]
        return u_scr[pl.ds((j - NP) * BS + rs.start, rs.size), ls]

    RC, LC = 32, 256

    def conv(i, c):
        rs = _rows(i, RC)
        for t in range(TS):
            for lc in range(dc // LC):
                ls = slice(lc * LC, (lc + 1) * LC)
                acc = jnp.zeros((RC, LC), F32)
                for k in range(CONV_W):
                    acc = acc + wdw_ref[k:k + 1, ls] * ext_conv(t + k, rs, ls)
                yc_scr[pl.ds(t * BS + rs.start, RC), ls] = acc + bdw_ref[:, ls]
        return c

    lax.fori_loop(0, BS // RC, conv, 0)

    _layernorm_swish(yc_scr, lng_ref, lnb_ref, mix_scr, T, dc)

    def pool(i, c):
        rs = _rows(i, RC)
        for t in range(TS):
            for g, w in enumerate(POOL_WINDOWS):
                ls = slice(g * gd, (g + 1) * gd)
                tot = ext_pool(NP + t, rs, ls)
                cur = tot
                for back in range(1, w):
                    tot = tot + ext_pool(NP + t - back, rs, ls)
                sd_scr[pl.ds(t * BS + rs.start, RC), ls] = (tot / jnp.float32(w) - cur).astype(BF16)
        return c

    lax.fori_loop(0, BS // RC, pool, 0)

    _pool_project(sd_scr, wpool_ref, pscale_ref, mix_scr, dc, gd)

    for j in range(NH):
        src = j + TS
        nconv_ref[j] = sconv_ref[src] if src < NH else a_scr[(src - NH) * BS:(src - NH + 1) * BS, :]
    for j in range(NP):
        src = j + TS
        npool_ref[j] = spool_ref[src] if src < NP else u_scr[(src - NP) * BS:(src - NP + 1) * BS, :]

    _out_proj_residual(xt_scr, mix_scr, wout_ref, x2_ref)
    _router(x2_ref, gffn_ref, wrh_ref, wrl_ref, br_ref, hf_ref, ids_ref, w0_ref, w1_ref,
            hhi_scr, hlo_scr, T, d)


def _const_spec(shape):
    nd = len(shape)
    return pl.BlockSpec(shape, lambda *a: (0,) * nd, pipeline_mode=pl.Buffered(1))


def _mixer_weight_specs(d, dc, dp, cols):
    gd = dp // len(POOL_WINDOWS)
    return [
        _const_spec((1, d)),
        _const_spec((d, cols)),
        _const_spec((CONV_W, dc)),
        _const_spec((1, dc)),
        _const_spec((1, dc)),
        _const_spec((1, dc)),
        _const_spec((len(POOL_WINDOWS), gd, gd)),
        _const_spec((1, dp)),
        _const_spec((dc + dp, d)),
        _const_spec((1, d)),
        _const_spec((d, LANES)),
        _const_spec((d, LANES)),
        _const_spec((1, LANES)),
    ]


def _mixer_prompt(x, wts, *, T, n_total):
    B, S, d = x.shape
    dc = wts[2].shape[1]
    dp = wts[7].shape[1]
    cols = wts[1].shape[1]
    n_s = S // T
    N = n_total
    n_blk = N // T
    assert N % T == 0 and N - B * S <= S
    tok = lambda b, s: (jnp.minimum(b * n_s + s, n_blk - 1), 0)
    tok4 = lambda b, s: (jnp.minimum(b * n_s + s, n_blk - 1), 0, 0, 0)
    bclamp = lambda b: jnp.minimum(b, B - 1)
    out_shape = (
        jax.ShapeDtypeStruct((N, d), F32),
        jax.ShapeDtypeStruct((N // TOK_CHUNK, TOK_CHUNK, d // LANES, LANES), BF16),
        jax.ShapeDtypeStruct((N, LANES), jnp.int32),
        jax.ShapeDtypeStruct((N, LANES), F32),
        jax.ShapeDtypeStruct((N, LANES), F32),
        jax.ShapeDtypeStruct((1, B, CONV_W - 1, dc), F32),
        jax.ShapeDtypeStruct((1, B, POOL_MAX_W - 1, dp), F32),
    )
    out_specs = (
        pl.BlockSpec((T, d), tok),
        pl.BlockSpec((T // TOK_CHUNK, TOK_CHUNK, d // LANES, LANES), tok4),
        pl.BlockSpec((T, LANES), tok),
        pl.BlockSpec((T, LANES), tok),
        pl.BlockSpec((T, LANES), tok),
        pl.BlockSpec((None, None, CONV_W - 1, dc), lambda b, s: (0, bclamp(b), 0, 0)),
        pl.BlockSpec((None, None, POOL_MAX_W - 1, dp), lambda b, s: (0, bclamp(b), 0, 0)),
    )
    scratch = [
        pltpu.VMEM((T, d), BF16),
        pltpu.VMEM((T, dc), F32),
        pltpu.VMEM((T + TAIL, dc), F32),
        pltpu.VMEM((T + PTAIL, dp), F32),
        pltpu.VMEM((T, dc), F32),
        pltpu.VMEM((T, dp), BF16),
        pltpu.VMEM((T, dc + dp), BF16),
        pltpu.VMEM((T, d), BF16),
        pltpu.VMEM((T, d), BF16),
    ]
    return pl.pallas_call(
        functools.partial(_mixer_prompt_kernel, T=T, d=d, dc=dc, dp=dp, n_batch=B),
        grid=(B + 1, n_s),
        in_specs=[pl.BlockSpec((None, T, d), lambda b, s: (bclamp(b), s, 0))]
                 + _mixer_weight_specs(d, dc, dp, cols),
        out_specs=out_specs,
        out_shape=out_shape,
        scratch_shapes=scratch,
        compiler_params=pltpu.CompilerParams(
            dimension_semantics=("arbitrary", "arbitrary"), vmem_limit_bytes=VMEM_LIMIT),
        name="mixer_prompt",
    )(x, *wts)


def _mixer_sample(x_t, sconv_t, spool_t, wts, tok_arrays, *, BS, tok0):
    TS, Bd, d = x_t.shape
    dc = wts[2].shape[1]
    dp = wts[7].shape[1]
    cols = wts[1].shape[1]
    T = TS * BS
    n_b = Bd // BS
    b0 = tok0 // T
    tok = lambda i: (b0 + i, 0)
    tok4 = lambda i: (b0 + i, 0, 0, 0)
    out_shape = tuple(jax.ShapeDtypeStruct(a.shape, a.dtype) for a in tok_arrays) + (
        jax.ShapeDtypeStruct((CONV_W - 1, Bd, dc), F32),
        jax.ShapeDtypeStruct((POOL_MAX_W - 1, Bd, dp), F32),
    )
    out_specs = (
        pl.BlockSpec((T, d), tok),
        pl.BlockSpec((T // TOK_CHUNK, TOK_CHUNK, d // LANES, LANES), tok4),
        pl.BlockSpec((T, LANES), tok),
        pl.BlockSpec((T, LANES), tok),
        pl.BlockSpec((T, LANES), tok),
        pl.BlockSpec((CONV_W - 1, BS, dc), lambda i: (0, i, 0)),
        pl.BlockSpec((POOL_MAX_W - 1, BS, dp), lambda i: (0, i, 0)),
    )
    scratch = [
        pltpu.VMEM((T, d), F32),
        pltpu.VMEM((T, d), BF16),
        pltpu.VMEM((T, dc), F32),
        pltpu.VMEM((T, dc), F32),
        pltpu.VMEM((T, dp), F32),
        pltpu.VMEM((T, dc), F32),
        pltpu.VMEM((T, dp), BF16),
        pltpu.VMEM((T, dc + dp), BF16),
        pltpu.VMEM((T, d), BF16),
        pltpu.VMEM((T, d), BF16),
    ]
    return pl.pallas_call(
        functools.partial(_mixer_sample_kernel, TS=TS, BS=BS, d=d, dc=dc, dp=dp),
        grid=(n_b,),
        in_specs=[pl.BlockSpec((TS, BS, d), lambda i: (0, i, 0)),
                  pl.BlockSpec((CONV_W - 1, BS, dc), lambda i: (0, i, 0), pipeline_mode=pl.Buffered(1)),
                  pl.BlockSpec((POOL_MAX_W - 1, BS, dp), lambda i: (0, i, 0), pipeline_mode=pl.Buffered(1))]
                 + _mixer_weight_specs(d, dc, dp, cols)
                 + [pl.BlockSpec(memory_space=pl.ANY)] * len(tok_arrays),
        out_specs=out_specs,
        out_shape=out_shape,
        scratch_shapes=scratch,
        input_output_aliases={3 + len(wts) + j: j for j in range(len(tok_arrays))},
        compiler_params=pltpu.CompilerParams(
            dimension_semantics=("arbitrary",), vmem_limit_bytes=VMEM_LIMIT),
        name="mixer_sample",
    )(x_t, sconv_t, spool_t, *wts, *tok_arrays)


def _invert_kernel(pos_ref, src_in, dst_in, src_ref, dst_ref, sem, *, n_pairs, n_tok):
    init_src = pltpu.make_async_copy(src_in, src_ref, sem.at[0])
    init_dst = pltpu.make_async_copy(dst_in, dst_ref, sem.at[1])
    init_src.start()
    init_dst.start()
    init_src.wait()
    init_dst.wait()
    G = 8

    def put(b, c):
        n0 = b * G
        ps = [pos_ref[2 * n0 + q] for q in range(2 * G)]
        for q in range(2 * G):
            n, k = n0 + q // 2, q % 2
            src_ref[ps[q]] = n
            dst_ref[ps[q]] = k * n_tok + n
        return c

    assert n_pairs % (2 * G) == 0
    lax.fori_loop(0, n_pairs // (2 * G), put, 0)


def _invert(pos, *, n_tiles, tm, n_tok):
    n_pairs = pos.shape[0]
    p = jnp.arange((n_tiles + 1) * tm, dtype=jnp.int32)
    tile = p // tm
    buf = jnp.where(tile == n_tiles, 1, tile % 2)
    dst_init = 2 * n_tok + buf * tm + p % tm
    src_init = jnp.zeros((n_tiles * tm,), jnp.int32)
    smem = pl.BlockSpec(memory_space=pltpu.SMEM)
    return pl.pallas_call(
        functools.partial(_invert_kernel, n_pairs=n_pairs, n_tok=n_tok),
        in_specs=[smem, pl.BlockSpec(memory_space=pl.ANY), pl.BlockSpec(memory_space=pl.ANY)],
        out_specs=(smem, smem),
        out_shape=(jax.ShapeDtypeStruct(src_init.shape, jnp.int32), jax.ShapeDtypeStruct(dst_init.shape, jnp.int32)),
        scratch_shapes=[pltpu.SemaphoreType.DMA((2,))],
        name="invert",
    )(pos, src_init, dst_init)


def _expert_mlp_kernel(te_ref, na_ref, src_next_ref, src_cur_ref, dst_prev_ref, dst_cur_ref, hf_ref,
                       wg_ref, wu_ref, wd_ref, o_ref, xbuf0, xbuf1, ybuf0, ybuf1, wg_scr, wu_scr, wd_scr,
                       gsem, ssem, *, tm, n_tiles, d):
    i = pl.program_id(0)
    n_act = na_ref[0]
    active = i < n_act
    slot = i % 2
    last = n_tiles - 1
    prev = te_ref[jnp.maximum(i - 1, 0)]
    new_expert = (i == 0) | (te_ref[i] != prev)
    xbuf = (xbuf0, xbuf1)
    ybuf = (ybuf0, ybuf1)

    def gather_start(src_ref, b):
        for r in range(tm):
            pltpu.make_async_copy(hf_ref.at[pl.ds(src_ref[0, 0, r], 1)], xbuf[b].at[pl.ds(r, 1)],
                                  gsem.at[b]).start(priority=r % 2)

    def gather_wait(b):
        pltpu.make_async_copy(hf_ref.at[pl.ds(0, tm)], xbuf[b], gsem.at[b]).wait()

    def scatter_start(dst_ref, b):
        for r in range(tm):
            pltpu.make_async_copy(ybuf[b].at[pl.ds(r, 1)], o_ref.at[pl.ds(dst_ref[0, 0, r], 1)],
                                  ssem.at[b]).start(priority=r % 2)

    def scatter_wait(b):
        pltpu.make_async_copy(ybuf[b], o_ref.at[pl.ds(0, tm)], ssem.at[b]).wait()

    @pl.when(i == 0)
    def _():
        gather_start(src_cur_ref, 0)
        ybuf1[...] = jnp.zeros(ybuf1.shape, BF16)
        trash0 = pltpu.make_async_copy(ybuf1, o_ref.at[pl.ds(o_ref.shape[0] - 2 * tm, tm)], ssem.at[0])
        trash0.start()
        trash0.wait()

    @pl.when(active & new_expert)
    def _():
        wg_scr[...] = wg_ref[...].astype(BF16)
        wu_scr[...] = wu_ref[...].astype(BF16)
        wd_scr[...] = wd_ref[...].astype(BF16)

    for par in range(2):
        is_par = slot == par

        @pl.when(is_par & (i <= n_act))
        def _():
            gather_wait(par)

        @pl.when(is_par & (i >= 1) & (i - 2 < n_act))
        def _():
            scatter_wait(par)

        @pl.when(is_par & active)
        def _():
            x = xbuf[par][...].reshape(tm, d)
            g = jnp.dot(x, wg_scr[...], preferred_element_type=F32)
            u = jnp.dot(x, wu_scr[...], preferred_element_type=F32)
            act = (g * _sigmoid(g) * u).astype(BF16)
            y = jnp.dot(act, wd_scr[...], preferred_element_type=F32)
            ybuf[par][...] = y.astype(BF16).reshape(tm, d // LANES, LANES)
            gather_start(src_next_ref, 1 - par)
            scatter_start(dst_prev_ref, 1 - par)

        @pl.when(is_par & (i == n_act))
        def _():
            scatter_start(dst_prev_ref, 1 - par)

        @pl.when(is_par & (i == last))
        def _():
            @pl.when(last - 1 < n_act)
            def _():
                scatter_wait(1 - par)

            @pl.when(last < n_act)
            def _():
                scatter_start(dst_cur_ref, par)
                scatter_wait(par)
                gather_wait(1 - par)


def _expert_mlp(tile_expert, n_active, src, dst, hf, w_gate, w_up, w_down, *, tm):
    n_tok = hf.shape[0]
    E, d, de = w_gate.shape
    n_tiles = src.shape[0]
    row = (tm, d // LANES, LANES)
    nxt = lambda i, te, na: (jnp.minimum(i + 1, n_tiles - 1), 0, 0)
    cur = lambda i, te, na: (i, 0, 0)
    prv = lambda i, te, na: (jnp.where(i == 0, n_tiles, i - 1), 0, 0)
    smem_tile = lambda imap: pl.BlockSpec((1, 1, tm), imap, memory_space=pltpu.SMEM)
    return pl.pallas_call(
        functools.partial(_expert_mlp_kernel, tm=tm, n_tiles=n_tiles, d=d),
        grid_spec=pltpu.PrefetchScalarGridSpec(
            num_scalar_prefetch=2,
            grid=(n_tiles,),
            in_specs=[
                smem_tile(nxt), smem_tile(cur), smem_tile(prv), smem_tile(cur),
                pl.BlockSpec(memory_space=pl.ANY),
                pl.BlockSpec((None, d, de), lambda i, te, na: (te[i], 0, 0)),
                pl.BlockSpec((None, d, de), lambda i, te, na: (te[i], 0, 0)),
                pl.BlockSpec((None, de, d), lambda i, te, na: (te[i], 0, 0)),
            ],
            out_specs=pl.BlockSpec(memory_space=pl.ANY),
            scratch_shapes=[pltpu.VMEM(row, BF16), pltpu.VMEM(row, BF16),
                            pltpu.VMEM(row, BF16), pltpu.VMEM(row, BF16),
                            pltpu.VMEM((d, de), BF16), pltpu.VMEM((d, de), BF16), pltpu.VMEM((de, d), BF16),
                            pltpu.SemaphoreType.DMA((2,)), pltpu.SemaphoreType.DMA((2,))],
        ),
        out_shape=jax.ShapeDtypeStruct((2 * n_tok + 2 * tm, d // LANES, LANES), BF16),
        compiler_params=pltpu.CompilerParams(
            dimension_semantics=("arbitrary",), vmem_limit_bytes=VMEM_LIMIT),
        name="expert_mlp",
    )(tile_expert, n_active, src, src, dst, dst, hf, w_gate, w_up, w_down)


def _combine_kernel(x2_ref, o0_ref, o1_ref, w0_ref, w1_ref, gfin_ref, y_ref, *, R, d):
    C = TOK_CHUNK

    def body(i, c):
        rs = _rows(i, C)
        w0 = w0_ref[rs, :]
        w1 = w1_ref[rs, :]
        o0 = o0_ref[i].reshape(C, d).astype(F32)
        o1 = o1_ref[i].reshape(C, d).astype(F32)
        parts = []
        ssq = jnp.zeros((C, LANES), F32)
        for j in range(d // LANES):
            ls = slice(j * LANES, (j + 1) * LANES)
            m = w0 * o0[:, ls] + w1 * o1[:, ls]
            v = x2_ref[rs, ls] + m
            ssq = ssq + v * v
            parts.append(v)
        ms = jnp.sum(ssq, axis=-1, keepdims=True) * jnp.float32(1.0 / d)
        inv = lax.rsqrt(ms + EPS)
        for j in range(d // LANES):
            ls = slice(j * LANES, (j + 1) * LANES)
            y_ref[rs, ls] = parts[j] * inv * gfin_ref[:, ls]
        return c

    lax.fori_loop(0, R // C, body, 0)


def _combine(x2, o, w0, w1, g_final, *, R, tok0, n_out):
    n_tok, d = x2.shape
    n_steps = n_out // R
    b0 = tok0 // R
    b1 = n_tok // R
    tok = lambda s: (b0 + s, 0)
    orow = (R // TOK_CHUNK, TOK_CHUNK, d // LANES, LANES)
    o4 = o.reshape(o.shape[0] // TOK_CHUNK, TOK_CHUNK, d // LANES, LANES)
    return pl.pallas_call(
        functools.partial(_combine_kernel, R=R, d=d),
        grid=(n_steps,),
        in_specs=[
            pl.BlockSpec((R, d), tok),
            pl.BlockSpec(orow, lambda s: (b0 + s, 0, 0, 0)),
            pl.BlockSpec(orow, lambda s: (b1 + b0 + s, 0, 0, 0)),
            pl.BlockSpec((R, LANES), tok),
            pl.BlockSpec((R, LANES), tok),
            pl.BlockSpec((1, d), lambda s: (0, 0)),
        ],
        out_specs=pl.BlockSpec((R, d), lambda s: (s, 0)),
        out_shape=jax.ShapeDtypeStruct((n_out, d), F32),
        compiler_params=pltpu.CompilerParams(
            dimension_semantics=("arbitrary",), vmem_limit_bytes=VMEM_LIMIT),
        name="combine",
    )(x2, o4, o4, w0, w1, g_final)


def _routing_plan(ids, tm, n_tiles):
    e0 = ids[:, 0]
    e1 = ids[:, 1]
    ar = jnp.arange(N_EXPERTS, dtype=jnp.int32)
    m = ((e0[:, None] == ar) | (e1[:, None] == ar)).astype(jnp.int32)
    incl = jnp.cumsum(m, axis=0)
    excl = incl - m
    counts = incl[-1]
    tiles_e = (counts + tm - 1) // tm
    tile_end = jnp.cumsum(tiles_e)
    offs = (tile_end - tiles_e) * tm
    base = offs[None, :] + excl
    pos0 = jnp.take_along_axis(base, e0[:, None], axis=1)[:, 0]
    pos1 = jnp.take_along_axis(base, e1[:, None], axis=1)[:, 0]
    pos = jnp.stack([pos0, pos1], axis=1).reshape(-1).astype(jnp.int32)
    n_active = tile_end[-1].astype(jnp.int32)
    t = jnp.arange(n_tiles, dtype=jnp.int32)
    tq = jnp.minimum(t, n_active - 1)
    te = jnp.sum((tile_end[None, :] <= tq[:, None]).astype(jnp.int32), axis=1)
    te = jnp.minimum(te, N_EXPERTS - 1)
    return pos, te, n_active.reshape(1)


T_PROMPT = 256
BS_SAMPLE = 32
TM_EXPERT = 256
R_COMBINE = 128


def kernel(x_prompt, x_sample, state_conv, state_pool, g_mix, w_in, w_dw, b_dw, ln_g, ln_b, w_pool, pool_scale, w_out, g_ffn, w_rg, b_rg, w_re, b_re, w_gate, w_up, w_down, g_final):
    depth = g_mix.shape[0]
    assert depth == 1
    B, S, d = x_prompt.shape
    Bd, TS, _ = x_sample.shape
    n_p = B * S
    n_s = Bd * TS
    N = n_p + n_s

    w_r = jnp.zeros((d, LANES), F32)
    w_r = w_r.at[:, 0:N_EXPERTS].set(w_re[0]).at[:, ROUTER_GROUP_LANE0:ROUTER_GROUP_LANE0 + N_GROUPS].set(w_rg[0])
    w_r_hi = w_r.astype(BF16)
    w_r_lo = (w_r - w_r_hi.astype(F32)).astype(BF16)
    b_r = jnp.zeros((1, LANES), F32)
    b_r = b_r.at[0, 0:N_EXPERTS].set(b_re[0]).at[0, ROUTER_GROUP_LANE0:ROUTER_GROUP_LANE0 + N_GROUPS].set(b_rg[0])
    wts = (g_mix[0][None], w_in[0].astype(BF16), w_dw[0], b_dw[0][None], ln_g[0][None], ln_b[0][None],
           w_pool[0].astype(BF16), pool_scale[0][None], w_out[0].astype(BF16), g_ffn[0][None],
           w_r_hi, w_r_lo, b_r)

    *tok_arrays, nconv_p, npool_p = _mixer_prompt(x_prompt, wts, T=T_PROMPT, n_total=N)

    x_t = jnp.transpose(x_sample, (1, 0, 2))
    sconv_t = jnp.transpose(state_conv[0], (1, 0, 2))
    spool_t = jnp.transpose(state_pool[0], (1, 0, 2))
    x2, hf, ids, w0, w1, nconv_t, npool_t = _mixer_sample(x_t, sconv_t, spool_t, wts, tok_arrays,
                                                          BS=BS_SAMPLE, tok0=n_p)

    tm = TM_EXPERT
    n_tiles = (2 * N + N_EXPERTS * (tm - 1) + tm - 1) // tm
    pos, tile_expert, n_active = _routing_plan(ids[:, 0:2], tm, n_tiles)
    src, dst = _invert(pos, n_tiles=n_tiles, tm=tm, n_tok=N)
    src = src.reshape(n_tiles, 1, tm)
    dst = dst.reshape(n_tiles + 1, 1, tm)

    hf = hf.reshape(N, d // LANES, LANES)
    o = _expert_mlp(tile_expert, n_active, src, dst, hf, w_gate[0], w_up[0], w_down[0], tm=tm)

    gfin = g_final[None]
    y_p = _combine(x2, o, w0, w1, gfin, R=R_COMBINE, tok0=0, n_out=n_p)
    y_s = _combine(x2, o, w0, w1, gfin, R=R_COMBINE, tok0=n_p, n_out=n_s)

    y_prompt = y_p.reshape(B, S, d)
    y_sample = y_s.reshape(Bd // BS_SAMPLE, TS, BS_SAMPLE, d).transpose(0, 2, 1, 3).reshape(Bd, TS, d)
    new_conv_s = jnp.transpose(nconv_t, (1, 0, 2))[None]
    new_pool_s = jnp.transpose(npool_t, (1, 0, 2))[None]
    return (y_prompt, y_sample, nconv_p, new_conv_s, npool_p, new_pool_s)
```

```python
import functools

import jax
import jax.numpy as jnp
from jax import lax
from jax.experimental import pallas as pl
from jax.experimental.pallas import tpu as pltpu

F32 = jnp.float32
BF16 = jnp.bfloat16
EPS = 1e-6

LANES = 128
SUBLANES = 8
VMEM_LIMIT = 56 * 1024 * 1024

CONV_W = 31
POOL_WINDOWS = (2, 4, 8, 16)
POOL_MAX_W = 16
N_GROUPS = 4
PER_GROUP = 8
N_EXPERTS = N_GROUPS * PER_GROUP

ROUTER_GROUP_LANE0 = N_EXPERTS
TOK_CHUNK = 16
ROW_LOOP_UNROLL = 4
TAIL = 32
PTAIL = 16


def _rows(i, r):
    return pl.ds(pl.multiple_of(i * r, r), r)


def _sigmoid(x):
    return 1.0 / (1.0 + jnp.exp(-x))


def _rmsnorm_to_bf16(src_ref, g_ref, dst_ref, T):
    R = 16

    def body(i, c):
        rs = _rows(i, R)
        x = src_ref[rs, :]
        ms = jnp.mean(x * x, axis=-1, keepdims=True)
        dst_ref[rs, :] = (x * lax.rsqrt(ms + EPS) * g_ref[...]).astype(BF16)
        return c

    lax.fori_loop(0, T // R, body, 0, unroll=ROW_LOOP_UNROLL)


def _layernorm_swish(y_ref, lng_ref, lnb_ref, mix_ref, T, dc):
    R = 16

    def body(i, c):
        rs = _rows(i, R)
        y = y_ref[rs, :]
        mu = jnp.mean(y, axis=-1, keepdims=True)
        d = y - mu
        var = jnp.mean(d * d, axis=-1, keepdims=True)
        z = d * lax.rsqrt(var + EPS) * lng_ref[...] + lnb_ref[...]
        mix_ref[rs, 0:dc] = (z * _sigmoid(z)).astype(BF16)
        return c

    lax.fori_loop(0, T // R, body, 0, unroll=ROW_LOOP_UNROLL)


def _pool_project(sd_ref, wpool_ref, pscale_ref, mix_ref, dc, gd):
    for g in range(len(POOL_WINDOWS)):
        sl = slice(g * gd, (g + 1) * gd)
        o = jnp.dot(sd_ref[:, sl], wpool_ref[g], preferred_element_type=F32)
        mix_ref[:, dc + g * gd: dc + (g + 1) * gd] = (o * pscale_ref[:, sl]).astype(BF16)


def _out_proj_residual(x_ref, mix_ref, wout_ref, x2_ref):
    x2_ref[...] = x_ref[...] + jnp.dot(mix_ref[...], wout_ref[...], preferred_element_type=F32)


def _router(x2_ref, gffn_ref, wrh_ref, wrl_ref, br_ref, hf_ref, ids_ref, w0_ref, w1_ref,
            hhi_scr, hlo_scr, T, d):
    R = TOK_CHUNK

    def body(i, c):
        rs = _rows(i, R)
        x = x2_ref[rs, :]
        ms = jnp.mean(x * x, axis=-1, keepdims=True)
        h = x * lax.rsqrt(ms + EPS) * gffn_ref[...]
        hi = h.astype(BF16)
        hf_ref[i] = hi.reshape(R, d // LANES, LANES)
        hhi_scr[rs, :] = hi
        hlo_scr[rs, :] = (h - hi.astype(F32)).astype(BF16)
        return c

    lax.fori_loop(0, T // R, body, 0, unroll=ROW_LOOP_UNROLL)

    lg = (jnp.dot(hhi_scr[...], wrh_ref[...], preferred_element_type=F32)
          + jnp.dot(hlo_scr[...], wrh_ref[...], preferred_element_type=F32)
          + jnp.dot(hhi_scr[...], wrl_ref[...], preferred_element_type=F32)
          + br_ref[...])

    lane = lax.broadcasted_iota(jnp.int32, lg.shape, 1).astype(F32)
    neg = jnp.float32(-jnp.inf)
    big = jnp.float32(1e9)
    g_lo = jnp.float32(ROUTER_GROUP_LANE0)
    gmask = (lane >= g_lo) & (lane < g_lo + N_GROUPS)
    lgg = jnp.where(gmask, lg, neg)
    gmax = jnp.max(lgg, axis=-1, keepdims=True)
    gsel = jnp.min(jnp.where(lgg == gmax, lane, big), axis=-1, keepdims=True) - g_lo
    gsum = jnp.sum(jnp.where(gmask, jnp.exp(lg - gmax), 0.0), axis=-1, keepdims=True)
    p_g = 1.0 / gsum

    e_lo = gsel * PER_GROUP
    emask = (lane >= e_lo) & (lane < e_lo + PER_GROUP)
    le = jnp.where(emask, lg, neg)
    v0 = jnp.max(le, axis=-1, keepdims=True)
    i0 = jnp.min(jnp.where(le == v0, lane, big), axis=-1, keepdims=True)
    le2 = jnp.where(lane == i0, neg, le)
    v1 = jnp.max(le2, axis=-1, keepdims=True)
    i1 = jnp.min(jnp.where(le2 == v1, lane, big), axis=-1, keepdims=True)
    ex = jnp.exp(v1 - v0)
    den = 1.0 / (1.0 + ex)
    w0 = den * p_g
    w1 = ex * den * p_g

    ids_ref[...] = jnp.where(lane == 0.0, i0, jnp.where(lane == 1.0, i1, 0.0)).astype(jnp.int32)
    w0_ref[...] = jnp.broadcast_to(w0, lg.shape)
    w1_ref[...] = jnp.broadcast_to(w1, lg.shape)


def _mixer_prompt_kernel(x_ref, gmix_ref, win_ref, wdw_ref, bdw_ref, lng_ref, lnb_ref, wpool_ref,
                         pscale_ref, wout_ref, gffn_ref, wrh_ref, wrl_ref, br_ref,
                         x2_ref, hf_ref, ids_ref, w0_ref, w1_ref, nconv_ref, npool_ref,
                         h_scr, pg_scr, ext_scr, extu_scr, yc_scr, sd_scr, mix_scr, hhi_scr, hlo_scr,
                         *, T, d, dc, dp, n_batch):
    @pl.when(pl.program_id(0) == n_batch)
    def _():
        x2_ref[...] = jnp.zeros(x2_ref.shape, F32)
        hf_ref[...] = jnp.zeros(hf_ref.shape, BF16)
        ids_ref[...] = jnp.zeros(ids_ref.shape, jnp.int32)
        w0_ref[...] = jnp.zeros(w0_ref.shape, F32)
        w1_ref[...] = jnp.zeros(w1_ref.shape, F32)

    @pl.when(pl.program_id(0) < n_batch)
    def _():
        _mixer_prompt_tile(x_ref, gmix_ref, win_ref, wdw_ref, bdw_ref, lng_ref, lnb_ref, wpool_ref,
                           pscale_ref, wout_ref, gffn_ref, wrh_ref, wrl_ref, br_ref,
                           x2_ref, hf_ref, ids_ref, w0_ref, w1_ref, nconv_ref, npool_ref,
                           h_scr, pg_scr, ext_scr, extu_scr, yc_scr, sd_scr, mix_scr, hhi_scr, hlo_scr,
                           T=T, d=d, dc=dc, dp=dp)


def _mixer_prompt_tile(x_ref, gmix_ref, win_ref, wdw_ref, bdw_ref, lng_ref, lnb_ref, wpool_ref,
                       pscale_ref, wout_ref, gffn_ref, wrh_ref, wrl_ref, br_ref,
                       x2_ref, hf_ref, ids_ref, w0_ref, w1_ref, nconv_ref, npool_ref,
                       h_scr, pg_scr, ext_scr, extu_scr, yc_scr, sd_scr, mix_scr, hhi_scr, hlo_scr,
                       *, T, d, dc, dp):
    s = pl.program_id(1)
    n_s = pl.num_programs(1)
    gd = dp // len(POOL_WINDOWS)

    @pl.when(s == 0)
    def _():
        ext_scr[0:TAIL, :] = jnp.zeros((TAIL, dc), F32)
        extu_scr[0:PTAIL, :] = jnp.zeros((PTAIL, dp), F32)

    _rmsnorm_to_bf16(x_ref, gmix_ref, h_scr, T)

    ext_scr[TAIL:TAIL + T, :] = jnp.dot(h_scr[...], win_ref[:, 0:dc], preferred_element_type=F32)
    pg_scr[...] = jnp.dot(h_scr[...], win_ref[:, dc:2 * dc], preferred_element_type=F32)
    extu_scr[PTAIL:PTAIL + T, :] = jnp.dot(h_scr[...], win_ref[:, 2 * dc:2 * dc + dp],
                                           preferred_element_type=F32)

    R = 16

    def glu(i, c):
        rs = _rows(i, R)
        es = pl.ds(pl.multiple_of(i * R, R) + TAIL, R)
        ext_scr[es, :] = ext_scr[es, :] * _sigmoid(pg_scr[rs, :])
        return c

    lax.fori_loop(0, T // R, glu, 0, unroll=ROW_LOOP_UNROLL)

    RC, LC = 32, 256
    BR = RC + TAIL
    shift0 = TAIL - (CONV_W - 1)

    def conv(i, c):
        r0 = pl.multiple_of(i * RC, RC)
        for lc in range(dc // LC):
            ls = slice(lc * LC, (lc + 1) * LC)
            blk = ext_scr[pl.ds(r0, BR), ls]
            acc = jnp.zeros((RC, LC), F32)
            for sft in range(SUBLANES):
                taps = [k for k in range(CONV_W) if (k + shift0) % SUBLANES == sft]
                if not taps:
                    continue
                rolled = blk if sft == 0 else pltpu.roll(blk, BR - sft, 0)
                for k in taps:
                    q = (k + shift0) // SUBLANES
                    acc = acc + wdw_ref[k:k + 1, ls] * rolled[q * SUBLANES:q * SUBLANES + RC, :]
            yc_scr[pl.ds(r0, RC), ls] = acc + bdw_ref[:, ls]
        return c

    lax.fori_loop(0, T // RC, conv, 0)

    _layernorm_swish(yc_scr, lng_ref, lnb_ref, mix_scr, T, dc)

    RP = 32
    BP = RP + PTAIL
    pos_base = s * T

    def pool(i, c):
        r0 = pl.multiple_of(i * RP, RP)
        pos = (pos_base + r0 + lax.broadcasted_iota(jnp.int32, (RP, gd), 0)).astype(F32)
        for g, w in enumerate(POOL_WINDOWS):
            ls = slice(g * gd, (g + 1) * gd)
            blk = extu_scr[pl.ds(r0, BP), ls]
            run = blk
            span = 1
            while span < w:
                run = run + pltpu.roll(run, span, 0)
                span *= 2
            cnt = jnp.minimum(pos + 1.0, jnp.float32(w))
            mean = run[PTAIL:PTAIL + RP, :] / cnt
            sd_scr[pl.ds(r0, RP), ls] = (mean - blk[PTAIL:PTAIL + RP, :]).astype(BF16)
        return c

    lax.fori_loop(0, T // RP, pool, 0)

    _pool_project(sd_scr, wpool_ref, pscale_ref, mix_scr, dc, gd)

    @pl.when(s == n_s - 1)
    def _():
        nconv_ref[...] = ext_scr[TAIL + T - (CONV_W - 1):TAIL + T, :]
        npool_ref[...] = extu_scr[PTAIL + T - (POOL_MAX_W - 1):PTAIL + T, :]

    ext_scr[0:TAIL, :] = ext_scr[T:T + TAIL, :]
    extu_scr[0:PTAIL, :] = extu_scr[T:T + PTAIL, :]

    _out_proj_residual(x_ref, mix_scr, wout_ref, x2_ref)
    _router(x2_ref, gffn_ref, wrh_ref, wrl_ref, br_ref, hf_ref, ids_ref, w0_ref, w1_ref,
            hhi_scr, hlo_scr, T, d)


def _mixer_sample_kernel(x_ref, sconv_ref, spool_ref, gmix_ref, win_ref, wdw_ref, bdw_ref, lng_ref,
                         lnb_ref, wpool_ref, pscale_ref, wout_ref, gffn_ref, wrh_ref, wrl_ref, br_ref,
                         x2_in, hf_in, ids_in, w0_in, w1_in,
                         x2_ref, hf_ref, ids_ref, w0_ref, w1_ref, nconv_ref, npool_ref,
                         xt_scr, h_scr, a_scr, pg_scr, u_scr, yc_scr, sd_scr, mix_scr, hhi_scr, hlo_scr,
                         *, TS, BS, d, dc, dp):
    del x2_in, hf_in, ids_in, w0_in, w1_in
    T = TS * BS
    gd = dp // len(POOL_WINDOWS)
    NH = CONV_W - 1
    NP = POOL_MAX_W - 1

    for t in range(TS):
        xt_scr[t * BS:(t + 1) * BS, :] = x_ref[t]

    _rmsnorm_to_bf16(xt_scr, gmix_ref, h_scr, T)
    a_scr[...] = jnp.dot(h_scr[...], win_ref[:, 0:dc], preferred_element_type=F32)
    pg_scr[...] = jnp.dot(h_scr[...], win_ref[:, dc:2 * dc], preferred_element_type=F32)
    u_scr[...] = jnp.dot(h_scr[...], win_ref[:, 2 * dc:2 * dc + dp], preferred_element_type=F32)

    R = 16

    def glu(i, c):
        rs = _rows(i, R)
        a_scr[rs, :] = a_scr[rs, :] * _sigmoid(pg_scr[rs, :])
        return c

    lax.fori_loop(0, T // R, glu, 0, unroll=ROW_LOOP_UNROLL)

    def ext_conv(j, rs, ls):
        if j < NH:
            return sconv_ref[j, rs, ls]
        return a_scr[pl.ds((j - NH) * BS + rs.start, rs.size), ls]

    def ext_pool(j, rs, ls):
        if j < NP:
            return spool_ref[j, rs, ls]
        return u_scr[pl.ds((j - NP) * BS + rs.start, rs.size), ls]

    RC, LC = 32, 256

    def conv(i, c):
        rs = _rows(i, RC)
        for t in range(TS):
            for lc in range(dc // LC):
                ls = slice(lc * LC, (lc + 1) * LC)
                acc = jnp.zeros((RC, LC), F32)
                for k in range(CONV_W):
                    acc = acc + wdw_ref[k:k + 1, ls] * ext_conv(t + k, rs, ls)
                yc_scr[pl.ds(t * BS + rs.start, RC), ls] = acc + bdw_ref[:, ls]
        return c

    lax.fori_loop(0, BS // RC, conv, 0)

    _layernorm_swish(yc_scr, lng_ref, lnb_ref, mix_scr, T, dc)

    def pool(i, c):
        rs = _rows(i, RC)
        for t in range(TS):
            for g, w in enumerate(POOL_WINDOWS):
                ls = slice(g * gd, (g + 1) * gd)
                tot = ext_pool(NP + t, rs, ls)
                cur = tot
                for back in range(1, w):
                    tot = tot + ext_pool(NP + t - back, rs, ls)
                sd_scr[pl.ds(t * BS + rs.start, RC), ls] = (tot / jnp.float32(w) - cur).astype(BF16)
        return c

    lax.fori_loop(0, BS // RC, pool, 0)

    _pool_project(sd_scr, wpool_ref, pscale_ref, mix_scr, dc, gd)

    for j in range(NH):
        src = j + TS
        nconv_ref[j] = sconv_ref[src] if src < NH else a_scr[(src - NH) * BS:(src - NH + 1) * BS, :]
    for j in range(NP):
        src = j + TS
        npool_ref[j] = spool_ref[src] if src < NP else u_scr[(src - NP) * BS:(src - NP + 1) * BS, :]

    _out_proj_residual(xt_scr, mix_scr, wout_ref, x2_ref)
    _router(x2_ref, gffn_ref, wrh_ref, wrl_ref, br_ref, hf_ref, ids_ref, w0_ref, w1_ref,
            hhi_scr, hlo_scr, T, d)


def _const_spec(shape):
    nd = len(shape)
    return pl.BlockSpec(shape, lambda *a: (0,) * nd, pipeline_mode=pl.Buffered(1))


def _mixer_weight_specs(d, dc, dp, cols):
    gd = dp // len(POOL_WINDOWS)
    return [
        _const_spec((1, d)),
        _const_spec((d, cols)),
        _const_spec((CONV_W, dc)),
        _const_spec((1, dc)),
        _const_spec((1, dc)),
        _const_spec((1, dc)),
        _const_spec((len(POOL_WINDOWS), gd, gd)),
        _const_spec((1, dp)),
        _const_spec((dc + dp, d)),
        _const_spec((1, d)),
        _const_spec((d, LANES)),
        _const_spec((d, LANES)),
        _const_spec((1, LANES)),
    ]


def _mixer_prompt(x, wts, *, T, n_total):
    B, S, d = x.shape
    dc = wts[2].shape[1]
    dp = wts[7].shape[1]
    cols = wts[1].shape[1]
    n_s = S // T
    N = n_total
    n_blk = N // T
    assert N % T == 0 and N - B * S <= S
    tok = lambda b, s: (jnp.minimum(b * n_s + s, n_blk - 1), 0)
    tok4 = lambda b, s: (jnp.minimum(b * n_s + s, n_blk - 1), 0, 0, 0)
    bclamp = lambda b: jnp.minimum(b, B - 1)
    out_shape = (
        jax.ShapeDtypeStruct((N, d), F32),
        jax.ShapeDtypeStruct((N // TOK_CHUNK, TOK_CHUNK, d // LANES, LANES), BF16),
        jax.ShapeDtypeStruct((N, LANES), jnp.int32),
        jax.ShapeDtypeStruct((N, LANES), F32),
        jax.ShapeDtypeStruct((N, LANES), F32),
        jax.ShapeDtypeStruct((1, B, CONV_W - 1, dc), F32),
        jax.ShapeDtypeStruct((1, B, POOL_MAX_W - 1, dp), F32),
    )
    out_specs = (
        pl.BlockSpec((T, d), tok),
        pl.BlockSpec((T // TOK_CHUNK, TOK_CHUNK, d // LANES, LANES), tok4),
        pl.BlockSpec((T, LANES), tok),
        pl.BlockSpec((T, LANES), tok),
        pl.BlockSpec((T, LANES), tok),
        pl.BlockSpec((None, None, CONV_W - 1, dc), lambda b, s: (0, bclamp(b), 0, 0)),
        pl.BlockSpec((None, None, POOL_MAX_W - 1, dp), lambda b, s: (0, bclamp(b), 0, 0)),
    )
    scratch = [
        pltpu.VMEM((T, d), BF16),
        pltpu.VMEM((T, dc), F32),
        pltpu.VMEM((T + TAIL, dc), F32),
        pltpu.VMEM((T + PTAIL, dp), F32),
        pltpu.VMEM((T, dc), F32),
        pltpu.VMEM((T, dp), BF16),
        pltpu.VMEM((T, dc + dp), BF16),
        pltpu.VMEM((T, d), BF16),
        pltpu.VMEM((T, d), BF16),
    ]
    return pl.pallas_call(
        functools.partial(_mixer_prompt_kernel, T=T, d=d, dc=dc, dp=dp, n_batch=B),
        grid=(B + 1, n_s),
        in_specs=[pl.BlockSpec((None, T, d), lambda b, s: (bclamp(b), s, 0))]
                 + _mixer_weight_specs(d, dc, dp, cols),
        out_specs=out_specs,
        out_shape=out_shape,
        scratch_shapes=scratch,
        compiler_params=pltpu.CompilerParams(
            dimension_semantics=("arbitrary", "arbitrary"), vmem_limit_bytes=VMEM_LIMIT),
        name="mixer_prompt",
    )(x, *wts)


def _mixer_sample(x_t, sconv_t, spool_t, wts, tok_arrays, *, BS, tok0):
    TS, Bd, d = x_t.shape
    dc = wts[2].shape[1]
    dp = wts[7].shape[1]
    cols = wts[1].shape[1]
    T = TS * BS
    n_b = Bd // BS
    b0 = tok0 // T
    tok = lambda i: (b0 + i, 0)
    tok4 = lambda i: (b0 + i, 0, 0, 0)
    out_shape = tuple(jax.ShapeDtypeStruct(a.shape, a.dtype) for a in tok_arrays) + (
        jax.ShapeDtypeStruct((CONV_W - 1, Bd, dc), F32),
        jax.ShapeDtypeStruct((POOL_MAX_W - 1, Bd, dp), F32),
    )
    out_specs = (
        pl.BlockSpec((T, d), tok),
        pl.BlockSpec((T // TOK_CHUNK, TOK_CHUNK, d // LANES, LANES), tok4),
        pl.BlockSpec((T, LANES), tok),
        pl.BlockSpec((T, LANES), tok),
        pl.BlockSpec((T, LANES), tok),
        pl.BlockSpec((CONV_W - 1, BS, dc), lambda i: (0, i, 0)),
        pl.BlockSpec((POOL_MAX_W - 1, BS, dp), lambda i: (0, i, 0)),
    )
    scratch = [
        pltpu.VMEM((T, d), F32),
        pltpu.VMEM((T, d), BF16),
        pltpu.VMEM((T, dc), F32),
        pltpu.VMEM((T, dc), F32),
        pltpu.VMEM((T, dp), F32),
        pltpu.VMEM((T, dc), F32),
        pltpu.VMEM((T, dp), BF16),
        pltpu.VMEM((T, dc + dp), BF16),
        pltpu.VMEM((T, d), BF16),
        pltpu.VMEM((T, d), BF16),
    ]
    return pl.pallas_call(
        functools.partial(_mixer_sample_kernel, TS=TS, BS=BS, d=d, dc=dc, dp=dp),
        grid=(n_b,),
        in_specs=[pl.BlockSpec((TS, BS, d), lambda i: (0, i, 0)),
                  pl.BlockSpec((CONV_W - 1, BS, dc), lambda i: (0, i, 0), pipeline_mode=pl.Buffered(1)),
                  pl.BlockSpec((POOL_MAX_W - 1, BS, dp), lambda i: (0, i, 0), pipeline_mode=pl.Buffered(1))]
                 + _mixer_weight_specs(d, dc, dp, cols)
                 + [pl.BlockSpec(memory_space=pl.ANY)] * len(tok_arrays),
        out_specs=out_specs,
        out_shape=out_shape,
        scratch_shapes=scratch,
        input_output_aliases={3 + len(wts) + j: j for j in range(len(tok_arrays))},
        compiler_params=pltpu.CompilerParams(
            dimension_semantics=("arbitrary",), vmem_limit_bytes=VMEM_LIMIT),
        name="mixer_sample",
    )(x_t, sconv_t, spool_t, *wts, *tok_arrays)


def _dispatch_kernel(pos_ref, hf_ref, xs_in_ref, xs_ref, sem, *, R):
    del xs_in_ref
    base = pl.program_id(0) * R

    def body(r, c):
        for k in range(2):
            dst = pos_ref[2 * (base + r) + k]
            pltpu.make_async_copy(hf_ref.at[pl.ds(r, 1)], xs_ref.at[pl.ds(dst, 1)], sem.at[0]).start(priority=k)
        return c

    lax.fori_loop(0, R, body, 0, unroll=8)
    for k in range(2):
        pltpu.make_async_copy(hf_ref, xs_ref.at[pl.ds(0, R)], sem.at[0]).wait()


def _dispatch(pos, hf, xs_zero, *, R):
    n_tok = hf.shape[0]
    assert n_tok % R == 0
    return pl.pallas_call(
        functools.partial(_dispatch_kernel, R=R),
        grid_spec=pltpu.PrefetchScalarGridSpec(
            num_scalar_prefetch=1,
            grid=(n_tok // R,),
            in_specs=[pl.BlockSpec((R,) + hf.shape[1:], lambda s, pos: (s, 0, 0)),
                      pl.BlockSpec(memory_space=pl.ANY)],
            out_specs=pl.BlockSpec(memory_space=pl.ANY),
            scratch_shapes=[pltpu.SemaphoreType.DMA((1,))],
        ),
        out_shape=jax.ShapeDtypeStruct(xs_zero.shape, xs_zero.dtype),
        input_output_aliases={2: 0},
        compiler_params=pltpu.CompilerParams(dimension_semantics=("arbitrary",)),
        name="dispatch",
    )(pos, hf, xs_zero)


def _invert_kernel(pos_ref, dst_in, dst_ref, sem, *, n_pairs, n_tok):
    init = pltpu.make_async_copy(dst_in, dst_ref, sem.at[0])
    init.start()
    init.wait()
    G = 8

    def put(b, c):
        n0 = b * G
        ps = [pos_ref[2 * n0 + q] for q in range(2 * G)]
        for q in range(2 * G):
            n, k = n0 + q // 2, q % 2
            dst_ref[ps[q]] = k * n_tok + n
        return c

    assert n_pairs % (2 * G) == 0
    lax.fori_loop(0, n_pairs // (2 * G), put, 0)


def _invert(pos, *, n_tiles, tm, n_tok):
    n_pairs = pos.shape[0]
    p = jnp.arange((n_tiles + 1) * tm, dtype=jnp.int32)
    tile = p // tm
    buf = jnp.where(tile == n_tiles, 1, tile % 2)
    dst_init = 2 * n_tok + buf * tm + p % tm
    smem = pl.BlockSpec(memory_space=pltpu.SMEM)
    return pl.pallas_call(
        functools.partial(_invert_kernel, n_pairs=n_pairs, n_tok=n_tok),
        in_specs=[smem, pl.BlockSpec(memory_space=pl.ANY)],
        out_specs=smem,
        out_shape=jax.ShapeDtypeStruct(dst_init.shape, jnp.int32),
        scratch_shapes=[pltpu.SemaphoreType.DMA((1,))],
        name="invert",
    )(pos, dst_init)


def _expert_mlp_kernel(te_ref, na_ref, dst_prev_ref, dst_cur_ref, xs_ref, wg_ref, wu_ref, wd_ref, o_ref,
                       ybuf0, ybuf1, wg_scr, wu_scr, wd_scr, ssem, *, tm, n_tiles, d):
    i = pl.program_id(0)
    n_act = na_ref[0]
    active = i < n_act
    slot = i % 2
    last = n_tiles - 1
    prev = te_ref[jnp.maximum(i - 1, 0)]
    new_expert = (i == 0) | (te_ref[i] != prev)
    ybuf = (ybuf0, ybuf1)

    def scatter_start(dst_ref, b):
        for r in range(tm):
            pltpu.make_async_copy(ybuf[b].at[pl.ds(r, 1)], o_ref.at[pl.ds(dst_ref[0, 0, r], 1)],
                                  ssem.at[b]).start(priority=r % 2)

    def scatter_wait(b):
        pltpu.make_async_copy(ybuf[b], o_ref.at[pl.ds(0, tm)], ssem.at[b]).wait()

    @pl.when(i == 0)
    def _():
        ybuf1[...] = jnp.zeros(ybuf1.shape, BF16)
        trash0 = pltpu.make_async_copy(ybuf1, o_ref.at[pl.ds(o_ref.shape[0] - 2 * tm, tm)], ssem.at[0])
        trash0.start()
        trash0.wait()

    @pl.when(active & new_expert)
    def _():
        wg_scr[...] = wg_ref[...].astype(BF16)
        wu_scr[...] = wu_ref[...].astype(BF16)
        wd_scr[...] = wd_ref[...].astype(BF16)

    for par in range(2):
        is_par = slot == par

        @pl.when(is_par & (i >= 1) & (i - 2 < n_act))
        def _():
            scatter_wait(par)

        @pl.when(is_par & active)
        def _():
            x = xs_ref[...].reshape(tm, d)
            g = jnp.dot(x, wg_scr[...], preferred_element_type=F32)
            u = jnp.dot(x, wu_scr[...], preferred_element_type=F32)
            act = (g * _sigmoid(g) * u).astype(BF16)
            y = jnp.dot(act, wd_scr[...], preferred_element_type=F32)
            ybuf[par][...] = y.astype(BF16).reshape(tm, d // LANES, LANES)
            scatter_start(dst_prev_ref, 1 - par)

        @pl.when(is_par & (i == n_act))
        def _():
            scatter_start(dst_prev_ref, 1 - par)

        @pl.when(is_par & (i == last))
        def _():
            @pl.when(last - 1 < n_act)
            def _():
                scatter_wait(1 - par)

            @pl.when(last < n_act)
            def _():
                scatter_start(dst_cur_ref, par)
                scatter_wait(par)


def _expert_mlp(tile_expert, n_active, dst, xs, w_gate, w_up, w_down, *, tm, n_tok):
    E, d, de = w_gate.shape
    n_tiles = xs.shape[0] // tm
    row = (tm, d // LANES, LANES)
    cur = lambda i, te, na: (i, 0, 0)
    prv = lambda i, te, na: (jnp.where(i == 0, n_tiles, i - 1), 0, 0)
    smem_tile = lambda imap: pl.BlockSpec((1, 1, tm), imap, memory_space=pltpu.SMEM)
    return pl.pallas_call(
        functools.partial(_expert_mlp_kernel, tm=tm, n_tiles=n_tiles, d=d),
        grid_spec=pltpu.PrefetchScalarGridSpec(
            num_scalar_prefetch=2,
            grid=(n_tiles,),
            in_specs=[
                smem_tile(prv), smem_tile(cur),
                pl.BlockSpec(row, lambda i, te, na: (i, 0, 0)),
                pl.BlockSpec((None, d, de), lambda i, te, na: (te[i], 0, 0)),
                pl.BlockSpec((None, d, de), lambda i, te, na: (te[i], 0, 0)),
                pl.BlockSpec((None, de, d), lambda i, te, na: (te[i], 0, 0)),
            ],
            out_specs=pl.BlockSpec(memory_space=pl.ANY),
            scratch_shapes=[pltpu.VMEM(row, BF16), pltpu.VMEM(row, BF16),
                            pltpu.VMEM((d, de), BF16), pltpu.VMEM((d, de), BF16), pltpu.VMEM((de, d), BF16),
                            pltpu.SemaphoreType.DMA((2,))],
        ),
        out_shape=jax.ShapeDtypeStruct((2 * n_tok + 2 * tm, d // LANES, LANES), BF16),
        compiler_params=pltpu.CompilerParams(
            dimension_semantics=("arbitrary",), vmem_limit_bytes=VMEM_LIMIT),
        name="expert_mlp",
    )(tile_expert, n_active, dst, dst, xs, w_gate, w_up, w_down)


def _combine_kernel(x2_ref, o0_ref, o1_ref, w0_ref, w1_ref, gfin_ref, y_ref, *, R, d):
    C = TOK_CHUNK

    def body(i, c):
        rs = _rows(i, C)
        w0 = w0_ref[rs, :]
        w1 = w1_ref[rs, :]
        o0 = o0_ref[i].reshape(C, d).astype(F32)
        o1 = o1_ref[i].reshape(C, d).astype(F32)
        parts = []
        ssq = jnp.zeros((C, LANES), F32)
        for j in range(d // LANES):
            ls = slice(j * LANES, (j + 1) * LANES)
            m = w0 * o0[:, ls] + w1 * o1[:, ls]
            v = x2_ref[rs, ls] + m
            ssq = ssq + v * v
            parts.append(v)
        ms = jnp.sum(ssq, axis=-1, keepdims=True) * jnp.float32(1.0 / d)
        inv = lax.rsqrt(ms + EPS)
        for j in range(d // LANES):
            ls = slice(j * LANES, (j + 1) * LANES)
            y_ref[rs, ls] = parts[j] * inv * gfin_ref[:, ls]
        return c

    lax.fori_loop(0, R // C, body, 0)


def _combine(x2, o, w0, w1, g_final, *, R, tok0, n_out):
    n_tok, d = x2.shape
    n_steps = n_out // R
    b0 = tok0 // R
    b1 = n_tok // R
    tok = lambda s: (b0 + s, 0)
    orow = (R // TOK_CHUNK, TOK_CHUNK, d // LANES, LANES)
    o4 = o.reshape(o.shape[0] // TOK_CHUNK, TOK_CHUNK, d // LANES, LANES)
    return pl.pallas_call(
        functools.partial(_combine_kernel, R=R, d=d),
        grid=(n_steps,),
        in_specs=[
            pl.BlockSpec((R, d), tok),
            pl.BlockSpec(orow, lambda s: (b0 + s, 0, 0, 0)),
            pl.BlockSpec(orow, lambda s: (b1 + b0 + s, 0, 0, 0)),
            pl.BlockSpec((R, LANES), tok),
            pl.BlockSpec((R, LANES), tok),
            pl.BlockSpec((1, d), lambda s: (0, 0)),
        ],
        out_specs=pl.BlockSpec((R, d), lambda s: (s, 0)),
        out_shape=jax.ShapeDtypeStruct((n_out, d), F32),
        compiler_params=pltpu.CompilerParams(
            dimension_semantics=("arbitrary",), vmem_limit_bytes=VMEM_LIMIT),
        name="combine",
    )(x2, o4, o4, w0, w1, g_final)


def _routing_plan(ids, tm, n_tiles):
    e0 = ids[:, 0]
    e1 = ids[:, 1]
    ar = jnp.arange(N_EXPERTS, dtype=jnp.int32)
    m = ((e0[:, None] == ar) | (e1[:, None] == ar)).astype(jnp.int32)
    incl = jnp.cumsum(m, axis=0)
    excl = incl - m
    counts = incl[-1]
    tiles_e = (counts + tm - 1) // tm
    tile_end = jnp.cumsum(tiles_e)
    offs = (tile_end - tiles_e) * tm
    base = offs[None, :] + excl
    pos0 = jnp.take_along_axis(base, e0[:, None], axis=1)[:, 0]
    pos1 = jnp.take_along_axis(base, e1[:, None], axis=1)[:, 0]
    pos = jnp.stack([pos0, pos1], axis=1).reshape(-1).astype(jnp.int32)
    n_active = tile_end[-1].astype(jnp.int32)
    t = jnp.arange(n_tiles, dtype=jnp.int32)
    tq = jnp.minimum(t, n_active - 1)
    te = jnp.sum((tile_end[None, :] <= tq[:, None]).astype(jnp.int32), axis=1)
    te = jnp.minimum(te, N_EXPERTS - 1)
    return pos, te, n_active.reshape(1)


T_PROMPT = 256
BS_SAMPLE = 32
TM_EXPERT = 256
R_DISPATCH_CHOICES = (512, 256, 128)
R_COMBINE = 128


def kernel(x_prompt, x_sample, state_conv, state_pool, g_mix, w_in, w_dw, b_dw, ln_g, ln_b, w_pool, pool_scale, w_out, g_ffn, w_rg, b_rg, w_re, b_re, w_gate, w_up, w_down, g_final):
    depth = g_mix.shape[0]
    assert depth == 1
    B, S, d = x_prompt.shape
    Bd, TS, _ = x_sample.shape
    n_p = B * S
    n_s = Bd * TS
    N = n_p + n_s

    w_r = jnp.zeros((d, LANES), F32)
    w_r = w_r.at[:, 0:N_EXPERTS].set(w_re[0]).at[:, ROUTER_GROUP_LANE0:ROUTER_GROUP_LANE0 + N_GROUPS].set(w_rg[0])
    w_r_hi = w_r.astype(BF16)
    w_r_lo = (w_r - w_r_hi.astype(F32)).astype(BF16)
    b_r = jnp.zeros((1, LANES), F32)
    b_r = b_r.at[0, 0:N_EXPERTS].set(b_re[0]).at[0, ROUTER_GROUP_LANE0:ROUTER_GROUP_LANE0 + N_GROUPS].set(b_rg[0])
    wts = (g_mix[0][None], w_in[0].astype(BF16), w_dw[0], b_dw[0][None], ln_g[0][None], ln_b[0][None],
           w_pool[0].astype(BF16), pool_scale[0][None], w_out[0].astype(BF16), g_ffn[0][None],
           w_r_hi, w_r_lo, b_r)

    *tok_arrays, nconv_p, npool_p = _mixer_prompt(x_prompt, wts, T=T_PROMPT, n_total=N)

    x_t = jnp.transpose(x_sample, (1, 0, 2))
    sconv_t = jnp.transpose(state_conv[0], (1, 0, 2))
    spool_t = jnp.transpose(state_pool[0], (1, 0, 2))
    x2, hf, ids, w0, w1, nconv_t, npool_t = _mixer_sample(x_t, sconv_t, spool_t, wts, tok_arrays,
                                                          BS=BS_SAMPLE, tok0=n_p)

    tm = TM_EXPERT
    n_tiles = (2 * N + N_EXPERTS * (tm - 1) + tm - 1) // tm
    pos, tile_expert, n_active = _routing_plan(ids[:, 0:2], tm, n_tiles)
    dst = _invert(pos, n_tiles=n_tiles, tm=tm, n_tok=N).reshape(n_tiles + 1, 1, tm)

    hf = hf.reshape(N, d // LANES, LANES)
    r_disp = max(r for r in R_DISPATCH_CHOICES if N % r == 0)
    xs = _dispatch(pos, hf, jnp.zeros((n_tiles * tm, d // LANES, LANES), BF16), R=r_disp)
    o = _expert_mlp(tile_expert, n_active, dst, xs, w_gate[0], w_up[0], w_down[0], tm=tm, n_tok=N)

    gfin = g_final[None]
    y_p = _combine(x2, o, w0, w1, gfin, R=R_COMBINE, tok0=0, n_out=n_p)
    y_s = _combine(x2, o, w0, w1, gfin, R=R_COMBINE, tok0=n_p, n_out=n_s)

    y_prompt = y_p.reshape(B, S, d)
    y_sample = y_s.reshape(Bd // BS_SAMPLE, TS, BS_SAMPLE, d).transpose(0, 2, 1, 3).reshape(Bd, TS, d)
    new_conv_s = jnp.transpose(nconv_t, (1, 0, 2))[None]
    new_pool_s = jnp.transpose(npool_t, (1, 0, 2))[None]
    return (y_prompt, y_sample, nconv_p, new_conv_s, npool_p, new_pool_s)
```

```python
import functools

import jax
import jax.numpy as jnp
from jax import lax
from jax.experimental import pallas as pl
from jax.experimental.pallas import tpu as pltpu

F32 = jnp.float32
BF16 = jnp.bfloat16
EPS = 1e-6

LANES = 128
SUBLANES = 8
VMEM_LIMIT = 56 * 1024 * 1024

CONV_W = 31
POOL_WINDOWS = (2, 4, 8, 16)
POOL_MAX_W = 16
N_GROUPS = 4
PER_GROUP = 8
N_EXPERTS = N_GROUPS * PER_GROUP

ROUTER_GROUP_LANE0 = N_EXPERTS
TOK_CHUNK = 16
ROW_LOOP_UNROLL = 4
TAIL = 32
PTAIL = 16


def _rows(i, r):
    return pl.ds(pl.multiple_of(i * r, r), r)


def _sigmoid(x):
    return 1.0 / (1.0 + jnp.exp(-x))


def _rmsnorm_to_bf16(src_ref, g_ref, dst_ref, T):
    R = 16

    def body(i, c):
        rs = _rows(i, R)
        x = src_ref[rs, :]
        ms = jnp.mean(x * x, axis=-1, keepdims=True)
        dst_ref[rs, :] = (x * lax.rsqrt(ms + EPS) * g_ref[...]).astype(BF16)
        return c

    lax.fori_loop(0, T // R, body, 0, unroll=ROW_LOOP_UNROLL)


def _layernorm_swish(y_ref, lng_ref, lnb_ref, mix_ref, T, dc):
    R = 16

    def body(i, c):
        rs = _rows(i, R)
        y = y_ref[rs, :]
        mu = jnp.mean(y, axis=-1, keepdims=True)
        d = y - mu
        var = jnp.mean(d * d, axis=-1, keepdims=True)
        z = d * lax.rsqrt(var + EPS) * lng_ref[...] + lnb_ref[...]
        mix_ref[rs, 0:dc] = (z * _sigmoid(z)).astype(BF16)
        return c

    lax.fori_loop(0, T // R, body, 0, unroll=ROW_LOOP_UNROLL)


def _pool_project(sd_ref, wpool_ref, pscale_ref, mix_ref, dc, gd):
    for g in range(len(POOL_WINDOWS)):
        sl = slice(g * gd, (g + 1) * gd)
        o = jnp.dot(sd_ref[:, sl], wpool_ref[g], preferred_element_type=F32)
        mix_ref[:, dc + g * gd: dc + (g + 1) * gd] = (o * pscale_ref[:, sl]).astype(BF16)


def _out_proj_residual(x_ref, mix_ref, wout_ref, x2_ref):
    x2_ref[...] = x_ref[...] + jnp.dot(mix_ref[...], wout_ref[...], preferred_element_type=F32)


def _router(x2_ref, gffn_ref, wrh_ref, wrl_ref, br_ref, hf_ref, ids_ref, w0_ref, w1_ref,
            hhi_scr, hlo_scr, T, d):
    R = TOK_CHUNK

    def body(i, c):
        rs = _rows(i, R)
        x = x2_ref[rs, :]
        ms = jnp.mean(x * x, axis=-1, keepdims=True)
        h = x * lax.rsqrt(ms + EPS) * gffn_ref[...]
        hi = h.astype(BF16)
        hf_ref[i] = hi.reshape(R, d // LANES, LANES)
        hhi_scr[rs, :] = hi
        hlo_scr[rs, :] = (h - hi.astype(F32)).astype(BF16)
        return c

    lax.fori_loop(0, T // R, body, 0, unroll=ROW_LOOP_UNROLL)

    lg = (jnp.dot(hhi_scr[...], wrh_ref[...], preferred_element_type=F32)
          + jnp.dot(hlo_scr[...], wrh_ref[...], preferred_element_type=F32)
          + jnp.dot(hhi_scr[...], wrl_ref[...], preferred_element_type=F32)
          + br_ref[...])

    lane = lax.broadcasted_iota(jnp.int32, lg.shape, 1).astype(F32)
    neg = jnp.float32(-jnp.inf)
    big = jnp.float32(1e9)
    g_lo = jnp.float32(ROUTER_GROUP_LANE0)
    gmask = (lane >= g_lo) & (lane < g_lo + N_GROUPS)
    lgg = jnp.where(gmask, lg, neg)
    gmax = jnp.max(lgg, axis=-1, keepdims=True)
    gsel = jnp.min(jnp.where(lgg == gmax, lane, big), axis=-1, keepdims=True) - g_lo
    gsum = jnp.sum(jnp.where(gmask, jnp.exp(lg - gmax), 0.0), axis=-1, keepdims=True)
    p_g = 1.0 / gsum

    e_lo = gsel * PER_GROUP
    emask = (lane >= e_lo) & (lane < e_lo + PER_GROUP)
    le = jnp.where(emask, lg, neg)
    v0 = jnp.max(le, axis=-1, keepdims=True)
    i0 = jnp.min(jnp.where(le == v0, lane, big), axis=-1, keepdims=True)
    le2 = jnp.where(lane == i0, neg, le)
    v1 = jnp.max(le2, axis=-1, keepdims=True)
    i1 = jnp.min(jnp.where(le2 == v1, lane, big), axis=-1, keepdims=True)
    ex = jnp.exp(v1 - v0)
    den = 1.0 / (1.0 + ex)
    w0 = den * p_g
    w1 = ex * den * p_g

    ids_ref[...] = jnp.where(lane == 0.0, i0, jnp.where(lane == 1.0, i1, 0.0)).astype(jnp.int32)
    w0_ref[...] = jnp.broadcast_to(w0, lg.shape)
    w1_ref[...] = jnp.broadcast_to(w1, lg.shape)


def _mixer_prompt_kernel(x_ref, gmix_ref, win_ref, wdw_ref, bdw_ref, lng_ref, lnb_ref, wpool_ref,
                         pscale_ref, wout_ref, gffn_ref, wrh_ref, wrl_ref, br_ref,
                         x2_ref, hf_ref, ids_ref, w0_ref, w1_ref, nconv_ref, npool_ref,
                         h_scr, pg_scr, ext_scr, extu_scr, yc_scr, sd_scr, mix_scr, hhi_scr, hlo_scr,
                         *, T, d, dc, dp, n_batch):
    @pl.when(pl.program_id(0) == n_batch)
    def _():
        x2_ref[...] = jnp.zeros(x2_ref.shape, F32)
        hf_ref[...] = jnp.zeros(hf_ref.shape, BF16)
        ids_ref[...] = jnp.zeros(ids_ref.shape, jnp.int32)
        w0_ref[...] = jnp.zeros(w0_ref.shape, F32)
        w1_ref[...] = jnp.zeros(w1_ref.shape, F32)

    @pl.when(pl.program_id(0) < n_batch)
    def _():
        _mixer_prompt_tile(x_ref, gmix_ref, win_ref, wdw_ref, bdw_ref, lng_ref, lnb_ref, wpool_ref,
                           pscale_ref, wout_ref, gffn_ref, wrh_ref, wrl_ref, br_ref,
                           x2_ref, hf_ref, ids_ref, w0_ref, w1_ref, nconv_ref, npool_ref,
                           h_scr, pg_scr, ext_scr, extu_scr, yc_scr, sd_scr, mix_scr, hhi_scr, hlo_scr,
                           T=T, d=d, dc=dc, dp=dp)


def _mixer_prompt_tile(x_ref, gmix_ref, win_ref, wdw_ref, bdw_ref, lng_ref, lnb_ref, wpool_ref,
                       pscale_ref, wout_ref, gffn_ref, wrh_ref, wrl_ref, br_ref,
                       x2_ref, hf_ref, ids_ref, w0_ref, w1_ref, nconv_ref, npool_ref,
                       h_scr, pg_scr, ext_scr, extu_scr, yc_scr, sd_scr, mix_scr, hhi_scr, hlo_scr,
                       *, T, d, dc, dp):
    s = pl.program_id(1)
    n_s = pl.num_programs(1)
    gd = dp // len(POOL_WINDOWS)

    @pl.when(s == 0)
    def _():
        ext_scr[0:TAIL, :] = jnp.zeros((TAIL, dc), F32)
        extu_scr[0:PTAIL, :] = jnp.zeros((PTAIL, dp), F32)

    _rmsnorm_to_bf16(x_ref, gmix_ref, h_scr, T)

    ext_scr[TAIL:TAIL + T, :] = jnp.dot(h_scr[...], win_ref[:, 0:dc], preferred_element_type=F32)
    pg_scr[...] = jnp.dot(h_scr[...], win_ref[:, dc:2 * dc], preferred_element_type=F32)
    extu_scr[PTAIL:PTAIL + T, :] = jnp.dot(h_scr[...], win_ref[:, 2 * dc:2 * dc + dp],
                                           preferred_element_type=F32)

    R = 16

    def glu(i, c):
        rs = _rows(i, R)
        es = pl.ds(pl.multiple_of(i * R, R) + TAIL, R)
        ext_scr[es, :] = ext_scr[es, :] * _sigmoid(pg_scr[rs, :])
        return c

    lax.fori_loop(0, T // R, glu, 0, unroll=ROW_LOOP_UNROLL)

    RC, LC = 32, 256
    BR = RC + TAIL
    shift0 = TAIL - (CONV_W - 1)

    def conv(i, c):
        r0 = pl.multiple_of(i * RC, RC)
        for lc in range(dc // LC):
            ls = slice(lc * LC, (lc + 1) * LC)
            blk = ext_scr[pl.ds(r0, BR), ls]
            acc = jnp.zeros((RC, LC), F32)
            for sft in range(SUBLANES):
                taps = [k for k in range(CONV_W) if (k + shift0) % SUBLANES == sft]
                if not taps:
                    continue
                rolled = blk if sft == 0 else pltpu.roll(blk, BR - sft, 0)
                for k in taps:
                    q = (k + shift0) // SUBLANES
                    acc = acc + wdw_ref[k:k + 1, ls] * rolled[q * SUBLANES:q * SUBLANES + RC, :]
            yc_scr[pl.ds(r0, RC), ls] = acc + bdw_ref[:, ls]
        return c

    lax.fori_loop(0, T // RC, conv, 0)

    _layernorm_swish(yc_scr, lng_ref, lnb_ref, mix_scr, T, dc)

    RP = 32
    BP = RP + PTAIL
    pos_base = s * T

    def pool(i, c):
        r0 = pl.multiple_of(i * RP, RP)
        pos = (pos_base + r0 + lax.broadcasted_iota(jnp.int32, (RP, gd), 0)).astype(F32)
        for g, w in enumerate(POOL_WINDOWS):
            ls = slice(g * gd, (g + 1) * gd)
            blk = extu_scr[pl.ds(r0, BP), ls]
            run = blk
            span = 1
            while span < w:
                run = run + pltpu.roll(run, span, 0)
                span *= 2
            cnt = jnp.minimum(pos + 1.0, jnp.float32(w))
            mean = run[PTAIL:PTAIL + RP, :] / cnt
            sd_scr[pl.ds(r0, RP), ls] = (mean - blk[PTAIL:PTAIL + RP, :]).astype(BF16)
        return c

    lax.fori_loop(0, T // RP, pool, 0)

    _pool_project(sd_scr, wpool_ref, pscale_ref, mix_scr, dc, gd)

    @pl.when(s == n_s - 1)
    def _():
        nconv_ref[...] = ext_scr[TAIL + T - (CONV_W - 1):TAIL + T, :]
        npool_ref[...] = extu_scr[PTAIL + T - (POOL_MAX_W - 1):PTAIL + T, :]

    ext_scr[0:TAIL, :] = ext_scr[T:T + TAIL, :]
    extu_scr[0:PTAIL, :] = extu_scr[T:T + PTAIL, :]

    _out_proj_residual(x_ref, mix_scr, wout_ref, x2_ref)
    _router(x2_ref, gffn_ref, wrh_ref, wrl_ref, br_ref, hf_ref, ids_ref, w0_ref, w1_ref,
            hhi_scr, hlo_scr, T, d)


def _mixer_sample_kernel(x_ref, sconv_ref, spool_ref, gmix_ref, win_ref, wdw_ref, bdw_ref, lng_ref,
                         lnb_ref, wpool_ref, pscale_ref, wout_ref, gffn_ref, wrh_ref, wrl_ref, br_ref,
                         x2_in, hf_in, ids_in, w0_in, w1_in,
                         x2_ref, hf_ref, ids_ref, w0_ref, w1_ref, nconv_ref, npool_ref,
                         xt_scr, h_scr, a_scr, pg_scr, u_scr, yc_scr, sd_scr, mix_scr, hhi_scr, hlo_scr,
                         *, TS, BS, d, dc, dp):
    del x2_in, hf_in, ids_in, w0_in, w1_in
    T = TS * BS
    gd = dp // len(POOL_WINDOWS)
    NH = CONV_W - 1
    NP = POOL_MAX_W - 1

    for t in range(TS):
        xt_scr[t * BS:(t + 1) * BS, :] = x_ref[t]

    _rmsnorm_to_bf16(xt_scr, gmix_ref, h_scr, T)
    a_scr[...] = jnp.dot(h_scr[...], win_ref[:, 0:dc], preferred_element_type=F32)
    pg_scr[...] = jnp.dot(h_scr[...], win_ref[:, dc:2 * dc], preferred_element_type=F32)
    u_scr[...] = jnp.dot(h_scr[...], win_ref[:, 2 * dc:2 * dc + dp], preferred_element_type=F32)

    R = 16

    def glu(i, c):
        rs = _rows(i, R)
        a_scr[rs, :] = a_scr[rs, :] * _sigmoid(pg_scr[rs, :])
        return c

    lax.fori_loop(0, T // R, glu, 0, unroll=ROW_LOOP_UNROLL)

    def ext_conv(j, rs, ls):
        if j < NH:
            return sconv_ref[j, rs, ls]
        return a_scr[pl.ds((j - NH) * BS + rs.start, rs.size), ls]

    def ext_pool(j, rs, ls):
        if j < NP:
            return spool_ref[j, rs, ls]
        return u_scr[pl.ds((j - NP) * BS + rs.start, rs.size), ls]

    RC, LC = 32, 256

    def conv(i, c):
        rs = _rows(i, RC)
        for t in range(TS):
            for lc in range(dc // LC):
                ls = slice(lc * LC, (lc + 1) * LC)
                acc = jnp.zeros((RC, LC), F32)
                for k in range(CONV_W):
                    acc = acc + wdw_ref[k:k + 1, ls] * ext_conv(t + k, rs, ls)
                yc_scr[pl.ds(t * BS + rs.start, RC), ls] = acc + bdw_ref[:, ls]
        return c

    lax.fori_loop(0, BS // RC, conv, 0)

    _layernorm_swish(yc_scr, lng_ref, lnb_ref, mix_scr, T, dc)

    def pool(i, c):
        rs = _rows(i, RC)
        for t in range(TS):
            for g, w in enumerate(POOL_WINDOWS):
                ls = slice(g * gd, (g + 1) * gd)
                tot = ext_pool(NP + t, rs, ls)
                cur = tot
                for back in range(1, w):
                    tot = tot + ext_pool(NP + t - back, rs, ls)
                sd_scr[pl.ds(t * BS + rs.start, RC), ls] = (tot / jnp.float32(w) - cur).astype(BF16)
        return c

    lax.fori_loop(0, BS // RC, pool, 0)

    _pool_project(sd_scr, wpool_ref, pscale_ref, mix_scr, dc, gd)

    for j in range(NH):
        src = j + TS
        nconv_ref[j] = sconv_ref[src] if src < NH else a_scr[(src - NH) * BS:(src - NH + 1) * BS, :]
    for j in range(NP):
        src = j + TS
        npool_ref[j] = spool_ref[src] if src < NP else u_scr[(src - NP) * BS:(src - NP + 1) * BS, :]

    _out_proj_residual(xt_scr, mix_scr, wout_ref, x2_ref)
    _router(x2_ref, gffn_ref, wrh_ref, wrl_ref, br_ref, hf_ref, ids_ref, w0_ref, w1_ref,
            hhi_scr, hlo_scr, T, d)


def _const_spec(shape):
    nd = len(shape)
    return pl.BlockSpec(shape, lambda *a: (0,) * nd, pipeline_mode=pl.Buffered(1))


def _mixer_weight_specs(d, dc, dp, cols):
    gd = dp // len(POOL_WINDOWS)
    return [
        _const_spec((1, d)),
        _const_spec((d, cols)),
        _const_spec((CONV_W, dc)),
        _const_spec((1, dc)),
        _const_spec((1, dc)),
        _const_spec((1, dc)),
        _const_spec((len(POOL_WINDOWS), gd, gd)),
        _const_spec((1, dp)),
        _const_spec((dc + dp, d)),
        _const_spec((1, d)),
        _const_spec((d, LANES)),
        _const_spec((d, LANES)),
        _const_spec((1, LANES)),
    ]


def _mixer_prompt(x, wts, *, T, n_total):
    B, S, d = x.shape
    dc = wts[2].shape[1]
    dp = wts[7].shape[1]
    cols = wts[1].shape[1]
    n_s = S // T
    N = n_total
    n_blk = N // T
    assert N % T == 0 and N - B * S <= S
    tok = lambda b, s: (jnp.minimum(b * n_s + s, n_blk - 1), 0)
    tok4 = lambda b, s: (jnp.minimum(b * n_s + s, n_blk - 1), 0, 0, 0)
    bclamp = lambda b: jnp.minimum(b, B - 1)
    out_shape = (
        jax.ShapeDtypeStruct((N, d), F32),
        jax.ShapeDtypeStruct((N // TOK_CHUNK, TOK_CHUNK, d // LANES, LANES), BF16),
        jax.ShapeDtypeStruct((N, LANES), jnp.int32),
        jax.ShapeDtypeStruct((N, LANES), F32),
        jax.ShapeDtypeStruct((N, LANES), F32),
        jax.ShapeDtypeStruct((1, B, CONV_W - 1, dc), F32),
        jax.ShapeDtypeStruct((1, B, POOL_MAX_W - 1, dp), F32),
    )
    out_specs = (
        pl.BlockSpec((T, d), tok),
        pl.BlockSpec((T // TOK_CHUNK, TOK_CHUNK, d // LANES, LANES), tok4),
        pl.BlockSpec((T, LANES), tok),
        pl.BlockSpec((T, LANES), tok),
        pl.BlockSpec((T, LANES), tok),
        pl.BlockSpec((None, None, CONV_W - 1, dc), lambda b, s: (0, bclamp(b), 0, 0)),
        pl.BlockSpec((None, None, POOL_MAX_W - 1, dp), lambda b, s: (0, bclamp(b), 0, 0)),
    )
    scratch = [
        pltpu.VMEM((T, d), BF16),
        pltpu.VMEM((T, dc), F32),
        pltpu.VMEM((T + TAIL, dc), F32),
        pltpu.VMEM((T + PTAIL, dp), F32),
        pltpu.VMEM((T, dc), F32),
        pltpu.VMEM((T, dp), BF16),
        pltpu.VMEM((T, dc + dp), BF16),
        pltpu.VMEM((T, d), BF16),
        pltpu.VMEM((T, d), BF16),
    ]
    return pl.pallas_call(
        functools.partial(_mixer_prompt_kernel, T=T, d=d, dc=dc, dp=dp, n_batch=B),
        grid=(B + 1, n_s),
        in_specs=[pl.BlockSpec((None, T, d), lambda b, s: (bclamp(b), s, 0))]
                 + _mixer_weight_specs(d, dc, dp, cols),
        out_specs=out_specs,
        out_shape=out_shape,
        scratch_shapes=scratch,
        compiler_params=pltpu.CompilerParams(
            dimension_semantics=("arbitrary", "arbitrary"), vmem_limit_bytes=VMEM_LIMIT),
        name="mixer_prompt",
    )(x, *wts)


def _mixer_sample(x_t, sconv_t, spool_t, wts, tok_arrays, *, BS, tok0):
    TS, Bd, d = x_t.shape
    dc = wts[2].shape[1]
    dp = wts[7].shape[1]
    cols = wts[1].shape[1]
    T = TS * BS
    n_b = Bd // BS
    b0 = tok0 // T
    tok = lambda i: (b0 + i, 0)
    tok4 = lambda i: (b0 + i, 0, 0, 0)
    out_shape = tuple(jax.ShapeDtypeStruct(a.shape, a.dtype) for a in tok_arrays) + (
        jax.ShapeDtypeStruct((CONV_W - 1, Bd, dc), F32),
        jax.ShapeDtypeStruct((POOL_MAX_W - 1, Bd, dp), F32),
    )
    out_specs = (
        pl.BlockSpec((T, d), tok),
        pl.BlockSpec((T // TOK_CHUNK, TOK_CHUNK, d // LANES, LANES), tok4),
        pl.BlockSpec((T, LANES), tok),
        pl.BlockSpec((T, LANES), tok),
        pl.BlockSpec((T, LANES), tok),
        pl.BlockSpec((CONV_W - 1, BS, dc), lambda i: (0, i, 0)),
        pl.BlockSpec((POOL_MAX_W - 1, BS, dp), lambda i: (0, i, 0)),
    )
    scratch = [
        pltpu.VMEM((T, d), F32),
        pltpu.VMEM((T, d), BF16),
        pltpu.VMEM((T, dc), F32),
        pltpu.VMEM((T, dc), F32),
        pltpu.VMEM((T, dp), F32),
        pltpu.VMEM((T, dc), F32),
        pltpu.VMEM((T, dp), BF16),
        pltpu.VMEM((T, dc + dp), BF16),
        pltpu.VMEM((T, d), BF16),
        pltpu.VMEM((T, d), BF16),
    ]
    return pl.pallas_call(
        functools.partial(_mixer_sample_kernel, TS=TS, BS=BS, d=d, dc=dc, dp=dp),
        grid=(n_b,),
        in_specs=[pl.BlockSpec((TS, BS, d), lambda i: (0, i, 0)),
                  pl.BlockSpec((CONV_W - 1, BS, dc), lambda i: (0, i, 0), pipeline_mode=pl.Buffered(1)),
                  pl.BlockSpec((POOL_MAX_W - 1, BS, dp), lambda i: (0, i, 0), pipeline_mode=pl.Buffered(1))]
                 + _mixer_weight_specs(d, dc, dp, cols)
                 + [pl.BlockSpec(memory_space=pl.ANY)] * len(tok_arrays),
        out_specs=out_specs,
        out_shape=out_shape,
        scratch_shapes=scratch,
        input_output_aliases={3 + len(wts) + j: j for j in range(len(tok_arrays))},
        compiler_params=pltpu.CompilerParams(
            dimension_semantics=("arbitrary",), vmem_limit_bytes=VMEM_LIMIT),
        name="mixer_sample",
    )(x_t, sconv_t, spool_t, *wts, *tok_arrays)


def _dispatch_kernel(pos_ref, hf_ref, xs_in_ref, xs_ref, sem, *, R):
    del xs_in_ref
    base = pl.program_id(0) * R

    def body(r, c):
        for k in range(2):
            dst = pos_ref[2 * (base + r) + k]
            pltpu.make_async_copy(hf_ref.at[pl.ds(r, 1)], xs_ref.at[pl.ds(dst, 1)], sem.at[0]).start(priority=k)
        return c

    lax.fori_loop(0, R, body, 0, unroll=8)
    for k in range(2):
        pltpu.make_async_copy(hf_ref, xs_ref.at[pl.ds(0, R)], sem.at[0]).wait()


def _dispatch(pos, hf, xs_zero, *, R):
    n_tok = hf.shape[0]
    assert n_tok % R == 0
    return pl.pallas_call(
        functools.partial(_dispatch_kernel, R=R),
        grid_spec=pltpu.PrefetchScalarGridSpec(
            num_scalar_prefetch=1,
            grid=(n_tok // R,),
            in_specs=[pl.BlockSpec((R,) + hf.shape[1:], lambda s, pos: (s, 0, 0)),
                      pl.BlockSpec(memory_space=pl.ANY)],
            out_specs=pl.BlockSpec(memory_space=pl.ANY),
            scratch_shapes=[pltpu.SemaphoreType.DMA((1,))],
        ),
        out_shape=jax.ShapeDtypeStruct(xs_zero.shape, xs_zero.dtype),
        input_output_aliases={2: 0},
        compiler_params=pltpu.CompilerParams(dimension_semantics=("arbitrary",)),
        name="dispatch",
    )(pos, hf, xs_zero)


def _invert_kernel(pos_ref, dst_in, dst_ref, sem, *, n_pairs, n_tok):
    init = pltpu.make_async_copy(dst_in, dst_ref, sem.at[0])
    init.start()
    init.wait()
    G = 8

    def put(b, c):
        n0 = b * G
        ps = [pos_ref[2 * n0 + q] for q in range(2 * G)]
        for q in range(2 * G):
            n, k = n0 + q // 2, q % 2
            dst_ref[ps[q]] = k * n_tok + n
        return c

    assert n_pairs % (2 * G) == 0
    lax.fori_loop(0, n_pairs // (2 * G), put, 0)


def _invert(pos, *, n_tiles, tm, n_tok):
    n_pairs = pos.shape[0]
    p = jnp.arange((n_tiles + 1) * tm, dtype=jnp.int32)
    tile = p // tm
    buf = jnp.where(tile == n_tiles, 1, tile % 2)
    dst_init = 2 * n_tok + buf * tm + p % tm
    smem = pl.BlockSpec(memory_space=pltpu.SMEM)
    return pl.pallas_call(
        functools.partial(_invert_kernel, n_pairs=n_pairs, n_tok=n_tok),
        in_specs=[smem, pl.BlockSpec(memory_space=pl.ANY)],
        out_specs=smem,
        out_shape=jax.ShapeDtypeStruct(dst_init.shape, jnp.int32),
        scratch_shapes=[pltpu.SemaphoreType.DMA((1,))],
        name="invert",
    )(pos, dst_init)


def _expert_mlp_kernel(te_ref, na_ref, nx_ref, dst_prev_ref, dst_cur_ref, xs_ref, wg_ref, wu_ref, wd_ref, o_ref,
                       ybuf0, ybuf1, wgf, wuf, wdf, wg_scr, wu_scr, wd_scr, wslot, ssem, wsem,
                       *, tm, n_tiles, d):
    i = pl.program_id(0)
    n_act = na_ref[0]
    active = i < n_act
    slot = i % 2
    last = n_tiles - 1
    prev = te_ref[jnp.maximum(i - 1, 0)]
    new_expert = (i == 0) | (te_ref[i] != prev)
    ybuf = (ybuf0, ybuf1)

    def scatter_start(dst_ref, b):
        for r in range(tm):
            pltpu.make_async_copy(ybuf[b].at[pl.ds(r, 1)], o_ref.at[pl.ds(dst_ref[0, 0, r], 1)],
                                  ssem.at[b]).start(priority=r % 2)

    def scatter_wait(b):
        pltpu.make_async_copy(ybuf[b], o_ref.at[pl.ds(0, tm)], ssem.at[b]).wait()

    def weight_copies(e, b):
        return [pltpu.make_async_copy(src.at[e], buf.at[b], wsem.at[b])
                for src, buf in ((wg_ref, wgf), (wu_ref, wuf), (wd_ref, wdf))]

    @pl.when(i == 0)
    def _():
        wslot[0] = 0
        for c in weight_copies(te_ref[0], 0):
            c.start()
        ybuf1[...] = jnp.zeros(ybuf1.shape, BF16)
        trash0 = pltpu.make_async_copy(ybuf1, o_ref.at[pl.ds(o_ref.shape[0] - 2 * tm, tm)], ssem.at[0])
        trash0.start()
        trash0.wait()

    @pl.when(active & new_expert)
    def _():
        b = wslot[0]
        nxt = nx_ref[i]

        @pl.when(nxt >= 0)
        def _():
            for c in weight_copies(nxt, 1 - b):
                c.start()

        for c in weight_copies(te_ref[i], b):
            c.wait()
        wg_scr[...] = wgf[b].astype(BF16)
        wu_scr[...] = wuf[b].astype(BF16)
        wd_scr[...] = wdf[b].astype(BF16)
        wslot[0] = 1 - b

    for par in range(2):
        is_par = slot == par

        @pl.when(is_par & (i >= 1) & (i - 2 < n_act))
        def _():
            scatter_wait(par)

        @pl.when(is_par & active)
        def _():
            x = xs_ref[...].reshape(tm, d)
            g = jnp.dot(x, wg_scr[...], preferred_element_type=F32)
            u = jnp.dot(x, wu_scr[...], preferred_element_type=F32)
            act = (g * _sigmoid(g) * u).astype(BF16)
            y = jnp.dot(act, wd_scr[...], preferred_element_type=F32)
            ybuf[par][...] = y.astype(BF16).reshape(tm, d // LANES, LANES)
            scatter_start(dst_prev_ref, 1 - par)

        @pl.when(is_par & (i == n_act))
        def _():
            scatter_start(dst_prev_ref, 1 - par)

        @pl.when(is_par & (i == last))
        def _():
            @pl.when(last - 1 < n_act)
            def _():
                scatter_wait(1 - par)

            @pl.when(last < n_act)
            def _():
                scatter_start(dst_cur_ref, par)
                scatter_wait(par)


def _expert_mlp(tile_expert, n_active, next_expert, dst, xs, w_gate, w_up, w_down, *, tm, n_tok):
    E, d, de = w_gate.shape
    n_tiles = xs.shape[0] // tm
    row = (tm, d // LANES, LANES)
    cur = lambda i, te, na, nx: (i, 0, 0)
    prv = lambda i, te, na, nx: (jnp.where(i == 0, n_tiles, i - 1), 0, 0)
    hbm = pl.BlockSpec(memory_space=pl.ANY)
    smem_tile = lambda imap: pl.BlockSpec((1, 1, tm), imap, memory_space=pltpu.SMEM)
    return pl.pallas_call(
        functools.partial(_expert_mlp_kernel, tm=tm, n_tiles=n_tiles, d=d),
        grid_spec=pltpu.PrefetchScalarGridSpec(
            num_scalar_prefetch=3,
            grid=(n_tiles,),
            in_specs=[
                smem_tile(prv), smem_tile(cur),
                pl.BlockSpec(row, lambda i, te, na, nx: (i, 0, 0)),
                hbm, hbm, hbm,
            ],
            out_specs=hbm,
            scratch_shapes=[pltpu.VMEM(row, BF16), pltpu.VMEM(row, BF16),
                            pltpu.VMEM((2, d, de), F32), pltpu.VMEM((2, d, de), F32), pltpu.VMEM((2, de, d), F32),
                            pltpu.VMEM((d, de), BF16), pltpu.VMEM((d, de), BF16), pltpu.VMEM((de, d), BF16),
                            pltpu.SMEM((1,), jnp.int32),
                            pltpu.SemaphoreType.DMA((2,)), pltpu.SemaphoreType.DMA((2,))],
        ),
        out_shape=jax.ShapeDtypeStruct((2 * n_tok + 2 * tm, d // LANES, LANES), BF16),
        compiler_params=pltpu.CompilerParams(
            dimension_semantics=("arbitrary",), vmem_limit_bytes=VMEM_LIMIT),
        name="expert_mlp",
    )(tile_expert, n_active, next_expert, dst, dst, xs, w_gate, w_up, w_down)


def _combine_kernel(x2_ref, o0_ref, o1_ref, w0_ref, w1_ref, gfin_ref, y_ref, *, R, d):
    C = TOK_CHUNK

    def body(i, c):
        rs = _rows(i, C)
        w0 = w0_ref[rs, :]
        w1 = w1_ref[rs, :]
        o0 = o0_ref[i].reshape(C, d).astype(F32)
        o1 = o1_ref[i].reshape(C, d).astype(F32)
        parts = []
        ssq = jnp.zeros((C, LANES), F32)
        for j in range(d // LANES):
            ls = slice(j * LANES, (j + 1) * LANES)
            m = w0 * o0[:, ls] + w1 * o1[:, ls]
            v = x2_ref[rs, ls] + m
            ssq = ssq + v * v
            parts.append(v)
        ms = jnp.sum(ssq, axis=-1, keepdims=True) * jnp.float32(1.0 / d)
        inv = lax.rsqrt(ms + EPS)
        for j in range(d // LANES):
            ls = slice(j * LANES, (j + 1) * LANES)
            y_ref[rs, ls] = parts[j] * inv * gfin_ref[:, ls]
        return c

    lax.fori_loop(0, R // C, body, 0)


def _combine(x2, o, w0, w1, g_final, *, R, tok0, n_out):
    n_tok, d = x2.shape
    n_steps = n_out // R
    b0 = tok0 // R
    b1 = n_tok // R
    tok = lambda s: (b0 + s, 0)
    orow = (R // TOK_CHUNK, TOK_CHUNK, d // LANES, LANES)
    o4 = o.reshape(o.shape[0] // TOK_CHUNK, TOK_CHUNK, d // LANES, LANES)
    return pl.pallas_call(
        functools.partial(_combine_kernel, R=R, d=d),
        grid=(n_steps,),
        in_specs=[
            pl.BlockSpec((R, d), tok),
            pl.BlockSpec(orow, lambda s: (b0 + s, 0, 0, 0)),
            pl.BlockSpec(orow, lambda s: (b1 + b0 + s, 0, 0, 0)),
            pl.BlockSpec((R, LANES), tok),
            pl.BlockSpec((R, LANES), tok),
            pl.BlockSpec((1, d), lambda s: (0, 0)),
        ],
        out_specs=pl.BlockSpec((R, d), lambda s: (s, 0)),
        out_shape=jax.ShapeDtypeStruct((n_out, d), F32),
        compiler_params=pltpu.CompilerParams(
            dimension_semantics=("arbitrary",), vmem_limit_bytes=VMEM_LIMIT),
        name="combine",
    )(x2, o4, o4, w0, w1, g_final)


def _routing_plan(ids, tm, n_tiles):
    e0 = ids[:, 0]
    e1 = ids[:, 1]
    ar = jnp.arange(N_EXPERTS, dtype=jnp.int32)
    m = ((e0[:, None] == ar) | (e1[:, None] == ar)).astype(jnp.int32)
    incl = jnp.cumsum(m, axis=0)
    excl = incl - m
    counts = incl[-1]
    tiles_e = (counts + tm - 1) // tm
    tile_end = jnp.cumsum(tiles_e)
    offs = (tile_end - tiles_e) * tm
    base = offs[None, :] + excl
    pos0 = jnp.take_along_axis(base, e0[:, None], axis=1)[:, 0]
    pos1 = jnp.take_along_axis(base, e1[:, None], axis=1)[:, 0]
    pos = jnp.stack([pos0, pos1], axis=1).reshape(-1).astype(jnp.int32)
    n_active = tile_end[-1].astype(jnp.int32)
    t = jnp.arange(n_tiles, dtype=jnp.int32)
    tq = jnp.minimum(t, n_active - 1)
    te = jnp.sum((tile_end[None, :] <= tq[:, None]).astype(jnp.int32), axis=1)
    te = jnp.minimum(te, N_EXPERTS - 1)
    later = (ar[None, :] > ar[:, None]) & (tiles_e[None, :] > 0)
    nxt_e = jnp.min(jnp.where(later, ar[None, :], N_EXPERTS), axis=1)
    nxt_e = jnp.where(nxt_e == N_EXPERTS, -1, nxt_e).astype(jnp.int32)
    return pos, te, n_active.reshape(1), nxt_e[te]


T_PROMPT = 256
BS_SAMPLE = 32
TM_EXPERT = 256
R_DISPATCH_CHOICES = (512, 256, 128)
R_COMBINE = 128


def kernel(x_prompt, x_sample, state_conv, state_pool, g_mix, w_in, w_dw, b_dw, ln_g, ln_b, w_pool, pool_scale, w_out, g_ffn, w_rg, b_rg, w_re, b_re, w_gate, w_up, w_down, g_final):
    depth = g_mix.shape[0]
    assert depth == 1
    B, S, d = x_prompt.shape
    Bd, TS, _ = x_sample.shape
    n_p = B * S
    n_s = Bd * TS
    N = n_p + n_s

    w_r = jnp.zeros((d, LANES), F32)
    w_r = w_r.at[:, 0:N_EXPERTS].set(w_re[0]).at[:, ROUTER_GROUP_LANE0:ROUTER_GROUP_LANE0 + N_GROUPS].set(w_rg[0])
    w_r_hi = w_r.astype(BF16)
    w_r_lo = (w_r - w_r_hi.astype(F32)).astype(BF16)
    b_r = jnp.zeros((1, LANES), F32)
    b_r = b_r.at[0, 0:N_EXPERTS].set(b_re[0]).at[0, ROUTER_GROUP_LANE0:ROUTER_GROUP_LANE0 + N_GROUPS].set(b_rg[0])
    wts = (g_mix[0][None], w_in[0].astype(BF16), w_dw[0], b_dw[0][None], ln_g[0][None], ln_b[0][None],
           w_pool[0].astype(BF16), pool_scale[0][None], w_out[0].astype(BF16), g_ffn[0][None],
           w_r_hi, w_r_lo, b_r)

    *tok_arrays, nconv_p, npool_p = _mixer_prompt(x_prompt, wts, T=T_PROMPT, n_total=N)

    x_t = jnp.transpose(x_sample, (1, 0, 2))
    sconv_t = jnp.transpose(state_conv[0], (1, 0, 2))
    spool_t = jnp.transpose(state_pool[0], (1, 0, 2))
    x2, hf, ids, w0, w1, nconv_t, npool_t = _mixer_sample(x_t, sconv_t, spool_t, wts, tok_arrays,
                                                          BS=BS_SAMPLE, tok0=n_p)

    tm = TM_EXPERT
    n_tiles = (2 * N + N_EXPERTS * (tm - 1) + tm - 1) // tm
    pos, tile_expert, n_active, next_expert = _routing_plan(ids[:, 0:2], tm, n_tiles)
    dst = _invert(pos, n_tiles=n_tiles, tm=tm, n_tok=N).reshape(n_tiles + 1, 1, tm)

    hf = hf.reshape(N, d // LANES, LANES)
    r_disp = max(r for r in R_DISPATCH_CHOICES if N % r == 0)
    xs = _dispatch(pos, hf, jnp.zeros((n_tiles * tm, d // LANES, LANES), BF16), R=r_disp)
    o = _expert_mlp(tile_expert, n_active, next_expert, dst, xs, w_gate[0], w_up[0], w_down[0], tm=tm, n_tok=N)

    gfin = g_final[None]
    y_p = _combine(x2, o, w0, w1, gfin, R=R_COMBINE, tok0=0, n_out=n_p)
    y_s = _combine(x2, o, w0, w1, gfin, R=R_COMBINE, tok0=n_p, n_out=n_s)

    y_prompt = y_p.reshape(B, S, d)
    y_sample = y_s.reshape(Bd // BS_SAMPLE, TS, BS_SAMPLE, d).transpose(0, 2, 1, 3).reshape(Bd, TS, d)
    new_conv_s = jnp.transpose(nconv_t, (1, 0, 2))[None]
    new_pool_s = jnp.transpose(npool_t, (1, 0, 2))[None]
    return (y_prompt, y_sample, nconv_p, new_conv_s, npool_p, new_pool_s)
```

```python
import functools

import jax
import jax.numpy as jnp
from jax import lax
from jax.experimental import pallas as pl
from jax.experimental.pallas import tpu as pltpu

F32 = jnp.float32
BF16 = jnp.bfloat16
EPS = 1e-6

LANES = 128
SUBLANES = 8
VMEM_LIMIT = 56 * 1024 * 1024

CONV_W = 31
POOL_WINDOWS = (2, 4, 8, 16)
POOL_MAX_W = 16
N_GROUPS = 4
PER_GROUP = 8
N_EXPERTS = N_GROUPS * PER_GROUP

ROUTER_GROUP_LANE0 = N_EXPERTS
TOK_CHUNK = 16
ROW_LOOP_UNROLL = 4
TAIL = 32
PTAIL = 16


def _rows(i, r):
    if isinstance(i, int):
        return pl.ds(i * r, r)
    return pl.ds(pl.multiple_of(i * r, r), r)


def _row_loop(n, body, *, static, unroll=1):
    if static:
        for i in range(n):
            body(i)
    else:
        def step(i, c):
            body(i)
            return c
        lax.fori_loop(0, n, step, 0, unroll=unroll)


def _sigmoid(x):
    return 1.0 / (1.0 + jnp.exp(-x))


def _rmsnorm_to_bf16(src_ref, g_ref, dst_ref, T, static=False):
    R = 16

    def body(i):
        rs = _rows(i, R)
        x = src_ref[rs, :]
        ms = jnp.mean(x * x, axis=-1, keepdims=True)
        dst_ref[rs, :] = (x * lax.rsqrt(ms + EPS) * g_ref[...]).astype(BF16)

    _row_loop(T // R, body, static=static, unroll=ROW_LOOP_UNROLL)


def _layernorm_swish(y_ref, lng_ref, lnb_ref, mix_ref, T, dc, static=False):
    R = 16

    def body(i):
        rs = _rows(i, R)
        y = y_ref[rs, :]
        mu = jnp.mean(y, axis=-1, keepdims=True)
        d = y - mu
        var = jnp.mean(d * d, axis=-1, keepdims=True)
        z = d * lax.rsqrt(var + EPS) * lng_ref[...] + lnb_ref[...]
        mix_ref[rs, 0:dc] = (z * _sigmoid(z)).astype(BF16)

    _row_loop(T // R, body, static=static, unroll=ROW_LOOP_UNROLL)


def _pool_project(sd_ref, wpool_ref, pscale_ref, mix_ref, dc, gd):
    for g in range(len(POOL_WINDOWS)):
        sl = slice(g * gd, (g + 1) * gd)
        o = jnp.dot(sd_ref[:, sl], wpool_ref[g], preferred_element_type=F32)
        mix_ref[:, dc + g * gd: dc + (g + 1) * gd] = (o * pscale_ref[:, sl]).astype(BF16)


def _out_proj_residual(x_ref, mix_ref, wout_ref, x2_ref):
    x2_ref[...] = x_ref[...] + jnp.dot(mix_ref[...], wout_ref[...], preferred_element_type=F32)


def _router(x2_ref, gffn_ref, wrh_ref, wrl_ref, br_ref, hf_ref, ids_ref, w0_ref, w1_ref,
            hhi_scr, hlo_scr, T, d):
    R = TOK_CHUNK

    def body(i, c):
        rs = _rows(i, R)
        x = x2_ref[rs, :]
        ms = jnp.mean(x * x, axis=-1, keepdims=True)
        h = x * lax.rsqrt(ms + EPS) * gffn_ref[...]
        hi = h.astype(BF16)
        hf_ref[i] = hi.reshape(R, d // LANES, LANES)
        hhi_scr[rs, :] = hi
        hlo_scr[rs, :] = (h - hi.astype(F32)).astype(BF16)
        return c

    lax.fori_loop(0, T // R, body, 0, unroll=ROW_LOOP_UNROLL)

    lg = (jnp.dot(hhi_scr[...], wrh_ref[...], preferred_element_type=F32)
          + jnp.dot(hlo_scr[...], wrh_ref[...], preferred_element_type=F32)
          + jnp.dot(hhi_scr[...], wrl_ref[...], preferred_element_type=F32)
          + br_ref[...])

    lane = lax.broadcasted_iota(jnp.int32, lg.shape, 1).astype(F32)
    neg = jnp.float32(-jnp.inf)
    big = jnp.float32(1e9)
    g_lo = jnp.float32(ROUTER_GROUP_LANE0)
    gmask = (lane >= g_lo) & (lane < g_lo + N_GROUPS)
    lgg = jnp.where(gmask, lg, neg)
    gmax = jnp.max(lgg, axis=-1, keepdims=True)
    gsel = jnp.min(jnp.where(lgg == gmax, lane, big), axis=-1, keepdims=True) - g_lo
    gsum = jnp.sum(jnp.where(gmask, jnp.exp(lg - gmax), 0.0), axis=-1, keepdims=True)
    p_g = 1.0 / gsum

    e_lo = gsel * PER_GROUP
    emask = (lane >= e_lo) & (lane < e_lo + PER_GROUP)
    le = jnp.where(emask, lg, neg)
    v0 = jnp.max(le, axis=-1, keepdims=True)
    i0 = jnp.min(jnp.where(le == v0, lane, big), axis=-1, keepdims=True)
    le2 = jnp.where(lane == i0, neg, le)
    v1 = jnp.max(le2, axis=-1, keepdims=True)
    i1 = jnp.min(jnp.where(le2 == v1, lane, big), axis=-1, keepdims=True)
    ex = jnp.exp(v1 - v0)
    den = 1.0 / (1.0 + ex)
    w0 = den * p_g
    w1 = ex * den * p_g

    ids_ref[...] = jnp.where(lane == 0.0, i0, jnp.where(lane == 1.0, i1, 0.0)).astype(jnp.int32)
    w0_ref[...] = jnp.broadcast_to(w0, lg.shape)
    w1_ref[...] = jnp.broadcast_to(w1, lg.shape)


def _mixer_prompt_kernel(xp_ref, xn_ref, gmix_ref, win_ref, wdw_ref, bdw_ref, lng_ref, lnb_ref, wpool_ref,
                         pscale_ref, wout_ref, gffn_ref, wrh_ref, wrl_ref, br_ref,
                         x2_ref, hf_ref, ids_ref, w0_ref, w1_ref, nconv_ref, npool_ref,
                         h_scr, pg0, pg1, ext0, ext1, extu0, extu1, yc_scr, sd0, sd1, mix0, mix1, hhi_scr, hlo_scr,
                         *, T, d, dc, dp, n_batch):
    n_s = pl.num_programs(1)
    b = pl.program_id(0)
    s = pl.program_id(1)
    t = b * n_s + s
    gd = dp // len(POOL_WINDOWS)
    pg_scr, ext_scr, extu_scr, sd_scr, mix_scr = (pg0, pg1), (ext0, ext1), (extu0, extu1), (sd0, sd1), (mix0, mix1)

    def in_proj(x_ref, slot, static):
        _rmsnorm_to_bf16(x_ref, gmix_ref, h_scr, T, static=static)
        ext_scr[slot][TAIL:TAIL + T, :] = jnp.dot(h_scr[...], win_ref[:, 0:dc], preferred_element_type=F32)
        pg_scr[slot][...] = jnp.dot(h_scr[...], win_ref[:, dc:2 * dc], preferred_element_type=F32)
        extu_scr[slot][PTAIL:PTAIL + T, :] = jnp.dot(h_scr[...], win_ref[:, 2 * dc:2 * dc + dp],
                                                     preferred_element_type=F32)

    def finish_prev(q):
        _pool_project(sd_scr[q], wpool_ref, pscale_ref, mix_scr[q], dc, gd)
        _out_proj_residual(xp_ref, mix_scr[q], wout_ref, x2_ref)

    @pl.when(t == 0)
    def _():
        ext0[0:TAIL, :] = jnp.zeros((TAIL, dc), F32)
        extu0[0:PTAIL, :] = jnp.zeros((PTAIL, dp), F32)
        sd1[...] = jnp.zeros(sd1.shape, BF16)
        mix1[...] = jnp.zeros(mix1.shape, BF16)
        in_proj(xp_ref, 0, False)

    for par in range(2):
        @pl.when((b == n_batch) & (s == 0) & (t % 2 == par))
        def _():
            finish_prev(1 - par)

    @pl.when((b == n_batch) & (s > 0))
    def _():
        x2_ref[...] = jnp.zeros(x2_ref.shape, F32)
        hf_ref[...] = jnp.zeros(hf_ref.shape, BF16)
        ids_ref[...] = jnp.zeros(ids_ref.shape, jnp.int32)
        w0_ref[...] = jnp.zeros(w0_ref.shape, F32)
        w1_ref[...] = jnp.zeros(w1_ref.shape, F32)

    def step(p):
        q = 1 - p
        in_proj(xn_ref, q, True)

        R = 16
        for i in range(T // R):
            es = pl.ds(i * R + TAIL, R)
            ext_scr[p][es, :] = ext_scr[p][es, :] * _sigmoid(pg_scr[p][pl.ds(i * R, R), :])

        RC, LC = 32, 128
        BR = RC + TAIL
        shift0 = TAIL - (CONV_W - 1)
        for i in range(T // RC):
            r0 = i * RC
            for lc in range(dc // LC):
                ls = slice(lc * LC, (lc + 1) * LC)
                blk = ext_scr[p][pl.ds(r0, BR), ls]
                acc = jnp.zeros((RC, LC), F32)
                for sft in range(SUBLANES):
                    taps = [k for k in range(CONV_W) if (k + shift0) % SUBLANES == sft]
                    if not taps:
                        continue
                    rolled = blk if sft == 0 else pltpu.roll(blk, BR - sft, 0)
                    for k in taps:
                        qq = (k + shift0) // SUBLANES
                        acc = acc + wdw_ref[k:k + 1, ls] * rolled[qq * SUBLANES:qq * SUBLANES + RC, :]
                yc_scr[pl.ds(r0, RC), ls] = acc + bdw_ref[:, ls]

        _layernorm_swish(yc_scr, lng_ref, lnb_ref, mix_scr[p], T, dc, static=True)

        RP = 32
        BP = RP + PTAIL
        pos_base = s * T
        for i in range(T // RP):
            r0 = i * RP
            pos = (pos_base + r0 + lax.broadcasted_iota(jnp.int32, (RP, gd), 0)).astype(F32)
            for g, w in enumerate(POOL_WINDOWS):
                ls = slice(g * gd, (g + 1) * gd)
                blk = extu_scr[p][pl.ds(r0, BP), ls]
                run = blk
                span = 1
                while span < w:
                    run = run + pltpu.roll(run, span, 0)
                    span *= 2
                cnt = jnp.minimum(pos + 1.0, jnp.float32(w))
                mean = run[PTAIL:PTAIL + RP, :] / cnt
                sd_scr[p][pl.ds(r0, RP), ls] = (mean - blk[PTAIL:PTAIL + RP, :]).astype(BF16)

        nconv_ref[...] = ext_scr[p][TAIL + T - (CONV_W - 1):TAIL + T, :]
        npool_ref[...] = extu_scr[p][PTAIL + T - (POOL_MAX_W - 1):PTAIL + T, :]
        keep = s != n_s - 1
        ext_scr[q][0:TAIL, :] = jnp.where(keep, ext_scr[p][T:T + TAIL, :], 0.0)
        extu_scr[q][0:PTAIL, :] = jnp.where(keep, extu_scr[p][T:T + PTAIL, :], 0.0)

        finish_prev(q)

    for par in range(2):
        pl.when((b < n_batch) & (t % 2 == par))(functools.partial(step, par))

    @pl.when((b < n_batch) | (s == 0))
    def _():
        _router(x2_ref, gffn_ref, wrh_ref, wrl_ref, br_ref, hf_ref, ids_ref, w0_ref, w1_ref,
                hhi_scr, hlo_scr, T, d)


def _mixer_sample_kernel(x_ref, sconv_ref, spool_ref, gmix_ref, win_ref, wdw_ref, bdw_ref, lng_ref,
                         lnb_ref, wpool_ref, pscale_ref, wout_ref, gffn_ref, wrh_ref, wrl_ref, br_ref,
                         x2_in, hf_in, ids_in, w0_in, w1_in,
                         x2_ref, hf_ref, ids_ref, w0_ref, w1_ref, nconv_ref, npool_ref,
                         xt_scr, h_scr, a_scr, pg_scr, u_scr, yc_scr, sd_scr, mix_scr, hhi_scr, hlo_scr,
                         *, TS, BS, d, dc, dp):
    del x2_in, hf_in, ids_in, w0_in, w1_in
    T = TS * BS
    gd = dp // len(POOL_WINDOWS)
    NH = CONV_W - 1
    NP = POOL_MAX_W - 1

    for t in range(TS):
        xt_scr[t * BS:(t + 1) * BS, :] = x_ref[t]

    _rmsnorm_to_bf16(xt_scr, gmix_ref, h_scr, T)
    a_scr[...] = jnp.dot(h_scr[...], win_ref[:, 0:dc], preferred_element_type=F32)
    pg_scr[...] = jnp.dot(h_scr[...], win_ref[:, dc:2 * dc], preferred_element_type=F32)
    u_scr[...] = jnp.dot(h_scr[...], win_ref[:, 2 * dc:2 * dc + dp], preferred_element_type=F32)

    R = 16

    def glu(i, c):
        rs = _rows(i, R)
        a_scr[rs, :] = a_scr[rs, :] * _sigmoid(pg_scr[rs, :])
        return c

    lax.fori_loop(0, T // R, glu, 0, unroll=ROW_LOOP_UNROLL)

    def ext_conv(j, rs, ls):
        if j < NH:
            return sconv_ref[j, rs, ls]
        return a_scr[pl.ds((j - NH) * BS + rs.start, rs.size), ls]

    def ext_pool(j, rs, ls):
        if j < NP:
            return spool_ref[j, rs, ls]
        return u_scr[pl.ds((j - NP) * BS + rs.start, rs.size), ls]

    RC, LC = 32, 256

    def conv(i, c):
        rs = _rows(i, RC)
        for t in range(TS):
            for lc in range(dc // LC):
                ls = slice(lc * LC, (lc + 1) * LC)
                acc = jnp.zeros((RC, LC), F32)
                for k in range(CONV_W):
                    acc = acc + wdw_ref[k:k + 1, ls] * ext_conv(t + k, rs, ls)
                yc_scr[pl.ds(t * BS + rs.start, RC), ls] = acc + bdw_ref[:, ls]
        return c

    lax.fori_loop(0, BS // RC, conv, 0)

    _layernorm_swish(yc_scr, lng_ref, lnb_ref, mix_scr, T, dc)

    def pool(i, c):
        rs = _rows(i, RC)
        for t in range(TS):
            for g, w in enumerate(POOL_WINDOWS):
                ls = slice(g * gd, (g + 1) * gd)
                tot = ext_pool(NP + t, rs, ls)
                cur = tot
                for back in range(1, w):
                    tot = tot + ext_pool(NP + t - back, rs, ls)
                sd_scr[pl.ds(t * BS + rs.start, RC), ls] = (tot / jnp.float32(w) - cur).astype(BF16)
        return c

    lax.fori_loop(0, BS // RC, pool, 0)

    _pool_project(sd_scr, wpool_ref, pscale_ref, mix_scr, dc, gd)

    for j in range(NH):
        src = j + TS
        nconv_ref[j] = sconv_ref[src] if src < NH else a_scr[(src - NH) * BS:(src - NH + 1) * BS, :]
    for j in range(NP):
        src = j + TS
        npool_ref[j] = spool_ref[src] if src < NP else u_scr[(src - NP) * BS:(src - NP + 1) * BS, :]

    _out_proj_residual(xt_scr, mix_scr, wout_ref, x2_ref)
    _router(x2_ref, gffn_ref, wrh_ref, wrl_ref, br_ref, hf_ref, ids_ref, w0_ref, w1_ref,
            hhi_scr, hlo_scr, T, d)


def _const_spec(shape):
    nd = len(shape)
    return pl.BlockSpec(shape, lambda *a: (0,) * nd, pipeline_mode=pl.Buffered(1))


def _mixer_weight_specs(d, dc, dp, cols):
    gd = dp // len(POOL_WINDOWS)
    return [
        _const_spec((1, d)),
        _const_spec((d, cols)),
        _const_spec((CONV_W, dc)),
        _const_spec((1, dc)),
        _const_spec((1, dc)),
        _const_spec((1, dc)),
        _const_spec((len(POOL_WINDOWS), gd, gd)),
        _const_spec((1, dp)),
        _const_spec((dc + dp, d)),
        _const_spec((1, d)),
        _const_spec((d, LANES)),
        _const_spec((d, LANES)),
        _const_spec((1, LANES)),
    ]


def _mixer_prompt(x, wts, *, T, n_total):
    B, S, d = x.shape
    dc = wts[2].shape[1]
    dp = wts[7].shape[1]
    cols = wts[1].shape[1]
    n_s = S // T
    N = n_total
    n_blk = N // T
    assert N % T == 0 and N - B * S <= S
    blk = lambda b, s: jnp.clip(b * n_s + s - 1, 0, n_blk - 1)
    tok = lambda b, s: (blk(b, s), 0)
    tok4 = lambda b, s: (blk(b, s), 0, 0, 0)
    bclamp = lambda b: jnp.minimum(b, B - 1)

    def nxt(b, s):
        wrap = s + 1 == n_s
        return (jnp.where(wrap, bclamp(b + 1), bclamp(b)), jnp.where(wrap, 0, s + 1), 0)

    def prv(b, s):
        tp = jnp.clip(b * n_s + s - 1, 0, B * n_s - 1)
        return (tp // n_s, tp % n_s, 0)
    out_shape = (
        jax.ShapeDtypeStruct((N, d), F32),
        jax.ShapeDtypeStruct((N // TOK_CHUNK, TOK_CHUNK, d // LANES, LANES), BF16),
        jax.ShapeDtypeStruct((N, LANES), jnp.int32),
        jax.ShapeDtypeStruct((N, LANES), F32),
        jax.ShapeDtypeStruct((N, LANES), F32),
        jax.ShapeDtypeStruct((1, B, CONV_W - 1, dc), F32),
        jax.ShapeDtypeStruct((1, B, POOL_MAX_W - 1, dp), F32),
    )
    out_specs = (
        pl.BlockSpec((T, d), tok),
        pl.BlockSpec((T // TOK_CHUNK, TOK_CHUNK, d // LANES, LANES), tok4),
        pl.BlockSpec((T, LANES), tok),
        pl.BlockSpec((T, LANES), tok),
        pl.BlockSpec((T, LANES), tok),
        pl.BlockSpec((None, None, CONV_W - 1, dc), lambda b, s: (0, bclamp(b), 0, 0)),
        pl.BlockSpec((None, None, POOL_MAX_W - 1, dp), lambda b, s: (0, bclamp(b), 0, 0)),
    )
    scratch = [
        pltpu.VMEM((T, d), BF16),
        pltpu.VMEM((T, dc), F32), pltpu.VMEM((T, dc), F32),
        pltpu.VMEM((T + TAIL, dc), F32), pltpu.VMEM((T + TAIL, dc), F32),
        pltpu.VMEM((T + PTAIL, dp), F32), pltpu.VMEM((T + PTAIL, dp), F32),
        pltpu.VMEM((T, dc), F32),
        pltpu.VMEM((T, dp), BF16), pltpu.VMEM((T, dp), BF16),
        pltpu.VMEM((T, dc + dp), BF16), pltpu.VMEM((T, dc + dp), BF16),
        pltpu.VMEM((T, d), BF16),
        pltpu.VMEM((T, d), BF16),
    ]
    return pl.pallas_call(
        functools.partial(_mixer_prompt_kernel, T=T, d=d, dc=dc, dp=dp, n_batch=B),
        grid=(B + 1, n_s),
        in_specs=[pl.BlockSpec((None, T, d), prv),
                  pl.BlockSpec((None, T, d), nxt)]
                 + _mixer_weight_specs(d, dc, dp, cols),
        out_specs=out_specs,
        out_shape=out_shape,
        scratch_shapes=scratch,
        compiler_params=pltpu.CompilerParams(
            dimension_semantics=("arbitrary", "arbitrary"), vmem_limit_bytes=VMEM_LIMIT),
        name="mixer_prompt",
    )(x, x, *wts)


def _mixer_sample(x_t, sconv_t, spool_t, wts, tok_arrays, *, BS, tok0):
    TS, Bd, d = x_t.shape
    dc = wts[2].shape[1]
    dp = wts[7].shape[1]
    cols = wts[1].shape[1]
    T = TS * BS
    n_b = Bd // BS
    b0 = tok0 // T
    tok = lambda i: (b0 + i, 0)
    tok4 = lambda i: (b0 + i, 0, 0, 0)
    out_shape = tuple(jax.ShapeDtypeStruct(a.shape, a.dtype) for a in tok_arrays) + (
        jax.ShapeDtypeStruct((CONV_W - 1, Bd, dc), F32),
        jax.ShapeDtypeStruct((POOL_MAX_W - 1, Bd, dp), F32),
    )
    out_specs = (
        pl.BlockSpec((T, d), tok),
        pl.BlockSpec((T // TOK_CHUNK, TOK_CHUNK, d // LANES, LANES), tok4),
        pl.BlockSpec((T, LANES), tok),
        pl.BlockSpec((T, LANES), tok),
        pl.BlockSpec((T, LANES), tok),
        pl.BlockSpec((CONV_W - 1, BS, dc), lambda i: (0, i, 0)),
        pl.BlockSpec((POOL_MAX_W - 1, BS, dp), lambda i: (0, i, 0)),
    )
    scratch = [
        pltpu.VMEM((T, d), F32),
        pltpu.VMEM((T, d), BF16),
        pltpu.VMEM((T, dc), F32),
        pltpu.VMEM((T, dc), F32),
        pltpu.VMEM((T, dp), F32),
        pltpu.VMEM((T, dc), F32),
        pltpu.VMEM((T, dp), BF16),
        pltpu.VMEM((T, dc + dp), BF16),
        pltpu.VMEM((T, d), BF16),
        pltpu.VMEM((T, d), BF16),
    ]
    return pl.pallas_call(
        functools.partial(_mixer_sample_kernel, TS=TS, BS=BS, d=d, dc=dc, dp=dp),
        grid=(n_b,),
        in_specs=[pl.BlockSpec((TS, BS, d), lambda i: (0, i, 0)),
                  pl.BlockSpec((CONV_W - 1, BS, dc), lambda i: (0, i, 0), pipeline_mode=pl.Buffered(1)),
                  pl.BlockSpec((POOL_MAX_W - 1, BS, dp), lambda i: (0, i, 0), pipeline_mode=pl.Buffered(1))]
                 + _mixer_weight_specs(d, dc, dp, cols)
                 + [pl.BlockSpec(memory_space=pl.ANY)] * len(tok_arrays),
        out_specs=out_specs,
        out_shape=out_shape,
        scratch_shapes=scratch,
        input_output_aliases={3 + len(wts) + j: j for j in range(len(tok_arrays))},
        compiler_params=pltpu.CompilerParams(
            dimension_semantics=("arbitrary",), vmem_limit_bytes=VMEM_LIMIT),
        name="mixer_sample",
    )(x_t, sconv_t, spool_t, *wts, *tok_arrays)


def _dispatch_kernel(pos_ref, hf_ref, xs_in_ref, xs_ref, sem, *, R):
    del xs_in_ref
    base = pl.program_id(0) * R

    def body(r, c):
        for k in range(2):
            dst = pos_ref[2 * (base + r) + k]
            pltpu.make_async_copy(hf_ref.at[pl.ds(r, 1)], xs_ref.at[pl.ds(dst, 1)], sem.at[0]).start(priority=k)
        return c

    lax.fori_loop(0, R, body, 0, unroll=8)
    for k in range(2):
        pltpu.make_async_copy(hf_ref, xs_ref.at[pl.ds(0, R)], sem.at[0]).wait()


def _dispatch(pos, hf, xs_zero, *, R):
    n_tok = hf.shape[0]
    assert n_tok % R == 0
    return pl.pallas_call(
        functools.partial(_dispatch_kernel, R=R),
        grid_spec=pltpu.PrefetchScalarGridSpec(
            num_scalar_prefetch=1,
            grid=(n_tok // R,),
            in_specs=[pl.BlockSpec((R,) + hf.shape[1:], lambda s, pos: (s, 0, 0)),
                      pl.BlockSpec(memory_space=pl.ANY)],
            out_specs=pl.BlockSpec(memory_space=pl.ANY),
            scratch_shapes=[pltpu.SemaphoreType.DMA((1,))],
        ),
        out_shape=jax.ShapeDtypeStruct(xs_zero.shape, xs_zero.dtype),
        input_output_aliases={2: 0},
        compiler_params=pltpu.CompilerParams(dimension_semantics=("arbitrary",)),
        name="dispatch",
    )(pos, hf, xs_zero)


def _invert_kernel(pos_ref, dst_in, dst_ref, sem, *, n_pairs, n_tok):
    init = pltpu.make_async_copy(dst_in, dst_ref, sem.at[0])
    init.start()
    init.wait()
    G = 8

    def put(b, c):
        n0 = b * G
        ps = [pos_ref[2 * n0 + q] for q in range(2 * G)]
        for q in range(2 * G):
            n, k = n0 + q // 2, q % 2
            dst_ref[ps[q]] = k * n_tok + n
        return c

    assert n_pairs % (2 * G) == 0
    lax.fori_loop(0, n_pairs // (2 * G), put, 0)


def _invert(pos, *, n_tiles, tm, n_tok):
    n_pairs = pos.shape[0]
    p = jnp.arange((n_tiles + 1) * tm, dtype=jnp.int32)
    tile = p // tm
    buf = jnp.where(tile == n_tiles, 1, tile % 2)
    dst_init = 2 * n_tok + buf * tm + p % tm
    smem = pl.BlockSpec(memory_space=pltpu.SMEM)
    return pl.pallas_call(
        functools.partial(_invert_kernel, n_pairs=n_pairs, n_tok=n_tok),
        in_specs=[smem, pl.BlockSpec(memory_space=pl.ANY)],
        out_specs=smem,
        out_shape=jax.ShapeDtypeStruct(dst_init.shape, jnp.int32),
        scratch_shapes=[pltpu.SemaphoreType.DMA((1,))],
        name="invert",
    )(pos, dst_init)


def _expert_mlp_kernel(te_ref, na_ref, nx_ref, dst_prev_ref, dst_cur_ref, xs_ref, wg_ref, wu_ref, wd_ref, o_ref,
                       ybuf0, ybuf1, wgf, wuf, wdf, wg_scr, wu_scr, wd_scr, wslot, ssem, wsem,
                       *, tm, n_tiles, d):
    i = pl.program_id(0)
    n_act = na_ref[0]
    active = i < n_act
    slot = i % 2
    last = n_tiles - 1
    prev = te_ref[jnp.maximum(i - 1, 0)]
    new_expert = (i == 0) | (te_ref[i] != prev)
    ybuf = (ybuf0, ybuf1)

    def scatter_start(dst_ref, b):
        for r in range(tm):
            pltpu.make_async_copy(ybuf[b].at[pl.ds(r, 1)], o_ref.at[pl.ds(dst_ref[0, 0, r], 1)],
                                  ssem.at[b]).start(priority=r % 2)

    def scatter_wait(b):
        pltpu.make_async_copy(ybuf[b], o_ref.at[pl.ds(0, tm)], ssem.at[b]).wait()

    def weight_copies(e, b):
        return [pltpu.make_async_copy(src.at[e], buf.at[b], wsem.at[b])
                for src, buf in ((wg_ref, wgf), (wu_ref, wuf), (wd_ref, wdf))]

    @pl.when(i == 0)
    def _():
        wslot[0] = 0
        for c in weight_copies(te_ref[0], 0):
            c.start()
        ybuf1[...] = jnp.zeros(ybuf1.shape, BF16)
        trash0 = pltpu.make_async_copy(ybuf1, o_ref.at[pl.ds(o_ref.shape[0] - 2 * tm, tm)], ssem.at[0])
        trash0.start()
        trash0.wait()

    @pl.when(active & new_expert)
    def _():
        b = wslot[0]
        nxt = nx_ref[i]

        @pl.when(nxt >= 0)
        def _():
            for c in weight_copies(nxt, 1 - b):
                c.start()

        for c in weight_copies(te_ref[i], b):
            c.wait()
        wg_scr[...] = wgf[b].astype(BF16)
        wu_scr[...] = wuf[b].astype(BF16)
        wd_scr[...] = wdf[b].astype(BF16)
        wslot[0] = 1 - b

    for par in range(2):
        is_par = slot == par

        @pl.when(is_par & (i >= 1) & (i - 2 < n_act))
        def _():
            scatter_wait(par)

        @pl.when(is_par & active)
        def _():
            x = xs_ref[...].reshape(tm, d)
            g = jnp.dot(x, wg_scr[...], preferred_element_type=F32)
            u = jnp.dot(x, wu_scr[...], preferred_element_type=F32)
            act = (g * _sigmoid(g) * u).astype(BF16)
            y = jnp.dot(act, wd_scr[...], preferred_element_type=F32)
            ybuf[par][...] = y.astype(BF16).reshape(tm, d // LANES, LANES)
            scatter_start(dst_prev_ref, 1 - par)

        @pl.when(is_par & (i == n_act))
        def _():
            scatter_start(dst_prev_ref, 1 - par)

        @pl.when(is_par & (i == last))
        def _():
            @pl.when(last - 1 < n_act)
            def _():
                scatter_wait(1 - par)

            @pl.when(last < n_act)
            def _():
                scatter_start(dst_cur_ref, par)
                scatter_wait(par)


def _expert_mlp(tile_expert, n_active, next_expert, dst, xs, w_gate, w_up, w_down, *, tm, n_tok):
    E, d, de = w_gate.shape
    n_tiles = xs.shape[0] // tm
    row = (tm, d // LANES, LANES)
    cur = lambda i, te, na, nx: (i, 0, 0)
    prv = lambda i, te, na, nx: (jnp.where(i == 0, n_tiles, i - 1), 0, 0)
    hbm = pl.BlockSpec(memory_space=pl.ANY)
    smem_tile = lambda imap: pl.BlockSpec((1, 1, tm), imap, memory_space=pltpu.SMEM)
    return pl.pallas_call(
        functools.partial(_expert_mlp_kernel, tm=tm, n_tiles=n_tiles, d=d),
        grid_spec=pltpu.PrefetchScalarGridSpec(
            num_scalar_prefetch=3,
            grid=(n_tiles,),
            in_specs=[
                smem_tile(prv), smem_tile(cur),
                pl.BlockSpec(row, lambda i, te, na, nx: (i, 0, 0)),
                hbm, hbm, hbm,
            ],
            out_specs=hbm,
            scratch_shapes=[pltpu.VMEM(row, BF16), pltpu.VMEM(row, BF16),
                            pltpu.VMEM((2, d, de), F32), pltpu.VMEM((2, d, de), F32), pltpu.VMEM((2, de, d), F32),
                            pltpu.VMEM((d, de), BF16), pltpu.VMEM((d, de), BF16), pltpu.VMEM((de, d), BF16),
                            pltpu.SMEM((1,), jnp.int32),
                            pltpu.SemaphoreType.DMA((2,)), pltpu.SemaphoreType.DMA((2,))],
        ),
        out_shape=jax.ShapeDtypeStruct((2 * n_tok + 2 * tm, d // LANES, LANES), BF16),
        compiler_params=pltpu.CompilerParams(
            dimension_semantics=("arbitrary",), vmem_limit_bytes=VMEM_LIMIT),
        name="expert_mlp",
    )(tile_expert, n_active, next_expert, dst, dst, xs, w_gate, w_up, w_down)


def _combine_kernel(x2_ref, o0_ref, o1_ref, w0_ref, w1_ref, gfin_ref, y_ref, *, R, d):
    C = TOK_CHUNK

    def body(i, c):
        rs = _rows(i, C)
        w0 = w0_ref[rs, :]
        w1 = w1_ref[rs, :]
        o0 = o0_ref[i].reshape(C, d).astype(F32)
        o1 = o1_ref[i].reshape(C, d).astype(F32)
        parts = []
        ssq = jnp.zeros((C, LANES), F32)
        for j in range(d // LANES):
            ls = slice(j * LANES, (j + 1) * LANES)
            m = w0 * o0[:, ls] + w1 * o1[:, ls]
            v = x2_ref[rs, ls] + m
            ssq = ssq + v * v
            parts.append(v)
        ms = jnp.sum(ssq, axis=-1, keepdims=True) * jnp.float32(1.0 / d)
        inv = lax.rsqrt(ms + EPS)
        for j in range(d // LANES):
            ls = slice(j * LANES, (j + 1) * LANES)
            y_ref[rs, ls] = parts[j] * inv * gfin_ref[:, ls]
        return c

    lax.fori_loop(0, R // C, body, 0)


def _combine(x2, o, w0, w1, g_final, *, R, tok0, n_out):
    n_tok, d = x2.shape
    n_steps = n_out // R
    b0 = tok0 // R
    b1 = n_tok // R
    tok = lambda s: (b0 + s, 0)
    orow = (R // TOK_CHUNK, TOK_CHUNK, d // LANES, LANES)
    o4 = o.reshape(o.shape[0] // TOK_CHUNK, TOK_CHUNK, d // LANES, LANES)
    return pl.pallas_call(
        functools.partial(_combine_kernel, R=R, d=d),
        grid=(n_steps,),
        in_specs=[
            pl.BlockSpec((R, d), tok),
            pl.BlockSpec(orow, lambda s: (b0 + s, 0, 0, 0)),
            pl.BlockSpec(orow, lambda s: (b1 + b0 + s, 0, 0, 0)),
            pl.BlockSpec((R, LANES), tok),
            pl.BlockSpec((R, LANES), tok),
            pl.BlockSpec((1, d), lambda s: (0, 0)),
        ],
        out_specs=pl.BlockSpec((R, d), lambda s: (s, 0)),
        out_shape=jax.ShapeDtypeStruct((n_out, d), F32),
        compiler_params=pltpu.CompilerParams(
            dimension_semantics=("arbitrary",), vmem_limit_bytes=VMEM_LIMIT),
        name="combine",
    )(x2, o4, o4, w0, w1, g_final)


def _routing_plan(ids, tm, n_tiles):
    e0 = ids[:, 0]
    e1 = ids[:, 1]
    ar = jnp.arange(N_EXPERTS, dtype=jnp.int32)
    m = ((e0[:, None] == ar) | (e1[:, None] == ar)).astype(jnp.int32)
    incl = jnp.cumsum(m, axis=0)
    excl = incl - m
    counts = incl[-1]
    tiles_e = (counts + tm - 1) // tm
    tile_end = jnp.cumsum(tiles_e)
    offs = (tile_end - tiles_e) * tm
    base = offs[None, :] + excl
    pos0 = jnp.take_along_axis(base, e0[:, None], axis=1)[:, 0]
    pos1 = jnp.take_along_axis(base, e1[:, None], axis=1)[:, 0]
    pos = jnp.stack([pos0, pos1], axis=1).reshape(-1).astype(jnp.int32)
    n_active = tile_end[-1].astype(jnp.int32)
    t = jnp.arange(n_tiles, dtype=jnp.int32)
    tq = jnp.minimum(t, n_active - 1)
    te = jnp.sum((tile_end[None, :] <= tq[:, None]).astype(jnp.int32), axis=1)
    te = jnp.minimum(te, N_EXPERTS - 1)
    later = (ar[None, :] > ar[:, None]) & (tiles_e[None, :] > 0)
    nxt_e = jnp.min(jnp.where(later, ar[None, :], N_EXPERTS), axis=1)
    nxt_e = jnp.where(nxt_e == N_EXPERTS, -1, nxt_e).astype(jnp.int32)
    return pos, te, n_active.reshape(1), nxt_e[te]


T_PROMPT = 256
BS_SAMPLE = 32
TM_EXPERT = 256
R_DISPATCH_CHOICES = (512, 256, 128)
R_COMBINE = 128


def kernel(x_prompt, x_sample, state_conv, state_pool, g_mix, w_in, w_dw, b_dw, ln_g, ln_b, w_pool, pool_scale, w_out, g_ffn, w_rg, b_rg, w_re, b_re, w_gate, w_up, w_down, g_final):
    depth = g_mix.shape[0]
    assert depth == 1
    B, S, d = x_prompt.shape
    Bd, TS, _ = x_sample.shape
    n_p = B * S
    n_s = Bd * TS
    N = n_p + n_s

    w_r = jnp.zeros((d, LANES), F32)
    w_r = w_r.at[:, 0:N_EXPERTS].set(w_re[0]).at[:, ROUTER_GROUP_LANE0:ROUTER_GROUP_LANE0 + N_GROUPS].set(w_rg[0])
    w_r_hi = w_r.astype(BF16)
    w_r_lo = (w_r - w_r_hi.astype(F32)).astype(BF16)
    b_r = jnp.zeros((1, LANES), F32)
    b_r = b_r.at[0, 0:N_EXPERTS].set(b_re[0]).at[0, ROUTER_GROUP_LANE0:ROUTER_GROUP_LANE0 + N_GROUPS].set(b_rg[0])
    wts = (g_mix[0][None], w_in[0].astype(BF16), w_dw[0], b_dw[0][None], ln_g[0][None], ln_b[0][None],
           w_pool[0].astype(BF16), pool_scale[0][None], w_out[0].astype(BF16), g_ffn[0][None],
           w_r_hi, w_r_lo, b_r)

    *tok_arrays, nconv_p, npool_p = _mixer_prompt(x_prompt, wts, T=T_PROMPT, n_total=N)

    x_t = jnp.transpose(x_sample, (1, 0, 2))
    sconv_t = jnp.transpose(state_conv[0], (1, 0, 2))
    spool_t = jnp.transpose(state_pool[0], (1, 0, 2))
    x2, hf, ids, w0, w1, nconv_t, npool_t = _mixer_sample(x_t, sconv_t, spool_t, wts, tok_arrays,
                                                          BS=BS_SAMPLE, tok0=n_p)

    tm = TM_EXPERT
    n_tiles = (2 * N + N_EXPERTS * (tm - 1) + tm - 1) // tm
    pos, tile_expert, n_active, next_expert = _routing_plan(ids[:, 0:2], tm, n_tiles)
    dst = _invert(pos, n_tiles=n_tiles, tm=tm, n_tok=N).reshape(n_tiles + 1, 1, tm)

    hf = hf.reshape(N, d // LANES, LANES)
    r_disp = max(r for r in R_DISPATCH_CHOICES if N % r == 0)
    xs = _dispatch(pos, hf, jnp.zeros((n_tiles * tm, d // LANES, LANES), BF16), R=r_disp)
    o = _expert_mlp(tile_expert, n_active, next_expert, dst, xs, w_gate[0], w_up[0], w_down[0], tm=tm, n_tok=N)

    gfin = g_final[None]
    y_p = _combine(x2, o, w0, w1, gfin, R=R_COMBINE, tok0=0, n_out=n_p)
    y_s = _combine(x2, o, w0, w1, gfin, R=R_COMBINE, tok0=n_p, n_out=n_s)

    y_prompt = y_p.reshape(B, S, d)
    y_sample = y_s.reshape(Bd // BS_SAMPLE, TS, BS_SAMPLE, d).transpose(0, 2, 1, 3).reshape(Bd, TS, d)
    new_conv_s = jnp.transpose(nconv_t, (1, 0, 2))[None]
    new_pool_s = jnp.transpose(npool_t, (1, 0, 2))[None]
    return (y_prompt, y_sample, nconv_p, new_conv_s, npool_p, new_pool_s)
```

```python
import functools

import jax
import jax.numpy as jnp
from jax import lax
from jax.experimental import pallas as pl
from jax.experimental.pallas import tpu as pltpu

F32 = jnp.float32
BF16 = jnp.bfloat16
EPS = 1e-6

LANES = 128
SUBLANES = 8
VMEM_LIMIT = 56 * 1024 * 1024

CONV_W = 31
POOL_WINDOWS = (2, 4, 8, 16)
POOL_MAX_W = 16
N_GROUPS = 4
PER_GROUP = 8
N_EXPERTS = N_GROUPS * PER_GROUP

ROUTER_GROUP_LANE0 = N_EXPERTS
TOK_CHUNK = 16
ROW_LOOP_UNROLL = 4
TAIL = 32
PTAIL = 16


def _rows(i, r):
    if isinstance(i, int):
        return pl.ds(i * r, r)
    return pl.ds(pl.multiple_of(i * r, r), r)


def _row_loop(n, body, *, static, unroll=1):
    if static:
        for i in range(n):
            body(i)
    else:
        def step(i, c):
            body(i)
            return c
        lax.fori_loop(0, n, step, 0, unroll=unroll)


def _sigmoid(x):
    return 1.0 / (1.0 + jnp.exp(-x))


def _rmsnorm_to_bf16(src_ref, g_ref, dst_ref, T, static=False):
    R = 16

    def body(i):
        rs = _rows(i, R)
        x = src_ref[rs, :]
        ms = jnp.mean(x * x, axis=-1, keepdims=True)
        dst_ref[rs, :] = (x * lax.rsqrt(ms + EPS) * g_ref[...]).astype(BF16)

    _row_loop(T // R, body, static=static, unroll=ROW_LOOP_UNROLL)


def _layernorm_swish(y_ref, lng_ref, lnb_ref, mix_ref, T, dc, static=False):
    R = 16

    def body(i):
        rs = _rows(i, R)
        y = y_ref[rs, :]
        mu = jnp.mean(y, axis=-1, keepdims=True)
        d = y - mu
        var = jnp.mean(d * d, axis=-1, keepdims=True)
        z = d * lax.rsqrt(var + EPS) * lng_ref[...] + lnb_ref[...]
        mix_ref[rs, 0:dc] = (z * _sigmoid(z)).astype(BF16)

    _row_loop(T // R, body, static=static, unroll=ROW_LOOP_UNROLL)


def _pool_project(sd_ref, wpool_ref, pscale_ref, mix_ref, dc, gd):
    for g in range(len(POOL_WINDOWS)):
        sl = slice(g * gd, (g + 1) * gd)
        o = jnp.dot(sd_ref[:, sl], wpool_ref[g], preferred_element_type=F32)
        mix_ref[:, dc + g * gd: dc + (g + 1) * gd] = (o * pscale_ref[:, sl]).astype(BF16)


def _out_proj_residual(x_ref, mix_ref, wout_ref, x2_ref):
    x2_ref[...] = x_ref[...] + jnp.dot(mix_ref[...], wout_ref[...], preferred_element_type=F32)


def _router(x2_ref, gffn_ref, wrh_ref, wrl_ref, br_ref, hf_ref, ids_ref, w0_ref, w1_ref,
            hhi_scr, hlo_scr, T, d):
    R = TOK_CHUNK

    def body(i, c):
        rs = _rows(i, R)
        x = x2_ref[rs, :]
        ms = jnp.mean(x * x, axis=-1, keepdims=True)
        h = x * lax.rsqrt(ms + EPS) * gffn_ref[...]
        hi = h.astype(BF16)
        hf_ref[i] = hi.reshape(R, d // LANES, LANES)
        hhi_scr[rs, :] = hi
        hlo_scr[rs, :] = (h - hi.astype(F32)).astype(BF16)
        return c

    lax.fori_loop(0, T // R, body, 0, unroll=ROW_LOOP_UNROLL)

    lg = (jnp.dot(hhi_scr[...], wrh_ref[...], preferred_element_type=F32)
          + jnp.dot(hlo_scr[...], wrh_ref[...], preferred_element_type=F32)
          + jnp.dot(hhi_scr[...], wrl_ref[...], preferred_element_type=F32)
          + br_ref[...])

    lane = lax.broadcasted_iota(jnp.int32, lg.shape, 1).astype(F32)
    neg = jnp.float32(-jnp.inf)
    big = jnp.float32(1e9)
    g_lo = jnp.float32(ROUTER_GROUP_LANE0)
    gmask = (lane >= g_lo) & (lane < g_lo + N_GROUPS)
    lgg = jnp.where(gmask, lg, neg)
    gmax = jnp.max(lgg, axis=-1, keepdims=True)
    gsel = jnp.min(jnp.where(lgg == gmax, lane, big), axis=-1, keepdims=True) - g_lo
    gsum = jnp.sum(jnp.where(gmask, jnp.exp(lg - gmax), 0.0), axis=-1, keepdims=True)
    p_g = 1.0 / gsum

    e_lo = gsel * PER_GROUP
    emask = (lane >= e_lo) & (lane < e_lo + PER_GROUP)
    le = jnp.where(emask, lg, neg)
    v0 = jnp.max(le, axis=-1, keepdims=True)
    i0 = jnp.min(jnp.where(le == v0, lane, big), axis=-1, keepdims=True)
    le2 = jnp.where(lane == i0, neg, le)
    v1 = jnp.max(le2, axis=-1, keepdims=True)
    i1 = jnp.min(jnp.where(le2 == v1, lane, big), axis=-1, keepdims=True)
    ex = jnp.exp(v1 - v0)
    den = 1.0 / (1.0 + ex)
    w0 = den * p_g
    w1 = ex * den * p_g

    ids_ref[...] = jnp.where(lane == 0.0, i0, jnp.where(lane == 1.0, i1, 0.0)).astype(jnp.int32)
    w0_ref[...] = jnp.broadcast_to(w0, lg.shape)
    w1_ref[...] = jnp.broadcast_to(w1, lg.shape)


def _mixer_prompt_kernel(xp_ref, xn_ref, gmix_ref, win_ref, wdw_ref, bdw_ref, lng_ref, lnb_ref, wpool_ref,
                         pscale_ref, wout_ref, gffn_ref, wrh_ref, wrl_ref, br_ref,
                         x2_ref, hf_ref, ids_ref, w0_ref, w1_ref, nconv_ref, npool_ref, xs_ref,
                         h_scr, pg0, pg1, ext0, ext1, extu0, extu1, yc_scr, sd0, sd1, mix0, mix1, hhi_scr, hlo_scr,
                         zbuf, zsem, *, T, d, dc, dp, n_batch):
    n_s = pl.num_programs(1)
    b = pl.program_id(0)
    s = pl.program_id(1)
    t = b * n_s + s
    gd = dp // len(POOL_WINDOWS)
    pg_scr, ext_scr, extu_scr, sd_scr, mix_scr = (pg0, pg1), (ext0, ext1), (extu0, extu1), (sd0, sd1), (mix0, mix1)

    def in_proj(x_ref, slot, static):
        _rmsnorm_to_bf16(x_ref, gmix_ref, h_scr, T, static=static)
        ext_scr[slot][TAIL:TAIL + T, :] = jnp.dot(h_scr[...], win_ref[:, 0:dc], preferred_element_type=F32)
        pg_scr[slot][...] = jnp.dot(h_scr[...], win_ref[:, dc:2 * dc], preferred_element_type=F32)
        extu_scr[slot][PTAIL:PTAIL + T, :] = jnp.dot(h_scr[...], win_ref[:, 2 * dc:2 * dc + dp],
                                                     preferred_element_type=F32)

    def zero_copies():
        zr = zbuf.shape[0]
        return [pltpu.make_async_copy(zbuf, xs_ref.at[pl.ds(j * zr, zr)], zsem.at[0])
                for j in range(xs_ref.shape[0] // zr)]

    def finish_prev(q):
        _pool_project(sd_scr[q], wpool_ref, pscale_ref, mix_scr[q], dc, gd)
        _out_proj_residual(xp_ref, mix_scr[q], wout_ref, x2_ref)

    @pl.when(t == 0)
    def _():
        ext0[0:TAIL, :] = jnp.zeros((TAIL, dc), F32)
        extu0[0:PTAIL, :] = jnp.zeros((PTAIL, dp), F32)
        sd1[...] = jnp.zeros(sd1.shape, BF16)
        mix1[...] = jnp.zeros(mix1.shape, BF16)
        zbuf[...] = jnp.zeros(zbuf.shape, BF16)
        for c in zero_copies():
            c.start()
        in_proj(xp_ref, 0, False)

    for par in range(2):
        @pl.when((b == n_batch) & (s == 0) & (t % 2 == par))
        def _():
            finish_prev(1 - par)

    @pl.when((b == n_batch) & (s == 0))
    def _():
        for c in zero_copies():
            c.wait()

    @pl.when((b == n_batch) & (s > 0))
    def _():
        x2_ref[...] = jnp.zeros(x2_ref.shape, F32)
        hf_ref[...] = jnp.zeros(hf_ref.shape, BF16)
        ids_ref[...] = jnp.zeros(ids_ref.shape, jnp.int32)
        w0_ref[...] = jnp.zeros(w0_ref.shape, F32)
        w1_ref[...] = jnp.zeros(w1_ref.shape, F32)

    def step(p):
        q = 1 - p
        in_proj(xn_ref, q, True)

        R = 16
        for i in range(T // R):
            es = pl.ds(i * R + TAIL, R)
            ext_scr[p][es, :] = ext_scr[p][es, :] * _sigmoid(pg_scr[p][pl.ds(i * R, R), :])

        RC, LC = 32, 128
        BR = RC + TAIL
        shift0 = TAIL - (CONV_W - 1)
        for i in range(T // RC):
            r0 = i * RC
            for lc in range(dc // LC):
                ls = slice(lc * LC, (lc + 1) * LC)
                blk = ext_scr[p][pl.ds(r0, BR), ls]
                acc = jnp.zeros((RC, LC), F32)
                for sft in range(SUBLANES):
                    taps = [k for k in range(CONV_W) if (k + shift0) % SUBLANES == sft]
                    if not taps:
                        continue
                    rolled = blk if sft == 0 else pltpu.roll(blk, BR - sft, 0)
                    for k in taps:
                        qq = (k + shift0) // SUBLANES
                        acc = acc + wdw_ref[k:k + 1, ls] * rolled[qq * SUBLANES:qq * SUBLANES + RC, :]
                yc_scr[pl.ds(r0, RC), ls] = acc + bdw_ref[:, ls]

        _layernorm_swish(yc_scr, lng_ref, lnb_ref, mix_scr[p], T, dc, static=True)

        RP = 32
        BP = RP + PTAIL
        pos_base = s * T
        for i in range(T // RP):
            r0 = i * RP
            pos = (pos_base + r0 + lax.broadcasted_iota(jnp.int32, (RP, gd), 0)).astype(F32)
            for g, w in enumerate(POOL_WINDOWS):
                ls = slice(g * gd, (g + 1) * gd)
                blk = extu_scr[p][pl.ds(r0, BP), ls]
                run = blk
                span = 1
                while span < w:
                    run = run + pltpu.roll(run, span, 0)
                    span *= 2
                cnt = jnp.minimum(pos + 1.0, jnp.float32(w))
                mean = run[PTAIL:PTAIL + RP, :] / cnt
                sd_scr[p][pl.ds(r0, RP), ls] = (mean - blk[PTAIL:PTAIL + RP, :]).astype(BF16)

        nconv_ref[...] = ext_scr[p][TAIL + T - (CONV_W - 1):TAIL + T, :]
        npool_ref[...] = extu_scr[p][PTAIL + T - (POOL_MAX_W - 1):PTAIL + T, :]
        keep = s != n_s - 1
        ext_scr[q][0:TAIL, :] = jnp.where(keep, ext_scr[p][T:T + TAIL, :], 0.0)
        extu_scr[q][0:PTAIL, :] = jnp.where(keep, extu_scr[p][T:T + PTAIL, :], 0.0)

        finish_prev(q)

    for par in range(2):
        pl.when((b < n_batch) & (t % 2 == par))(functools.partial(step, par))

    @pl.when((b < n_batch) | (s == 0))
    def _():
        _router(x2_ref, gffn_ref, wrh_ref, wrl_ref, br_ref, hf_ref, ids_ref, w0_ref, w1_ref,
                hhi_scr, hlo_scr, T, d)


def _mixer_sample_kernel(x_ref, sconv_ref, spool_ref, gmix_ref, win_ref, wdw_ref, bdw_ref, lng_ref,
                         lnb_ref, wpool_ref, pscale_ref, wout_ref, gffn_ref, wrh_ref, wrl_ref, br_ref,
                         x2_in, hf_in, ids_in, w0_in, w1_in,
                         x2_ref, hf_ref, ids_ref, w0_ref, w1_ref, nconv_ref, npool_ref,
                         xt_scr, h_scr, a_scr, pg_scr, u_scr, yc_scr, sd_scr, mix_scr, hhi_scr, hlo_scr,
                         *, TS, BS, d, dc, dp):
    del x2_in, hf_in, ids_in, w0_in, w1_in
    T = TS * BS
    gd = dp // len(POOL_WINDOWS)
    NH = CONV_W - 1
    NP = POOL_MAX_W - 1

    for t in range(TS):
        xt_scr[t * BS:(t + 1) * BS, :] = x_ref[t]

    _rmsnorm_to_bf16(xt_scr, gmix_ref, h_scr, T)
    a_scr[...] = jnp.dot(h_scr[...], win_ref[:, 0:dc], preferred_element_type=F32)
    pg_scr[...] = jnp.dot(h_scr[...], win_ref[:, dc:2 * dc], preferred_element_type=F32)
    u_scr[...] = jnp.dot(h_scr[...], win_ref[:, 2 * dc:2 * dc + dp], preferred_element_type=F32)

    R = 16

    def glu(i, c):
        rs = _rows(i, R)
        a_scr[rs, :] = a_scr[rs, :] * _sigmoid(pg_scr[rs, :])
        return c

    lax.fori_loop(0, T // R, glu, 0, unroll=ROW_LOOP_UNROLL)

    def ext_conv(j, rs, ls):
        if j < NH:
            return sconv_ref[j, rs, ls]
        return a_scr[pl.ds((j - NH) * BS + rs.start, rs.size), ls]

    def ext_pool(j, rs, ls):
        if j < NP:
            return spool_ref[j, rs, ls]
        return u_scr[pl.ds((j - NP) * BS + rs.start, rs.size), ls]

    RC, LC = 32, 256

    def conv(i, c):
        rs = _rows(i, RC)
        for t in range(TS):
            for lc in range(dc // LC):
                ls = slice(lc * LC, (lc + 1) * LC)
                acc = jnp.zeros((RC, LC), F32)
                for k in range(CONV_W):
                    acc = acc + wdw_ref[k:k + 1, ls] * ext_conv(t + k, rs, ls)
                yc_scr[pl.ds(t * BS + rs.start, RC), ls] = acc + bdw_ref[:, ls]
        return c

    lax.fori_loop(0, BS // RC, conv, 0)

    _layernorm_swish(yc_scr, lng_ref, lnb_ref, mix_scr, T, dc)

    def pool(i, c):
        rs = _rows(i, RC)
        for t in range(TS):
            for g, w in enumerate(POOL_WINDOWS):
                ls = slice(g * gd, (g + 1) * gd)
                tot = ext_pool(NP + t, rs, ls)
                cur = tot
                for back in range(1, w):
                    tot = tot + ext_pool(NP + t - back, rs, ls)
                sd_scr[pl.ds(t * BS + rs.start, RC), ls] = (tot / jnp.float32(w) - cur).astype(BF16)
        return c

    lax.fori_loop(0, BS // RC, pool, 0)

    _pool_project(sd_scr, wpool_ref, pscale_ref, mix_scr, dc, gd)

    for j in range(NH):
        src = j + TS
        nconv_ref[j] = sconv_ref[src] if src < NH else a_scr[(src - NH) * BS:(src - NH + 1) * BS, :]
    for j in range(NP):
        src = j + TS
        npool_ref[j] = spool_ref[src] if src < NP else u_scr[(src - NP) * BS:(src - NP + 1) * BS, :]

    _out_proj_residual(xt_scr, mix_scr, wout_ref, x2_ref)
    _router(x2_ref, gffn_ref, wrh_ref, wrl_ref, br_ref, hf_ref, ids_ref, w0_ref, w1_ref,
            hhi_scr, hlo_scr, T, d)


def _const_spec(shape):
    nd = len(shape)
    return pl.BlockSpec(shape, lambda *a: (0,) * nd, pipeline_mode=pl.Buffered(1))


def _mixer_weight_specs(d, dc, dp, cols):
    gd = dp // len(POOL_WINDOWS)
    return [
        _const_spec((1, d)),
        _const_spec((d, cols)),
        _const_spec((CONV_W, dc)),
        _const_spec((1, dc)),
        _const_spec((1, dc)),
        _const_spec((1, dc)),
        _const_spec((len(POOL_WINDOWS), gd, gd)),
        _const_spec((1, dp)),
        _const_spec((dc + dp, d)),
        _const_spec((1, d)),
        _const_spec((d, LANES)),
        _const_spec((d, LANES)),
        _const_spec((1, LANES)),
    ]


def _mixer_prompt(x, wts, *, T, n_total, n_sorted_rows, zero_rows):
    B, S, d = x.shape
    dc = wts[2].shape[1]
    dp = wts[7].shape[1]
    cols = wts[1].shape[1]
    n_s = S // T
    N = n_total
    n_blk = N // T
    assert N % T == 0 and N - B * S <= S
    blk = lambda b, s: jnp.clip(b * n_s + s - 1, 0, n_blk - 1)
    tok = lambda b, s: (blk(b, s), 0)
    tok4 = lambda b, s: (blk(b, s), 0, 0, 0)
    bclamp = lambda b: jnp.minimum(b, B - 1)

    def nxt(b, s):
        wrap = s + 1 == n_s
        return (jnp.where(wrap, bclamp(b + 1), bclamp(b)), jnp.where(wrap, 0, s + 1), 0)

    def prv(b, s):
        tp = jnp.clip(b * n_s + s - 1, 0, B * n_s - 1)
        return (tp // n_s, tp % n_s, 0)
    out_shape = (
        jax.ShapeDtypeStruct((N, d), F32),
        jax.ShapeDtypeStruct((N // TOK_CHUNK, TOK_CHUNK, d // LANES, LANES), BF16),
        jax.ShapeDtypeStruct((N, LANES), jnp.int32),
        jax.ShapeDtypeStruct((N, LANES), F32),
        jax.ShapeDtypeStruct((N, LANES), F32),
        jax.ShapeDtypeStruct((1, B, CONV_W - 1, dc), F32),
        jax.ShapeDtypeStruct((1, B, POOL_MAX_W - 1, dp), F32),
        jax.ShapeDtypeStruct((n_sorted_rows, d // LANES, LANES), BF16),
    )
    assert n_sorted_rows % zero_rows == 0
    out_specs = (
        pl.BlockSpec((T, d), tok),
        pl.BlockSpec((T // TOK_CHUNK, TOK_CHUNK, d // LANES, LANES), tok4),
        pl.BlockSpec((T, LANES), tok),
        pl.BlockSpec((T, LANES), tok),
        pl.BlockSpec((T, LANES), tok),
        pl.BlockSpec((None, None, CONV_W - 1, dc), lambda b, s: (0, bclamp(b), 0, 0)),
        pl.BlockSpec((None, None, POOL_MAX_W - 1, dp), lambda b, s: (0, bclamp(b), 0, 0)),
        pl.BlockSpec(memory_space=pl.ANY),
    )
    scratch = [
        pltpu.VMEM((T, d), BF16),
        pltpu.VMEM((T, dc), F32), pltpu.VMEM((T, dc), F32),
        pltpu.VMEM((T + TAIL, dc), F32), pltpu.VMEM((T + TAIL, dc), F32),
        pltpu.VMEM((T + PTAIL, dp), F32), pltpu.VMEM((T + PTAIL, dp), F32),
        pltpu.VMEM((T, dc), F32),
        pltpu.VMEM((T, dp), BF16), pltpu.VMEM((T, dp), BF16),
        pltpu.VMEM((T, dc + dp), BF16), pltpu.VMEM((T, dc + dp), BF16),
        pltpu.VMEM((T, d), BF16),
        pltpu.VMEM((T, d), BF16),
        pltpu.VMEM((zero_rows, d // LANES, LANES), BF16),
        pltpu.SemaphoreType.DMA((1,)),
    ]
    return pl.pallas_call(
        functools.partial(_mixer_prompt_kernel, T=T, d=d, dc=dc, dp=dp, n_batch=B),
        grid=(B + 1, n_s),
        in_specs=[pl.BlockSpec((None, T, d), prv),
                  pl.BlockSpec((None, T, d), nxt)]
                 + _mixer_weight_specs(d, dc, dp, cols),
        out_specs=out_specs,
        out_shape=out_shape,
        scratch_shapes=scratch,
        compiler_params=pltpu.CompilerParams(
            dimension_semantics=("arbitrary", "arbitrary"), vmem_limit_bytes=VMEM_LIMIT),
        name="mixer_prompt",
    )(x, x, *wts)


def _mixer_sample(x_t, sconv_t, spool_t, wts, tok_arrays, *, BS, tok0):
    TS, Bd, d = x_t.shape
    dc = wts[2].shape[1]
    dp = wts[7].shape[1]
    cols = wts[1].shape[1]
    T = TS * BS
    n_b = Bd // BS
    b0 = tok0 // T
    tok = lambda i: (b0 + i, 0)
    tok4 = lambda i: (b0 + i, 0, 0, 0)
    out_shape = tuple(jax.ShapeDtypeStruct(a.shape, a.dtype) for a in tok_arrays) + (
        jax.ShapeDtypeStruct((CONV_W - 1, Bd, dc), F32),
        jax.ShapeDtypeStruct((POOL_MAX_W - 1, Bd, dp), F32),
    )
    out_specs = (
        pl.BlockSpec((T, d), tok),
        pl.BlockSpec((T // TOK_CHUNK, TOK_CHUNK, d // LANES, LANES), tok4),
        pl.BlockSpec((T, LANES), tok),
        pl.BlockSpec((T, LANES), tok),
        pl.BlockSpec((T, LANES), tok),
        pl.BlockSpec((CONV_W - 1, BS, dc), lambda i: (0, i, 0)),
        pl.BlockSpec((POOL_MAX_W - 1, BS, dp), lambda i: (0, i, 0)),
    )
    scratch = [
        pltpu.VMEM((T, d), F32),
        pltpu.VMEM((T, d), BF16),
        pltpu.VMEM((T, dc), F32),
        pltpu.VMEM((T, dc), F32),
        pltpu.VMEM((T, dp), F32),
        pltpu.VMEM((T, dc), F32),
        pltpu.VMEM((T, dp), BF16),
        pltpu.VMEM((T, dc + dp), BF16),
        pltpu.VMEM((T, d), BF16),
        pltpu.VMEM((T, d), BF16),
    ]
    return pl.pallas_call(
        functools.partial(_mixer_sample_kernel, TS=TS, BS=BS, d=d, dc=dc, dp=dp),
        grid=(n_b,),
        in_specs=[pl.BlockSpec((TS, BS, d), lambda i: (0, i, 0)),
                  pl.BlockSpec((CONV_W - 1, BS, dc), lambda i: (0, i, 0), pipeline_mode=pl.Buffered(1)),
                  pl.BlockSpec((POOL_MAX_W - 1, BS, dp), lambda i: (0, i, 0), pipeline_mode=pl.Buffered(1))]
                 + _mixer_weight_specs(d, dc, dp, cols)
                 + [pl.BlockSpec(memory_space=pl.ANY)] * len(tok_arrays),
        out_specs=out_specs,
        out_shape=out_shape,
        scratch_shapes=scratch,
        input_output_aliases={3 + len(wts) + j: j for j in range(len(tok_arrays))},
        compiler_params=pltpu.CompilerParams(
            dimension_semantics=("arbitrary",), vmem_limit_bytes=VMEM_LIMIT),
        name="mixer_sample",
    )(x_t, sconv_t, spool_t, *wts, *tok_arrays)


def _dispatch_kernel(pos_ref, hf_ref, dst_in, xs_in_ref, xs_ref, dst_ref, sem, isem, *, R, n_tok):
    del xs_in_ref
    base = pl.program_id(0) * R

    @pl.when(pl.program_id(0) == 0)
    def _():
        init = pltpu.make_async_copy(dst_in, dst_ref, isem.at[0])
        init.start()
        init.wait()

    G = 8

    def body(g, c):
        r0 = g * G
        ps = [pos_ref[2 * (base + r0) + q] for q in range(2 * G)]
        for q in range(2 * G):
            r, k = r0 + q // 2, q % 2
            pltpu.make_async_copy(hf_ref.at[pl.ds(r, 1)], xs_ref.at[pl.ds(ps[q], 1)], sem.at[0]).start(priority=k)
            dst_ref[ps[q]] = k * n_tok + base + r
        return c

    lax.fori_loop(0, R // G, body, 0)
    for k in range(2):
        pltpu.make_async_copy(hf_ref, xs_ref.at[pl.ds(0, R)], sem.at[0]).wait()


def _dispatch(pos, hf, xs_zero, *, R, n_tiles, tm):
    n_tok = hf.shape[0]
    assert n_tok % R == 0
    p = jnp.arange((n_tiles + 1) * tm, dtype=jnp.int32)
    tile = p // tm
    buf = jnp.where(tile == n_tiles, 1, tile % 2)
    dst_init = 2 * n_tok + buf * tm + p % tm
    hbm = pl.BlockSpec(memory_space=pl.ANY)
    return pl.pallas_call(
        functools.partial(_dispatch_kernel, R=R, n_tok=n_tok),
        grid_spec=pltpu.PrefetchScalarGridSpec(
            num_scalar_prefetch=1,
            grid=(n_tok // R,),
            in_specs=[pl.BlockSpec((R,) + hf.shape[1:], lambda s, pos: (s, 0, 0)), hbm, hbm],
            out_specs=(hbm, pl.BlockSpec(memory_space=pltpu.SMEM)),
            scratch_shapes=[pltpu.SemaphoreType.DMA((1,)), pltpu.SemaphoreType.DMA((1,))],
        ),
        out_shape=(jax.ShapeDtypeStruct(xs_zero.shape, xs_zero.dtype),
                   jax.ShapeDtypeStruct(dst_init.shape, jnp.int32)),
        input_output_aliases={3: 0},
        compiler_params=pltpu.CompilerParams(dimension_semantics=("arbitrary",)),
        name="dispatch",
    )(pos, hf, dst_init, xs_zero)


def _expert_mlp_kernel(te_ref, na_ref, nx_ref, dst_prev_ref, dst_cur_ref, xs_ref, wg_ref, wu_ref, wd_ref, o_ref,
                       ybuf0, ybuf1, wgf, wuf, wdf, wg_scr, wu_scr, wd_scr, wslot, ssem, wsem,
                       *, tm, n_tiles, d):
    i = pl.program_id(0)
    n_act = na_ref[0]
    active = i < n_act
    slot = i % 2
    last = n_tiles - 1
    prev = te_ref[jnp.maximum(i - 1, 0)]
    new_expert = (i == 0) | (te_ref[i] != prev)
    ybuf = (ybuf0, ybuf1)

    def scatter_start(dst_ref, b):
        for r in range(tm):
            pltpu.make_async_copy(ybuf[b].at[pl.ds(r, 1)], o_ref.at[pl.ds(dst_ref[0, 0, r], 1)],
                                  ssem.at[b]).start(priority=r % 2)

    def scatter_wait(b):
        pltpu.make_async_copy(ybuf[b], o_ref.at[pl.ds(0, tm)], ssem.at[b]).wait()

    def weight_copies(e, b):
        return [pltpu.make_async_copy(src.at[e], buf.at[b], wsem.at[b])
                for src, buf in ((wg_ref, wgf), (wu_ref, wuf), (wd_ref, wdf))]

    @pl.when(i == 0)
    def _():
        wslot[0] = 0
        for c in weight_copies(te_ref[0], 0):
            c.start()
        ybuf1[...] = jnp.zeros(ybuf1.shape, BF16)
        trash0 = pltpu.make_async_copy(ybuf1, o_ref.at[pl.ds(o_ref.shape[0] - 2 * tm, tm)], ssem.at[0])
        trash0.start()
        trash0.wait()

    @pl.when(active & new_expert)
    def _():
        b = wslot[0]
        nxt = nx_ref[i]

        @pl.when(nxt >= 0)
        def _():
            for c in weight_copies(nxt, 1 - b):
                c.start()

        for c in weight_copies(te_ref[i], b):
            c.wait()
        wg_scr[...] = wgf[b].astype(BF16)
        wu_scr[...] = wuf[b].astype(BF16)
        wd_scr[...] = wdf[b].astype(BF16)
        wslot[0] = 1 - b

    for par in range(2):
        is_par = slot == par

        @pl.when(is_par & (i >= 1) & (i - 2 < n_act))
        def _():
            scatter_wait(par)

        @pl.when(is_par & active)
        def _():
            x = xs_ref[...].reshape(tm, d)
            g = jnp.dot(x, wg_scr[...], preferred_element_type=F32)
            u = jnp.dot(x, wu_scr[...], preferred_element_type=F32)
            act = (g * _sigmoid(g) * u).astype(BF16)
            y = jnp.dot(act, wd_scr[...], preferred_element_type=F32)
            ybuf[par][...] = y.astype(BF16).reshape(tm, d // LANES, LANES)
            scatter_start(dst_prev_ref, 1 - par)

        @pl.when(is_par & (i == n_act))
        def _():
            scatter_start(dst_prev_ref, 1 - par)

        @pl.when(is_par & (i == last))
        def _():
            @pl.when(last - 1 < n_act)
            def _():
                scatter_wait(1 - par)

            @pl.when(last < n_act)
            def _():
                scatter_start(dst_cur_ref, par)
                scatter_wait(par)


def _expert_mlp(tile_expert, n_active, next_expert, dst, xs, w_gate, w_up, w_down, *, tm, n_tok):
    E, d, de = w_gate.shape
    n_tiles = xs.shape[0] // tm
    row = (tm, d // LANES, LANES)
    cur = lambda i, te, na, nx: (i, 0, 0)
    prv = lambda i, te, na, nx: (jnp.where(i == 0, n_tiles, i - 1), 0, 0)
    hbm = pl.BlockSpec(memory_space=pl.ANY)
    smem_tile = lambda imap: pl.BlockSpec((1, 1, tm), imap, memory_space=pltpu.SMEM)
    return pl.pallas_call(
        functools.partial(_expert_mlp_kernel, tm=tm, n_tiles=n_tiles, d=d),
        grid_spec=pltpu.PrefetchScalarGridSpec(
            num_scalar_prefetch=3,
            grid=(n_tiles,),
            in_specs=[
                smem_tile(prv), smem_tile(cur),
                pl.BlockSpec(row, lambda i, te, na, nx: (i, 0, 0)),
                hbm, hbm, hbm,
            ],
            out_specs=hbm,
            scratch_shapes=[pltpu.VMEM(row, BF16), pltpu.VMEM(row, BF16),
                            pltpu.VMEM((2, d, de), F32), pltpu.VMEM((2, d, de), F32), pltpu.VMEM((2, de, d), F32),
                            pltpu.VMEM((d, de), BF16), pltpu.VMEM((d, de), BF16), pltpu.VMEM((de, d), BF16),
                            pltpu.SMEM((1,), jnp.int32),
                            pltpu.SemaphoreType.DMA((2,)), pltpu.SemaphoreType.DMA((2,))],
        ),
        out_shape=jax.ShapeDtypeStruct((2 * n_tok + 2 * tm, d // LANES, LANES), BF16),
        compiler_params=pltpu.CompilerParams(
            dimension_semantics=("arbitrary",), vmem_limit_bytes=VMEM_LIMIT),
        name="expert_mlp",
    )(tile_expert, n_active, next_expert, dst, dst, xs, w_gate, w_up, w_down)


def _combine_kernel(x2_ref, o0_ref, o1_ref, w0_ref, w1_ref, gfin_ref, y_ref, *, R, d):
    C = TOK_CHUNK

    def body(i, c):
        rs = _rows(i, C)
        w0 = w0_ref[rs, :]
        w1 = w1_ref[rs, :]
        o0 = o0_ref[i].reshape(C, d).astype(F32)
        o1 = o1_ref[i].reshape(C, d).astype(F32)
        parts = []
        ssq = jnp.zeros((C, LANES), F32)
        for j in range(d // LANES):
            ls = slice(j * LANES, (j + 1) * LANES)
            m = w0 * o0[:, ls] + w1 * o1[:, ls]
            v = x2_ref[rs, ls] + m
            ssq = ssq + v * v
            parts.append(v)
        ms = jnp.sum(ssq, axis=-1, keepdims=True) * jnp.float32(1.0 / d)
        inv = lax.rsqrt(ms + EPS)
        for j in range(d // LANES):
            ls = slice(j * LANES, (j + 1) * LANES)
            y_ref[rs, ls] = parts[j] * inv * gfin_ref[:, ls]
        return c

    lax.fori_loop(0, R // C, body, 0)


def _combine(x2, o, w0, w1, g_final, *, R, tok0, n_out):
    n_tok, d = x2.shape
    n_steps = n_out // R
    b0 = tok0 // R
    b1 = n_tok // R
    tok = lambda s: (b0 + s, 0)
    orow = (R // TOK_CHUNK, TOK_CHUNK, d // LANES, LANES)
    o4 = o.reshape(o.shape[0] // TOK_CHUNK, TOK_CHUNK, d // LANES, LANES)
    return pl.pallas_call(
        functools.partial(_combine_kernel, R=R, d=d),
        grid=(n_steps,),
        in_specs=[
            pl.BlockSpec((R, d), tok),
            pl.BlockSpec(orow, lambda s: (b0 + s, 0, 0, 0)),
            pl.BlockSpec(orow, lambda s: (b1 + b0 + s, 0, 0, 0)),
            pl.BlockSpec((R, LANES), tok),
            pl.BlockSpec((R, LANES), tok),
            pl.BlockSpec((1, d), lambda s: (0, 0)),
        ],
        out_specs=pl.BlockSpec((R, d), lambda s: (s, 0)),
        out_shape=jax.ShapeDtypeStruct((n_out, d), F32),
        compiler_params=pltpu.CompilerParams(
            dimension_semantics=("arbitrary",), vmem_limit_bytes=VMEM_LIMIT),
        name="combine",
    )(x2, o4, o4, w0, w1, g_final)


def _routing_plan(ids, tm, n_tiles):
    e0 = ids[:, 0]
    e1 = ids[:, 1]
    ar = jnp.arange(N_EXPERTS, dtype=jnp.int32)
    m = ((e0[:, None] == ar) | (e1[:, None] == ar)).astype(jnp.int32)
    incl = jnp.cumsum(m, axis=0)
    excl = incl - m
    counts = incl[-1]
    tiles_e = (counts + tm - 1) // tm
    tile_end = jnp.cumsum(tiles_e)
    offs = (tile_end - tiles_e) * tm
    base = offs[None, :] + excl
    pos0 = jnp.take_along_axis(base, e0[:, None], axis=1)[:, 0]
    pos1 = jnp.take_along_axis(base, e1[:, None], axis=1)[:, 0]
    pos = jnp.stack([pos0, pos1], axis=1).reshape(-1).astype(jnp.int32)
    n_active = tile_end[-1].astype(jnp.int32)
    t = jnp.arange(n_tiles, dtype=jnp.int32)
    tq = jnp.minimum(t, n_active - 1)
    te = jnp.sum((tile_end[None, :] <= tq[:, None]).astype(jnp.int32), axis=1)
    te = jnp.minimum(te, N_EXPERTS - 1)
    later = (ar[None, :] > ar[:, None]) & (tiles_e[None, :] > 0)
    nxt_e = jnp.min(jnp.where(later, ar[None, :], N_EXPERTS), axis=1)
    nxt_e = jnp.where(nxt_e == N_EXPERTS, -1, nxt_e).astype(jnp.int32)
    return pos, te, n_active.reshape(1), nxt_e[te]


T_PROMPT = 256
BS_SAMPLE = 32
TM_EXPERT = 256
R_DISPATCH_CHOICES = (512, 256, 128)
R_COMBINE = 256


def kernel(x_prompt, x_sample, state_conv, state_pool, g_mix, w_in, w_dw, b_dw, ln_g, ln_b, w_pool, pool_scale, w_out, g_ffn, w_rg, b_rg, w_re, b_re, w_gate, w_up, w_down, g_final):
    depth = g_mix.shape[0]
    assert depth == 1
    B, S, d = x_prompt.shape
    Bd, TS, _ = x_sample.shape
    n_p = B * S
    n_s = Bd * TS
    N = n_p + n_s

    w_r = jnp.zeros((d, LANES), F32)
    w_r = w_r.at[:, 0:N_EXPERTS].set(w_re[0]).at[:, ROUTER_GROUP_LANE0:ROUTER_GROUP_LANE0 + N_GROUPS].set(w_rg[0])
    w_r_hi = w_r.astype(BF16)
    w_r_lo = (w_r - w_r_hi.astype(F32)).astype(BF16)
    b_r = jnp.zeros((1, LANES), F32)
    b_r = b_r.at[0, 0:N_EXPERTS].set(b_re[0]).at[0, ROUTER_GROUP_LANE0:ROUTER_GROUP_LANE0 + N_GROUPS].set(b_rg[0])
    wts = (g_mix[0][None], w_in[0].astype(BF16), w_dw[0], b_dw[0][None], ln_g[0][None], ln_b[0][None],
           w_pool[0].astype(BF16), pool_scale[0][None], w_out[0].astype(BF16), g_ffn[0][None],
           w_r_hi, w_r_lo, b_r)

    tm = TM_EXPERT
    n_tiles = (2 * N + N_EXPERTS * (tm - 1) + tm - 1) // tm
    *tok_arrays, nconv_p, npool_p, xs = _mixer_prompt(x_prompt, wts, T=T_PROMPT, n_total=N,
                                                      n_sorted_rows=n_tiles * tm, zero_rows=tm)

    x_t = jnp.transpose(x_sample, (1, 0, 2))
    sconv_t = jnp.transpose(state_conv[0], (1, 0, 2))
    spool_t = jnp.transpose(state_pool[0], (1, 0, 2))
    x2, hf, ids, w0, w1, nconv_t, npool_t = _mixer_sample(x_t, sconv_t, spool_t, wts, tok_arrays,
                                                          BS=BS_SAMPLE, tok0=n_p)

    pos, tile_expert, n_active, next_expert = _routing_plan(ids[:, 0:2], tm, n_tiles)
    hf = hf.reshape(N, d // LANES, LANES)
    r_disp = max(r for r in R_DISPATCH_CHOICES if N % r == 0)
    xs, dst = _dispatch(pos, hf, xs, R=r_disp, n_tiles=n_tiles, tm=tm)
    dst = dst.reshape(n_tiles + 1, 1, tm)
    o = _expert_mlp(tile_expert, n_active, next_expert, dst, xs, w_gate[0], w_up[0], w_down[0], tm=tm, n_tok=N)

    gfin = g_final[None]
    y_p = _combine(x2, o, w0, w1, gfin, R=R_COMBINE, tok0=0, n_out=n_p)
    y_s = _combine(x2, o, w0, w1, gfin, R=R_COMBINE, tok0=n_p, n_out=n_s)

    y_prompt = y_p.reshape(B, S, d)
    y_sample = y_s.reshape(Bd // BS_SAMPLE, TS, BS_SAMPLE, d).transpose(0, 2, 1, 3).reshape(Bd, TS, d)
    new_conv_s = jnp.transpose(nconv_t, (1, 0, 2))[None]
    new_pool_s = jnp.transpose(npool_t, (1, 0, 2))[None]
    return (y_prompt, y_sample, nconv_p, new_conv_s, npool_p, new_pool_s)
```

```python
import functools

import jax
import jax.numpy as jnp
from jax import lax
from jax.experimental import pallas as pl
from jax.experimental.pallas import tpu as pltpu

F32 = jnp.float32
BF16 = jnp.bfloat16
EPS = 1e-6

LANES = 128
SUBLANES = 8
VMEM_LIMIT = 56 * 1024 * 1024

CONV_W = 31
POOL_WINDOWS = (2, 4, 8, 16)
POOL_MAX_W = 16
N_GROUPS = 4
PER_GROUP = 8
N_EXPERTS = N_GROUPS * PER_GROUP

ROUTER_GROUP_LANE0 = N_EXPERTS
TOK_CHUNK = 16
ROW_LOOP_UNROLL = 4
TAIL = 32
PTAIL = 16


def _rows(i, r):
    if isinstance(i, int):
        return pl.ds(i * r, r)
    return pl.ds(pl.multiple_of(i * r, r), r)


def _row_loop(n, body, *, static, unroll=1):
    if static:
        for i in range(n):
            body(i)
    else:
        def step(i, c):
            body(i)
            return c
        lax.fori_loop(0, n, step, 0, unroll=unroll)


def _sigmoid(x):
    return 1.0 / (1.0 + jnp.exp(-x))


def _rmsnorm_to_bf16(src_ref, g_ref, dst_ref, T, static=False):
    R = 16

    def body(i):
        rs = _rows(i, R)
        x = src_ref[rs, :]
        ms = jnp.mean(x * x, axis=-1, keepdims=True)
        dst_ref[rs, :] = (x * lax.rsqrt(ms + EPS) * g_ref[...]).astype(BF16)

    _row_loop(T // R, body, static=static, unroll=ROW_LOOP_UNROLL)


def _layernorm_swish(y_ref, lng_ref, lnb_ref, mix_ref, T, dc, static=False):
    R = 16

    def body(i):
        rs = _rows(i, R)
        y = y_ref[rs, :]
        mu = jnp.mean(y, axis=-1, keepdims=True)
        d = y - mu
        var = jnp.mean(d * d, axis=-1, keepdims=True)
        z = d * lax.rsqrt(var + EPS) * lng_ref[...] + lnb_ref[...]
        mix_ref[rs, 0:dc] = (z * _sigmoid(z)).astype(BF16)

    _row_loop(T // R, body, static=static, unroll=ROW_LOOP_UNROLL)


def _pool_project(sd_ref, wpool_ref, pscale_ref, mix_ref, dc, gd):
    for g in range(len(POOL_WINDOWS)):
        sl = slice(g * gd, (g + 1) * gd)
        o = jnp.dot(sd_ref[:, sl], wpool_ref[g], preferred_element_type=F32)
        mix_ref[:, dc + g * gd: dc + (g + 1) * gd] = (o * pscale_ref[:, sl]).astype(BF16)


def _out_proj_residual(x_ref, mix_ref, wout_ref, x2_ref):
    x2_ref[...] = x_ref[...] + jnp.dot(mix_ref[...], wout_ref[...], preferred_element_type=F32)


def _router(x2_ref, gffn_ref, wrh_ref, wrl_ref, br_ref, ltri_ref, hf_ref, ids_ref, w0_ref, w1_ref, cnt_ref,
            hhi_scr, hlo_scr, cnt_scr, T, d):
    R = TOK_CHUNK

    def body(i, c):
        rs = _rows(i, R)
        x = x2_ref[rs, :]
        ms = jnp.mean(x * x, axis=-1, keepdims=True)
        h = x * lax.rsqrt(ms + EPS) * gffn_ref[...]
        hi = h.astype(BF16)
        hf_ref[i] = hi.reshape(R, d // LANES, LANES)
        hhi_scr[rs, :] = hi
        hlo_scr[rs, :] = (h - hi.astype(F32)).astype(BF16)
        return c

    lax.fori_loop(0, T // R, body, 0, unroll=ROW_LOOP_UNROLL)

    lg = (jnp.dot(hhi_scr[...], wrh_ref[...], preferred_element_type=F32)
          + jnp.dot(hlo_scr[...], wrh_ref[...], preferred_element_type=F32)
          + jnp.dot(hhi_scr[...], wrl_ref[...], preferred_element_type=F32)
          + br_ref[...])

    lane = lax.broadcasted_iota(jnp.int32, lg.shape, 1).astype(F32)
    neg = jnp.float32(-jnp.inf)
    big = jnp.float32(1e9)
    g_lo = jnp.float32(ROUTER_GROUP_LANE0)
    gmask = (lane >= g_lo) & (lane < g_lo + N_GROUPS)
    lgg = jnp.where(gmask, lg, neg)
    gmax = jnp.max(lgg, axis=-1, keepdims=True)
    gsel = jnp.min(jnp.where(lgg == gmax, lane, big), axis=-1, keepdims=True) - g_lo
    gsum = jnp.sum(jnp.where(gmask, jnp.exp(lg - gmax), 0.0), axis=-1, keepdims=True)
    p_g = 1.0 / gsum

    e_lo = gsel * PER_GROUP
    emask = (lane >= e_lo) & (lane < e_lo + PER_GROUP)
    le = jnp.where(emask, lg, neg)
    v0 = jnp.max(le, axis=-1, keepdims=True)
    i0 = jnp.min(jnp.where(le == v0, lane, big), axis=-1, keepdims=True)
    le2 = jnp.where(lane == i0, neg, le)
    v1 = jnp.max(le2, axis=-1, keepdims=True)
    i1 = jnp.min(jnp.where(le2 == v1, lane, big), axis=-1, keepdims=True)
    ex = jnp.exp(v1 - v0)
    den = 1.0 / (1.0 + ex)
    w0 = den * p_g
    w1 = ex * den * p_g

    sel0 = lane == i0
    sel1 = lane == i1
    m = jnp.where(sel0 | sel1, 1.0, 0.0)
    before = jnp.dot(ltri_ref[...], m.astype(BF16), preferred_element_type=F32) + cnt_scr[0:1, :]
    rank0 = jnp.sum(jnp.where(sel0, before, 0.0), axis=-1, keepdims=True)
    rank1 = jnp.sum(jnp.where(sel1, before, 0.0), axis=-1, keepdims=True)
    total = cnt_scr[0:1, :] + jnp.sum(m, axis=0, keepdims=True)
    cnt_scr[...] = jnp.broadcast_to(total, cnt_scr.shape)
    cnt_ref[...] = jnp.broadcast_to(total, cnt_ref.shape)

    packed = jnp.where(lane == 0.0, i0, jnp.where(lane == 1.0, i1,
                                                  jnp.where(lane == 2.0, rank0, jnp.where(lane == 3.0, rank1, 0.0))))
    ids_ref[...] = packed.astype(jnp.int32)
    w0_ref[...] = jnp.broadcast_to(w0, lg.shape)
    w1_ref[...] = jnp.broadcast_to(w1, lg.shape)


def _mixer_prompt_kernel(xp_ref, xn_ref, gmix_ref, win_ref, wdw_ref, bdw_ref, lng_ref, lnb_ref, wpool_ref,
                         pscale_ref, wout_ref, gffn_ref, wrh_ref, wrl_ref, br_ref, ltri_ref,
                         x2_ref, hf_ref, ids_ref, w0_ref, w1_ref, nconv_ref, npool_ref, xs_ref, cnt_ref,
                         h_scr, pg0, pg1, ext0, ext1, extu0, extu1, yc_scr, sd0, sd1, mix0, mix1, hhi_scr, hlo_scr,
                         zbuf, cnt_scr, zsem, *, T, d, dc, dp, n_batch):
    n_s = pl.num_programs(1)
    b = pl.program_id(0)
    s = pl.program_id(1)
    t = b * n_s + s
    gd = dp // len(POOL_WINDOWS)
    pg_scr, ext_scr, extu_scr, sd_scr, mix_scr = (pg0, pg1), (ext0, ext1), (extu0, extu1), (sd0, sd1), (mix0, mix1)

    def in_proj(x_ref, slot, static):
        _rmsnorm_to_bf16(x_ref, gmix_ref, h_scr, T, static=static)
        ext_scr[slot][TAIL:TAIL + T, :] = jnp.dot(h_scr[...], win_ref[:, 0:dc], preferred_element_type=F32)
        pg_scr[slot][...] = jnp.dot(h_scr[...], win_ref[:, dc:2 * dc], preferred_element_type=F32)
        extu_scr[slot][PTAIL:PTAIL + T, :] = jnp.dot(h_scr[...], win_ref[:, 2 * dc:2 * dc + dp],
                                                     preferred_element_type=F32)

    def zero_copies():
        zr = zbuf.shape[0]
        return [pltpu.make_async_copy(zbuf, xs_ref.at[pl.ds(j * zr, zr)], zsem.at[0])
                for j in range(xs_ref.shape[0] // zr)]

    def finish_prev(q):
        _pool_project(sd_scr[q], wpool_ref, pscale_ref, mix_scr[q], dc, gd)
        _out_proj_residual(xp_ref, mix_scr[q], wout_ref, x2_ref)

    @pl.when(t == 0)
    def _():
        ext0[0:TAIL, :] = jnp.zeros((TAIL, dc), F32)
        extu0[0:PTAIL, :] = jnp.zeros((PTAIL, dp), F32)
        sd1[...] = jnp.zeros(sd1.shape, BF16)
        mix1[...] = jnp.zeros(mix1.shape, BF16)
        zbuf[...] = jnp.zeros(zbuf.shape, BF16)
        cnt_scr[...] = jnp.zeros(cnt_scr.shape, F32)
        for c in zero_copies():
            c.start()
        in_proj(xp_ref, 0, False)

    for par in range(2):
        @pl.when((b == n_batch) & (s == 0) & (t % 2 == par))
        def _():
            finish_prev(1 - par)

    @pl.when((b == n_batch) & (s == 0))
    def _():
        for c in zero_copies():
            c.wait()

    @pl.when((b == n_batch) & (s > 0))
    def _():
        x2_ref[...] = jnp.zeros(x2_ref.shape, F32)
        hf_ref[...] = jnp.zeros(hf_ref.shape, BF16)
        ids_ref[...] = jnp.zeros(ids_ref.shape, jnp.int32)
        w0_ref[...] = jnp.zeros(w0_ref.shape, F32)
        w1_ref[...] = jnp.zeros(w1_ref.shape, F32)

    def step(p):
        q = 1 - p
        in_proj(xn_ref, q, True)

        R = 16
        for i in range(T // R):
            es = pl.ds(i * R + TAIL, R)
            ext_scr[p][es, :] = ext_scr[p][es, :] * _sigmoid(pg_scr[p][pl.ds(i * R, R), :])

        RC, LC = 32, 128
        BR = RC + TAIL
        shift0 = TAIL - (CONV_W - 1)
        for i in range(T // RC):
            r0 = i * RC
            for lc in range(dc // LC):
                ls = slice(lc * LC, (lc + 1) * LC)
                blk = ext_scr[p][pl.ds(r0, BR), ls]
                acc = jnp.zeros((RC, LC), F32)
                for sft in range(SUBLANES):
                    taps = [k for k in range(CONV_W) if (k + shift0) % SUBLANES == sft]
                    if not taps:
                        continue
                    rolled = blk if sft == 0 else pltpu.roll(blk, BR - sft, 0)
                    for k in taps:
                        qq = (k + shift0) // SUBLANES
                        acc = acc + wdw_ref[k:k + 1, ls] * rolled[qq * SUBLANES:qq * SUBLANES + RC, :]
                yc_scr[pl.ds(r0, RC), ls] = acc + bdw_ref[:, ls]

        _layernorm_swish(yc_scr, lng_ref, lnb_ref, mix_scr[p], T, dc, static=True)

        RP = 32
        BP = RP + PTAIL
        pos_base = s * T
        for i in range(T // RP):
            r0 = i * RP
            pos = (pos_base + r0 + lax.broadcasted_iota(jnp.int32, (RP, gd), 0)).astype(F32)
            for g, w in enumerate(POOL_WINDOWS):
                ls = slice(g * gd, (g + 1) * gd)
                blk = extu_scr[p][pl.ds(r0, BP), ls]
                run = blk
                span = 1
                while span < w:
                    run = run + pltpu.roll(run, span, 0)
                    span *= 2
                cnt = jnp.minimum(pos + 1.0, jnp.float32(w))
                mean = run[PTAIL:PTAIL + RP, :] / cnt
                sd_scr[p][pl.ds(r0, RP), ls] = (mean - blk[PTAIL:PTAIL + RP, :]).astype(BF16)

        nconv_ref[...] = ext_scr[p][TAIL + T - (CONV_W - 1):TAIL + T, :]
        npool_ref[...] = extu_scr[p][PTAIL + T - (POOL_MAX_W - 1):PTAIL + T, :]
        keep = s != n_s - 1
        ext_scr[q][0:TAIL, :] = jnp.where(keep, ext_scr[p][T:T + TAIL, :], 0.0)
        extu_scr[q][0:PTAIL, :] = jnp.where(keep, extu_scr[p][T:T + PTAIL, :], 0.0)

        finish_prev(q)

    for par in range(2):
        pl.when((b < n_batch) & (t % 2 == par))(functools.partial(step, par))

    @pl.when(((b < n_batch) & (t > 0)) | ((b == n_batch) & (s == 0)))
    def _():
        _router(x2_ref, gffn_ref, wrh_ref, wrl_ref, br_ref, ltri_ref, hf_ref, ids_ref, w0_ref, w1_ref, cnt_ref,
                hhi_scr, hlo_scr, cnt_scr, T, d)


def _mixer_sample_kernel(x_ref, sconv_ref, spool_ref, gmix_ref, win_ref, wdw_ref, bdw_ref, lng_ref,
                         lnb_ref, wpool_ref, pscale_ref, wout_ref, gffn_ref, wrh_ref, wrl_ref, br_ref,
                         ltri_ref, cnt_in_ref,
                         x2_in, hf_in, ids_in, w0_in, w1_in,
                         x2_ref, hf_ref, ids_ref, w0_ref, w1_ref, nconv_ref, npool_ref, cnt_ref,
                         xt_scr, h_scr, a_scr, pg_scr, u_scr, yc_scr, sd_scr, mix_scr, hhi_scr, hlo_scr, cnt_scr,
                         *, TS, BS, d, dc, dp):
    del x2_in, hf_in, ids_in, w0_in, w1_in

    @pl.when(pl.program_id(0) == 0)
    def _():
        cnt_scr[...] = cnt_in_ref[...]

    T = TS * BS
    gd = dp // len(POOL_WINDOWS)
    NH = CONV_W - 1
    NP = POOL_MAX_W - 1

    for t in range(TS):
        xt_scr[t * BS:(t + 1) * BS, :] = x_ref[t]

    _rmsnorm_to_bf16(xt_scr, gmix_ref, h_scr, T)
    a_scr[...] = jnp.dot(h_scr[...], win_ref[:, 0:dc], preferred_element_type=F32)
    pg_scr[...] = jnp.dot(h_scr[...], win_ref[:, dc:2 * dc], preferred_element_type=F32)
    u_scr[...] = jnp.dot(h_scr[...], win_ref[:, 2 * dc:2 * dc + dp], preferred_element_type=F32)

    R = 16

    def glu(i, c):
        rs = _rows(i, R)
        a_scr[rs, :] = a_scr[rs, :] * _sigmoid(pg_scr[rs, :])
        return c

    lax.fori_loop(0, T // R, glu, 0, unroll=ROW_LOOP_UNROLL)

    def ext_conv(j, rs, ls):
        if j < NH:
            return sconv_ref[j, rs, ls]
        return a_scr[pl.ds((j - NH) * BS + rs.start, rs.size), ls]

    def ext_pool(j, rs, ls):
        if j < NP:
            return spool_ref[j, rs, ls]
        return u_scr[pl.ds((j - NP) * BS + rs.start, rs.size), ls]

    RC, LC = 32, 256

    def conv(i, c):
        rs = _rows(i, RC)
        for t in range(TS):
            for lc in range(dc // LC):
                ls = slice(lc * LC, (lc + 1) * LC)
                acc = jnp.zeros((RC, LC), F32)
                for k in range(CONV_W):
                    acc = acc + wdw_ref[k:k + 1, ls] * ext_conv(t + k, rs, ls)
                yc_scr[pl.ds(t * BS + rs.start, RC), ls] = acc + bdw_ref[:, ls]
        return c

    lax.fori_loop(0, BS // RC, conv, 0)

    _layernorm_swish(yc_scr, lng_ref, lnb_ref, mix_scr, T, dc)

    def pool(i, c):
        rs = _rows(i, RC)
        for t in range(TS):
            for g, w in enumerate(POOL_WINDOWS):
                ls = slice(g * gd, (g + 1) * gd)
                tot = ext_pool(NP + t, rs, ls)
                cur = tot
                for back in range(1, w):
                    tot = tot + ext_pool(NP + t - back, rs, ls)
                sd_scr[pl.ds(t * BS + rs.start, RC), ls] = (tot / jnp.float32(w) - cur).astype(BF16)
        return c

    lax.fori_loop(0, BS // RC, pool, 0)

    _pool_project(sd_scr, wpool_ref, pscale_ref, mix_scr, dc, gd)

    for j in range(NH):
        src = j + TS
        nconv_ref[j] = sconv_ref[src] if src < NH else a_scr[(src - NH) * BS:(src - NH + 1) * BS, :]
    for j in range(NP):
        src = j + TS
        npool_ref[j] = spool_ref[src] if src < NP else u_scr[(src - NP) * BS:(src - NP + 1) * BS, :]

    _out_proj_residual(xt_scr, mix_scr, wout_ref, x2_ref)
    _router(x2_ref, gffn_ref, wrh_ref, wrl_ref, br_ref, ltri_ref, hf_ref, ids_ref, w0_ref, w1_ref, cnt_ref,
            hhi_scr, hlo_scr, cnt_scr, T, d)


def _strict_lower_ones(n):
    return jnp.tril(jnp.ones((n, n), BF16), -1)


def _const_spec(shape):
    nd = len(shape)
    return pl.BlockSpec(shape, lambda *a: (0,) * nd, pipeline_mode=pl.Buffered(1))


def _mixer_weight_specs(d, dc, dp, cols):
    gd = dp // len(POOL_WINDOWS)
    return [
        _const_spec((1, d)),
        _const_spec((d, cols)),
        _const_spec((CONV_W, dc)),
        _const_spec((1, dc)),
        _const_spec((1, dc)),
        _const_spec((1, dc)),
        _const_spec((len(POOL_WINDOWS), gd, gd)),
        _const_spec((1, dp)),
        _const_spec((dc + dp, d)),
        _const_spec((1, d)),
        _const_spec((d, LANES)),
        _const_spec((d, LANES)),
        _const_spec((1, LANES)),
    ]


def _mixer_prompt(x, wts, *, T, n_total, n_sorted_rows, zero_rows):
    B, S, d = x.shape
    dc = wts[2].shape[1]
    dp = wts[7].shape[1]
    cols = wts[1].shape[1]
    n_s = S // T
    N = n_total
    n_blk = N // T
    assert N % T == 0 and N - B * S <= S
    blk = lambda b, s: jnp.clip(b * n_s + s - 1, 0, n_blk - 1)
    tok = lambda b, s: (blk(b, s), 0)
    tok4 = lambda b, s: (blk(b, s), 0, 0, 0)
    bclamp = lambda b: jnp.minimum(b, B - 1)

    def nxt(b, s):
        wrap = s + 1 == n_s
        return (jnp.where(wrap, bclamp(b + 1), bclamp(b)), jnp.where(wrap, 0, s + 1), 0)

    def prv(b, s):
        tp = jnp.clip(b * n_s + s - 1, 0, B * n_s - 1)
        return (tp // n_s, tp % n_s, 0)
    out_shape = (
        jax.ShapeDtypeStruct((N, d), F32),
        jax.ShapeDtypeStruct((N // TOK_CHUNK, TOK_CHUNK, d // LANES, LANES), BF16),
        jax.ShapeDtypeStruct((N, LANES), jnp.int32),
        jax.ShapeDtypeStruct((N, LANES), F32),
        jax.ShapeDtypeStruct((N, LANES), F32),
        jax.ShapeDtypeStruct((1, B, CONV_W - 1, dc), F32),
        jax.ShapeDtypeStruct((1, B, POOL_MAX_W - 1, dp), F32),
        jax.ShapeDtypeStruct((n_sorted_rows, d // LANES, LANES), BF16),
        jax.ShapeDtypeStruct((SUBLANES, LANES), F32),
    )
    assert n_sorted_rows % zero_rows == 0
    out_specs = (
        pl.BlockSpec((T, d), tok),
        pl.BlockSpec((T // TOK_CHUNK, TOK_CHUNK, d // LANES, LANES), tok4),
        pl.BlockSpec((T, LANES), tok),
        pl.BlockSpec((T, LANES), tok),
        pl.BlockSpec((T, LANES), tok),
        pl.BlockSpec((None, None, CONV_W - 1, dc), lambda b, s: (0, bclamp(b), 0, 0)),
        pl.BlockSpec((None, None, POOL_MAX_W - 1, dp), lambda b, s: (0, bclamp(b), 0, 0)),
        pl.BlockSpec(memory_space=pl.ANY),
        pl.BlockSpec((SUBLANES, LANES), lambda b, s: (0, 0)),
    )
    scratch = [
        pltpu.VMEM((T, d), BF16),
        pltpu.VMEM((T, dc), F32), pltpu.VMEM((T, dc), F32),
        pltpu.VMEM((T + TAIL, dc), F32), pltpu.VMEM((T + TAIL, dc), F32),
        pltpu.VMEM((T + PTAIL, dp), F32), pltpu.VMEM((T + PTAIL, dp), F32),
        pltpu.VMEM((T, dc), F32),
        pltpu.VMEM((T, dp), BF16), pltpu.VMEM((T, dp), BF16),
        pltpu.VMEM((T, dc + dp), BF16), pltpu.VMEM((T, dc + dp), BF16),
        pltpu.VMEM((T, d), BF16),
        pltpu.VMEM((T, d), BF16),
        pltpu.VMEM((zero_rows, d // LANES, LANES), BF16),
        pltpu.VMEM((SUBLANES, LANES), F32),
        pltpu.SemaphoreType.DMA((1,)),
    ]
    return pl.pallas_call(
        functools.partial(_mixer_prompt_kernel, T=T, d=d, dc=dc, dp=dp, n_batch=B),
        grid=(B + 1, n_s),
        in_specs=[pl.BlockSpec((None, T, d), prv),
                  pl.BlockSpec((None, T, d), nxt)]
                 + _mixer_weight_specs(d, dc, dp, cols) + [_const_spec((T, T))],
        out_specs=out_specs,
        out_shape=out_shape,
        scratch_shapes=scratch,
        compiler_params=pltpu.CompilerParams(
            dimension_semantics=("arbitrary", "arbitrary"), vmem_limit_bytes=VMEM_LIMIT),
        name="mixer_prompt",
    )(x, x, *wts, _strict_lower_ones(T))


def _mixer_sample(x_t, sconv_t, spool_t, wts, tok_arrays, counts, *, BS, tok0):
    TS, Bd, d = x_t.shape
    dc = wts[2].shape[1]
    dp = wts[7].shape[1]
    cols = wts[1].shape[1]
    T = TS * BS
    n_b = Bd // BS
    b0 = tok0 // T
    tok = lambda i: (b0 + i, 0)
    tok4 = lambda i: (b0 + i, 0, 0, 0)
    out_shape = tuple(jax.ShapeDtypeStruct(a.shape, a.dtype) for a in tok_arrays) + (
        jax.ShapeDtypeStruct((CONV_W - 1, Bd, dc), F32),
        jax.ShapeDtypeStruct((POOL_MAX_W - 1, Bd, dp), F32),
        jax.ShapeDtypeStruct(counts.shape, F32),
    )
    out_specs = (
        pl.BlockSpec((T, d), tok),
        pl.BlockSpec((T // TOK_CHUNK, TOK_CHUNK, d // LANES, LANES), tok4),
        pl.BlockSpec((T, LANES), tok),
        pl.BlockSpec((T, LANES), tok),
        pl.BlockSpec((T, LANES), tok),
        pl.BlockSpec((CONV_W - 1, BS, dc), lambda i: (0, i, 0)),
        pl.BlockSpec((POOL_MAX_W - 1, BS, dp), lambda i: (0, i, 0)),
        pl.BlockSpec(counts.shape, lambda i: (0, 0)),
    )
    scratch = [
        pltpu.VMEM((T, d), F32),
        pltpu.VMEM((T, d), BF16),
        pltpu.VMEM((T, dc), F32),
        pltpu.VMEM((T, dc), F32),
        pltpu.VMEM((T, dp), F32),
        pltpu.VMEM((T, dc), F32),
        pltpu.VMEM((T, dp), BF16),
        pltpu.VMEM((T, dc + dp), BF16),
        pltpu.VMEM((T, d), BF16),
        pltpu.VMEM((T, d), BF16),
        pltpu.VMEM(counts.shape, F32),
    ]
    return pl.pallas_call(
        functools.partial(_mixer_sample_kernel, TS=TS, BS=BS, d=d, dc=dc, dp=dp),
        grid=(n_b,),
        in_specs=[pl.BlockSpec((TS, BS, d), lambda i: (0, i, 0)),
                  pl.BlockSpec((CONV_W - 1, BS, dc), lambda i: (0, i, 0), pipeline_mode=pl.Buffered(1)),
                  pl.BlockSpec((POOL_MAX_W - 1, BS, dp), lambda i: (0, i, 0), pipeline_mode=pl.Buffered(1))]
                 + _mixer_weight_specs(d, dc, dp, cols)
                 + [_const_spec((T, T)), _const_spec(counts.shape)]
                 + [pl.BlockSpec(memory_space=pl.ANY)] * len(tok_arrays),
        out_specs=out_specs,
        out_shape=out_shape,
        scratch_shapes=scratch,
        input_output_aliases={5 + len(wts) + j: j for j in range(len(tok_arrays))},
        compiler_params=pltpu.CompilerParams(
            dimension_semantics=("arbitrary",), vmem_limit_bytes=VMEM_LIMIT),
        name="mixer_sample",
    )(x_t, sconv_t, spool_t, *wts, _strict_lower_ones(T), counts, *tok_arrays)


def _dispatch_kernel(pos_ref, hf_ref, dst_in, xs_in_ref, xs_ref, dst_ref, sem, isem, *, R, n_tok):
    del xs_in_ref
    base = pl.program_id(0) * R

    @pl.when(pl.program_id(0) == 0)
    def _():
        init = pltpu.make_async_copy(dst_in, dst_ref, isem.at[0])
        init.start()
        init.wait()

    G = 8

    def body(g, c):
        r0 = g * G
        ps = [pos_ref[2 * (base + r0) + q] for q in range(2 * G)]
        for q in range(2 * G):
            r, k = r0 + q // 2, q % 2
            pltpu.make_async_copy(hf_ref.at[pl.ds(r, 1)], xs_ref.at[pl.ds(ps[q], 1)], sem.at[0]).start(priority=k)
            dst_ref[ps[q]] = k * n_tok + base + r
        return c

    lax.fori_loop(0, R // G, body, 0)
    for k in range(2):
        pltpu.make_async_copy(hf_ref, xs_ref.at[pl.ds(0, R)], sem.at[0]).wait()


def _dispatch(pos, hf, xs_zero, *, R, n_tiles, tm):
    n_tok = hf.shape[0]
    assert n_tok % R == 0
    p = jnp.arange((n_tiles + 1) * tm, dtype=jnp.int32)
    tile = p // tm
    buf = jnp.where(tile == n_tiles, 1, tile % 2)
    dst_init = 2 * n_tok + buf * tm + p % tm
    hbm = pl.BlockSpec(memory_space=pl.ANY)
    return pl.pallas_call(
        functools.partial(_dispatch_kernel, R=R, n_tok=n_tok),
        grid_spec=pltpu.PrefetchScalarGridSpec(
            num_scalar_prefetch=1,
            grid=(n_tok // R,),
            in_specs=[pl.BlockSpec((R,) + hf.shape[1:], lambda s, pos: (s, 0, 0)), hbm, hbm],
            out_specs=(hbm, pl.BlockSpec(memory_space=pltpu.SMEM)),
            scratch_shapes=[pltpu.SemaphoreType.DMA((1,)), pltpu.SemaphoreType.DMA((1,))],
        ),
        out_shape=(jax.ShapeDtypeStruct(xs_zero.shape, xs_zero.dtype),
                   jax.ShapeDtypeStruct(dst_init.shape, jnp.int32)),
        input_output_aliases={3: 0},
        compiler_params=pltpu.CompilerParams(dimension_semantics=("arbitrary",)),
        name="dispatch",
    )(pos, hf, dst_init, xs_zero)


def _expert_mlp_kernel(te_ref, na_ref, nx_ref, dst_prev_ref, dst_cur_ref, xs_ref, wg_ref, wu_ref, wd_ref, o_ref,
                       ybuf0, ybuf1, wgf, wuf, wdf, wg_scr, wu_scr, wd_scr, wslot, ssem, wsem,
                       *, tm, n_tiles, d):
    i = pl.program_id(0)
    n_act = na_ref[0]
    active = i < n_act
    slot = i % 2
    last = n_tiles - 1
    prev = te_ref[jnp.maximum(i - 1, 0)]
    new_expert = (i == 0) | (te_ref[i] != prev)
    ybuf = (ybuf0, ybuf1)

    def scatter_start(dst_ref, b):
        for r in range(tm):
            pltpu.make_async_copy(ybuf[b].at[pl.ds(r, 1)], o_ref.at[pl.ds(dst_ref[0, 0, r], 1)],
                                  ssem.at[b]).start(priority=r % 2)

    def scatter_wait(b):
        pltpu.make_async_copy(ybuf[b], o_ref.at[pl.ds(0, tm)], ssem.at[b]).wait()

    def weight_copies(e, b):
        return [pltpu.make_async_copy(src.at[e], buf.at[b], wsem.at[b])
                for src, buf in ((wg_ref, wgf), (wu_ref, wuf), (wd_ref, wdf))]

    @pl.when(i == 0)
    def _():
        wslot[0] = 0
        for c in weight_copies(te_ref[0], 0):
            c.start()
        ybuf1[...] = jnp.zeros(ybuf1.shape, BF16)
        trash0 = pltpu.make_async_copy(ybuf1, o_ref.at[pl.ds(o_ref.shape[0] - 2 * tm, tm)], ssem.at[0])
        trash0.start()
        trash0.wait()

    @pl.when(active & new_expert)
    def _():
        b = wslot[0]
        nxt = nx_ref[i]

        @pl.when(nxt >= 0)
        def _():
            for c in weight_copies(nxt, 1 - b):
                c.start()

        for c in weight_copies(te_ref[i], b):
            c.wait()
        wg_scr[...] = wgf[b].astype(BF16)
        wu_scr[...] = wuf[b].astype(BF16)
        wd_scr[...] = wdf[b].astype(BF16)
        wslot[0] = 1 - b

    for par in range(2):
        is_par = slot == par

        @pl.when(is_par & (i >= 1) & (i - 2 < n_act))
        def _():
            scatter_wait(par)

        @pl.when(is_par & active)
        def _():
            x = xs_ref[...].reshape(tm, d)
            g = jnp.dot(x, wg_scr[...], preferred_element_type=F32)
            u = jnp.dot(x, wu_scr[...], preferred_element_type=F32)
            act = (g * _sigmoid(g) * u).astype(BF16)
            y = jnp.dot(act, wd_scr[...], preferred_element_type=F32)
            ybuf[par][...] = y.astype(BF16).reshape(tm, d // LANES, LANES)
            scatter_start(dst_prev_ref, 1 - par)

        @pl.when(is_par & (i == n_act))
        def _():
            scatter_start(dst_prev_ref, 1 - par)

        @pl.when(is_par & (i == last))
        def _():
            @pl.when(last - 1 < n_act)
            def _():
                scatter_wait(1 - par)

            @pl.when(last < n_act)
            def _():
                scatter_start(dst_cur_ref, par)
                scatter_wait(par)


def _expert_mlp(tile_expert, n_active, next_expert, dst, xs, w_gate, w_up, w_down, *, tm, n_tok):
    E, d, de = w_gate.shape
    n_tiles = xs.shape[0] // tm
    row = (tm, d // LANES, LANES)
    cur = lambda i, te, na, nx: (i, 0, 0)
    prv = lambda i, te, na, nx: (jnp.where(i == 0, n_tiles, i - 1), 0, 0)
    hbm = pl.BlockSpec(memory_space=pl.ANY)
    smem_tile = lambda imap: pl.BlockSpec((1, 1, tm), imap, memory_space=pltpu.SMEM)
    return pl.pallas_call(
        functools.partial(_expert_mlp_kernel, tm=tm, n_tiles=n_tiles, d=d),
        grid_spec=pltpu.PrefetchScalarGridSpec(
            num_scalar_prefetch=3,
            grid=(n_tiles,),
            in_specs=[
                smem_tile(prv), smem_tile(cur),
                pl.BlockSpec(row, lambda i, te, na, nx: (i, 0, 0)),
                hbm, hbm, hbm,
            ],
            out_specs=hbm,
            scratch_shapes=[pltpu.VMEM(row, BF16), pltpu.VMEM(row, BF16),
                            pltpu.VMEM((2, d, de), F32), pltpu.VMEM((2, d, de), F32), pltpu.VMEM((2, de, d), F32),
                            pltpu.VMEM((d, de), BF16), pltpu.VMEM((d, de), BF16), pltpu.VMEM((de, d), BF16),
                            pltpu.SMEM((1,), jnp.int32),
                            pltpu.SemaphoreType.DMA((2,)), pltpu.SemaphoreType.DMA((2,))],
        ),
        out_shape=jax.ShapeDtypeStruct((2 * n_tok + 2 * tm, d // LANES, LANES), BF16),
        compiler_params=pltpu.CompilerParams(
            dimension_semantics=("arbitrary",), vmem_limit_bytes=VMEM_LIMIT),
        name="expert_mlp",
    )(tile_expert, n_active, next_expert, dst, dst, xs, w_gate, w_up, w_down)


def _combine_kernel(x2_ref, o0_ref, o1_ref, w0_ref, w1_ref, gfin_ref, y_ref, *, R, d):
    C = TOK_CHUNK

    def body(i, c):
        rs = _rows(i, C)
        w0 = w0_ref[rs, :]
        w1 = w1_ref[rs, :]
        o0 = o0_ref[i].reshape(C, d).astype(F32)
        o1 = o1_ref[i].reshape(C, d).astype(F32)
        parts = []
        ssq = jnp.zeros((C, LANES), F32)
        for j in range(d // LANES):
            ls = slice(j * LANES, (j + 1) * LANES)
            m = w0 * o0[:, ls] + w1 * o1[:, ls]
            v = x2_ref[rs, ls] + m
            ssq = ssq + v * v
            parts.append(v)
        ms = jnp.sum(ssq, axis=-1, keepdims=True) * jnp.float32(1.0 / d)
        inv = lax.rsqrt(ms + EPS)
        for j in range(d // LANES):
            ls = slice(j * LANES, (j + 1) * LANES)
            y_ref[rs, ls] = parts[j] * inv * gfin_ref[:, ls]
        return c

    lax.fori_loop(0, R // C, body, 0)


def _combine(x2, o, w0, w1, g_final, *, R, tok0, n_out):
    n_tok, d = x2.shape
    n_steps = n_out // R
    b0 = tok0 // R
    b1 = n_tok // R
    tok = lambda s: (b0 + s, 0)
    orow = (R // TOK_CHUNK, TOK_CHUNK, d // LANES, LANES)
    o4 = o.reshape(o.shape[0] // TOK_CHUNK, TOK_CHUNK, d // LANES, LANES)
    return pl.pallas_call(
        functools.partial(_combine_kernel, R=R, d=d),
        grid=(n_steps,),
        in_specs=[
            pl.BlockSpec((R, d), tok),
            pl.BlockSpec(orow, lambda s: (b0 + s, 0, 0, 0)),
            pl.BlockSpec(orow, lambda s: (b1 + b0 + s, 0, 0, 0)),
            pl.BlockSpec((R, LANES), tok),
            pl.BlockSpec((R, LANES), tok),
            pl.BlockSpec((1, d), lambda s: (0, 0)),
        ],
        out_specs=pl.BlockSpec((R, d), lambda s: (s, 0)),
        out_shape=jax.ShapeDtypeStruct((n_out, d), F32),
        compiler_params=pltpu.CompilerParams(
            dimension_semantics=("arbitrary",), vmem_limit_bytes=VMEM_LIMIT),
        name="combine",
    )(x2, o4, o4, w0, w1, g_final)


def _routing_plan(ids, counts, tm, n_tiles):
    e0, e1, r0, r1 = ids[:, 0], ids[:, 1], ids[:, 2], ids[:, 3]
    ar = jnp.arange(N_EXPERTS, dtype=jnp.int32)
    tiles_e = (counts + tm - 1) // tm
    tile_end = jnp.cumsum(tiles_e)
    offs = (tile_end - tiles_e) * tm
    pos0 = jnp.sum(jnp.where(e0[:, None] == ar, offs[None, :], 0), axis=1) + r0
    pos1 = jnp.sum(jnp.where(e1[:, None] == ar, offs[None, :], 0), axis=1) + r1
    pos = jnp.stack([pos0, pos1], axis=1).reshape(-1).astype(jnp.int32)
    n_active = tile_end[-1].astype(jnp.int32)
    t = jnp.arange(n_tiles, dtype=jnp.int32)
    tq = jnp.minimum(t, n_active - 1)
    te = jnp.sum((tile_end[None, :] <= tq[:, None]).astype(jnp.int32), axis=1)
    te = jnp.minimum(te, N_EXPERTS - 1)
    later = (ar[None, :] > ar[:, None]) & (tiles_e[None, :] > 0)
    nxt_e = jnp.min(jnp.where(later, ar[None, :], N_EXPERTS), axis=1)
    nxt_e = jnp.where(nxt_e == N_EXPERTS, -1, nxt_e).astype(jnp.int32)
    return pos, te, n_active.reshape(1), nxt_e[te]


T_PROMPT = 256
BS_SAMPLE = 32
TM_EXPERT = 256
R_DISPATCH_CHOICES = (512, 256, 128)
R_COMBINE = 256


def kernel(x_prompt, x_sample, state_conv, state_pool, g_mix, w_in, w_dw, b_dw, ln_g, ln_b, w_pool, pool_scale, w_out, g_ffn, w_rg, b_rg, w_re, b_re, w_gate, w_up, w_down, g_final):
    depth = g_mix.shape[0]
    assert depth == 1
    B, S, d = x_prompt.shape
    Bd, TS, _ = x_sample.shape
    n_p = B * S
    n_s = Bd * TS
    N = n_p + n_s

    w_r = jnp.zeros((d, LANES), F32)
    w_r = w_r.at[:, 0:N_EXPERTS].set(w_re[0]).at[:, ROUTER_GROUP_LANE0:ROUTER_GROUP_LANE0 + N_GROUPS].set(w_rg[0])
    w_r_hi = w_r.astype(BF16)
    w_r_lo = (w_r - w_r_hi.astype(F32)).astype(BF16)
    b_r = jnp.zeros((1, LANES), F32)
    b_r = b_r.at[0, 0:N_EXPERTS].set(b_re[0]).at[0, ROUTER_GROUP_LANE0:ROUTER_GROUP_LANE0 + N_GROUPS].set(b_rg[0])
    wts = (g_mix[0][None], w_in[0].astype(BF16), w_dw[0], b_dw[0][None], ln_g[0][None], ln_b[0][None],
           w_pool[0].astype(BF16), pool_scale[0][None], w_out[0].astype(BF16), g_ffn[0][None],
           w_r_hi, w_r_lo, b_r)

    tm = TM_EXPERT
    n_tiles = (2 * N + N_EXPERTS * (tm - 1) + tm - 1) // tm
    *tok_arrays, nconv_p, npool_p, xs, counts_p = _mixer_prompt(x_prompt, wts, T=T_PROMPT, n_total=N,
                                                                n_sorted_rows=n_tiles * tm, zero_rows=tm)

    x_t = jnp.transpose(x_sample, (1, 0, 2))
    sconv_t = jnp.transpose(state_conv[0], (1, 0, 2))
    spool_t = jnp.transpose(state_pool[0], (1, 0, 2))
    x2, hf, ids, w0, w1, nconv_t, npool_t, counts = _mixer_sample(x_t, sconv_t, spool_t, wts, tok_arrays, counts_p,
                                                                  BS=BS_SAMPLE, tok0=n_p)

    pos, tile_expert, n_active, next_expert = _routing_plan(ids[:, 0:4], counts[0, 0:N_EXPERTS].astype(jnp.int32),
                                                            tm, n_tiles)
    hf = hf.reshape(N, d // LANES, LANES)
    r_disp = max(r for r in R_DISPATCH_CHOICES if N % r == 0)
    xs, dst = _dispatch(pos, hf, xs, R=r_disp, n_tiles=n_tiles, tm=tm)
    dst = dst.reshape(n_tiles + 1, 1, tm)
    o = _expert_mlp(tile_expert, n_active, next_expert, dst, xs, w_gate[0], w_up[0], w_down[0], tm=tm, n_tok=N)

    gfin = g_final[None]
    y_p = _combine(x2, o, w0, w1, gfin, R=R_COMBINE, tok0=0, n_out=n_p)
    y_s = _combine(x2, o, w0, w1, gfin, R=R_COMBINE, tok0=n_p, n_out=n_s)

    y_prompt = y_p.reshape(B, S, d)
    y_sample = y_s.reshape(Bd // BS_SAMPLE, TS, BS_SAMPLE, d).transpose(0, 2, 1, 3).reshape(Bd, TS, d)
    new_conv_s = jnp.transpose(nconv_t, (1, 0, 2))[None]
    new_pool_s = jnp.transpose(npool_t, (1, 0, 2))[None]
    return (y_prompt, y_sample, nconv_p, new_conv_s, npool_p, new_pool_s)
```

```python
import functools

import jax
import jax.numpy as jnp
from jax import lax
from jax.experimental import pallas as pl
from jax.experimental.pallas import tpu as pltpu

F32 = jnp.float32
BF16 = jnp.bfloat16
EPS = 1e-6

LANES = 128
SUBLANES = 8
VMEM_LIMIT = 56 * 1024 * 1024

CONV_W = 31
POOL_WINDOWS = (2, 4, 8, 16)
POOL_MAX_W = 16
N_GROUPS = 4
PER_GROUP = 8
N_EXPERTS = N_GROUPS * PER_GROUP

ROUTER_GROUP_LANE0 = N_EXPERTS
TOK_CHUNK = 16
ROW_LOOP_UNROLL = 4
TAIL = 32
PTAIL = 16


def _rows(i, r):
    if isinstance(i, int):
        return pl.ds(i * r, r)
    return pl.ds(pl.multiple_of(i * r, r), r)


def _row_loop(n, body, *, static, unroll=1):
    if static:
        for i in range(n):
            body(i)
    else:
        def step(i, c):
            body(i)
            return c
        lax.fori_loop(0, n, step, 0, unroll=unroll)


def _sigmoid(x):
    return 1.0 / (1.0 + jnp.exp(-x))


def _rmsnorm_to_bf16(src_ref, g_ref, dst_ref, T, static=False):
    R = 16

    def body(i):
        rs = _rows(i, R)
        x = src_ref[rs, :]
        ms = jnp.mean(x * x, axis=-1, keepdims=True)
        dst_ref[rs, :] = (x * lax.rsqrt(ms + EPS) * g_ref[...]).astype(BF16)

    _row_loop(T // R, body, static=static, unroll=ROW_LOOP_UNROLL)


def _layernorm_swish(y_ref, lng_ref, lnb_ref, mix_ref, T, dc, static=False):
    R = 16

    def body(i):
        rs = _rows(i, R)
        y = y_ref[rs, :]
        mu = jnp.mean(y, axis=-1, keepdims=True)
        d = y - mu
        var = jnp.mean(d * d, axis=-1, keepdims=True)
        z = d * lax.rsqrt(var + EPS) * lng_ref[...] + lnb_ref[...]
        mix_ref[rs, 0:dc] = (z * _sigmoid(z)).astype(BF16)

    _row_loop(T // R, body, static=static, unroll=ROW_LOOP_UNROLL)


def _pool_project(sd_ref, wpool_ref, pscale_ref, mix_ref, dc, gd):
    for g in range(len(POOL_WINDOWS)):
        sl = slice(g * gd, (g + 1) * gd)
        o = jnp.dot(sd_ref[:, sl], wpool_ref[g], preferred_element_type=F32)
        mix_ref[:, dc + g * gd: dc + (g + 1) * gd] = (o * pscale_ref[:, sl]).astype(BF16)


def _out_proj_residual(x_ref, mix_ref, wout_ref, x2_ref):
    x2_ref[...] = x_ref[...] + jnp.dot(mix_ref[...], wout_ref[...], preferred_element_type=F32)


def _router(x2_ref, gffn_ref, wrh_ref, wrl_ref, br_ref, ltri_ref, hf_ref, ids_ref, w0_ref, w1_ref, cnt_ref,
            hhi_scr, hlo_scr, cnt_scr, T, d):
    R = TOK_CHUNK

    def body(i, c):
        rs = _rows(i, R)
        x = x2_ref[rs, :]
        ms = jnp.mean(x * x, axis=-1, keepdims=True)
        h = x * lax.rsqrt(ms + EPS) * gffn_ref[...]
        hi = h.astype(BF16)
        hf_ref[i] = hi.reshape(R, d // LANES, LANES)
        hhi_scr[rs, :] = hi
        hlo_scr[rs, :] = (h - hi.astype(F32)).astype(BF16)
        return c

    lax.fori_loop(0, T // R, body, 0, unroll=ROW_LOOP_UNROLL)

    lg = (jnp.dot(hhi_scr[...], wrh_ref[...], preferred_element_type=F32)
          + jnp.dot(hlo_scr[...], wrh_ref[...], preferred_element_type=F32)
          + jnp.dot(hhi_scr[...], wrl_ref[...], preferred_element_type=F32)
          + br_ref[...])

    lane = lax.broadcasted_iota(jnp.int32, lg.shape, 1).astype(F32)
    neg = jnp.float32(-jnp.inf)
    big = jnp.float32(1e9)
    g_lo = jnp.float32(ROUTER_GROUP_LANE0)
    gmask = (lane >= g_lo) & (lane < g_lo + N_GROUPS)
    lgg = jnp.where(gmask, lg, neg)
    gmax = jnp.max(lgg, axis=-1, keepdims=True)
    gsel = jnp.min(jnp.where(lgg == gmax, lane, big), axis=-1, keepdims=True) - g_lo
    gsum = jnp.sum(jnp.where(gmask, jnp.exp(lg - gmax), 0.0), axis=-1, keepdims=True)
    p_g = 1.0 / gsum

    e_lo = gsel * PER_GROUP
    emask = (lane >= e_lo) & (lane < e_lo + PER_GROUP)
    le = jnp.where(emask, lg, neg)
    v0 = jnp.max(le, axis=-1, keepdims=True)
    i0 = jnp.min(jnp.where(le == v0, lane, big), axis=-1, keepdims=True)
    le2 = jnp.where(lane == i0, neg, le)
    v1 = jnp.max(le2, axis=-1, keepdims=True)
    i1 = jnp.min(jnp.where(le2 == v1, lane, big), axis=-1, keepdims=True)
    ex = jnp.exp(v1 - v0)
    den = 1.0 / (1.0 + ex)
    w0 = den * p_g
    w1 = ex * den * p_g

    sel0 = lane == i0
    sel1 = lane == i1
    m = jnp.where(sel0 | sel1, 1.0, 0.0)
    before = jnp.dot(ltri_ref[...], m.astype(BF16), preferred_element_type=F32) + cnt_scr[0:1, :]
    rank0 = jnp.sum(jnp.where(sel0, before, 0.0), axis=-1, keepdims=True)
    rank1 = jnp.sum(jnp.where(sel1, before, 0.0), axis=-1, keepdims=True)
    total = cnt_scr[0:1, :] + jnp.sum(m, axis=0, keepdims=True)
    cnt_scr[...] = jnp.broadcast_to(total, cnt_scr.shape)
    cnt_ref[...] = jnp.broadcast_to(total, cnt_ref.shape)

    packed = jnp.where(lane == 0.0, i0, jnp.where(lane == 1.0, i1,
                                                  jnp.where(lane == 2.0, rank0, jnp.where(lane == 3.0, rank1, 0.0))))
    ids_ref[...] = packed.astype(jnp.int32)
    w0_ref[...] = jnp.broadcast_to(w0, lg.shape)
    w1_ref[...] = jnp.broadcast_to(w1, lg.shape)


def _mixer_prompt_kernel(xp_ref, xn_ref, gmix_ref, win_ref, wdw_ref, bdw_ref, lng_ref, lnb_ref, wpool_ref,
                         pscale_ref, wout_ref, gffn_ref, wrh_ref, wrl_ref, br_ref, ltri_ref,
                         x2_ref, hf_ref, ids_ref, w0_ref, w1_ref, nconv_ref, npool_ref, xs_ref, cnt_ref,
                         h_scr, pg0, pg1, ext0, ext1, extu0, extu1, yc_scr, sd0, sd1, mix0, mix1, hhi_scr, hlo_scr,
                         zbuf, cnt_scr, zsem, *, T, d, dc, dp, n_batch):
    n_s = pl.num_programs(1)
    b = pl.program_id(0)
    s = pl.program_id(1)
    t = b * n_s + s
    gd = dp // len(POOL_WINDOWS)
    pg_scr, ext_scr, extu_scr, sd_scr, mix_scr = (pg0, pg1), (ext0, ext1), (extu0, extu1), (sd0, sd1), (mix0, mix1)

    def in_proj(x_ref, slot, static):
        _rmsnorm_to_bf16(x_ref, gmix_ref, h_scr, T, static=static)
        ext_scr[slot][TAIL:TAIL + T, :] = jnp.dot(h_scr[...], win_ref[:, 0:dc], preferred_element_type=F32)
        pg_scr[slot][...] = jnp.dot(h_scr[...], win_ref[:, dc:2 * dc], preferred_element_type=F32)
        extu_scr[slot][PTAIL:PTAIL + T, :] = jnp.dot(h_scr[...], win_ref[:, 2 * dc:2 * dc + dp],
                                                     preferred_element_type=F32)

    def zero_copies():
        zr = zbuf.shape[0]
        return [pltpu.make_async_copy(zbuf, xs_ref.at[pl.ds(j * zr, zr)], zsem.at[0])
                for j in range(xs_ref.shape[0] // zr)]

    def finish_prev(q):
        _pool_project(sd_scr[q], wpool_ref, pscale_ref, mix_scr[q], dc, gd)
        _out_proj_residual(xp_ref, mix_scr[q], wout_ref, x2_ref)

    @pl.when(t == 0)
    def _():
        ext0[0:TAIL, :] = jnp.zeros((TAIL, dc), F32)
        extu0[0:PTAIL, :] = jnp.zeros((PTAIL, dp), F32)
        sd1[...] = jnp.zeros(sd1.shape, BF16)
        mix1[...] = jnp.zeros(mix1.shape, BF16)
        zbuf[...] = jnp.zeros(zbuf.shape, BF16)
        cnt_scr[...] = jnp.zeros(cnt_scr.shape, F32)
        for c in zero_copies():
            c.start()
        in_proj(xp_ref, 0, False)

    for par in range(2):
        @pl.when((b == n_batch) & (s == 0) & (t % 2 == par))
        def _():
            finish_prev(1 - par)

    @pl.when((b == n_batch) & (s == 0))
    def _():
        for c in zero_copies():
            c.wait()

    @pl.when((b == n_batch) & (s > 0))
    def _():
        x2_ref[...] = jnp.zeros(x2_ref.shape, F32)
        hf_ref[...] = jnp.zeros(hf_ref.shape, BF16)
        ids_ref[...] = jnp.zeros(ids_ref.shape, jnp.int32)
        w0_ref[...] = jnp.zeros(w0_ref.shape, F32)
        w1_ref[...] = jnp.zeros(w1_ref.shape, F32)

    def step(p):
        q = 1 - p
        in_proj(xn_ref, q, True)

        R = 16
        for i in range(T // R):
            es = pl.ds(i * R + TAIL, R)
            ext_scr[p][es, :] = ext_scr[p][es, :] * _sigmoid(pg_scr[p][pl.ds(i * R, R), :])

        RC, LC = 32, 128
        BR = RC + TAIL
        shift0 = TAIL - (CONV_W - 1)
        for i in range(T // RC):
            r0 = i * RC
            for lc in range(dc // LC):
                ls = slice(lc * LC, (lc + 1) * LC)
                blk = ext_scr[p][pl.ds(r0, BR), ls]
                acc = jnp.zeros((RC, LC), F32)
                for sft in range(SUBLANES):
                    taps = [k for k in range(CONV_W) if (k + shift0) % SUBLANES == sft]
                    if not taps:
                        continue
                    rolled = blk if sft == 0 else pltpu.roll(blk, BR - sft, 0)
                    for k in taps:
                        qq = (k + shift0) // SUBLANES
                        acc = acc + wdw_ref[k:k + 1, ls] * rolled[qq * SUBLANES:qq * SUBLANES + RC, :]
                yc_scr[pl.ds(r0, RC), ls] = acc + bdw_ref[:, ls]

        _layernorm_swish(yc_scr, lng_ref, lnb_ref, mix_scr[p], T, dc, static=True)

        RP = 32
        BP = RP + PTAIL
        pos_base = s * T
        for i in range(T // RP):
            r0 = i * RP
            pos = (pos_base + r0 + lax.broadcasted_iota(jnp.int32, (RP, gd), 0)).astype(F32)
            for g, w in enumerate(POOL_WINDOWS):
                ls = slice(g * gd, (g + 1) * gd)
                blk = extu_scr[p][pl.ds(r0, BP), ls]
                run = blk
                span = 1
                while span < w:
                    run = run + pltpu.roll(run, span, 0)
                    span *= 2
                cnt = jnp.minimum(pos + 1.0, jnp.float32(w))
                mean = run[PTAIL:PTAIL + RP, :] / cnt
                sd_scr[p][pl.ds(r0, RP), ls] = (mean - blk[PTAIL:PTAIL + RP, :]).astype(BF16)

        nconv_ref[...] = ext_scr[p][TAIL + T - (CONV_W - 1):TAIL + T, :]
        npool_ref[...] = extu_scr[p][PTAIL + T - (POOL_MAX_W - 1):PTAIL + T, :]
        keep = s != n_s - 1
        ext_scr[q][0:TAIL, :] = jnp.where(keep, ext_scr[p][T:T + TAIL, :], 0.0)
        extu_scr[q][0:PTAIL, :] = jnp.where(keep, extu_scr[p][T:T + PTAIL, :], 0.0)

        finish_prev(q)

    for par in range(2):
        pl.when((b < n_batch) & (t % 2 == par))(functools.partial(step, par))

    @pl.when(((b < n_batch) & (t > 0)) | ((b == n_batch) & (s == 0)))
    def _():
        _router(x2_ref, gffn_ref, wrh_ref, wrl_ref, br_ref, ltri_ref, hf_ref, ids_ref, w0_ref, w1_ref, cnt_ref,
                hhi_scr, hlo_scr, cnt_scr, T, d)


def _mixer_sample_kernel(x_ref, sconv_ref, spool_ref, gmix_ref, win_ref, wdw_ref, bdw_ref, lng_ref,
                         lnb_ref, wpool_ref, pscale_ref, wout_ref, gffn_ref, wrh_ref, wrl_ref, br_ref,
                         ltri_ref, cnt_in_ref,
                         x2_in, hf_in, ids_in, w0_in, w1_in,
                         x2_ref, hf_ref, ids_ref, w0_ref, w1_ref, nconv_ref, npool_ref, cnt_ref,
                         xt_scr, h_scr, a_scr, pg_scr, u_scr, yc_scr, sd_scr, mix_scr, hhi_scr, hlo_scr, cnt_scr,
                         *, TS, BS, d, dc, dp):
    del x2_in, hf_in, ids_in, w0_in, w1_in

    @pl.when(pl.program_id(0) == 0)
    def _():
        cnt_scr[...] = cnt_in_ref[...]

    T = TS * BS
    gd = dp // len(POOL_WINDOWS)
    NH = CONV_W - 1
    NP = POOL_MAX_W - 1

    for t in range(TS):
        xt_scr[t * BS:(t + 1) * BS, :] = x_ref[t]

    _rmsnorm_to_bf16(xt_scr, gmix_ref, h_scr, T)
    a_scr[...] = jnp.dot(h_scr[...], win_ref[:, 0:dc], preferred_element_type=F32)
    pg_scr[...] = jnp.dot(h_scr[...], win_ref[:, dc:2 * dc], preferred_element_type=F32)
    u_scr[...] = jnp.dot(h_scr[...], win_ref[:, 2 * dc:2 * dc + dp], preferred_element_type=F32)

    R = 16

    def glu(i, c):
        rs = _rows(i, R)
        a_scr[rs, :] = a_scr[rs, :] * _sigmoid(pg_scr[rs, :])
        return c

    lax.fori_loop(0, T // R, glu, 0, unroll=ROW_LOOP_UNROLL)

    def ext_conv(j, rs, ls):
        if j < NH:
            return sconv_ref[j, rs, ls]
        return a_scr[pl.ds((j - NH) * BS + rs.start, rs.size), ls]

    def ext_pool(j, rs, ls):
        if j < NP:
            return spool_ref[j, rs, ls]
        return u_scr[pl.ds((j - NP) * BS + rs.start, rs.size), ls]

    RC, LC = 32, 256

    def conv(i, c):
        rs = _rows(i, RC)
        for t in range(TS):
            for lc in range(dc // LC):
                ls = slice(lc * LC, (lc + 1) * LC)
                acc = jnp.zeros((RC, LC), F32)
                for k in range(CONV_W):
                    acc = acc + wdw_ref[k:k + 1, ls] * ext_conv(t + k, rs, ls)
                yc_scr[pl.ds(t * BS + rs.start, RC), ls] = acc + bdw_ref[:, ls]
        return c

    lax.fori_loop(0, BS // RC, conv, 0)

    _layernorm_swish(yc_scr, lng_ref, lnb_ref, mix_scr, T, dc)

    def pool(i, c):
        rs = _rows(i, RC)
        for t in range(TS):
            for g, w in enumerate(POOL_WINDOWS):
                ls = slice(g * gd, (g + 1) * gd)
                tot = ext_pool(NP + t, rs, ls)
                cur = tot
                for back in range(1, w):
                    tot = tot + ext_pool(NP + t - back, rs, ls)
                sd_scr[pl.ds(t * BS + rs.start, RC), ls] = (tot / jnp.float32(w) - cur).astype(BF16)
        return c

    lax.fori_loop(0, BS // RC, pool, 0)

    _pool_project(sd_scr, wpool_ref, pscale_ref, mix_scr, dc, gd)

    for j in range(NH):
        src = j + TS
        nconv_ref[j] = sconv_ref[src] if src < NH else a_scr[(src - NH) * BS:(src - NH + 1) * BS, :]
    for j in range(NP):
        src = j + TS
        npool_ref[j] = spool_ref[src] if src < NP else u_scr[(src - NP) * BS:(src - NP + 1) * BS, :]

    _out_proj_residual(xt_scr, mix_scr, wout_ref, x2_ref)
    _router(x2_ref, gffn_ref, wrh_ref, wrl_ref, br_ref, ltri_ref, hf_ref, ids_ref, w0_ref, w1_ref, cnt_ref,
            hhi_scr, hlo_scr, cnt_scr, T, d)


def _strict_lower_ones(n):
    return jnp.tril(jnp.ones((n, n), BF16), -1)


def _const_spec(shape):
    nd = len(shape)
    return pl.BlockSpec(shape, lambda *a: (0,) * nd, pipeline_mode=pl.Buffered(1))


def _mixer_weight_specs(d, dc, dp, cols):
    gd = dp // len(POOL_WINDOWS)
    return [
        _const_spec((1, d)),
        _const_spec((d, cols)),
        _const_spec((CONV_W, dc)),
        _const_spec((1, dc)),
        _const_spec((1, dc)),
        _const_spec((1, dc)),
        _const_spec((len(POOL_WINDOWS), gd, gd)),
        _const_spec((1, dp)),
        _const_spec((dc + dp, d)),
        _const_spec((1, d)),
        _const_spec((d, LANES)),
        _const_spec((d, LANES)),
        _const_spec((1, LANES)),
    ]


def _mixer_prompt(x, wts, *, T, n_total, n_sorted_rows, zero_rows):
    B, S, d = x.shape
    dc = wts[2].shape[1]
    dp = wts[7].shape[1]
    cols = wts[1].shape[1]
    n_s = S // T
    N = n_total
    n_blk = N // T
    assert N % T == 0 and N - B * S <= S
    blk = lambda b, s: jnp.clip(b * n_s + s - 1, 0, n_blk - 1)
    tok = lambda b, s: (blk(b, s), 0)
    tok4 = lambda b, s: (blk(b, s), 0, 0, 0)
    bclamp = lambda b: jnp.minimum(b, B - 1)

    def nxt(b, s):
        wrap = s + 1 == n_s
        return (jnp.where(wrap, bclamp(b + 1), bclamp(b)), jnp.where(wrap, 0, s + 1), 0)

    def prv(b, s):
        tp = jnp.clip(b * n_s + s - 1, 0, B * n_s - 1)
        return (tp // n_s, tp % n_s, 0)
    out_shape = (
        jax.ShapeDtypeStruct((N, d), F32),
        jax.ShapeDtypeStruct((N // TOK_CHUNK, TOK_CHUNK, d // LANES, LANES), BF16),
        jax.ShapeDtypeStruct((N, LANES), jnp.int32),
        jax.ShapeDtypeStruct((N, LANES), F32),
        jax.ShapeDtypeStruct((N, LANES), F32),
        jax.ShapeDtypeStruct((1, B, CONV_W - 1, dc), F32),
        jax.ShapeDtypeStruct((1, B, POOL_MAX_W - 1, dp), F32),
        jax.ShapeDtypeStruct((n_sorted_rows, d // LANES, LANES), BF16),
        jax.ShapeDtypeStruct((SUBLANES, LANES), F32),
    )
    assert n_sorted_rows % zero_rows == 0
    out_specs = (
        pl.BlockSpec((T, d), tok),
        pl.BlockSpec((T // TOK_CHUNK, TOK_CHUNK, d // LANES, LANES), tok4),
        pl.BlockSpec((T, LANES), tok),
        pl.BlockSpec((T, LANES), tok),
        pl.BlockSpec((T, LANES), tok),
        pl.BlockSpec((None, None, CONV_W - 1, dc), lambda b, s: (0, bclamp(b), 0, 0)),
        pl.BlockSpec((None, None, POOL_MAX_W - 1, dp), lambda b, s: (0, bclamp(b), 0, 0)),
        pl.BlockSpec(memory_space=pl.ANY),
        pl.BlockSpec((SUBLANES, LANES), lambda b, s: (0, 0)),
    )
    scratch = [
        pltpu.VMEM((T, d), BF16),
        pltpu.VMEM((T, dc), F32), pltpu.VMEM((T, dc), F32),
        pltpu.VMEM((T + TAIL, dc), F32), pltpu.VMEM((T + TAIL, dc), F32),
        pltpu.VMEM((T + PTAIL, dp), F32), pltpu.VMEM((T + PTAIL, dp), F32),
        pltpu.VMEM((T, dc), F32),
        pltpu.VMEM((T, dp), BF16), pltpu.VMEM((T, dp), BF16),
        pltpu.VMEM((T, dc + dp), BF16), pltpu.VMEM((T, dc + dp), BF16),
        pltpu.VMEM((T, d), BF16),
        pltpu.VMEM((T, d), BF16),
        pltpu.VMEM((zero_rows, d // LANES, LANES), BF16),
        pltpu.VMEM((SUBLANES, LANES), F32),
        pltpu.SemaphoreType.DMA((1,)),
    ]
    return pl.pallas_call(
        functools.partial(_mixer_prompt_kernel, T=T, d=d, dc=dc, dp=dp, n_batch=B),
        grid=(B + 1, n_s),
        in_specs=[pl.BlockSpec((None, T, d), prv),
                  pl.BlockSpec((None, T, d), nxt)]
                 + _mixer_weight_specs(d, dc, dp, cols) + [_const_spec((T, T))],
        out_specs=out_specs,
        out_shape=out_shape,
        scratch_shapes=scratch,
        compiler_params=pltpu.CompilerParams(
            dimension_semantics=("arbitrary", "arbitrary"), vmem_limit_bytes=VMEM_LIMIT),
        name="mixer_prompt",
    )(x, x, *wts, _strict_lower_ones(T))


def _mixer_sample(x_t, sconv_t, spool_t, wts, tok_arrays, counts, *, BS, tok0):
    TS, Bd, d = x_t.shape
    dc = wts[2].shape[1]
    dp = wts[7].shape[1]
    cols = wts[1].shape[1]
    T = TS * BS
    n_b = Bd // BS
    b0 = tok0 // T
    tok = lambda i: (b0 + i, 0)
    tok4 = lambda i: (b0 + i, 0, 0, 0)
    out_shape = tuple(jax.ShapeDtypeStruct(a.shape, a.dtype) for a in tok_arrays) + (
        jax.ShapeDtypeStruct((CONV_W - 1, Bd, dc), F32),
        jax.ShapeDtypeStruct((POOL_MAX_W - 1, Bd, dp), F32),
        jax.ShapeDtypeStruct(counts.shape, F32),
    )
    out_specs = (
        pl.BlockSpec((T, d), tok),
        pl.BlockSpec((T // TOK_CHUNK, TOK_CHUNK, d // LANES, LANES), tok4),
        pl.BlockSpec((T, LANES), tok),
        pl.BlockSpec((T, LANES), tok),
        pl.BlockSpec((T, LANES), tok),
        pl.BlockSpec((CONV_W - 1, BS, dc), lambda i: (0, i, 0)),
        pl.BlockSpec((POOL_MAX_W - 1, BS, dp), lambda i: (0, i, 0)),
        pl.BlockSpec(counts.shape, lambda i: (0, 0)),
    )
    scratch = [
        pltpu.VMEM((T, d), F32),
        pltpu.VMEM((T, d), BF16),
        pltpu.VMEM((T, dc), F32),
        pltpu.VMEM((T, dc), F32),
        pltpu.VMEM((T, dp), F32),
        pltpu.VMEM((T, dc), F32),
        pltpu.VMEM((T, dp), BF16),
        pltpu.VMEM((T, dc + dp), BF16),
        pltpu.VMEM((T, d), BF16),
        pltpu.VMEM((T, d), BF16),
        pltpu.VMEM(counts.shape, F32),
    ]
    return pl.pallas_call(
        functools.partial(_mixer_sample_kernel, TS=TS, BS=BS, d=d, dc=dc, dp=dp),
        grid=(n_b,),
        in_specs=[pl.BlockSpec((TS, BS, d), lambda i: (0, i, 0)),
                  pl.BlockSpec((CONV_W - 1, BS, dc), lambda i: (0, i, 0), pipeline_mode=pl.Buffered(1)),
                  pl.BlockSpec((POOL_MAX_W - 1, BS, dp), lambda i: (0, i, 0), pipeline_mode=pl.Buffered(1))]
                 + _mixer_weight_specs(d, dc, dp, cols)
                 + [_const_spec((T, T)), _const_spec(counts.shape)]
                 + [pl.BlockSpec(memory_space=pl.ANY)] * len(tok_arrays),
        out_specs=out_specs,
        out_shape=out_shape,
        scratch_shapes=scratch,
        input_output_aliases={5 + len(wts) + j: j for j in range(len(tok_arrays))},
        compiler_params=pltpu.CompilerParams(
            dimension_semantics=("arbitrary",), vmem_limit_bytes=VMEM_LIMIT),
        name="mixer_sample",
    )(x_t, sconv_t, spool_t, *wts, _strict_lower_ones(T), counts, *tok_arrays)


def _dispatch_kernel(pos_ref, hf_ref, dst_in, xs_in_ref, xs_ref, dst_ref, sem, isem, *, R, n_tok):
    del xs_in_ref
    base = pl.program_id(0) * R

    @pl.when(pl.program_id(0) == 0)
    def _():
        init = pltpu.make_async_copy(dst_in, dst_ref, isem.at[0])
        init.start()
        init.wait()

    G = 8

    def body(g, c):
        r0 = g * G
        ps = [pos_ref[2 * (base + r0) + q] for q in range(2 * G)]
        for q in range(2 * G):
            r, k = r0 + q // 2, q % 2
            pltpu.make_async_copy(hf_ref.at[pl.ds(r, 1)], xs_ref.at[pl.ds(ps[q], 1)], sem.at[0]).start(priority=k)
            dst_ref[ps[q]] = k * n_tok + base + r
        return c

    lax.fori_loop(0, R // G, body, 0)
    for k in range(2):
        pltpu.make_async_copy(hf_ref, xs_ref.at[pl.ds(0, R)], sem.at[0]).wait()


def _dispatch(pos, hf, xs_zero, *, R, n_tiles, tm):
    n_tok = hf.shape[0]
    assert n_tok % R == 0
    p = jnp.arange((n_tiles + 1) * tm, dtype=jnp.int32)
    tile = p // tm
    buf = jnp.where(tile == n_tiles, 1, tile % 2)
    dst_init = 2 * n_tok + buf * tm + p % tm
    hbm = pl.BlockSpec(memory_space=pl.ANY)
    return pl.pallas_call(
        functools.partial(_dispatch_kernel, R=R, n_tok=n_tok),
        grid_spec=pltpu.PrefetchScalarGridSpec(
            num_scalar_prefetch=1,
            grid=(n_tok // R,),
            in_specs=[pl.BlockSpec((R,) + hf.shape[1:], lambda s, pos: (s, 0, 0)), hbm, hbm],
            out_specs=(hbm, pl.BlockSpec(memory_space=pltpu.SMEM)),
            scratch_shapes=[pltpu.SemaphoreType.DMA((1,)), pltpu.SemaphoreType.DMA((1,))],
        ),
        out_shape=(jax.ShapeDtypeStruct(xs_zero.shape, xs_zero.dtype),
                   jax.ShapeDtypeStruct(dst_init.shape, jnp.int32)),
        input_output_aliases={3: 0},
        compiler_params=pltpu.CompilerParams(dimension_semantics=("arbitrary",)),
        name="dispatch",
    )(pos, hf, dst_init, xs_zero)


def _expert_mlp_kernel(te_ref, na_ref, nx_ref, nx2_ref, dst_prev_ref, dst_cur_ref, xs_ref, wg_ref, wu_ref, wd_ref, o_ref,
                       ybuf0, ybuf1, wgf, wuf, wdf, wg_scr, wu_scr, wd_scr, wslot, ssem, wsem,
                       *, tm, n_tiles, d):
    i = pl.program_id(0)
    n_act = na_ref[0]
    active = i < n_act
    slot = i % 2
    last = n_tiles - 1
    prev = te_ref[jnp.maximum(i - 1, 0)]
    new_expert = (i == 0) | (te_ref[i] != prev)
    ybuf = (ybuf0, ybuf1)

    def scatter_start(dst_ref, b):
        for r in range(tm):
            pltpu.make_async_copy(ybuf[b].at[pl.ds(r, 1)], o_ref.at[pl.ds(dst_ref[0, 0, r], 1)],
                                  ssem.at[b]).start(priority=r % 2)

    def scatter_wait(b):
        pltpu.make_async_copy(ybuf[b], o_ref.at[pl.ds(0, tm)], ssem.at[b]).wait()

    def weight_copies(e, b):
        return [pltpu.make_async_copy(src.at[e], buf.at[b], wsem.at[b])
                for src, buf in ((wg_ref, wgf), (wu_ref, wuf), (wd_ref, wdf))]

    @pl.when(i == 0)
    def _():
        wslot[0] = 0
        for c in weight_copies(te_ref[0], 0):
            c.start()

        @pl.when(nx_ref[0] >= 0)
        def _():
            for c in weight_copies(nx_ref[0], 1):
                c.start()
        ybuf1[...] = jnp.zeros(ybuf1.shape, BF16)
        trash0 = pltpu.make_async_copy(ybuf1, o_ref.at[pl.ds(o_ref.shape[0] - 2 * tm, tm)], ssem.at[0])
        trash0.start()
        trash0.wait()

    @pl.when(active & new_expert)
    def _():
        b = wslot[0]
        after_next = nx2_ref[i]

        @pl.when(after_next >= 0)
        def _():
            for c in weight_copies(after_next, (b + 2) % N_WEIGHT_BUFS):
                c.start()

        for c in weight_copies(te_ref[i], b):
            c.wait()
        wg_scr[...] = wgf[b].astype(BF16)
        wu_scr[...] = wuf[b].astype(BF16)
        wd_scr[...] = wdf[b].astype(BF16)
        wslot[0] = (b + 1) % N_WEIGHT_BUFS

    for par in range(2):
        is_par = slot == par

        @pl.when(is_par & (i >= 1) & (i - 2 < n_act))
        def _():
            scatter_wait(par)

        @pl.when(is_par & active)
        def _():
            x = xs_ref[...].reshape(tm, d)
            g = jnp.dot(x, wg_scr[...], preferred_element_type=F32)
            u = jnp.dot(x, wu_scr[...], preferred_element_type=F32)
            act = (g * _sigmoid(g) * u).astype(BF16)
            y = jnp.dot(act, wd_scr[...], preferred_element_type=F32)
            ybuf[par][...] = y.astype(BF16).reshape(tm, d // LANES, LANES)
            scatter_start(dst_prev_ref, 1 - par)

        @pl.when(is_par & (i == n_act))
        def _():
            scatter_start(dst_prev_ref, 1 - par)

        @pl.when(is_par & (i == last))
        def _():
            @pl.when(last - 1 < n_act)
            def _():
                scatter_wait(1 - par)

            @pl.when(last < n_act)
            def _():
                scatter_start(dst_cur_ref, par)
                scatter_wait(par)


def _expert_mlp(tile_expert, n_active, next_expert, after_next_expert, dst, xs, w_gate, w_up, w_down, *, tm, n_tok):
    E, d, de = w_gate.shape
    n_tiles = xs.shape[0] // tm
    row = (tm, d // LANES, LANES)
    cur = lambda i, te, na, nx, nx2: (i, 0, 0)
    prv = lambda i, te, na, nx, nx2: (jnp.where(i == 0, n_tiles, i - 1), 0, 0)
    hbm = pl.BlockSpec(memory_space=pl.ANY)
    smem_tile = lambda imap: pl.BlockSpec((1, 1, tm), imap, memory_space=pltpu.SMEM)
    return pl.pallas_call(
        functools.partial(_expert_mlp_kernel, tm=tm, n_tiles=n_tiles, d=d),
        grid_spec=pltpu.PrefetchScalarGridSpec(
            num_scalar_prefetch=4,
            grid=(n_tiles,),
            in_specs=[
                smem_tile(prv), smem_tile(cur),
                pl.BlockSpec(row, lambda i, te, na, nx, nx2: (jnp.minimum(i, na[0] - 1), 0, 0)),
                hbm, hbm, hbm,
            ],
            out_specs=hbm,
            scratch_shapes=[pltpu.VMEM(row, BF16), pltpu.VMEM(row, BF16),
                            pltpu.VMEM((N_WEIGHT_BUFS, d, de), F32), pltpu.VMEM((N_WEIGHT_BUFS, d, de), F32),
                            pltpu.VMEM((N_WEIGHT_BUFS, de, d), F32),
                            pltpu.VMEM((d, de), BF16), pltpu.VMEM((d, de), BF16), pltpu.VMEM((de, d), BF16),
                            pltpu.SMEM((1,), jnp.int32),
                            pltpu.SemaphoreType.DMA((2,)), pltpu.SemaphoreType.DMA((N_WEIGHT_BUFS,))],
        ),
        out_shape=jax.ShapeDtypeStruct((2 * n_tok + 2 * tm, d // LANES, LANES), BF16),
        compiler_params=pltpu.CompilerParams(
            dimension_semantics=("arbitrary",), vmem_limit_bytes=VMEM_LIMIT),
        name="expert_mlp",
    )(tile_expert, n_active, next_expert, after_next_expert, dst, dst, xs, w_gate, w_up, w_down)


def _combine_kernel(x2_ref, o0_ref, o1_ref, w0_ref, w1_ref, gfin_ref, y_ref, *, R, d):
    C = TOK_CHUNK

    def body(i, c):
        rs = _rows(i, C)
        w0 = w0_ref[rs, :]
        w1 = w1_ref[rs, :]
        o0 = o0_ref[i].reshape(C, d).astype(F32)
        o1 = o1_ref[i].reshape(C, d).astype(F32)
        parts = []
        ssq = jnp.zeros((C, LANES), F32)
        for j in range(d // LANES):
            ls = slice(j * LANES, (j + 1) * LANES)
            m = w0 * o0[:, ls] + w1 * o1[:, ls]
            v = x2_ref[rs, ls] + m
            ssq = ssq + v * v
            parts.append(v)
        ms = jnp.sum(ssq, axis=-1, keepdims=True) * jnp.float32(1.0 / d)
        inv = lax.rsqrt(ms + EPS)
        for j in range(d // LANES):
            ls = slice(j * LANES, (j + 1) * LANES)
            y_ref[rs, ls] = parts[j] * inv * gfin_ref[:, ls]
        return c

    lax.fori_loop(0, R // C, body, 0)


def _combine(x2, o, w0, w1, g_final, *, R, tok0, n_out):
    n_tok, d = x2.shape
    n_steps = n_out // R
    b0 = tok0 // R
    b1 = n_tok // R
    tok = lambda s: (b0 + s, 0)
    orow = (R // TOK_CHUNK, TOK_CHUNK, d // LANES, LANES)
    o4 = o.reshape(o.shape[0] // TOK_CHUNK, TOK_CHUNK, d // LANES, LANES)
    return pl.pallas_call(
        functools.partial(_combine_kernel, R=R, d=d),
        grid=(n_steps,),
        in_specs=[
            pl.BlockSpec((R, d), tok),
            pl.BlockSpec(orow, lambda s: (b0 + s, 0, 0, 0)),
            pl.BlockSpec(orow, lambda s: (b1 + b0 + s, 0, 0, 0)),
            pl.BlockSpec((R, LANES), tok),
            pl.BlockSpec((R, LANES), tok),
            pl.BlockSpec((1, d), lambda s: (0, 0)),
        ],
        out_specs=pl.BlockSpec((R, d), lambda s: (s, 0)),
        out_shape=jax.ShapeDtypeStruct((n_out, d), F32),
        compiler_params=pltpu.CompilerParams(
            dimension_semantics=("arbitrary",), vmem_limit_bytes=VMEM_LIMIT),
        name="combine",
    )(x2, o4, o4, w0, w1, g_final)


def _routing_plan(ids, counts, tm, n_tiles):
    e0, e1, r0, r1 = ids[:, 0], ids[:, 1], ids[:, 2], ids[:, 3]
    ar = jnp.arange(N_EXPERTS, dtype=jnp.int32)
    tiles_e = (counts + tm - 1) // tm
    tile_end = jnp.cumsum(tiles_e)
    offs = (tile_end - tiles_e) * tm
    pos0 = jnp.sum(jnp.where(e0[:, None] == ar, offs[None, :], 0), axis=1) + r0
    pos1 = jnp.sum(jnp.where(e1[:, None] == ar, offs[None, :], 0), axis=1) + r1
    pos = jnp.stack([pos0, pos1], axis=1).reshape(-1).astype(jnp.int32)
    n_active = tile_end[-1].astype(jnp.int32)
    t = jnp.arange(n_tiles, dtype=jnp.int32)
    tq = jnp.minimum(t, n_active - 1)
    te = jnp.sum((tile_end[None, :] <= tq[:, None]).astype(jnp.int32), axis=1)
    te = jnp.minimum(te, N_EXPERTS - 1)
    later = (ar[None, :] > ar[:, None]) & (tiles_e[None, :] > 0)
    nxt_e = jnp.min(jnp.where(later, ar[None, :], N_EXPERTS), axis=1)
    nxt_e = jnp.where(nxt_e == N_EXPERTS, -1, nxt_e).astype(jnp.int32)
    nxt2_e = jnp.where(nxt_e >= 0, nxt_e[jnp.maximum(nxt_e, 0)], -1)
    return pos, te, n_active.reshape(1), nxt_e[te], nxt2_e[te]


T_PROMPT = 256
BS_SAMPLE = 32
TM_EXPERT = 256
N_WEIGHT_BUFS = 3
R_DISPATCH_CHOICES = (512, 256, 128)
R_COMBINE = 256


def kernel(x_prompt, x_sample, state_conv, state_pool, g_mix, w_in, w_dw, b_dw, ln_g, ln_b, w_pool, pool_scale, w_out, g_ffn, w_rg, b_rg, w_re, b_re, w_gate, w_up, w_down, g_final):
    depth = g_mix.shape[0]
    assert depth == 1
    B, S, d = x_prompt.shape
    Bd, TS, _ = x_sample.shape
    n_p = B * S
    n_s = Bd * TS
    N = n_p + n_s

    assert ROUTER_GROUP_LANE0 == N_EXPERTS
    n_pad = LANES - N_EXPERTS - N_GROUPS
    w_r = jnp.concatenate([w_re[0], w_rg[0], jnp.zeros((d, n_pad), F32)], axis=1)
    w_r_hi = w_r.astype(BF16)
    w_r_lo = (w_r - w_r_hi.astype(F32)).astype(BF16)
    b_r = jnp.concatenate([b_re[0], b_rg[0], jnp.zeros((n_pad,), F32)])[None]
    wts = (g_mix[0][None], w_in[0].astype(BF16), w_dw[0], b_dw[0][None], ln_g[0][None], ln_b[0][None],
           w_pool[0].astype(BF16), pool_scale[0][None], w_out[0].astype(BF16), g_ffn[0][None],
           w_r_hi, w_r_lo, b_r)

    tm = TM_EXPERT
    n_tiles = (2 * N + N_EXPERTS * (tm - 1) + tm - 1) // tm
    *tok_arrays, nconv_p, npool_p, xs, counts_p = _mixer_prompt(x_prompt, wts, T=T_PROMPT, n_total=N,
                                                                n_sorted_rows=n_tiles * tm, zero_rows=tm)

    x_t = jnp.transpose(x_sample, (1, 0, 2))
    sconv_t = jnp.transpose(state_conv[0], (1, 0, 2))
    spool_t = jnp.transpose(state_pool[0], (1, 0, 2))
    x2, hf, ids, w0, w1, nconv_t, npool_t, counts = _mixer_sample(x_t, sconv_t, spool_t, wts, tok_arrays, counts_p,
                                                                  BS=BS_SAMPLE, tok0=n_p)

    pos, tile_expert, n_active, next_expert, after_next = _routing_plan(ids[:, 0:4], counts[0, 0:N_EXPERTS].astype(jnp.int32),
                                                            tm, n_tiles)
    hf = hf.reshape(N, d // LANES, LANES)
    r_disp = max(r for r in R_DISPATCH_CHOICES if N % r == 0)
    xs, dst = _dispatch(pos, hf, xs, R=r_disp, n_tiles=n_tiles, tm=tm)
    dst = dst.reshape(n_tiles + 1, 1, tm)
    o = _expert_mlp(tile_expert, n_active, next_expert, after_next, dst, xs, w_gate[0], w_up[0], w_down[0],
                    tm=tm, n_tok=N)

    gfin = g_final[None]
    y_p = _combine(x2, o, w0, w1, gfin, R=R_COMBINE, tok0=0, n_out=n_p)
    y_s = _combine(x2, o, w0, w1, gfin, R=R_COMBINE, tok0=n_p, n_out=n_s)

    y_prompt = y_p.reshape(B, S, d)
    y_sample = y_s.reshape(Bd // BS_SAMPLE, TS, BS_SAMPLE, d).transpose(0, 2, 1, 3).reshape(Bd, TS, d)
    new_conv_s = jnp.transpose(nconv_t, (1, 0, 2))[None]
    new_pool_s = jnp.transpose(npool_t, (1, 0, 2))[None]
    return (y_prompt, y_sample, nconv_p, new_conv_s, npool_p, new_pool_s)
```

```python
import functools

import jax
import jax.numpy as jnp
from jax import lax
from jax.experimental import pallas as pl
from jax.experimental.pallas import tpu as pltpu

F32 = jnp.float32
BF16 = jnp.bfloat16
EPS = 1e-6

LANES = 128
SUBLANES = 8
VMEM_LIMIT = 56 * 1024 * 1024

CONV_W = 31
POOL_WINDOWS = (2, 4, 8, 16)
POOL_MAX_W = 16
N_GROUPS = 4
PER_GROUP = 8
N_EXPERTS = N_GROUPS * PER_GROUP

ROUTER_GROUP_LANE0 = N_EXPERTS
TOK_CHUNK = 16
ROW_LOOP_UNROLL = 4
TAIL = 32
PTAIL = 16


def _rows(i, r):
    if isinstance(i, int):
        return pl.ds(i * r, r)
    return pl.ds(pl.multiple_of(i * r, r), r)


def _row_loop(n, body, *, static, unroll=1):
    if static:
        for i in range(n):
            body(i)
    else:
        def step(i, c):
            body(i)
            return c
        lax.fori_loop(0, n, step, 0, unroll=unroll)


def _sigmoid(x):
    return 1.0 / (1.0 + jnp.exp(-x))


def _rmsnorm_to_bf16(src_ref, g_ref, dst_ref, T, static=False):
    R = 16

    def body(i):
        rs = _rows(i, R)
        x = src_ref[rs, :]
        ms = jnp.mean(x * x, axis=-1, keepdims=True)
        dst_ref[rs, :] = (x * lax.rsqrt(ms + EPS) * g_ref[...]).astype(BF16)

    _row_loop(T // R, body, static=static, unroll=ROW_LOOP_UNROLL)


def _layernorm_swish(y_ref, lng_ref, lnb_ref, mix_ref, T, dc, static=False):
    R = 16

    def body(i):
        rs = _rows(i, R)
        y = y_ref[rs, :]
        mu = jnp.mean(y, axis=-1, keepdims=True)
        d = y - mu
        var = jnp.mean(d * d, axis=-1, keepdims=True)
        z = d * lax.rsqrt(var + EPS) * lng_ref[...] + lnb_ref[...]
        mix_ref[rs, 0:dc] = (z * _sigmoid(z)).astype(BF16)

    _row_loop(T // R, body, static=static, unroll=ROW_LOOP_UNROLL)


def _pool_project(sd_ref, wpool_ref, pscale_ref, mix_ref, dc, gd):
    for g in range(len(POOL_WINDOWS)):
        sl = slice(g * gd, (g + 1) * gd)
        o = jnp.dot(sd_ref[:, sl], wpool_ref[g], preferred_element_type=F32)
        mix_ref[:, dc + g * gd: dc + (g + 1) * gd] = (o * pscale_ref[:, sl]).astype(BF16)


def _out_proj_residual(x_ref, mix_ref, wout_ref, x2_ref):
    x2_ref[...] = x_ref[...] + jnp.dot(mix_ref[...], wout_ref[...], preferred_element_type=F32)


def _router(x2_ref, gffn_ref, wrh_ref, wrl_ref, br_ref, ltri_ref, hf_ref, ids_ref, w0_ref, w1_ref, cnt_ref,
            hhi_scr, hlo_scr, cnt_scr, T, d):
    R = TOK_CHUNK

    def body(i, c):
        rs = _rows(i, R)
        x = x2_ref[rs, :]
        ms = jnp.mean(x * x, axis=-1, keepdims=True)
        h = x * lax.rsqrt(ms + EPS) * gffn_ref[...]
        hi = h.astype(BF16)
        hf_ref[i] = hi.reshape(R, d // LANES, LANES)
        hhi_scr[rs, :] = hi
        hlo_scr[rs, :] = (h - hi.astype(F32)).astype(BF16)
        return c

    lax.fori_loop(0, T // R, body, 0, unroll=ROW_LOOP_UNROLL)

    lg = (jnp.dot(hhi_scr[...], wrh_ref[...], preferred_element_type=F32)
          + jnp.dot(hlo_scr[...], wrh_ref[...], preferred_element_type=F32)
          + jnp.dot(hhi_scr[...], wrl_ref[...], preferred_element_type=F32)
          + br_ref[...])

    lane = lax.broadcasted_iota(jnp.int32, lg.shape, 1).astype(F32)
    neg = jnp.float32(-jnp.inf)
    big = jnp.float32(1e9)
    g_lo = jnp.float32(ROUTER_GROUP_LANE0)
    gmask = (lane >= g_lo) & (lane < g_lo + N_GROUPS)
    lgg = jnp.where(gmask, lg, neg)
    gmax = jnp.max(lgg, axis=-1, keepdims=True)
    gsel = jnp.min(jnp.where(lgg == gmax, lane, big), axis=-1, keepdims=True) - g_lo
    gsum = jnp.sum(jnp.where(gmask, jnp.exp(lg - gmax), 0.0), axis=-1, keepdims=True)
    p_g = 1.0 / gsum

    e_lo = gsel * PER_GROUP
    emask = (lane >= e_lo) & (lane < e_lo + PER_GROUP)
    le = jnp.where(emask, lg, neg)
    v0 = jnp.max(le, axis=-1, keepdims=True)
    i0 = jnp.min(jnp.where(le == v0, lane, big), axis=-1, keepdims=True)
    le2 = jnp.where(lane == i0, neg, le)
    v1 = jnp.max(le2, axis=-1, keepdims=True)
    i1 = jnp.min(jnp.where(le2 == v1, lane, big), axis=-1, keepdims=True)
    ex = jnp.exp(v1 - v0)
    den = 1.0 / (1.0 + ex)
    w0 = den * p_g
    w1 = ex * den * p_g

    sel0 = lane == i0
    sel1 = lane == i1
    m = jnp.where(sel0 | sel1, 1.0, 0.0)
    before = jnp.dot(ltri_ref[...], m.astype(BF16), preferred_element_type=F32) + cnt_scr[0:1, :]
    rank0 = jnp.sum(jnp.where(sel0, before, 0.0), axis=-1, keepdims=True)
    rank1 = jnp.sum(jnp.where(sel1, before, 0.0), axis=-1, keepdims=True)
    total = cnt_scr[0:1, :] + jnp.sum(m, axis=0, keepdims=True)
    cnt_scr[...] = jnp.broadcast_to(total, cnt_scr.shape)
    cnt_ref[...] = jnp.broadcast_to(total, cnt_ref.shape)

    packed = jnp.where(lane == 0.0, i0, jnp.where(lane == 1.0, i1,
                                                  jnp.where(lane == 2.0, rank0, jnp.where(lane == 3.0, rank1, 0.0))))
    ids_ref[...] = packed.astype(jnp.int32)
    w0_ref[...] = jnp.broadcast_to(w0, lg.shape)
    w1_ref[...] = jnp.broadcast_to(w1, lg.shape)


def _mixer_prompt_kernel(xp_ref, xn_ref, gmix_ref, win_ref, wdw_ref, bdw_ref, lng_ref, lnb_ref, wpool_ref,
                         pscale_ref, wout_ref, gffn_ref, wrh_ref, wrl_ref, br_ref, ltri_ref,
                         x2_ref, hf_ref, ids_ref, w0_ref, w1_ref, nconv_ref, npool_ref, xs_ref, cnt_ref,
                         h_scr, pg0, pg1, ext0, ext1, extu0, extu1, yc_scr, sd0, sd1, mix0, mix1, hhi_scr, hlo_scr,
                         zbuf, cnt_scr, zsem, *, T, d, dc, dp, n_batch):
    n_s = pl.num_programs(1)
    b = pl.program_id(0)
    s = pl.program_id(1)
    t = b * n_s + s
    gd = dp // len(POOL_WINDOWS)
    pg_scr, ext_scr, extu_scr, sd_scr, mix_scr = (pg0, pg1), (ext0, ext1), (extu0, extu1), (sd0, sd1), (mix0, mix1)

    def in_proj(x_ref, slot, static):
        _rmsnorm_to_bf16(x_ref, gmix_ref, h_scr, T, static=static)
        ext_scr[slot][TAIL:TAIL + T, :] = jnp.dot(h_scr[...], win_ref[:, 0:dc], preferred_element_type=F32)
        pg_scr[slot][...] = jnp.dot(h_scr[...], win_ref[:, dc:2 * dc], preferred_element_type=F32)
        extu_scr[slot][PTAIL:PTAIL + T, :] = jnp.dot(h_scr[...], win_ref[:, 2 * dc:2 * dc + dp],
                                                     preferred_element_type=F32)

    def zero_copies():
        zr = zbuf.shape[0]
        return [pltpu.make_async_copy(zbuf, xs_ref.at[pl.ds(j * zr, zr)], zsem.at[0])
                for j in range(xs_ref.shape[0] // zr)]

    def finish_prev(q):
        _pool_project(sd_scr[q], wpool_ref, pscale_ref, mix_scr[q], dc, gd)
        _out_proj_residual(xp_ref, mix_scr[q], wout_ref, x2_ref)

    @pl.when(t == 0)
    def _():
        ext0[0:TAIL, :] = jnp.zeros((TAIL, dc), F32)
        extu0[0:PTAIL, :] = jnp.zeros((PTAIL, dp), F32)
        sd1[...] = jnp.zeros(sd1.shape, BF16)
        mix1[...] = jnp.zeros(mix1.shape, BF16)
        zbuf[...] = jnp.zeros(zbuf.shape, BF16)
        cnt_scr[...] = jnp.zeros(cnt_scr.shape, F32)
        for c in zero_copies():
            c.start()
        in_proj(xp_ref, 0, False)

    for par in range(2):
        @pl.when((b == n_batch) & (s == 0) & (t % 2 == par))
        def _():
            finish_prev(1 - par)

    @pl.when((b == n_batch) & (s == 0))
    def _():
        for c in zero_copies():
            c.wait()

    @pl.when((b == n_batch) & (s > 0))
    def _():
        x2_ref[...] = jnp.zeros(x2_ref.shape, F32)
        hf_ref[...] = jnp.zeros(hf_ref.shape, BF16)
        ids_ref[...] = jnp.zeros(ids_ref.shape, jnp.int32)
        w0_ref[...] = jnp.zeros(w0_ref.shape, F32)
        w1_ref[...] = jnp.zeros(w1_ref.shape, F32)

    def step(p):
        q = 1 - p
        in_proj(xn_ref, q, True)

        R = 16
        for i in range(T // R):
            es = pl.ds(i * R + TAIL, R)
            ext_scr[p][es, :] = ext_scr[p][es, :] * _sigmoid(pg_scr[p][pl.ds(i * R, R), :])

        RC, LC = 32, 128
        BR = RC + TAIL
        shift0 = TAIL - (CONV_W - 1)
        for i in range(T // RC):
            r0 = i * RC
            for lc in range(dc // LC):
                ls = slice(lc * LC, (lc + 1) * LC)
                blk = ext_scr[p][pl.ds(r0, BR), ls]
                acc = jnp.zeros((RC, LC), F32)
                for sft in range(SUBLANES):
                    taps = [k for k in range(CONV_W) if (k + shift0) % SUBLANES == sft]
                    if not taps:
                        continue
                    rolled = blk if sft == 0 else pltpu.roll(blk, BR - sft, 0)
                    for k in taps:
                        qq = (k + shift0) // SUBLANES
                        acc = acc + wdw_ref[k:k + 1, ls] * rolled[qq * SUBLANES:qq * SUBLANES + RC, :]
                yc_scr[pl.ds(r0, RC), ls] = acc + bdw_ref[:, ls]

        _layernorm_swish(yc_scr, lng_ref, lnb_ref, mix_scr[p], T, dc, static=True)

        RP = 32
        BP = RP + PTAIL
        pos_base = s * T
        for i in range(T // RP):
            r0 = i * RP
            pos = (pos_base + r0 + lax.broadcasted_iota(jnp.int32, (RP, gd), 0)).astype(F32)
            for g, w in enumerate(POOL_WINDOWS):
                ls = slice(g * gd, (g + 1) * gd)
                blk = extu_scr[p][pl.ds(r0, BP), ls]
                run = blk
                span = 1
                while span < w:
                    run = run + pltpu.roll(run, span, 0)
                    span *= 2
                cnt = jnp.minimum(pos + 1.0, jnp.float32(w))
                mean = run[PTAIL:PTAIL + RP, :] / cnt
                sd_scr[p][pl.ds(r0, RP), ls] = (mean - blk[PTAIL:PTAIL + RP, :]).astype(BF16)

        nconv_ref[...] = ext_scr[p][TAIL + T - (CONV_W - 1):TAIL + T, :]
        npool_ref[...] = extu_scr[p][PTAIL + T - (POOL_MAX_W - 1):PTAIL + T, :]
        keep = s != n_s - 1
        ext_scr[q][0:TAIL, :] = jnp.where(keep, ext_scr[p][T:T + TAIL, :], 0.0)
        extu_scr[q][0:PTAIL, :] = jnp.where(keep, extu_scr[p][T:T + PTAIL, :], 0.0)

        finish_prev(q)

    for par in range(2):
        pl.when((b < n_batch) & (t % 2 == par))(functools.partial(step, par))

    @pl.when(((b < n_batch) & (t > 0)) | ((b == n_batch) & (s == 0)))
    def _():
        _router(x2_ref, gffn_ref, wrh_ref, wrl_ref, br_ref, ltri_ref, hf_ref, ids_ref, w0_ref, w1_ref, cnt_ref,
                hhi_scr, hlo_scr, cnt_scr, T, d)


def _mixer_sample_kernel(x_ref, sconv_ref, spool_ref, gmix_ref, win_ref, wdw_ref, bdw_ref, lng_ref,
                         lnb_ref, wpool_ref, pscale_ref, wout_ref, gffn_ref, wrh_ref, wrl_ref, br_ref,
                         ltri_ref, cnt_in_ref,
                         x2_in, hf_in, ids_in, w0_in, w1_in,
                         x2_ref, hf_ref, ids_ref, w0_ref, w1_ref, nconv_ref, npool_ref, cnt_ref,
                         xt_scr, h_scr, a_scr, pg_scr, u_scr, yc_scr, sd_scr, mix_scr, hhi_scr, hlo_scr, cnt_scr,
                         *, TS, BS, d, dc, dp):
    del x2_in, hf_in, ids_in, w0_in, w1_in

    @pl.when(pl.program_id(0) == 0)
    def _():
        cnt_scr[...] = cnt_in_ref[...]

    T = TS * BS
    gd = dp // len(POOL_WINDOWS)
    NH = CONV_W - 1
    NP = POOL_MAX_W - 1

    for t in range(TS):
        xt_scr[t * BS:(t + 1) * BS, :] = x_ref[t]

    _rmsnorm_to_bf16(xt_scr, gmix_ref, h_scr, T)
    a_scr[...] = jnp.dot(h_scr[...], win_ref[:, 0:dc], preferred_element_type=F32)
    pg_scr[...] = jnp.dot(h_scr[...], win_ref[:, dc:2 * dc], preferred_element_type=F32)
    u_scr[...] = jnp.dot(h_scr[...], win_ref[:, 2 * dc:2 * dc + dp], preferred_element_type=F32)

    R = 16

    def glu(i, c):
        rs = _rows(i, R)
        a_scr[rs, :] = a_scr[rs, :] * _sigmoid(pg_scr[rs, :])
        return c

    lax.fori_loop(0, T // R, glu, 0, unroll=ROW_LOOP_UNROLL)

    def ext_conv(j, rs, ls):
        if j < NH:
            return sconv_ref[j, rs, ls]
        return a_scr[pl.ds((j - NH) * BS + rs.start, rs.size), ls]

    def ext_pool(j, rs, ls):
        if j < NP:
            return spool_ref[j, rs, ls]
        return u_scr[pl.ds((j - NP) * BS + rs.start, rs.size), ls]

    RC, LC = 32, 256

    def conv(i, c):
        rs = _rows(i, RC)
        for t in range(TS):
            for lc in range(dc // LC):
                ls = slice(lc * LC, (lc + 1) * LC)
                acc = jnp.zeros((RC, LC), F32)
                for k in range(CONV_W):
                    acc = acc + wdw_ref[k:k + 1, ls] * ext_conv(t + k, rs, ls)
                yc_scr[pl.ds(t * BS + rs.start, RC), ls] = acc + bdw_ref[:, ls]
        return c

    lax.fori_loop(0, BS // RC, conv, 0)

    _layernorm_swish(yc_scr, lng_ref, lnb_ref, mix_scr, T, dc)

    def pool(i, c):
        rs = _rows(i, RC)
        for t in range(TS):
            for g, w in enumerate(POOL_WINDOWS):
                ls = slice(g * gd, (g + 1) * gd)
                tot = ext_pool(NP + t, rs, ls)
                cur = tot
                for back in range(1, w):
                    tot = tot + ext_pool(NP + t - back, rs, ls)
                sd_scr[pl.ds(t * BS + rs.start, RC), ls] = (tot / jnp.float32(w) - cur).astype(BF16)
        return c

    lax.fori_loop(0, BS // RC, pool, 0)

    _pool_project(sd_scr, wpool_ref, pscale_ref, mix_scr, dc, gd)

    for j in range(NH):
        src = j + TS
        nconv_ref[j] = sconv_ref[src] if src < NH else a_scr[(src - NH) * BS:(src - NH + 1) * BS, :]
    for j in range(NP):
        src = j + TS
        npool_ref[j] = spool_ref[src] if src < NP else u_scr[(src - NP) * BS:(src - NP + 1) * BS, :]

    _out_proj_residual(xt_scr, mix_scr, wout_ref, x2_ref)
    _router(x2_ref, gffn_ref, wrh_ref, wrl_ref, br_ref, ltri_ref, hf_ref, ids_ref, w0_ref, w1_ref, cnt_ref,
            hhi_scr, hlo_scr, cnt_scr, T, d)


def _strict_lower_ones(n):
    return jnp.tril(jnp.ones((n, n), BF16), -1)


def _const_spec(shape):
    nd = len(shape)
    return pl.BlockSpec(shape, lambda *a: (0,) * nd, pipeline_mode=pl.Buffered(1))


def _mixer_weight_specs(d, dc, dp, cols):
    gd = dp // len(POOL_WINDOWS)
    return [
        _const_spec((1, d)),
        _const_spec((d, cols)),
        _const_spec((CONV_W, dc)),
        _const_spec((1, dc)),
        _const_spec((1, dc)),
        _const_spec((1, dc)),
        _const_spec((len(POOL_WINDOWS), gd, gd)),
        _const_spec((1, dp)),
        _const_spec((dc + dp, d)),
        _const_spec((1, d)),
        _const_spec((d, LANES)),
        _const_spec((d, LANES)),
        _const_spec((1, LANES)),
    ]


def _mixer_prompt(x, wts, *, T, n_total, n_sorted_rows, zero_rows):
    B, S, d = x.shape
    dc = wts[2].shape[1]
    dp = wts[7].shape[1]
    cols = wts[1].shape[1]
    n_s = S // T
    N = n_total
    n_blk = N // T
    assert N % T == 0 and N - B * S <= S
    blk = lambda b, s: jnp.clip(b * n_s + s - 1, 0, n_blk - 1)
    tok = lambda b, s: (blk(b, s), 0)
    tok4 = lambda b, s: (blk(b, s), 0, 0, 0)
    bclamp = lambda b: jnp.minimum(b, B - 1)

    def nxt(b, s):
        wrap = s + 1 == n_s
        return (jnp.where(wrap, bclamp(b + 1), bclamp(b)), jnp.where(wrap, 0, s + 1), 0)

    def prv(b, s):
        tp = jnp.clip(b * n_s + s - 1, 0, B * n_s - 1)
        return (tp // n_s, tp % n_s, 0)
    out_shape = (
        jax.ShapeDtypeStruct((N, d), F32),
        jax.ShapeDtypeStruct((N // TOK_CHUNK, TOK_CHUNK, d // LANES, LANES), BF16),
        jax.ShapeDtypeStruct((N, LANES), jnp.int32),
        jax.ShapeDtypeStruct((N, LANES), F32),
        jax.ShapeDtypeStruct((N, LANES), F32),
        jax.ShapeDtypeStruct((1, B, CONV_W - 1, dc), F32),
        jax.ShapeDtypeStruct((1, B, POOL_MAX_W - 1, dp), F32),
        jax.ShapeDtypeStruct((n_sorted_rows, d // LANES, LANES), BF16),
        jax.ShapeDtypeStruct((SUBLANES, LANES), F32),
    )
    assert n_sorted_rows % zero_rows == 0
    out_specs = (
        pl.BlockSpec((T, d), tok),
        pl.BlockSpec((T // TOK_CHUNK, TOK_CHUNK, d // LANES, LANES), tok4),
        pl.BlockSpec((T, LANES), tok),
        pl.BlockSpec((T, LANES), tok),
        pl.BlockSpec((T, LANES), tok),
        pl.BlockSpec((None, None, CONV_W - 1, dc), lambda b, s: (0, bclamp(b), 0, 0)),
        pl.BlockSpec((None, None, POOL_MAX_W - 1, dp), lambda b, s: (0, bclamp(b), 0, 0)),
        pl.BlockSpec(memory_space=pl.ANY),
        pl.BlockSpec((SUBLANES, LANES), lambda b, s: (0, 0)),
    )
    scratch = [
        pltpu.VMEM((T, d), BF16),
        pltpu.VMEM((T, dc), F32), pltpu.VMEM((T, dc), F32),
        pltpu.VMEM((T + TAIL, dc), F32), pltpu.VMEM((T + TAIL, dc), F32),
        pltpu.VMEM((T + PTAIL, dp), F32), pltpu.VMEM((T + PTAIL, dp), F32),
        pltpu.VMEM((T, dc), F32),
        pltpu.VMEM((T, dp), BF16), pltpu.VMEM((T, dp), BF16),
        pltpu.VMEM((T, dc + dp), BF16), pltpu.VMEM((T, dc + dp), BF16),
        pltpu.VMEM((T, d), BF16),
        pltpu.VMEM((T, d), BF16),
        pltpu.VMEM((zero_rows, d // LANES, LANES), BF16),
        pltpu.VMEM((SUBLANES, LANES), F32),
        pltpu.SemaphoreType.DMA((1,)),
    ]
    return pl.pallas_call(
        functools.partial(_mixer_prompt_kernel, T=T, d=d, dc=dc, dp=dp, n_batch=B),
        grid=(B + 1, n_s),
        in_specs=[pl.BlockSpec((None, T, d), prv),
                  pl.BlockSpec((None, T, d), nxt)]
                 + _mixer_weight_specs(d, dc, dp, cols) + [_const_spec((T, T))],
        out_specs=out_specs,
        out_shape=out_shape,
        scratch_shapes=scratch,
        compiler_params=pltpu.CompilerParams(
            dimension_semantics=("arbitrary", "arbitrary"), vmem_limit_bytes=VMEM_LIMIT),
        name="mixer_prompt",
    )(x, x, *wts, _strict_lower_ones(T))


def _mixer_sample(x_t, sconv_t, spool_t, wts, tok_arrays, counts, *, BS, tok0):
    TS, Bd, d = x_t.shape
    dc = wts[2].shape[1]
    dp = wts[7].shape[1]
    cols = wts[1].shape[1]
    T = TS * BS
    n_b = Bd // BS
    b0 = tok0 // T
    tok = lambda i: (b0 + i, 0)
    tok4 = lambda i: (b0 + i, 0, 0, 0)
    out_shape = tuple(jax.ShapeDtypeStruct(a.shape, a.dtype) for a in tok_arrays) + (
        jax.ShapeDtypeStruct((CONV_W - 1, Bd, dc), F32),
        jax.ShapeDtypeStruct((POOL_MAX_W - 1, Bd, dp), F32),
        jax.ShapeDtypeStruct(counts.shape, F32),
    )
    out_specs = (
        pl.BlockSpec((T, d), tok),
        pl.BlockSpec((T // TOK_CHUNK, TOK_CHUNK, d // LANES, LANES), tok4),
        pl.BlockSpec((T, LANES), tok),
        pl.BlockSpec((T, LANES), tok),
        pl.BlockSpec((T, LANES), tok),
        pl.BlockSpec((CONV_W - 1, BS, dc), lambda i: (0, i, 0)),
        pl.BlockSpec((POOL_MAX_W - 1, BS, dp), lambda i: (0, i, 0)),
        pl.BlockSpec(counts.shape, lambda i: (0, 0)),
    )
    scratch = [
        pltpu.VMEM((T, d), F32),
        pltpu.VMEM((T, d), BF16),
        pltpu.VMEM((T, dc), F32),
        pltpu.VMEM((T, dc), F32),
        pltpu.VMEM((T, dp), F32),
        pltpu.VMEM((T, dc), F32),
        pltpu.VMEM((T, dp), BF16),
        pltpu.VMEM((T, dc + dp), BF16),
        pltpu.VMEM((T, d), BF16),
        pltpu.VMEM((T, d), BF16),
        pltpu.VMEM(counts.shape, F32),
    ]
    return pl.pallas_call(
        functools.partial(_mixer_sample_kernel, TS=TS, BS=BS, d=d, dc=dc, dp=dp),
        grid=(n_b,),
        in_specs=[pl.BlockSpec((TS, BS, d), lambda i: (0, i, 0)),
                  pl.BlockSpec((CONV_W - 1, BS, dc), lambda i: (0, i, 0), pipeline_mode=pl.Buffered(1)),
                  pl.BlockSpec((POOL_MAX_W - 1, BS, dp), lambda i: (0, i, 0), pipeline_mode=pl.Buffered(1))]
                 + _mixer_weight_specs(d, dc, dp, cols)
                 + [_const_spec((T, T)), _const_spec(counts.shape)]
                 + [pl.BlockSpec(memory_space=pl.ANY)] * len(tok_arrays),
        out_specs=out_specs,
        out_shape=out_shape,
        scratch_shapes=scratch,
        input_output_aliases={5 + len(wts) + j: j for j in range(len(tok_arrays))},
        compiler_params=pltpu.CompilerParams(
            dimension_semantics=("arbitrary",), vmem_limit_bytes=VMEM_LIMIT),
        name="mixer_sample",
    )(x_t, sconv_t, spool_t, *wts, _strict_lower_ones(T), counts, *tok_arrays)


def _dispatch_kernel(pos_ref, hf_ref, dst_in, xs_in_ref, xs_ref, dst_ref, sem, isem, *, R, n_tok):
    del xs_in_ref
    base = pl.program_id(0) * R

    @pl.when(pl.program_id(0) == 0)
    def _():
        init = pltpu.make_async_copy(dst_in, dst_ref, isem.at[0])
        init.start()
        init.wait()

    G = 8

    def body(g, c):
        r0 = g * G
        ps = [pos_ref[2 * (base + r0) + q] for q in range(2 * G)]
        for q in range(2 * G):
            r, k = r0 + q // 2, q % 2
            pltpu.make_async_copy(hf_ref.at[pl.ds(r, 1)], xs_ref.at[pl.ds(ps[q], 1)], sem.at[0]).start(priority=k)
            dst_ref[ps[q]] = k * n_tok + base + r
        return c

    lax.fori_loop(0, R // G, body, 0)
    for k in range(2):
        pltpu.make_async_copy(hf_ref, xs_ref.at[pl.ds(0, R)], sem.at[0]).wait()


def _dispatch(pos, hf, xs_zero, *, R, n_tiles, tm):
    n_tok = hf.shape[0]
    assert n_tok % R == 0
    p = jnp.arange((n_tiles + 1) * tm, dtype=jnp.int32)
    tile = p // tm
    buf = jnp.where(tile == n_tiles, 1, tile % 2)
    dst_init = 2 * n_tok + buf * tm + p % tm
    hbm = pl.BlockSpec(memory_space=pl.ANY)
    return pl.pallas_call(
        functools.partial(_dispatch_kernel, R=R, n_tok=n_tok),
        grid_spec=pltpu.PrefetchScalarGridSpec(
            num_scalar_prefetch=1,
            grid=(n_tok // R,),
            in_specs=[pl.BlockSpec((R,) + hf.shape[1:], lambda s, pos: (s, 0, 0)), hbm, hbm],
            out_specs=(hbm, pl.BlockSpec(memory_space=pltpu.SMEM)),
            scratch_shapes=[pltpu.SemaphoreType.DMA((1,)), pltpu.SemaphoreType.DMA((1,))],
        ),
        out_shape=(jax.ShapeDtypeStruct(xs_zero.shape, xs_zero.dtype),
                   jax.ShapeDtypeStruct(dst_init.shape, jnp.int32)),
        input_output_aliases={3: 0},
        compiler_params=pltpu.CompilerParams(dimension_semantics=("arbitrary",)),
        name="dispatch",
    )(pos, hf, dst_init, xs_zero)


def _expert_mlp_kernel(te_ref, na_ref, nx_ref, nx2_ref, dst_prev_ref, dst_cur_ref, xs_ref, wg_ref, wu_ref, wd_ref, o_ref,
                       ybuf0, ybuf1, wgf, wuf, wdf, wg_scr, wu_scr, wd_scr, wslot, ssem, wsem,
                       *, tm, n_tiles, d):
    i = pl.program_id(0)
    n_act = na_ref[0]
    active = i < n_act
    slot = i % 2
    last = n_tiles - 1
    prev = te_ref[jnp.maximum(i - 1, 0)]
    new_expert = (i == 0) | (te_ref[i] != prev)
    ybuf = (ybuf0, ybuf1)

    def scatter_start(dst_ref, b):
        for r in range(tm):
            pltpu.make_async_copy(ybuf[b].at[pl.ds(r, 1)], o_ref.at[pl.ds(dst_ref[0, 0, r], 1)],
                                  ssem.at[b]).start(priority=r % 2)

    def scatter_wait(b):
        pltpu.make_async_copy(ybuf[b], o_ref.at[pl.ds(0, tm)], ssem.at[b]).wait()

    def weight_copies(e, b):
        return [pltpu.make_async_copy(src.at[e], buf.at[b], wsem.at[b])
                for src, buf in ((wg_ref, wgf), (wu_ref, wuf), (wd_ref, wdf))]

    @pl.when(i == 0)
    def _():
        wslot[0] = 0
        for c in weight_copies(te_ref[0], 0):
            c.start()

        @pl.when(nx_ref[0] >= 0)
        def _():
            for c in weight_copies(nx_ref[0], 1):
                c.start()
        ybuf1[...] = jnp.zeros(ybuf1.shape, BF16)
        trash0 = pltpu.make_async_copy(ybuf1, o_ref.at[pl.ds(o_ref.shape[0] - 2 * tm, tm)], ssem.at[0])
        trash0.start()
        trash0.wait()

    @pl.when(active & new_expert)
    def _():
        b = wslot[0]
        after_next = nx2_ref[i]

        @pl.when(after_next >= 0)
        def _():
            for c in weight_copies(after_next, (b + 2) % N_WEIGHT_BUFS):
                c.start()

        for c in weight_copies(te_ref[i], b):
            c.wait()
        wg_scr[...] = wgf[b].astype(BF16)
        wu_scr[...] = wuf[b].astype(BF16)
        wd_scr[...] = wdf[b].astype(BF16)
        wslot[0] = (b + 1) % N_WEIGHT_BUFS

    for par in range(2):
        is_par = slot == par

        @pl.when(is_par & (i >= 1) & (i - 2 < n_act))
        def _():
            scatter_wait(par)

        @pl.when(is_par & active)
        def _():
            x = xs_ref[...].reshape(tm, d)
            g = jnp.dot(x, wg_scr[...], preferred_element_type=F32)
            u = jnp.dot(x, wu_scr[...], preferred_element_type=F32)
            act = (g * _sigmoid(g) * u).astype(BF16)
            y = jnp.dot(act, wd_scr[...], preferred_element_type=F32)
            ybuf[par][...] = y.astype(BF16).reshape(tm, d // LANES, LANES)
            scatter_start(dst_prev_ref, 1 - par)

        @pl.when(is_par & (i == n_act))
        def _():
            scatter_start(dst_prev_ref, 1 - par)

        @pl.when(is_par & (i == last))
        def _():
            @pl.when(last - 1 < n_act)
            def _():
                scatter_wait(1 - par)

            @pl.when(last < n_act)
            def _():
                scatter_start(dst_cur_ref, par)
                scatter_wait(par)


def _expert_mlp(tile_expert, n_active, next_expert, after_next_expert, dst, xs, w_gate, w_up, w_down, *, tm, n_tok):
    E, d, de = w_gate.shape
    n_tiles = xs.shape[0] // tm
    row = (tm, d // LANES, LANES)
    cur = lambda i, te, na, nx, nx2: (i, 0, 0)
    prv = lambda i, te, na, nx, nx2: (jnp.where(i == 0, n_tiles, i - 1), 0, 0)
    hbm = pl.BlockSpec(memory_space=pl.ANY)
    smem_tile = lambda imap: pl.BlockSpec((1, 1, tm), imap, memory_space=pltpu.SMEM)
    return pl.pallas_call(
        functools.partial(_expert_mlp_kernel, tm=tm, n_tiles=n_tiles, d=d),
        grid_spec=pltpu.PrefetchScalarGridSpec(
            num_scalar_prefetch=4,
            grid=(n_tiles,),
            in_specs=[
                smem_tile(prv), smem_tile(cur),
                pl.BlockSpec(row, lambda i, te, na, nx, nx2: (jnp.clip(i, 0, jnp.maximum(na[0] - 1, 0)), 0, 0)),
                hbm, hbm, hbm,
            ],
            out_specs=hbm,
            scratch_shapes=[pltpu.VMEM(row, BF16), pltpu.VMEM(row, BF16),
                            pltpu.VMEM((N_WEIGHT_BUFS, d, de), F32), pltpu.VMEM((N_WEIGHT_BUFS, d, de), F32),
                            pltpu.VMEM((N_WEIGHT_BUFS, de, d), F32),
                            pltpu.VMEM((d, de), BF16), pltpu.VMEM((d, de), BF16), pltpu.VMEM((de, d), BF16),
                            pltpu.SMEM((1,), jnp.int32),
                            pltpu.SemaphoreType.DMA((2,)), pltpu.SemaphoreType.DMA((N_WEIGHT_BUFS,))],
        ),
        out_shape=jax.ShapeDtypeStruct((2 * n_tok + 2 * tm, d // LANES, LANES), BF16),
        compiler_params=pltpu.CompilerParams(
            dimension_semantics=("arbitrary",), vmem_limit_bytes=VMEM_LIMIT),
        name="expert_mlp",
    )(tile_expert, n_active, next_expert, after_next_expert, dst, dst, xs, w_gate, w_up, w_down)


def _combine_kernel(x2_ref, o0_ref, o1_ref, w0_ref, w1_ref, gfin_ref, y_ref, *, R, d):
    C = TOK_CHUNK

    def body(i, c):
        rs = _rows(i, C)
        w0 = w0_ref[rs, :]
        w1 = w1_ref[rs, :]
        o0 = o0_ref[i].reshape(C, d).astype(F32)
        o1 = o1_ref[i].reshape(C, d).astype(F32)
        parts = []
        ssq = jnp.zeros((C, LANES), F32)
        for j in range(d // LANES):
            ls = slice(j * LANES, (j + 1) * LANES)
            m = w0 * o0[:, ls] + w1 * o1[:, ls]
            v = x2_ref[rs, ls] + m
            ssq = ssq + v * v
            parts.append(v)
        ms = jnp.sum(ssq, axis=-1, keepdims=True) * jnp.float32(1.0 / d)
        inv = lax.rsqrt(ms + EPS)
        for j in range(d // LANES):
            ls = slice(j * LANES, (j + 1) * LANES)
            y_ref[rs, ls] = parts[j] * inv * gfin_ref[:, ls]
        return c

    lax.fori_loop(0, R // C, body, 0)


def _combine(x2, o, w0, w1, g_final, *, tok0, n_out):
    n_tok, d = x2.shape
    R = max(r for r in R_COMBINE_CHOICES if n_out % r == 0 and tok0 % r == 0 and n_tok % r == 0)
    n_steps = n_out // R
    b0 = tok0 // R
    b1 = n_tok // R
    tok = lambda s: (b0 + s, 0)
    orow = (R // TOK_CHUNK, TOK_CHUNK, d // LANES, LANES)
    o4 = o.reshape(o.shape[0] // TOK_CHUNK, TOK_CHUNK, d // LANES, LANES)
    return pl.pallas_call(
        functools.partial(_combine_kernel, R=R, d=d),
        grid=(n_steps,),
        in_specs=[
            pl.BlockSpec((R, d), tok),
            pl.BlockSpec(orow, lambda s: (b0 + s, 0, 0, 0)),
            pl.BlockSpec(orow, lambda s: (b1 + b0 + s, 0, 0, 0)),
            pl.BlockSpec((R, LANES), tok),
            pl.BlockSpec((R, LANES), tok),
            pl.BlockSpec((1, d), lambda s: (0, 0)),
        ],
        out_specs=pl.BlockSpec((R, d), lambda s: (s, 0)),
        out_shape=jax.ShapeDtypeStruct((n_out, d), F32),
        compiler_params=pltpu.CompilerParams(
            dimension_semantics=("arbitrary",), vmem_limit_bytes=VMEM_LIMIT),
        name="combine",
    )(x2, o4, o4, w0, w1, g_final)


def _routing_plan(ids, counts, tm, n_tiles):
    e0, e1, r0, r1 = ids[:, 0], ids[:, 1], ids[:, 2], ids[:, 3]
    ar = jnp.arange(N_EXPERTS, dtype=jnp.int32)
    tiles_e = (counts + tm - 1) // tm
    tile_end = jnp.cumsum(tiles_e)
    offs = (tile_end - tiles_e) * tm
    pos0 = jnp.sum(jnp.where(e0[:, None] == ar, offs[None, :], 0), axis=1) + r0
    pos1 = jnp.sum(jnp.where(e1[:, None] == ar, offs[None, :], 0), axis=1) + r1
    pos = jnp.stack([pos0, pos1], axis=1).reshape(-1).astype(jnp.int32)
    n_active = tile_end[-1].astype(jnp.int32)
    t = jnp.arange(n_tiles, dtype=jnp.int32)
    tq = jnp.minimum(t, n_active - 1)
    te = jnp.sum((tile_end[None, :] <= tq[:, None]).astype(jnp.int32), axis=1)
    te = jnp.minimum(te, N_EXPERTS - 1)
    later = (ar[None, :] > ar[:, None]) & (tiles_e[None, :] > 0)
    nxt_e = jnp.min(jnp.where(later, ar[None, :], N_EXPERTS), axis=1)
    nxt_e = jnp.where(nxt_e == N_EXPERTS, -1, nxt_e).astype(jnp.int32)
    nxt2_e = jnp.where(nxt_e >= 0, nxt_e[jnp.maximum(nxt_e, 0)], -1)
    return pos, te, n_active.reshape(1), nxt_e[te], nxt2_e[te]


T_PROMPT = 256
BS_SAMPLE = 32
TM_EXPERT = 256
N_WEIGHT_BUFS = 3
R_DISPATCH_CHOICES = (2176, 1088, 512, 256, 128)
R_COMBINE_CHOICES = (512, 256, 128)


def kernel(x_prompt, x_sample, state_conv, state_pool, g_mix, w_in, w_dw, b_dw, ln_g, ln_b, w_pool, pool_scale, w_out, g_ffn, w_rg, b_rg, w_re, b_re, w_gate, w_up, w_down, g_final):
    depth = g_mix.shape[0]
    assert depth == 1
    B, S, d = x_prompt.shape
    Bd, TS, _ = x_sample.shape
    n_p = B * S
    n_s = Bd * TS
    N = n_p + n_s

    assert ROUTER_GROUP_LANE0 == N_EXPERTS
    n_pad = LANES - N_EXPERTS - N_GROUPS
    w_r = jnp.concatenate([w_re[0], w_rg[0], jnp.zeros((d, n_pad), F32)], axis=1)
    w_r_hi = w_r.astype(BF16)
    w_r_lo = (w_r - w_r_hi.astype(F32)).astype(BF16)
    b_r = jnp.concatenate([b_re[0], b_rg[0], jnp.zeros((n_pad,), F32)])[None]
    wts = (g_mix[0][None], w_in[0].astype(BF16), w_dw[0], b_dw[0][None], ln_g[0][None], ln_b[0][None],
           w_pool[0].astype(BF16), pool_scale[0][None], w_out[0].astype(BF16), g_ffn[0][None],
           w_r_hi, w_r_lo, b_r)

    tm = TM_EXPERT
    n_tiles = (2 * N + N_EXPERTS * (tm - 1) + tm - 1) // tm
    *tok_arrays, nconv_p, npool_p, xs, counts_p = _mixer_prompt(x_prompt, wts, T=T_PROMPT, n_total=N,
                                                                n_sorted_rows=n_tiles * tm, zero_rows=tm)

    x_t = jnp.transpose(x_sample, (1, 0, 2))
    sconv_t = jnp.transpose(state_conv[0], (1, 0, 2))
    spool_t = jnp.transpose(state_pool[0], (1, 0, 2))
    x2, hf, ids, w0, w1, nconv_t, npool_t, counts = _mixer_sample(x_t, sconv_t, spool_t, wts, tok_arrays, counts_p,
                                                                  BS=BS_SAMPLE, tok0=n_p)

    pos, tile_expert, n_active, next_expert, after_next = _routing_plan(ids[:, 0:4], counts[0, 0:N_EXPERTS].astype(jnp.int32),
                                                            tm, n_tiles)
    hf = hf.reshape(N, d // LANES, LANES)
    r_disp = max(r for r in R_DISPATCH_CHOICES if N % r == 0)
    xs, dst = _dispatch(pos, hf, xs, R=r_disp, n_tiles=n_tiles, tm=tm)
    dst = dst.reshape(n_tiles + 1, 1, tm)
    o = _expert_mlp(tile_expert, n_active, next_expert, after_next, dst, xs, w_gate[0], w_up[0], w_down[0],
                    tm=tm, n_tok=N)

    gfin = g_final[None]
    y_p = _combine(x2, o, w0, w1, gfin, tok0=0, n_out=n_p)
    y_s = _combine(x2, o, w0, w1, gfin, tok0=n_p, n_out=n_s)

    y_prompt = y_p.reshape(B, S, d)
    y_sample = y_s.reshape(Bd // BS_SAMPLE, TS, BS_SAMPLE, d).transpose(0, 2, 1, 3).reshape(Bd, TS, d)
    new_conv_s = jnp.transpose(nconv_t, (1, 0, 2))[None]
    new_pool_s = jnp.transpose(npool_t, (1, 0, 2))[None]
    return (y_prompt, y_sample, nconv_p, new_conv_s, npool_p, new_pool_s)
```

```python
import functools

import jax
import jax.numpy as jnp
from jax import lax
from jax.experimental import pallas as pl
from jax.experimental.pallas import tpu as pltpu

F32 = jnp.float32
BF16 = jnp.bfloat16
EPS = 1e-6

LANES = 128
SUBLANES = 8
VMEM_LIMIT = 56 * 1024 * 1024

CONV_W = 31
POOL_WINDOWS = (2, 4, 8, 16)
POOL_MAX_W = 16
N_GROUPS = 4
PER_GROUP = 8
N_EXPERTS = N_GROUPS * PER_GROUP

ROUTER_GROUP_LANE0 = N_EXPERTS
TOK_CHUNK = 16
ROW_LOOP_UNROLL = 4
TAIL = 32
PTAIL = 16


def _rows(i, r):
    if isinstance(i, int):
        return pl.ds(i * r, r)
    return pl.ds(pl.multiple_of(i * r, r), r)


def _row_loop(n, body, *, static, unroll=1):
    if static:
        for i in range(n):
            body(i)
    else:
        def step(i, c):
            body(i)
            return c
        lax.fori_loop(0, n, step, 0, unroll=unroll)


def _sigmoid(x):
    return 1.0 / (1.0 + jnp.exp(-x))


def _rmsnorm_to_bf16(src_ref, g_ref, dst_ref, T, static=False):
    R = 16

    def body(i):
        rs = _rows(i, R)
        x = src_ref[rs, :]
        ms = jnp.mean(x * x, axis=-1, keepdims=True)
        dst_ref[rs, :] = (x * lax.rsqrt(ms + EPS) * g_ref[...]).astype(BF16)

    _row_loop(T // R, body, static=static, unroll=ROW_LOOP_UNROLL)


def _layernorm_swish(y_ref, lng_ref, lnb_ref, mix_ref, T, dc, static=False):
    R = 16

    def body(i):
        rs = _rows(i, R)
        y = y_ref[rs, :]
        mu = jnp.mean(y, axis=-1, keepdims=True)
        d = y - mu
        var = jnp.mean(d * d, axis=-1, keepdims=True)
        z = d * lax.rsqrt(var + EPS) * lng_ref[...] + lnb_ref[...]
        mix_ref[rs, 0:dc] = (z * _sigmoid(z)).astype(BF16)

    _row_loop(T // R, body, static=static, unroll=ROW_LOOP_UNROLL)


def _pool_project(sd_ref, wpool_ref, pscale_ref, mix_ref, dc, gd):
    for g in range(len(POOL_WINDOWS)):
        sl = slice(g * gd, (g + 1) * gd)
        o = jnp.dot(sd_ref[:, sl], wpool_ref[g], preferred_element_type=F32)
        mix_ref[:, dc + g * gd: dc + (g + 1) * gd] = (o * pscale_ref[:, sl]).astype(BF16)


def _out_proj_residual(x_ref, mix_ref, wout_ref, x2_ref):
    x2_ref[...] = x_ref[...] + jnp.dot(mix_ref[...], wout_ref[...], preferred_element_type=F32)


def _router(x2_ref, gffn_ref, wrh_ref, wrl_ref, br_ref, ltri_ref, hf_ref, ids_ref, w0_ref, w1_ref, cnt_ref,
            hhi_scr, hlo_scr, cnt_scr, T, d):
    R = TOK_CHUNK

    def body(i, c):
        rs = _rows(i, R)
        x = x2_ref[rs, :]
        ms = jnp.mean(x * x, axis=-1, keepdims=True)
        h = x * lax.rsqrt(ms + EPS) * gffn_ref[...]
        hi = h.astype(BF16)
        hf_ref[i] = hi.reshape(R, d // LANES, LANES)
        hhi_scr[rs, :] = hi
        hlo_scr[rs, :] = (h - hi.astype(F32)).astype(BF16)
        return c

    lax.fori_loop(0, T // R, body, 0, unroll=ROW_LOOP_UNROLL)

    lg = (jnp.dot(hhi_scr[...], wrh_ref[...], preferred_element_type=F32)
          + jnp.dot(hlo_scr[...], wrh_ref[...], preferred_element_type=F32)
          + jnp.dot(hhi_scr[...], wrl_ref[...], preferred_element_type=F32)
          + br_ref[...])

    lane = lax.broadcasted_iota(jnp.int32, lg.shape, 1).astype(F32)
    neg = jnp.float32(-jnp.inf)
    big = jnp.float32(1e9)
    g_lo = jnp.float32(ROUTER_GROUP_LANE0)
    gmask = (lane >= g_lo) & (lane < g_lo + N_GROUPS)
    lgg = jnp.where(gmask, lg, neg)
    gmax = jnp.max(lgg, axis=-1, keepdims=True)
    gsel = jnp.min(jnp.where(lgg == gmax, lane, big), axis=-1, keepdims=True) - g_lo
    gsum = jnp.sum(jnp.where(gmask, jnp.exp(lg - gmax), 0.0), axis=-1, keepdims=True)
    p_g = 1.0 / gsum

    e_lo = gsel * PER_GROUP
    emask = (lane >= e_lo) & (lane < e_lo + PER_GROUP)
    le = jnp.where(emask, lg, neg)
    v0 = jnp.max(le, axis=-1, keepdims=True)
    i0 = jnp.min(jnp.where(le == v0, lane, big), axis=-1, keepdims=True)
    le2 = jnp.where(lane == i0, neg, le)
    v1 = jnp.max(le2, axis=-1, keepdims=True)
    i1 = jnp.min(jnp.where(le2 == v1, lane, big), axis=-1, keepdims=True)
    ex = jnp.exp(v1 - v0)
    den = 1.0 / (1.0 + ex)
    w0 = den * p_g
    w1 = ex * den * p_g

    sel0 = lane == i0
    sel1 = lane == i1
    m = jnp.where(sel0 | sel1, 1.0, 0.0)
    before = jnp.dot(ltri_ref[...], m.astype(BF16), preferred_element_type=F32) + cnt_scr[0:1, :]
    rank0 = jnp.sum(jnp.where(sel0, before, 0.0), axis=-1, keepdims=True)
    rank1 = jnp.sum(jnp.where(sel1, before, 0.0), axis=-1, keepdims=True)
    total = cnt_scr[0:1, :] + jnp.sum(m, axis=0, keepdims=True)
    cnt_scr[...] = jnp.broadcast_to(total, cnt_scr.shape)
    cnt_ref[...] = jnp.broadcast_to(total, cnt_ref.shape)

    packed = jnp.where(lane == 0.0, i0, jnp.where(lane == 1.0, i1,
                                                  jnp.where(lane == 2.0, rank0, jnp.where(lane == 3.0, rank1, 0.0))))
    ids_ref[...] = packed.astype(jnp.int32)
    w0_ref[...] = jnp.broadcast_to(w0, lg.shape)
    w1_ref[...] = jnp.broadcast_to(w1, lg.shape)


def _mixer_prompt_kernel(xp_ref, xn_ref, gmix_ref, win_ref, wdw_ref, bdw_ref, lng_ref, lnb_ref, wpool_ref,
                         pscale_ref, wout_ref, gffn_ref, wrh_ref, wrl_ref, br_ref, ltri_ref,
                         x2_ref, hf_ref, ids_ref, w0_ref, w1_ref, nconv_ref, npool_ref, xs_ref, cnt_ref,
                         h_scr, pg0, pg1, ext0, ext1, extu0, extu1, yc_scr, sd0, sd1, mix0, mix1, hhi_scr, hlo_scr,
                         zbuf, cnt_scr, zsem, *, T, d, dc, dp, n_batch):
    n_s = pl.num_programs(1)
    b = pl.program_id(0)
    s = pl.program_id(1)
    t = b * n_s + s
    gd = dp // len(POOL_WINDOWS)
    pg_scr, ext_scr, extu_scr, sd_scr, mix_scr = (pg0, pg1), (ext0, ext1), (extu0, extu1), (sd0, sd1), (mix0, mix1)

    def in_proj(x_ref, slot, static):
        _rmsnorm_to_bf16(x_ref, gmix_ref, h_scr, T, static=static)
        ext_scr[slot][TAIL:TAIL + T, :] = jnp.dot(h_scr[...], win_ref[:, 0:dc], preferred_element_type=F32)
        pg_scr[slot][...] = jnp.dot(h_scr[...], win_ref[:, dc:2 * dc], preferred_element_type=F32)
        extu_scr[slot][PTAIL:PTAIL + T, :] = jnp.dot(h_scr[...], win_ref[:, 2 * dc:2 * dc + dp],
                                                     preferred_element_type=F32)

    def zero_copies():
        zr = zbuf.shape[0]
        return [pltpu.make_async_copy(zbuf, xs_ref.at[pl.ds(j * zr, zr)], zsem.at[0])
                for j in range(xs_ref.shape[0] // zr)]

    def finish_prev(q):
        _pool_project(sd_scr[q], wpool_ref, pscale_ref, mix_scr[q], dc, gd)
        _out_proj_residual(xp_ref, mix_scr[q], wout_ref, x2_ref)

    @pl.when(t == 0)
    def _():
        ext0[0:TAIL, :] = jnp.zeros((TAIL, dc), F32)
        extu0[0:PTAIL, :] = jnp.zeros((PTAIL, dp), F32)
        sd1[...] = jnp.zeros(sd1.shape, BF16)
        mix1[...] = jnp.zeros(mix1.shape, BF16)
        zbuf[...] = jnp.zeros(zbuf.shape, BF16)
        cnt_scr[...] = jnp.zeros(cnt_scr.shape, F32)
        for c in zero_copies():
            c.start()
        in_proj(xp_ref, 0, False)

    for par in range(2):
        @pl.when((b == n_batch) & (s == 0) & (t % 2 == par))
        def _():
            finish_prev(1 - par)

    @pl.when((b == n_batch) & (s == 0))
    def _():
        for c in zero_copies():
            c.wait()

    @pl.when((b == n_batch) & (s > 0))
    def _():
        x2_ref[...] = jnp.zeros(x2_ref.shape, F32)
        hf_ref[...] = jnp.zeros(hf_ref.shape, BF16)
        ids_ref[...] = jnp.zeros(ids_ref.shape, jnp.int32)
        w0_ref[...] = jnp.zeros(w0_ref.shape, F32)
        w1_ref[...] = jnp.zeros(w1_ref.shape, F32)

    def step(p):
        q = 1 - p
        in_proj(xn_ref, q, True)

        R = 16
        for i in range(T // R):
            es = pl.ds(i * R + TAIL, R)
            ext_scr[p][es, :] = ext_scr[p][es, :] * _sigmoid(pg_scr[p][pl.ds(i * R, R), :])

        RC, LC = 32, 128
        BR = RC + TAIL
        shift0 = TAIL - (CONV_W - 1)
        chain = None
        for i in range(T // RC):
            r0 = i * RC
            for lc in range(dc // LC):
                ls = slice(lc * LC, (lc + 1) * LC)
                blk = ext_scr[p][pl.ds(r0, BR), ls]
                if chain is None:
                    acc = jnp.zeros((RC, LC), F32)
                else:
                    z = (lax.bitcast_convert_type(chain, jnp.uint32) >> 16) >> 16
                    acc = jnp.tile(lax.bitcast_convert_type(z, F32), (RC // SUBLANES, 1))
                for sft in range(SUBLANES):
                    taps = [k for k in range(CONV_W) if (k + shift0) % SUBLANES == sft]
                    if not taps:
                        continue
                    rolled = blk if sft == 0 else pltpu.roll(blk, BR - sft, 0)
                    for k in taps:
                        qq = (k + shift0) // SUBLANES
                        acc = acc + wdw_ref[k:k + 1, ls] * rolled[qq * SUBLANES:qq * SUBLANES + RC, :]
                yc_scr[pl.ds(r0, RC), ls] = acc + bdw_ref[:, ls]
                chain = acc[0:SUBLANES, :]

        _layernorm_swish(yc_scr, lng_ref, lnb_ref, mix_scr[p], T, dc, static=True)

        RP = 32
        BP = RP + PTAIL
        pos_base = s * T
        for i in range(T // RP):
            r0 = i * RP
            pos = (pos_base + r0 + lax.broadcasted_iota(jnp.int32, (RP, gd), 0)).astype(F32)
            for g, w in enumerate(POOL_WINDOWS):
                ls = slice(g * gd, (g + 1) * gd)
                blk = extu_scr[p][pl.ds(r0, BP), ls]
                run = blk
                span = 1
                while span < w:
                    run = run + pltpu.roll(run, span, 0)
                    span *= 2
                cnt = jnp.minimum(pos + 1.0, jnp.float32(w))
                mean = run[PTAIL:PTAIL + RP, :] / cnt
                sd_scr[p][pl.ds(r0, RP), ls] = (mean - blk[PTAIL:PTAIL + RP, :]).astype(BF16)

        nconv_ref[...] = ext_scr[p][TAIL + T - (CONV_W - 1):TAIL + T, :]
        npool_ref[...] = extu_scr[p][PTAIL + T - (POOL_MAX_W - 1):PTAIL + T, :]
        keep = s != n_s - 1
        ext_scr[q][0:TAIL, :] = jnp.where(keep, ext_scr[p][T:T + TAIL, :], 0.0)
        extu_scr[q][0:PTAIL, :] = jnp.where(keep, extu_scr[p][T:T + PTAIL, :], 0.0)

        finish_prev(q)

    for par in range(2):
        pl.when((b < n_batch) & (t % 2 == par))(functools.partial(step, par))

    @pl.when(((b < n_batch) & (t > 0)) | ((b == n_batch) & (s == 0)))
    def _():
        _router(x2_ref, gffn_ref, wrh_ref, wrl_ref, br_ref, ltri_ref, hf_ref, ids_ref, w0_ref, w1_ref, cnt_ref,
                hhi_scr, hlo_scr, cnt_scr, T, d)


def _mixer_sample_kernel(x_ref, sconv_ref, spool_ref, gmix_ref, win_ref, wdw_ref, bdw_ref, lng_ref,
                         lnb_ref, wpool_ref, pscale_ref, wout_ref, gffn_ref, wrh_ref, wrl_ref, br_ref,
                         ltri_ref, cnt_in_ref,
                         x2_in, hf_in, ids_in, w0_in, w1_in,
                         x2_ref, hf_ref, ids_ref, w0_ref, w1_ref, nconv_ref, npool_ref, cnt_ref,
                         xt_scr, h_scr, a_scr, pg_scr, u_scr, yc_scr, sd_scr, mix_scr, hhi_scr, hlo_scr, cnt_scr,
                         *, TS, BS, d, dc, dp):
    del x2_in, hf_in, ids_in, w0_in, w1_in

    @pl.when(pl.program_id(0) == 0)
    def _():
        cnt_scr[...] = cnt_in_ref[...]

    T = TS * BS
    gd = dp // len(POOL_WINDOWS)
    NH = CONV_W - 1
    NP = POOL_MAX_W - 1

    for t in range(TS):
        xt_scr[t * BS:(t + 1) * BS, :] = x_ref[t]

    _rmsnorm_to_bf16(xt_scr, gmix_ref, h_scr, T)
    a_scr[...] = jnp.dot(h_scr[...], win_ref[:, 0:dc], preferred_element_type=F32)
    pg_scr[...] = jnp.dot(h_scr[...], win_ref[:, dc:2 * dc], preferred_element_type=F32)
    u_scr[...] = jnp.dot(h_scr[...], win_ref[:, 2 * dc:2 * dc + dp], preferred_element_type=F32)

    R = 16

    def glu(i, c):
        rs = _rows(i, R)
        a_scr[rs, :] = a_scr[rs, :] * _sigmoid(pg_scr[rs, :])
        return c

    lax.fori_loop(0, T // R, glu, 0, unroll=ROW_LOOP_UNROLL)

    def ext_conv(j, rs, ls):
        if j < NH:
            return sconv_ref[j, rs, ls]
        return a_scr[pl.ds((j - NH) * BS + rs.start, rs.size), ls]

    def ext_pool(j, rs, ls):
        if j < NP:
            return spool_ref[j, rs, ls]
        return u_scr[pl.ds((j - NP) * BS + rs.start, rs.size), ls]

    RC, LC = 32, 256

    def conv(i, c):
        rs = _rows(i, RC)
        for t in range(TS):
            for lc in range(dc // LC):
                ls = slice(lc * LC, (lc + 1) * LC)
                acc = jnp.zeros((RC, LC), F32)
                for k in range(CONV_W):
                    acc = acc + wdw_ref[k:k + 1, ls] * ext_conv(t + k, rs, ls)
                yc_scr[pl.ds(t * BS + rs.start, RC), ls] = acc + bdw_ref[:, ls]
        return c

    lax.fori_loop(0, BS // RC, conv, 0)

    _layernorm_swish(yc_scr, lng_ref, lnb_ref, mix_scr, T, dc)

    def pool(i, c):
        rs = _rows(i, RC)
        for t in range(TS):
            for g, w in enumerate(POOL_WINDOWS):
                ls = slice(g * gd, (g + 1) * gd)
                tot = ext_pool(NP + t, rs, ls)
                cur = tot
                for back in range(1, w):
                    tot = tot + ext_pool(NP + t - back, rs, ls)
                sd_scr[pl.ds(t * BS + rs.start, RC), ls] = (tot / jnp.float32(w) - cur).astype(BF16)
        return c

    lax.fori_loop(0, BS // RC, pool, 0)

    _pool_project(sd_scr, wpool_ref, pscale_ref, mix_scr, dc, gd)

    for j in range(NH):
        src = j + TS
        nconv_ref[j] = sconv_ref[src] if src < NH else a_scr[(src - NH) * BS:(src - NH + 1) * BS, :]
    for j in range(NP):
        src = j + TS
        npool_ref[j] = spool_ref[src] if src < NP else u_scr[(src - NP) * BS:(src - NP + 1) * BS, :]

    _out_proj_residual(xt_scr, mix_scr, wout_ref, x2_ref)
    _router(x2_ref, gffn_ref, wrh_ref, wrl_ref, br_ref, ltri_ref, hf_ref, ids_ref, w0_ref, w1_ref, cnt_ref,
            hhi_scr, hlo_scr, cnt_scr, T, d)


def _strict_lower_ones(n):
    return jnp.tril(jnp.ones((n, n), BF16), -1)


def _const_spec(shape):
    nd = len(shape)
    return pl.BlockSpec(shape, lambda *a: (0,) * nd, pipeline_mode=pl.Buffered(1))


def _mixer_weight_specs(d, dc, dp, cols):
    gd = dp // len(POOL_WINDOWS)
    return [
        _const_spec((1, d)),
        _const_spec((d, cols)),
        _const_spec((CONV_W, dc)),
        _const_spec((1, dc)),
        _const_spec((1, dc)),
        _const_spec((1, dc)),
        _const_spec((len(POOL_WINDOWS), gd, gd)),
        _const_spec((1, dp)),
        _const_spec((dc + dp, d)),
        _const_spec((1, d)),
        _const_spec((d, LANES)),
        _const_spec((d, LANES)),
        _const_spec((1, LANES)),
    ]


def _mixer_prompt(x, wts, *, T, n_total, n_sorted_rows, zero_rows):
    B, S, d = x.shape
    dc = wts[2].shape[1]
    dp = wts[7].shape[1]
    cols = wts[1].shape[1]
    n_s = S // T
    N = n_total
    n_blk = N // T
    assert N % T == 0 and N - B * S <= S
    blk = lambda b, s: jnp.clip(b * n_s + s - 1, 0, n_blk - 1)
    tok = lambda b, s: (blk(b, s), 0)
    tok4 = lambda b, s: (blk(b, s), 0, 0, 0)
    bclamp = lambda b: jnp.minimum(b, B - 1)

    def nxt(b, s):
        wrap = s + 1 == n_s
        return (jnp.where(wrap, bclamp(b + 1), bclamp(b)), jnp.where(wrap, 0, s + 1), 0)

    def prv(b, s):
        tp = jnp.clip(b * n_s + s - 1, 0, B * n_s - 1)
        return (tp // n_s, tp % n_s, 0)
    out_shape = (
        jax.ShapeDtypeStruct((N, d), F32),
        jax.ShapeDtypeStruct((N // TOK_CHUNK, TOK_CHUNK, d // LANES, LANES), BF16),
        jax.ShapeDtypeStruct((N, LANES), jnp.int32),
        jax.ShapeDtypeStruct((N, LANES), F32),
        jax.ShapeDtypeStruct((N, LANES), F32),
        jax.ShapeDtypeStruct((1, B, CONV_W - 1, dc), F32),
        jax.ShapeDtypeStruct((1, B, POOL_MAX_W - 1, dp), F32),
        jax.ShapeDtypeStruct((n_sorted_rows, d // LANES, LANES), BF16),
        jax.ShapeDtypeStruct((SUBLANES, LANES), F32),
    )
    assert n_sorted_rows % zero_rows == 0
    out_specs = (
        pl.BlockSpec((T, d), tok),
        pl.BlockSpec((T // TOK_CHUNK, TOK_CHUNK, d // LANES, LANES), tok4),
        pl.BlockSpec((T, LANES), tok),
        pl.BlockSpec((T, LANES), tok),
        pl.BlockSpec((T, LANES), tok),
        pl.BlockSpec((None, None, CONV_W - 1, dc), lambda b, s: (0, bclamp(b), 0, 0)),
        pl.BlockSpec((None, None, POOL_MAX_W - 1, dp), lambda b, s: (0, bclamp(b), 0, 0)),
        pl.BlockSpec(memory_space=pl.ANY),
        pl.BlockSpec((SUBLANES, LANES), lambda b, s: (0, 0)),
    )
    scratch = [
        pltpu.VMEM((T, d), BF16),
        pltpu.VMEM((T, dc), F32), pltpu.VMEM((T, dc), F32),
        pltpu.VMEM((T + TAIL, dc), F32), pltpu.VMEM((T + TAIL, dc), F32),
        pltpu.VMEM((T + PTAIL, dp), F32), pltpu.VMEM((T + PTAIL, dp), F32),
        pltpu.VMEM((T, dc), F32),
        pltpu.VMEM((T, dp), BF16), pltpu.VMEM((T, dp), BF16),
        pltpu.VMEM((T, dc + dp), BF16), pltpu.VMEM((T, dc + dp), BF16),
        pltpu.VMEM((T, d), BF16),
        pltpu.VMEM((T, d), BF16),
        pltpu.VMEM((zero_rows, d // LANES, LANES), BF16),
        pltpu.VMEM((SUBLANES, LANES), F32),
        pltpu.SemaphoreType.DMA((1,)),
    ]
    return pl.pallas_call(
        functools.partial(_mixer_prompt_kernel, T=T, d=d, dc=dc, dp=dp, n_batch=B),
        grid=(B + 1, n_s),
        in_specs=[pl.BlockSpec((None, T, d), prv),
                  pl.BlockSpec((None, T, d), nxt)]
                 + _mixer_weight_specs(d, dc, dp, cols) + [_const_spec((T, T))],
        out_specs=out_specs,
        out_shape=out_shape,
        scratch_shapes=scratch,
        compiler_params=pltpu.CompilerParams(
            dimension_semantics=("arbitrary", "arbitrary"), vmem_limit_bytes=VMEM_LIMIT),
        name="mixer_prompt",
    )(x, x, *wts, _strict_lower_ones(T))


def _mixer_sample(x_t, sconv_t, spool_t, wts, tok_arrays, counts, *, BS, tok0):
    TS, Bd, d = x_t.shape
    dc = wts[2].shape[1]
    dp = wts[7].shape[1]
    cols = wts[1].shape[1]
    T = TS * BS
    n_b = Bd // BS
    b0 = tok0 // T
    tok = lambda i: (b0 + i, 0)
    tok4 = lambda i: (b0 + i, 0, 0, 0)
    out_shape = tuple(jax.ShapeDtypeStruct(a.shape, a.dtype) for a in tok_arrays) + (
        jax.ShapeDtypeStruct((CONV_W - 1, Bd, dc), F32),
        jax.ShapeDtypeStruct((POOL_MAX_W - 1, Bd, dp), F32),
        jax.ShapeDtypeStruct(counts.shape, F32),
    )
    out_specs = (
        pl.BlockSpec((T, d), tok),
        pl.BlockSpec((T // TOK_CHUNK, TOK_CHUNK, d // LANES, LANES), tok4),
        pl.BlockSpec((T, LANES), tok),
        pl.BlockSpec((T, LANES), tok),
        pl.BlockSpec((T, LANES), tok),
        pl.BlockSpec((CONV_W - 1, BS, dc), lambda i: (0, i, 0)),
        pl.BlockSpec((POOL_MAX_W - 1, BS, dp), lambda i: (0, i, 0)),
        pl.BlockSpec(counts.shape, lambda i: (0, 0)),
    )
    scratch = [
        pltpu.VMEM((T, d), F32),
        pltpu.VMEM((T, d), BF16),
        pltpu.VMEM((T, dc), F32),
        pltpu.VMEM((T, dc), F32),
        pltpu.VMEM((T, dp), F32),
        pltpu.VMEM((T, dc), F32),
        pltpu.VMEM((T, dp), BF16),
        pltpu.VMEM((T, dc + dp), BF16),
        pltpu.VMEM((T, d), BF16),
        pltpu.VMEM((T, d), BF16),
        pltpu.VMEM(counts.shape, F32),
    ]
    return pl.pallas_call(
        functools.partial(_mixer_sample_kernel, TS=TS, BS=BS, d=d, dc=dc, dp=dp),
        grid=(n_b,),
        in_specs=[pl.BlockSpec((TS, BS, d), lambda i: (0, i, 0)),
                  pl.BlockSpec((CONV_W - 1, BS, dc), lambda i: (0, i, 0), pipeline_mode=pl.Buffered(1)),
                  pl.BlockSpec((POOL_MAX_W - 1, BS, dp), lambda i: (0, i, 0), pipeline_mode=pl.Buffered(1))]
                 + _mixer_weight_specs(d, dc, dp, cols)
                 + [_const_spec((T, T)), _const_spec(counts.shape)]
                 + [pl.BlockSpec(memory_space=pl.ANY)] * len(tok_arrays),
        out_specs=out_specs,
        out_shape=out_shape,
        scratch_shapes=scratch,
        input_output_aliases={5 + len(wts) + j: j for j in range(len(tok_arrays))},
        compiler_params=pltpu.CompilerParams(
            dimension_semantics=("arbitrary",), vmem_limit_bytes=VMEM_LIMIT),
        name="mixer_sample",
    )(x_t, sconv_t, spool_t, *wts, _strict_lower_ones(T), counts, *tok_arrays)


def _dispatch_kernel(pos_ref, hf_ref, dst_in, xs_in_ref, xs_ref, dst_ref, sem, isem, *, R, n_tok):
    del xs_in_ref
    base = pl.program_id(0) * R

    @pl.when(pl.program_id(0) == 0)
    def _():
        init = pltpu.make_async_copy(dst_in, dst_ref, isem.at[0])
        init.start()
        init.wait()

    G = 8

    def body(g, c):
        r0 = g * G
        ps = [pos_ref[2 * (base + r0) + q] for q in range(2 * G)]
        for q in range(2 * G):
            r, k = r0 + q // 2, q % 2
            pltpu.make_async_copy(hf_ref.at[pl.ds(r, 1)], xs_ref.at[pl.ds(ps[q], 1)], sem.at[0]).start(priority=k)
            dst_ref[ps[q]] = k * n_tok + base + r
        return c

    lax.fori_loop(0, R // G, body, 0)
    for k in range(2):
        pltpu.make_async_copy(hf_ref, xs_ref.at[pl.ds(0, R)], sem.at[0]).wait()


def _dispatch(pos, hf, xs_zero, *, R, n_tiles, tm):
    n_tok = hf.shape[0]
    assert n_tok % R == 0
    p = jnp.arange((n_tiles + 1) * tm, dtype=jnp.int32)
    tile = p // tm
    buf = jnp.where(tile == n_tiles, 1, tile % 2)
    dst_init = 2 * n_tok + buf * tm + p % tm
    hbm = pl.BlockSpec(memory_space=pl.ANY)
    return pl.pallas_call(
        functools.partial(_dispatch_kernel, R=R, n_tok=n_tok),
        grid_spec=pltpu.PrefetchScalarGridSpec(
            num_scalar_prefetch=1,
            grid=(n_tok // R,),
            in_specs=[pl.BlockSpec((R,) + hf.shape[1:], lambda s, pos: (s, 0, 0)), hbm, hbm],
            out_specs=(hbm, pl.BlockSpec(memory_space=pltpu.SMEM)),
            scratch_shapes=[pltpu.SemaphoreType.DMA((1,)), pltpu.SemaphoreType.DMA((1,))],
        ),
        out_shape=(jax.ShapeDtypeStruct(xs_zero.shape, xs_zero.dtype),
                   jax.ShapeDtypeStruct(dst_init.shape, jnp.int32)),
        input_output_aliases={3: 0},
        compiler_params=pltpu.CompilerParams(dimension_semantics=("arbitrary",)),
        name="dispatch",
    )(pos, hf, dst_init, xs_zero)


def _expert_mlp_kernel(te_ref, na_ref, nx_ref, nx2_ref, dst_prev_ref, dst_cur_ref, xs_ref, wg_ref, wu_ref, wd_ref, o_ref,
                       ybuf0, ybuf1, wgf, wuf, wdf, wg_scr, wu_scr, wd_scr, wslot, ssem, wsem,
                       *, tm, n_tiles, d):
    i = pl.program_id(0)
    n_act = na_ref[0]
    active = i < n_act
    slot = i % 2
    last = n_tiles - 1
    prev = te_ref[jnp.maximum(i - 1, 0)]
    new_expert = (i == 0) | (te_ref[i] != prev)
    ybuf = (ybuf0, ybuf1)

    def scatter_start(dst_ref, b):
        for r in range(tm):
            pltpu.make_async_copy(ybuf[b].at[pl.ds(r, 1)], o_ref.at[pl.ds(dst_ref[0, 0, r], 1)],
                                  ssem.at[b]).start(priority=r % 2)

    def scatter_wait(b):
        pltpu.make_async_copy(ybuf[b], o_ref.at[pl.ds(0, tm)], ssem.at[b]).wait()

    def weight_copies(e, b):
        return [pltpu.make_async_copy(src.at[e], buf.at[b], wsem.at[b])
                for src, buf in ((wg_ref, wgf), (wu_ref, wuf), (wd_ref, wdf))]

    @pl.when(i == 0)
    def _():
        wslot[0] = 0
        for c in weight_copies(te_ref[0], 0):
            c.start()

        @pl.when(nx_ref[0] >= 0)
        def _():
            for c in weight_copies(nx_ref[0], 1):
                c.start()
        ybuf1[...] = jnp.zeros(ybuf1.shape, BF16)
        trash0 = pltpu.make_async_copy(ybuf1, o_ref.at[pl.ds(o_ref.shape[0] - 2 * tm, tm)], ssem.at[0])
        trash0.start()
        trash0.wait()

    @pl.when(active & new_expert)
    def _():
        b = wslot[0]
        after_next = nx2_ref[i]

        @pl.when(after_next >= 0)
        def _():
            for c in weight_copies(after_next, (b + 2) % N_WEIGHT_BUFS):
                c.start()

        for c in weight_copies(te_ref[i], b):
            c.wait()
        wg_scr[...] = wgf[b].astype(BF16)
        wu_scr[...] = wuf[b].astype(BF16)
        wd_scr[...] = wdf[b].astype(BF16)
        wslot[0] = (b + 1) % N_WEIGHT_BUFS

    for par in range(2):
        is_par = slot == par

        @pl.when(is_par & (i >= 1) & (i - 2 < n_act))
        def _():
            scatter_wait(par)

        @pl.when(is_par & active)
        def _():
            x = xs_ref[...].reshape(tm, d)
            g = jnp.dot(x, wg_scr[...], preferred_element_type=F32)
            u = jnp.dot(x, wu_scr[...], preferred_element_type=F32)
            act = (g * _sigmoid(g) * u).astype(BF16)
            y = jnp.dot(act, wd_scr[...], preferred_element_type=F32)
            ybuf[par][...] = y.astype(BF16).reshape(tm, d // LANES, LANES)
            scatter_start(dst_prev_ref, 1 - par)

        @pl.when(is_par & (i == n_act))
        def _():
            scatter_start(dst_prev_ref, 1 - par)

        @pl.when(is_par & (i == last))
        def _():
            @pl.when(last - 1 < n_act)
            def _():
                scatter_wait(1 - par)

            @pl.when(last < n_act)
            def _():
                scatter_start(dst_cur_ref, par)
                scatter_wait(par)


def _expert_mlp(tile_expert, n_active, next_expert, after_next_expert, dst, xs, w_gate, w_up, w_down, *, tm, n_tok):
    E, d, de = w_gate.shape
    n_tiles = xs.shape[0] // tm
    row = (tm, d // LANES, LANES)
    cur = lambda i, te, na, nx, nx2: (i, 0, 0)
    prv = lambda i, te, na, nx, nx2: (jnp.where(i == 0, n_tiles, i - 1), 0, 0)
    hbm = pl.BlockSpec(memory_space=pl.ANY)
    smem_tile = lambda imap: pl.BlockSpec((1, 1, tm), imap, memory_space=pltpu.SMEM)
    return pl.pallas_call(
        functools.partial(_expert_mlp_kernel, tm=tm, n_tiles=n_tiles, d=d),
        grid_spec=pltpu.PrefetchScalarGridSpec(
            num_scalar_prefetch=4,
            grid=(n_tiles,),
            in_specs=[
                smem_tile(prv), smem_tile(cur),
                pl.BlockSpec(row, lambda i, te, na, nx, nx2: (jnp.clip(i, 0, jnp.maximum(na[0] - 1, 0)), 0, 0)),
                hbm, hbm, hbm,
            ],
            out_specs=hbm,
            scratch_shapes=[pltpu.VMEM(row, BF16), pltpu.VMEM(row, BF16),
                            pltpu.VMEM((N_WEIGHT_BUFS, d, de), F32), pltpu.VMEM((N_WEIGHT_BUFS, d, de), F32),
                            pltpu.VMEM((N_WEIGHT_BUFS, de, d), F32),
                            pltpu.VMEM((d, de), BF16), pltpu.VMEM((d, de), BF16), pltpu.VMEM((de, d), BF16),
                            pltpu.SMEM((1,), jnp.int32),
                            pltpu.SemaphoreType.DMA((2,)), pltpu.SemaphoreType.DMA((N_WEIGHT_BUFS,))],
        ),
        out_shape=jax.ShapeDtypeStruct((2 * n_tok + 2 * tm, d // LANES, LANES), BF16),
        compiler_params=pltpu.CompilerParams(
            dimension_semantics=("arbitrary",), vmem_limit_bytes=VMEM_LIMIT),
        name="expert_mlp",
    )(tile_expert, n_active, next_expert, after_next_expert, dst, dst, xs, w_gate, w_up, w_down)


def _combine_kernel(x2_ref, o0_ref, o1_ref, w0_ref, w1_ref, gfin_ref, y_ref, *, R, d):
    C = TOK_CHUNK

    def body(i, c):
        rs = _rows(i, C)
        w0 = w0_ref[rs, :]
        w1 = w1_ref[rs, :]
        o0 = o0_ref[i].reshape(C, d).astype(F32)
        o1 = o1_ref[i].reshape(C, d).astype(F32)
        parts = []
        ssq = jnp.zeros((C, LANES), F32)
        for j in range(d // LANES):
            ls = slice(j * LANES, (j + 1) * LANES)
            m = w0 * o0[:, ls] + w1 * o1[:, ls]
            v = x2_ref[rs, ls] + m
            ssq = ssq + v * v
            parts.append(v)
        ms = jnp.sum(ssq, axis=-1, keepdims=True) * jnp.float32(1.0 / d)
        inv = lax.rsqrt(ms + EPS)
        for j in range(d // LANES):
            ls = slice(j * LANES, (j + 1) * LANES)
            y_ref[rs, ls] = parts[j] * inv * gfin_ref[:, ls]
        return c

    lax.fori_loop(0, R // C, body, 0)


def _combine(x2, o, w0, w1, g_final, *, tok0, n_out):
    n_tok, d = x2.shape
    R = max(r for r in R_COMBINE_CHOICES if n_out % r == 0 and tok0 % r == 0 and n_tok % r == 0)
    n_steps = n_out // R
    b0 = tok0 // R
    b1 = n_tok // R
    tok = lambda s: (b0 + s, 0)
    orow = (R // TOK_CHUNK, TOK_CHUNK, d // LANES, LANES)
    o4 = o.reshape(o.shape[0] // TOK_CHUNK, TOK_CHUNK, d // LANES, LANES)
    return pl.pallas_call(
        functools.partial(_combine_kernel, R=R, d=d),
        grid=(n_steps,),
        in_specs=[
            pl.BlockSpec((R, d), tok),
            pl.BlockSpec(orow, lambda s: (b0 + s, 0, 0, 0)),
            pl.BlockSpec(orow, lambda s: (b1 + b0 + s, 0, 0, 0)),
            pl.BlockSpec((R, LANES), tok),
            pl.BlockSpec((R, LANES), tok),
            pl.BlockSpec((1, d), lambda s: (0, 0)),
        ],
        out_specs=pl.BlockSpec((R, d), lambda s: (s, 0)),
        out_shape=jax.ShapeDtypeStruct((n_out, d), F32),
        compiler_params=pltpu.CompilerParams(
            dimension_semantics=("arbitrary",), vmem_limit_bytes=VMEM_LIMIT),
        name="combine",
    )(x2, o4, o4, w0, w1, g_final)


def _routing_plan(ids, counts, tm, n_tiles):
    e0, e1, r0, r1 = ids[:, 0], ids[:, 1], ids[:, 2], ids[:, 3]
    ar = jnp.arange(N_EXPERTS, dtype=jnp.int32)
    tiles_e = (counts + tm - 1) // tm
    tile_end = jnp.cumsum(tiles_e)
    offs = (tile_end - tiles_e) * tm
    pos0 = jnp.sum(jnp.where(e0[:, None] == ar, offs[None, :], 0), axis=1) + r0
    pos1 = jnp.sum(jnp.where(e1[:, None] == ar, offs[None, :], 0), axis=1) + r1
    pos = jnp.stack([pos0, pos1], axis=1).reshape(-1).astype(jnp.int32)
    n_active = tile_end[-1].astype(jnp.int32)
    t = jnp.arange(n_tiles, dtype=jnp.int32)
    tq = jnp.minimum(t, n_active - 1)
    te = jnp.sum((tile_end[None, :] <= tq[:, None]).astype(jnp.int32), axis=1)
    te = jnp.minimum(te, N_EXPERTS - 1)
    later = (ar[None, :] > ar[:, None]) & (tiles_e[None, :] > 0)
    nxt_e = jnp.min(jnp.where(later, ar[None, :], N_EXPERTS), axis=1)
    nxt_e = jnp.where(nxt_e == N_EXPERTS, -1, nxt_e).astype(jnp.int32)
    nxt2_e = jnp.where(nxt_e >= 0, nxt_e[jnp.maximum(nxt_e, 0)], -1)
    return pos, te, n_active.reshape(1), nxt_e[te], nxt2_e[te]


T_PROMPT = 256
BS_SAMPLE = 32
TM_EXPERT = 256
N_WEIGHT_BUFS = 3
R_DISPATCH_CHOICES = (2176, 1088, 512, 256, 128)
R_COMBINE_CHOICES = (512, 256, 128)


def kernel(x_prompt, x_sample, state_conv, state_pool, g_mix, w_in, w_dw, b_dw, ln_g, ln_b, w_pool, pool_scale, w_out, g_ffn, w_rg, b_rg, w_re, b_re, w_gate, w_up, w_down, g_final):
    depth = g_mix.shape[0]
    assert depth == 1
    B, S, d = x_prompt.shape
    Bd, TS, _ = x_sample.shape
    n_p = B * S
    n_s = Bd * TS
    N = n_p + n_s

    assert ROUTER_GROUP_LANE0 == N_EXPERTS
    n_pad = LANES - N_EXPERTS - N_GROUPS
    w_r = jnp.concatenate([w_re[0], w_rg[0], jnp.zeros((d, n_pad), F32)], axis=1)
    w_r_hi = w_r.astype(BF16)
    w_r_lo = (w_r - w_r_hi.astype(F32)).astype(BF16)
    b_r = jnp.concatenate([b_re[0], b_rg[0], jnp.zeros((n_pad,), F32)])[None]
    wts = (g_mix[0][None], w_in[0].astype(BF16), w_dw[0], b_dw[0][None], ln_g[0][None], ln_b[0][None],
           w_pool[0].astype(BF16), pool_scale[0][None], w_out[0].astype(BF16), g_ffn[0][None],
           w_r_hi, w_r_lo, b_r)

    tm = TM_EXPERT
    n_tiles = (2 * N + N_EXPERTS * (tm - 1) + tm - 1) // tm
    *tok_arrays, nconv_p, npool_p, xs, counts_p = _mixer_prompt(x_prompt, wts, T=T_PROMPT, n_total=N,
                                                                n_sorted_rows=n_tiles * tm, zero_rows=tm)

    x_t = jnp.transpose(x_sample, (1, 0, 2))
    sconv_t = jnp.transpose(state_conv[0], (1, 0, 2))
    spool_t = jnp.transpose(state_pool[0], (1, 0, 2))
    x2, hf, ids, w0, w1, nconv_t, npool_t, counts = _mixer_sample(x_t, sconv_t, spool_t, wts, tok_arrays, counts_p,
                                                                  BS=BS_SAMPLE, tok0=n_p)

    pos, tile_expert, n_active, next_expert, after_next = _routing_plan(ids[:, 0:4], counts[0, 0:N_EXPERTS].astype(jnp.int32),
                                                            tm, n_tiles)
    hf = hf.reshape(N, d // LANES, LANES)
    r_disp = max(r for r in R_DISPATCH_CHOICES if N % r == 0)
    xs, dst = _dispatch(pos, hf, xs, R=r_disp, n_tiles=n_tiles, tm=tm)
    dst = dst.reshape(n_tiles + 1, 1, tm)
    o = _expert_mlp(tile_expert, n_active, next_expert, after_next, dst, xs, w_gate[0], w_up[0], w_down[0],
                    tm=tm, n_tok=N)

    gfin = g_final[None]
    y_p = _combine(x2, o, w0, w1, gfin, tok0=0, n_out=n_p)
    y_s = _combine(x2, o, w0, w1, gfin, tok0=n_p, n_out=n_s)

    y_prompt = y_p.reshape(B, S, d)
    y_sample = y_s.reshape(Bd // BS_SAMPLE, TS, BS_SAMPLE, d).transpose(0, 2, 1, 3).reshape(Bd, TS, d)
    new_conv_s = jnp.transpose(nconv_t, (1, 0, 2))[None]
    new_pool_s = jnp.transpose(npool_t, (1, 0, 2))[None]
    return (y_prompt, y_sample, nconv_p, new_conv_s, npool_p, new_pool_s)
```

```python
import functools

import jax
import jax.numpy as jnp
from jax import lax
from jax.experimental import pallas as pl
from jax.experimental.pallas import tpu as pltpu

F32 = jnp.float32
BF16 = jnp.bfloat16
EPS = 1e-6

LANES = 128
SUBLANES = 8
VMEM_LIMIT = 56 * 1024 * 1024

CONV_W = 31
POOL_WINDOWS = (2, 4, 8, 16)
POOL_MAX_W = 16
N_GROUPS = 4
PER_GROUP = 8
N_EXPERTS = N_GROUPS * PER_GROUP

ROUTER_GROUP_LANE0 = N_EXPERTS
TOK_CHUNK = 16
ROW_LOOP_UNROLL = 4
TAIL = 32
PTAIL = 16


def _rows(i, r):
    if isinstance(i, int):
        return pl.ds(i * r, r)
    return pl.ds(pl.multiple_of(i * r, r), r)


def _row_loop(n, body, *, static, unroll=1):
    if static:
        for i in range(n):
            body(i)
    else:
        def step(i, c):
            body(i)
            return c
        lax.fori_loop(0, n, step, 0, unroll=unroll)


def _sigmoid(x):
    return 1.0 / (1.0 + jnp.exp(-x))


def _rmsnorm_to_bf16(src_ref, g_ref, dst_ref, T, static=False):
    R = 16

    def body(i):
        rs = _rows(i, R)
        x = src_ref[rs, :]
        ms = jnp.mean(x * x, axis=-1, keepdims=True)
        dst_ref[rs, :] = (x * lax.rsqrt(ms + EPS) * g_ref[...]).astype(BF16)

    _row_loop(T // R, body, static=static, unroll=ROW_LOOP_UNROLL)


def _zero_from(v):
    return lax.bitcast_convert_type((lax.bitcast_convert_type(v, jnp.uint32) >> 16) >> 16, F32)


def _layernorm_swish(y_ref, lng_ref, lnb_ref, mix_ref, T, dc, static=False):
    R = 16
    chain = [None]

    def body(i):
        rs = _rows(i, R)
        y = y_ref[rs, :]
        mu = jnp.mean(y, axis=-1, keepdims=True)
        if static and chain[0] is not None:
            mu = mu + _zero_from(chain[0])
        d = y - mu
        var = jnp.mean(d * d, axis=-1, keepdims=True)
        z = d * lax.rsqrt(var + EPS) * lng_ref[...] + lnb_ref[...]
        mix_ref[rs, 0:dc] = (z * _sigmoid(z)).astype(BF16)
        chain[0] = z[:, 0:1]

    _row_loop(T // R, body, static=static, unroll=ROW_LOOP_UNROLL)


def _pool_project(sd_ref, wpool_ref, pscale_ref, mix_ref, dc, gd):
    for g in range(len(POOL_WINDOWS)):
        sl = slice(g * gd, (g + 1) * gd)
        o = jnp.dot(sd_ref[:, sl], wpool_ref[g], preferred_element_type=F32)
        mix_ref[:, dc + g * gd: dc + (g + 1) * gd] = (o * pscale_ref[:, sl]).astype(BF16)


def _out_proj_residual(x_ref, mix_ref, wout_ref, x2_ref):
    x2_ref[...] = x_ref[...] + jnp.dot(mix_ref[...], wout_ref[...], preferred_element_type=F32)


def _router(x2_ref, gffn_ref, wrh_ref, wrl_ref, br_ref, ltri_ref, hf_ref, ids_ref, w0_ref, w1_ref, cnt_ref,
            hhi_scr, hlo_scr, cnt_scr, T, d):
    R = TOK_CHUNK

    def body(i, c):
        rs = _rows(i, R)
        x = x2_ref[rs, :]
        ms = jnp.mean(x * x, axis=-1, keepdims=True)
        h = x * lax.rsqrt(ms + EPS) * gffn_ref[...]
        hi = h.astype(BF16)
        hf_ref[i] = hi.reshape(R, d // LANES, LANES)
        hhi_scr[rs, :] = hi
        hlo_scr[rs, :] = (h - hi.astype(F32)).astype(BF16)
        return c

    lax.fori_loop(0, T // R, body, 0, unroll=ROW_LOOP_UNROLL)

    lg = (jnp.dot(hhi_scr[...], wrh_ref[...], preferred_element_type=F32)
          + jnp.dot(hlo_scr[...], wrh_ref[...], preferred_element_type=F32)
          + jnp.dot(hhi_scr[...], wrl_ref[...], preferred_element_type=F32)
          + br_ref[...])

    lane = lax.broadcasted_iota(jnp.int32, lg.shape, 1).astype(F32)
    neg = jnp.float32(-jnp.inf)
    big = jnp.float32(1e9)
    g_lo = jnp.float32(ROUTER_GROUP_LANE0)
    gmask = (lane >= g_lo) & (lane < g_lo + N_GROUPS)
    lgg = jnp.where(gmask, lg, neg)
    gmax = jnp.max(lgg, axis=-1, keepdims=True)
    gsel = jnp.min(jnp.where(lgg == gmax, lane, big), axis=-1, keepdims=True) - g_lo
    gsum = jnp.sum(jnp.where(gmask, jnp.exp(lg - gmax), 0.0), axis=-1, keepdims=True)
    p_g = 1.0 / gsum

    e_lo = gsel * PER_GROUP
    emask = (lane >= e_lo) & (lane < e_lo + PER_GROUP)
    le = jnp.where(emask, lg, neg)
    v0 = jnp.max(le, axis=-1, keepdims=True)
    i0 = jnp.min(jnp.where(le == v0, lane, big), axis=-1, keepdims=True)
    le2 = jnp.where(lane == i0, neg, le)
    v1 = jnp.max(le2, axis=-1, keepdims=True)
    i1 = jnp.min(jnp.where(le2 == v1, lane, big), axis=-1, keepdims=True)
    ex = jnp.exp(v1 - v0)
    den = 1.0 / (1.0 + ex)
    w0 = den * p_g
    w1 = ex * den * p_g

    sel0 = lane == i0
    sel1 = lane == i1
    m = jnp.where(sel0 | sel1, 1.0, 0.0)
    before = jnp.dot(ltri_ref[...], m.astype(BF16), preferred_element_type=F32) + cnt_scr[0:1, :]
    rank0 = jnp.sum(jnp.where(sel0, before, 0.0), axis=-1, keepdims=True)
    rank1 = jnp.sum(jnp.where(sel1, before, 0.0), axis=-1, keepdims=True)
    total = cnt_scr[0:1, :] + jnp.sum(m, axis=0, keepdims=True)
    cnt_scr[...] = jnp.broadcast_to(total, cnt_scr.shape)
    cnt_ref[...] = jnp.broadcast_to(total, cnt_ref.shape)

    packed = jnp.where(lane == 0.0, i0, jnp.where(lane == 1.0, i1,
                                                  jnp.where(lane == 2.0, rank0, jnp.where(lane == 3.0, rank1, 0.0))))
    ids_ref[...] = packed.astype(jnp.int32)
    w0_ref[...] = jnp.broadcast_to(w0, lg.shape)
    w1_ref[...] = jnp.broadcast_to(w1, lg.shape)


def _mixer_prompt_kernel(xp_ref, xn_ref, gmix_ref, win_ref, wdw_ref, bdw_ref, lng_ref, lnb_ref, wpool_ref,
                         pscale_ref, wout_ref, gffn_ref, wrh_ref, wrl_ref, br_ref, ltri_ref,
                         x2_ref, hf_ref, ids_ref, w0_ref, w1_ref, nconv_ref, npool_ref, xs_ref, cnt_ref,
                         h_scr, pg0, pg1, ext0, ext1, extu0, extu1, yc_scr, sd0, sd1, mix0, mix1, hhi_scr, hlo_scr,
                         zbuf, cnt_scr, zsem, *, T, d, dc, dp, n_batch):
    n_s = pl.num_programs(1)
    b = pl.program_id(0)
    s = pl.program_id(1)
    t = b * n_s + s
    gd = dp // len(POOL_WINDOWS)
    pg_scr, ext_scr, extu_scr, sd_scr, mix_scr = (pg0, pg1), (ext0, ext1), (extu0, extu1), (sd0, sd1), (mix0, mix1)

    def in_proj(x_ref, slot, static):
        _rmsnorm_to_bf16(x_ref, gmix_ref, h_scr, T, static=static)
        ext_scr[slot][TAIL:TAIL + T, :] = jnp.dot(h_scr[...], win_ref[:, 0:dc], preferred_element_type=F32)
        pg_scr[slot][...] = jnp.dot(h_scr[...], win_ref[:, dc:2 * dc], preferred_element_type=F32)
        extu_scr[slot][PTAIL:PTAIL + T, :] = jnp.dot(h_scr[...], win_ref[:, 2 * dc:2 * dc + dp],
                                                     preferred_element_type=F32)

    def zero_copies():
        zr = zbuf.shape[0]
        return [pltpu.make_async_copy(zbuf, xs_ref.at[pl.ds(j * zr, zr)], zsem.at[0])
                for j in range(xs_ref.shape[0] // zr)]

    def finish_prev(q):
        _pool_project(sd_scr[q], wpool_ref, pscale_ref, mix_scr[q], dc, gd)
        _out_proj_residual(xp_ref, mix_scr[q], wout_ref, x2_ref)

    @pl.when(t == 0)
    def _():
        ext0[0:TAIL, :] = jnp.zeros((TAIL, dc), F32)
        extu0[0:PTAIL, :] = jnp.zeros((PTAIL, dp), F32)
        sd1[...] = jnp.zeros(sd1.shape, BF16)
        mix1[...] = jnp.zeros(mix1.shape, BF16)
        zbuf[...] = jnp.zeros(zbuf.shape, BF16)
        cnt_scr[...] = jnp.zeros(cnt_scr.shape, F32)
        for c in zero_copies():
            c.start()
        in_proj(xp_ref, 0, False)

    for par in range(2):
        @pl.when((b == n_batch) & (s == 0) & (t % 2 == par))
        def _():
            finish_prev(1 - par)

    @pl.when((b == n_batch) & (s == 0))
    def _():
        for c in zero_copies():
            c.wait()

    @pl.when((b == n_batch) & (s > 0))
    def _():
        x2_ref[...] = jnp.zeros(x2_ref.shape, F32)
        hf_ref[...] = jnp.zeros(hf_ref.shape, BF16)
        ids_ref[...] = jnp.zeros(ids_ref.shape, jnp.int32)
        w0_ref[...] = jnp.zeros(w0_ref.shape, F32)
        w1_ref[...] = jnp.zeros(w1_ref.shape, F32)

    def step(p):
        q = 1 - p
        in_proj(xn_ref, q, True)

        R = 16
        for i in range(T // R):
            es = pl.ds(i * R + TAIL, R)
            ext_scr[p][es, :] = ext_scr[p][es, :] * _sigmoid(pg_scr[p][pl.ds(i * R, R), :])

        RC, LC = 32, 128
        BR = RC + TAIL
        shift0 = TAIL - (CONV_W - 1)
        chain = None
        for i in range(T // RC):
            r0 = i * RC
            for lc in range(dc // LC):
                ls = slice(lc * LC, (lc + 1) * LC)
                blk = ext_scr[p][pl.ds(r0, BR), ls]
                if chain is None:
                    acc = jnp.zeros((RC, LC), F32)
                else:
                    acc = jnp.tile(_zero_from(chain), (RC // SUBLANES, 1))
                for sft in range(SUBLANES):
                    taps = [k for k in range(CONV_W) if (k + shift0) % SUBLANES == sft]
                    if not taps:
                        continue
                    rolled = blk if sft == 0 else pltpu.roll(blk, BR - sft, 0)
                    for k in taps:
                        qq = (k + shift0) // SUBLANES
                        acc = acc + wdw_ref[k:k + 1, ls] * rolled[qq * SUBLANES:qq * SUBLANES + RC, :]
                yc_scr[pl.ds(r0, RC), ls] = acc + bdw_ref[:, ls]
                chain = acc[0:SUBLANES, :]

        _layernorm_swish(yc_scr, lng_ref, lnb_ref, mix_scr[p], T, dc, static=True)

        RP = 32
        BP = RP + PTAIL
        pos_base = s * T
        pchain = None
        for i in range(T // RP):
            r0 = i * RP
            pos = (pos_base + r0 + lax.broadcasted_iota(jnp.int32, (RP, gd), 0)).astype(F32)
            if pchain is not None:
                pos = pos + _zero_from(pchain)
            for g, w in enumerate(POOL_WINDOWS):
                ls = slice(g * gd, (g + 1) * gd)
                blk = extu_scr[p][pl.ds(r0, BP), ls]
                run = blk
                span = 1
                while span < w:
                    run = run + pltpu.roll(run, span, 0)
                    span *= 2
                cnt = jnp.minimum(pos + 1.0, jnp.float32(w))
                mean = run[PTAIL:PTAIL + RP, :] / cnt
                diff = mean - blk[PTAIL:PTAIL + RP, :]
                sd_scr[p][pl.ds(r0, RP), ls] = diff.astype(BF16)
            pchain = diff[:, 0:1]

        nconv_ref[...] = ext_scr[p][TAIL + T - (CONV_W - 1):TAIL + T, :]
        npool_ref[...] = extu_scr[p][PTAIL + T - (POOL_MAX_W - 1):PTAIL + T, :]
        keep = s != n_s - 1
        ext_scr[q][0:TAIL, :] = jnp.where(keep, ext_scr[p][T:T + TAIL, :], 0.0)
        extu_scr[q][0:PTAIL, :] = jnp.where(keep, extu_scr[p][T:T + PTAIL, :], 0.0)

        finish_prev(q)

    for par in range(2):
        pl.when((b < n_batch) & (t % 2 == par))(functools.partial(step, par))

    @pl.when(((b < n_batch) & (t > 0)) | ((b == n_batch) & (s == 0)))
    def _():
        _router(x2_ref, gffn_ref, wrh_ref, wrl_ref, br_ref, ltri_ref, hf_ref, ids_ref, w0_ref, w1_ref, cnt_ref,
                hhi_scr, hlo_scr, cnt_scr, T, d)


def _mixer_sample_kernel(x_ref, sconv_ref, spool_ref, gmix_ref, win_ref, wdw_ref, bdw_ref, lng_ref,
                         lnb_ref, wpool_ref, pscale_ref, wout_ref, gffn_ref, wrh_ref, wrl_ref, br_ref,
                         ltri_ref, cnt_in_ref,
                         x2_in, hf_in, ids_in, w0_in, w1_in,
                         x2_ref, hf_ref, ids_ref, w0_ref, w1_ref, nconv_ref, npool_ref, cnt_ref,
                         xt_scr, h_scr, a_scr, pg_scr, u_scr, yc_scr, sd_scr, mix_scr, hhi_scr, hlo_scr, cnt_scr,
                         *, TS, BS, d, dc, dp):
    del x2_in, hf_in, ids_in, w0_in, w1_in

    @pl.when(pl.program_id(0) == 0)
    def _():
        cnt_scr[...] = cnt_in_ref[...]

    T = TS * BS
    gd = dp // len(POOL_WINDOWS)
    NH = CONV_W - 1
    NP = POOL_MAX_W - 1

    for t in range(TS):
        xt_scr[t * BS:(t + 1) * BS, :] = x_ref[t]

    _rmsnorm_to_bf16(xt_scr, gmix_ref, h_scr, T)
    a_scr[...] = jnp.dot(h_scr[...], win_ref[:, 0:dc], preferred_element_type=F32)
    pg_scr[...] = jnp.dot(h_scr[...], win_ref[:, dc:2 * dc], preferred_element_type=F32)
    u_scr[...] = jnp.dot(h_scr[...], win_ref[:, 2 * dc:2 * dc + dp], preferred_element_type=F32)

    R = 16

    def glu(i, c):
        rs = _rows(i, R)
        a_scr[rs, :] = a_scr[rs, :] * _sigmoid(pg_scr[rs, :])
        return c

    lax.fori_loop(0, T // R, glu, 0, unroll=ROW_LOOP_UNROLL)

    def ext_conv(j, rs, ls):
        if j < NH:
            return sconv_ref[j, rs, ls]
        return a_scr[pl.ds((j - NH) * BS + rs.start, rs.size), ls]

    def ext_pool(j, rs, ls):
        if j < NP:
            return spool_ref[j, rs, ls]
        return u_scr[pl.ds((j - NP) * BS + rs.start, rs.size), ls]

    RC, LC = 32, 256

    def conv(i, c):
        rs = _rows(i, RC)
        for t in range(TS):
            for lc in range(dc // LC):
                ls = slice(lc * LC, (lc + 1) * LC)
                acc = jnp.zeros((RC, LC), F32)
                for k in range(CONV_W):
                    acc = acc + wdw_ref[k:k + 1, ls] * ext_conv(t + k, rs, ls)
                yc_scr[pl.ds(t * BS + rs.start, RC), ls] = acc + bdw_ref[:, ls]
        return c

    lax.fori_loop(0, BS // RC, conv, 0)

    _layernorm_swish(yc_scr, lng_ref, lnb_ref, mix_scr, T, dc)

    def pool(i, c):
        rs = _rows(i, RC)
        for t in range(TS):
            for g, w in enumerate(POOL_WINDOWS):
                ls = slice(g * gd, (g + 1) * gd)
                tot = ext_pool(NP + t, rs, ls)
                cur = tot
                for back in range(1, w):
                    tot = tot + ext_pool(NP + t - back, rs, ls)
                sd_scr[pl.ds(t * BS + rs.start, RC), ls] = (tot / jnp.float32(w) - cur).astype(BF16)
        return c

    lax.fori_loop(0, BS // RC, pool, 0)

    _pool_project(sd_scr, wpool_ref, pscale_ref, mix_scr, dc, gd)

    for j in range(NH):
        src = j + TS
        nconv_ref[j] = sconv_ref[src] if src < NH else a_scr[(src - NH) * BS:(src - NH + 1) * BS, :]
    for j in range(NP):
        src = j + TS
        npool_ref[j] = spool_ref[src] if src < NP else u_scr[(src - NP) * BS:(src - NP + 1) * BS, :]

    _out_proj_residual(xt_scr, mix_scr, wout_ref, x2_ref)
    _router(x2_ref, gffn_ref, wrh_ref, wrl_ref, br_ref, ltri_ref, hf_ref, ids_ref, w0_ref, w1_ref, cnt_ref,
            hhi_scr, hlo_scr, cnt_scr, T, d)


def _strict_lower_ones(n):
    return jnp.tril(jnp.ones((n, n), BF16), -1)


def _const_spec(shape):
    nd = len(shape)
    return pl.BlockSpec(shape, lambda *a: (0,) * nd, pipeline_mode=pl.Buffered(1))


def _mixer_weight_specs(d, dc, dp, cols):
    gd = dp // len(POOL_WINDOWS)
    return [
        _const_spec((1, d)),
        _const_spec((d, cols)),
        _const_spec((CONV_W, dc)),
        _const_spec((1, dc)),
        _const_spec((1, dc)),
        _const_spec((1, dc)),
        _const_spec((len(POOL_WINDOWS), gd, gd)),
        _const_spec((1, dp)),
        _const_spec((dc + dp, d)),
        _const_spec((1, d)),
        _const_spec((d, LANES)),
        _const_spec((d, LANES)),
        _const_spec((1, LANES)),
    ]


def _mixer_prompt(x, wts, *, T, n_total, n_sorted_rows, zero_rows):
    B, S, d = x.shape
    dc = wts[2].shape[1]
    dp = wts[7].shape[1]
    cols = wts[1].shape[1]
    n_s = S // T
    N = n_total
    n_blk = N // T
    assert N % T == 0 and N - B * S <= S
    blk = lambda b, s: jnp.clip(b * n_s + s - 1, 0, n_blk - 1)
    tok = lambda b, s: (blk(b, s), 0)
    tok4 = lambda b, s: (blk(b, s), 0, 0, 0)
    bclamp = lambda b: jnp.minimum(b, B - 1)

    def nxt(b, s):
        wrap = s + 1 == n_s
        return (jnp.where(wrap, bclamp(b + 1), bclamp(b)), jnp.where(wrap, 0, s + 1), 0)

    def prv(b, s):
        tp = jnp.clip(b * n_s + s - 1, 0, B * n_s - 1)
        return (tp // n_s, tp % n_s, 0)
    out_shape = (
        jax.ShapeDtypeStruct((N, d), F32),
        jax.ShapeDtypeStruct((N // TOK_CHUNK, TOK_CHUNK, d // LANES, LANES), BF16),
        jax.ShapeDtypeStruct((N, LANES), jnp.int32),
        jax.ShapeDtypeStruct((N, LANES), F32),
        jax.ShapeDtypeStruct((N, LANES), F32),
        jax.ShapeDtypeStruct((1, B, CONV_W - 1, dc), F32),
        jax.ShapeDtypeStruct((1, B, POOL_MAX_W - 1, dp), F32),
        jax.ShapeDtypeStruct((n_sorted_rows, d // LANES, LANES), BF16),
        jax.ShapeDtypeStruct((SUBLANES, LANES), F32),
    )
    assert n_sorted_rows % zero_rows == 0
    out_specs = (
        pl.BlockSpec((T, d), tok),
        pl.BlockSpec((T // TOK_CHUNK, TOK_CHUNK, d // LANES, LANES), tok4),
        pl.BlockSpec((T, LANES), tok),
        pl.BlockSpec((T, LANES), tok),
        pl.BlockSpec((T, LANES), tok),
        pl.BlockSpec((None, None, CONV_W - 1, dc), lambda b, s: (0, bclamp(b), 0, 0)),
        pl.BlockSpec((None, None, POOL_MAX_W - 1, dp), lambda b, s: (0, bclamp(b), 0, 0)),
        pl.BlockSpec(memory_space=pl.ANY),
        pl.BlockSpec((SUBLANES, LANES), lambda b, s: (0, 0)),
    )
    scratch = [
        pltpu.VMEM((T, d), BF16),
        pltpu.VMEM((T, dc), F32), pltpu.VMEM((T, dc), F32),
        pltpu.VMEM((T + TAIL, dc), F32), pltpu.VMEM((T + TAIL, dc), F32),
        pltpu.VMEM((T + PTAIL, dp), F32), pltpu.VMEM((T + PTAIL, dp), F32),
        pltpu.VMEM((T, dc), F32),
        pltpu.VMEM((T, dp), BF16), pltpu.VMEM((T, dp), BF16),
        pltpu.VMEM((T, dc + dp), BF16), pltpu.VMEM((T, dc + dp), BF16),
        pltpu.VMEM((T, d), BF16),
        pltpu.VMEM((T, d), BF16),
        pltpu.VMEM((zero_rows, d // LANES, LANES), BF16),
        pltpu.VMEM((SUBLANES, LANES), F32),
        pltpu.SemaphoreType.DMA((1,)),
    ]
    return pl.pallas_call(
        functools.partial(_mixer_prompt_kernel, T=T, d=d, dc=dc, dp=dp, n_batch=B),
        grid=(B + 1, n_s),
        in_specs=[pl.BlockSpec((None, T, d), prv),
                  pl.BlockSpec((None, T, d), nxt)]
                 + _mixer_weight_specs(d, dc, dp, cols) + [_const_spec((T, T))],
        out_specs=out_specs,
        out_shape=out_shape,
        scratch_shapes=scratch,
        compiler_params=pltpu.CompilerParams(
            dimension_semantics=("arbitrary", "arbitrary"), vmem_limit_bytes=VMEM_LIMIT),
        name="mixer_prompt",
    )(x, x, *wts, _strict_lower_ones(T))


def _mixer_sample(x_t, sconv_t, spool_t, wts, tok_arrays, counts, *, BS, tok0):
    TS, Bd, d = x_t.shape
    dc = wts[2].shape[1]
    dp = wts[7].shape[1]
    cols = wts[1].shape[1]
    T = TS * BS
    n_b = Bd // BS
    b0 = tok0 // T
    tok = lambda i: (b0 + i, 0)
    tok4 = lambda i: (b0 + i, 0, 0, 0)
    out_shape = tuple(jax.ShapeDtypeStruct(a.shape, a.dtype) for a in tok_arrays) + (
        jax.ShapeDtypeStruct((CONV_W - 1, Bd, dc), F32),
        jax.ShapeDtypeStruct((POOL_MAX_W - 1, Bd, dp), F32),
        jax.ShapeDtypeStruct(counts.shape, F32),
    )
    out_specs = (
        pl.BlockSpec((T, d), tok),
        pl.BlockSpec((T // TOK_CHUNK, TOK_CHUNK, d // LANES, LANES), tok4),
        pl.BlockSpec((T, LANES), tok),
        pl.BlockSpec((T, LANES), tok),
        pl.BlockSpec((T, LANES), tok),
        pl.BlockSpec((CONV_W - 1, BS, dc), lambda i: (0, i, 0)),
        pl.BlockSpec((POOL_MAX_W - 1, BS, dp), lambda i: (0, i, 0)),
        pl.BlockSpec(counts.shape, lambda i: (0, 0)),
    )
    scratch = [
        pltpu.VMEM((T, d), F32),
        pltpu.VMEM((T, d), BF16),
        pltpu.VMEM((T, dc), F32),
        pltpu.VMEM((T, dc), F32),
        pltpu.VMEM((T, dp), F32),
        pltpu.VMEM((T, dc), F32),
        pltpu.VMEM((T, dp), BF16),
        pltpu.VMEM((T, dc + dp), BF16),
        pltpu.VMEM((T, d), BF16),
        pltpu.VMEM((T, d), BF16),
        pltpu.VMEM(counts.shape, F32),
    ]
    return pl.pallas_call(
        functools.partial(_mixer_sample_kernel, TS=TS, BS=BS, d=d, dc=dc, dp=dp),
        grid=(n_b,),
        in_specs=[pl.BlockSpec((TS, BS, d), lambda i: (0, i, 0)),
                  pl.BlockSpec((CONV_W - 1, BS, dc), lambda i: (0, i, 0), pipeline_mode=pl.Buffered(1)),
                  pl.BlockSpec((POOL_MAX_W - 1, BS, dp), lambda i: (0, i, 0), pipeline_mode=pl.Buffered(1))]
                 + _mixer_weight_specs(d, dc, dp, cols)
                 + [_const_spec((T, T)), _const_spec(counts.shape)]
                 + [pl.BlockSpec(memory_space=pl.ANY)] * len(tok_arrays),
        out_specs=out_specs,
        out_shape=out_shape,
        scratch_shapes=scratch,
        input_output_aliases={5 + len(wts) + j: j for j in range(len(tok_arrays))},
        compiler_params=pltpu.CompilerParams(
            dimension_semantics=("arbitrary",), vmem_limit_bytes=VMEM_LIMIT),
        name="mixer_sample",
    )(x_t, sconv_t, spool_t, *wts, _strict_lower_ones(T), counts, *tok_arrays)


def _dispatch_kernel(pos_ref, hf_ref, dst_in, xs_in_ref, xs_ref, dst_ref, sem, isem, *, R, n_tok):
    del xs_in_ref
    base = pl.program_id(0) * R

    @pl.when(pl.program_id(0) == 0)
    def _():
        init = pltpu.make_async_copy(dst_in, dst_ref, isem.at[0])
        init.start()
        init.wait()

    G = 8

    def body(g, c):
        r0 = g * G
        ps = [pos_ref[2 * (base + r0) + q] for q in range(2 * G)]
        for q in range(2 * G):
            r, k = r0 + q // 2, q % 2
            pltpu.make_async_copy(hf_ref.at[pl.ds(r, 1)], xs_ref.at[pl.ds(ps[q], 1)], sem.at[0]).start(priority=k)
            dst_ref[ps[q]] = k * n_tok + base + r
        return c

    lax.fori_loop(0, R // G, body, 0)
    for k in range(2):
        pltpu.make_async_copy(hf_ref, xs_ref.at[pl.ds(0, R)], sem.at[0]).wait()


def _dispatch(pos, hf, xs_zero, *, R, n_tiles, tm):
    n_tok = hf.shape[0]
    assert n_tok % R == 0
    p = jnp.arange((n_tiles + 1) * tm, dtype=jnp.int32)
    tile = p // tm
    buf = jnp.where(tile == n_tiles, 1, tile % 2)
    dst_init = 2 * n_tok + buf * tm + p % tm
    hbm = pl.BlockSpec(memory_space=pl.ANY)
    return pl.pallas_call(
        functools.partial(_dispatch_kernel, R=R, n_tok=n_tok),
        grid_spec=pltpu.PrefetchScalarGridSpec(
            num_scalar_prefetch=1,
            grid=(n_tok // R,),
            in_specs=[pl.BlockSpec((R,) + hf.shape[1:], lambda s, pos: (s, 0, 0)), hbm, hbm],
            out_specs=(hbm, pl.BlockSpec(memory_space=pltpu.SMEM)),
            scratch_shapes=[pltpu.SemaphoreType.DMA((1,)), pltpu.SemaphoreType.DMA((1,))],
        ),
        out_shape=(jax.ShapeDtypeStruct(xs_zero.shape, xs_zero.dtype),
                   jax.ShapeDtypeStruct(dst_init.shape, jnp.int32)),
        input_output_aliases={3: 0},
        compiler_params=pltpu.CompilerParams(dimension_semantics=("arbitrary",)),
        name="dispatch",
    )(pos, hf, dst_init, xs_zero)


def _expert_mlp_kernel(te_ref, na_ref, nx_ref, nx2_ref, dst_prev_ref, dst_cur_ref, xs_ref, wg_ref, wu_ref, wd_ref, o_ref,
                       ybuf0, ybuf1, wgf, wuf, wdf, wg_scr, wu_scr, wd_scr, wslot, ssem, wsem,
                       *, tm, n_tiles, d):
    i = pl.program_id(0)
    n_act = na_ref[0]
    active = i < n_act
    slot = i % 2
    last = n_tiles - 1
    prev = te_ref[jnp.maximum(i - 1, 0)]
    new_expert = (i == 0) | (te_ref[i] != prev)
    ybuf = (ybuf0, ybuf1)

    def scatter_start(dst_ref, b):
        for r in range(tm):
            pltpu.make_async_copy(ybuf[b].at[pl.ds(r, 1)], o_ref.at[pl.ds(dst_ref[0, 0, r], 1)],
                                  ssem.at[b]).start(priority=r % 2)

    def scatter_wait(b):
        pltpu.make_async_copy(ybuf[b], o_ref.at[pl.ds(0, tm)], ssem.at[b]).wait()

    def weight_copies(e, b):
        return [pltpu.make_async_copy(src.at[e], buf.at[b], wsem.at[b])
                for src, buf in ((wg_ref, wgf), (wu_ref, wuf), (wd_ref, wdf))]

    @pl.when(i == 0)
    def _():
        wslot[0] = 0
        for c in weight_copies(te_ref[0], 0):
            c.start()

        @pl.when(nx_ref[0] >= 0)
        def _():
            for c in weight_copies(nx_ref[0], 1):
                c.start()
        ybuf1[...] = jnp.zeros(ybuf1.shape, BF16)
        trash0 = pltpu.make_async_copy(ybuf1, o_ref.at[pl.ds(o_ref.shape[0] - 2 * tm, tm)], ssem.at[0])
        trash0.start()
        trash0.wait()

    @pl.when(active & new_expert)
    def _():
        b = wslot[0]
        after_next = nx2_ref[i]

        @pl.when(after_next >= 0)
        def _():
            for c in weight_copies(after_next, (b + 2) % N_WEIGHT_BUFS):
                c.start()

        for c in weight_copies(te_ref[i], b):
            c.wait()
        wg_scr[...] = wgf[b].astype(BF16)
        wu_scr[...] = wuf[b].astype(BF16)
        wd_scr[...] = wdf[b].astype(BF16)
        wslot[0] = (b + 1) % N_WEIGHT_BUFS

    for par in range(2):
        is_par = slot == par

        @pl.when(is_par & (i >= 1) & (i - 2 < n_act))
        def _():
            scatter_wait(par)

        @pl.when(is_par & active)
        def _():
            x = xs_ref[...].reshape(tm, d)
            g = jnp.dot(x, wg_scr[...], preferred_element_type=F32)
            u = jnp.dot(x, wu_scr[...], preferred_element_type=F32)
            act = (g * _sigmoid(g) * u).astype(BF16)
            y = jnp.dot(act, wd_scr[...], preferred_element_type=F32)
            ybuf[par][...] = y.astype(BF16).reshape(tm, d // LANES, LANES)
            scatter_start(dst_prev_ref, 1 - par)

        @pl.when(is_par & (i == n_act))
        def _():
            scatter_start(dst_prev_ref, 1 - par)

        @pl.when(is_par & (i == last))
        def _():
            @pl.when(last - 1 < n_act)
            def _():
                scatter_wait(1 - par)

            @pl.when(last < n_act)
            def _():
                scatter_start(dst_cur_ref, par)
                scatter_wait(par)


def _expert_mlp(tile_expert, n_active, next_expert, after_next_expert, dst, xs, w_gate, w_up, w_down, *, tm, n_tok):
    E, d, de = w_gate.shape
    n_tiles = xs.shape[0] // tm
    row = (tm, d // LANES, LANES)
    cur = lambda i, te, na, nx, nx2: (i, 0, 0)
    prv = lambda i, te, na, nx, nx2: (jnp.where(i == 0, n_tiles, i - 1), 0, 0)
    hbm = pl.BlockSpec(memory_space=pl.ANY)
    smem_tile = lambda imap: pl.BlockSpec((1, 1, tm), imap, memory_space=pltpu.SMEM)
    return pl.pallas_call(
        functools.partial(_expert_mlp_kernel, tm=tm, n_tiles=n_tiles, d=d),
        grid_spec=pltpu.PrefetchScalarGridSpec(
            num_scalar_prefetch=4,
            grid=(n_tiles,),
            in_specs=[
                smem_tile(prv), smem_tile(cur),
                pl.BlockSpec(row, lambda i, te, na, nx, nx2: (jnp.clip(i, 0, jnp.maximum(na[0] - 1, 0)), 0, 0)),
                hbm, hbm, hbm,
            ],
            out_specs=hbm,
            scratch_shapes=[pltpu.VMEM(row, BF16), pltpu.VMEM(row, BF16),
                            pltpu.VMEM((N_WEIGHT_BUFS, d, de), F32), pltpu.VMEM((N_WEIGHT_BUFS, d, de), F32),
                            pltpu.VMEM((N_WEIGHT_BUFS, de, d), F32),
                            pltpu.VMEM((d, de), BF16), pltpu.VMEM((d, de), BF16), pltpu.VMEM((de, d), BF16),
                            pltpu.SMEM((1,), jnp.int32),
                            pltpu.SemaphoreType.DMA((2,)), pltpu.SemaphoreType.DMA((N_WEIGHT_BUFS,))],
        ),
        out_shape=jax.ShapeDtypeStruct((2 * n_tok + 2 * tm, d // LANES, LANES), BF16),
        compiler_params=pltpu.CompilerParams(
            dimension_semantics=("arbitrary",), vmem_limit_bytes=VMEM_LIMIT),
        name="expert_mlp",
    )(tile_expert, n_active, next_expert, after_next_expert, dst, dst, xs, w_gate, w_up, w_down)


def _combine_kernel(x2_ref, o0_ref, o1_ref, w0_ref, w1_ref, gfin_ref, y_ref, *, R, d):
    C = TOK_CHUNK

    def body(i, c):
        rs = _rows(i, C)
        w0 = w0_ref[rs, :]
        w1 = w1_ref[rs, :]
        o0 = o0_ref[i].reshape(C, d).astype(F32)
        o1 = o1_ref[i].reshape(C, d).astype(F32)
        parts = []
        ssq = jnp.zeros((C, LANES), F32)
        for j in range(d // LANES):
            ls = slice(j * LANES, (j + 1) * LANES)
            m = w0 * o0[:, ls] + w1 * o1[:, ls]
            v = x2_ref[rs, ls] + m
            ssq = ssq + v * v
            parts.append(v)
        ms = jnp.sum(ssq, axis=-1, keepdims=True) * jnp.float32(1.0 / d)
        inv = lax.rsqrt(ms + EPS)
        for j in range(d // LANES):
            ls = slice(j * LANES, (j + 1) * LANES)
            y_ref[rs, ls] = parts[j] * inv * gfin_ref[:, ls]
        return c

    lax.fori_loop(0, R // C, body, 0)


def _combine(x2, o, w0, w1, g_final, *, tok0, n_out):
    n_tok, d = x2.shape
    R = max(r for r in R_COMBINE_CHOICES if n_out % r == 0 and tok0 % r == 0 and n_tok % r == 0)
    n_steps = n_out // R
    b0 = tok0 // R
    b1 = n_tok // R
    tok = lambda s: (b0 + s, 0)
    orow = (R // TOK_CHUNK, TOK_CHUNK, d // LANES, LANES)
    o4 = o.reshape(o.shape[0] // TOK_CHUNK, TOK_CHUNK, d // LANES, LANES)
    return pl.pallas_call(
        functools.partial(_combine_kernel, R=R, d=d),
        grid=(n_steps,),
        in_specs=[
            pl.BlockSpec((R, d), tok),
            pl.BlockSpec(orow, lambda s: (b0 + s, 0, 0, 0)),
            pl.BlockSpec(orow, lambda s: (b1 + b0 + s, 0, 0, 0)),
            pl.BlockSpec((R, LANES), tok),
            pl.BlockSpec((R, LANES), tok),
            pl.BlockSpec((1, d), lambda s: (0, 0)),
        ],
        out_specs=pl.BlockSpec((R, d), lambda s: (s, 0)),
        out_shape=jax.ShapeDtypeStruct((n_out, d), F32),
        compiler_params=pltpu.CompilerParams(
            dimension_semantics=("arbitrary",), vmem_limit_bytes=VMEM_LIMIT),
        name="combine",
    )(x2, o4, o4, w0, w1, g_final)


def _routing_plan(ids, counts, tm, n_tiles):
    e0, e1, r0, r1 = ids[:, 0], ids[:, 1], ids[:, 2], ids[:, 3]
    ar = jnp.arange(N_EXPERTS, dtype=jnp.int32)
    tiles_e = (counts + tm - 1) // tm
    tile_end = jnp.cumsum(tiles_e)
    offs = (tile_end - tiles_e) * tm
    pos0 = jnp.sum(jnp.where(e0[:, None] == ar, offs[None, :], 0), axis=1) + r0
    pos1 = jnp.sum(jnp.where(e1[:, None] == ar, offs[None, :], 0), axis=1) + r1
    pos = jnp.stack([pos0, pos1], axis=1).reshape(-1).astype(jnp.int32)
    n_active = tile_end[-1].astype(jnp.int32)
    t = jnp.arange(n_tiles, dtype=jnp.int32)
    tq = jnp.minimum(t, n_active - 1)
    te = jnp.sum((tile_end[None, :] <= tq[:, None]).astype(jnp.int32), axis=1)
    te = jnp.minimum(te, N_EXPERTS - 1)
    later = (ar[None, :] > ar[:, None]) & (tiles_e[None, :] > 0)
    nxt_e = jnp.min(jnp.where(later, ar[None, :], N_EXPERTS), axis=1)
    nxt_e = jnp.where(nxt_e == N_EXPERTS, -1, nxt_e).astype(jnp.int32)
    nxt2_e = jnp.where(nxt_e >= 0, nxt_e[jnp.maximum(nxt_e, 0)], -1)
    return pos, te, n_active.reshape(1), nxt_e[te], nxt2_e[te]


T_PROMPT = 256
BS_SAMPLE = 32
TM_EXPERT = 256
N_WEIGHT_BUFS = 3
R_DISPATCH_CHOICES = (2176, 1088, 512, 256, 128)
R_COMBINE_CHOICES = (512, 256, 128)


def kernel(x_prompt, x_sample, state_conv, state_pool, g_mix, w_in, w_dw, b_dw, ln_g, ln_b, w_pool, pool_scale, w_out, g_ffn, w_rg, b_rg, w_re, b_re, w_gate, w_up, w_down, g_final):
    depth = g_mix.shape[0]
    assert depth == 1
    B, S, d = x_prompt.shape
    Bd, TS, _ = x_sample.shape
    n_p = B * S
    n_s = Bd * TS
    N = n_p + n_s

    assert ROUTER_GROUP_LANE0 == N_EXPERTS
    n_pad = LANES - N_EXPERTS - N_GROUPS
    w_r = jnp.concatenate([w_re[0], w_rg[0], jnp.zeros((d, n_pad), F32)], axis=1)
    w_r_hi = w_r.astype(BF16)
    w_r_lo = (w_r - w_r_hi.astype(F32)).astype(BF16)
    b_r = jnp.concatenate([b_re[0], b_rg[0], jnp.zeros((n_pad,), F32)])[None]
    wts = (g_mix[0][None], w_in[0].astype(BF16), w_dw[0], b_dw[0][None], ln_g[0][None], ln_b[0][None],
           w_pool[0].astype(BF16), pool_scale[0][None], w_out[0].astype(BF16), g_ffn[0][None],
           w_r_hi, w_r_lo, b_r)

    tm = TM_EXPERT
    n_tiles = (2 * N + N_EXPERTS * (tm - 1) + tm - 1) // tm
    *tok_arrays, nconv_p, npool_p, xs, counts_p = _mixer_prompt(x_prompt, wts, T=T_PROMPT, n_total=N,
                                                                n_sorted_rows=n_tiles * tm, zero_rows=tm)

    x_t = jnp.transpose(x_sample, (1, 0, 2))
    sconv_t = jnp.transpose(state_conv[0], (1, 0, 2))
    spool_t = jnp.transpose(state_pool[0], (1, 0, 2))
    x2, hf, ids, w0, w1, nconv_t, npool_t, counts = _mixer_sample(x_t, sconv_t, spool_t, wts, tok_arrays, counts_p,
                                                                  BS=BS_SAMPLE, tok0=n_p)

    pos, tile_expert, n_active, next_expert, after_next = _routing_plan(ids[:, 0:4], counts[0, 0:N_EXPERTS].astype(jnp.int32),
                                                            tm, n_tiles)
    hf = hf.reshape(N, d // LANES, LANES)
    r_disp = max(r for r in R_DISPATCH_CHOICES if N % r == 0)
    xs, dst = _dispatch(pos, hf, xs, R=r_disp, n_tiles=n_tiles, tm=tm)
    dst = dst.reshape(n_tiles + 1, 1, tm)
    o = _expert_mlp(tile_expert, n_active, next_expert, after_next, dst, xs, w_gate[0], w_up[0], w_down[0],
                    tm=tm, n_tok=N)

    gfin = g_final[None]
    y_p = _combine(x2, o, w0, w1, gfin, tok0=0, n_out=n_p)
    y_s = _combine(x2, o, w0, w1, gfin, tok0=n_p, n_out=n_s)

    y_prompt = y_p.reshape(B, S, d)
    y_sample = y_s.reshape(Bd // BS_SAMPLE, TS, BS_SAMPLE, d).transpose(0, 2, 1, 3).reshape(Bd, TS, d)
    new_conv_s = jnp.transpose(nconv_t, (1, 0, 2))[None]
    new_pool_s = jnp.transpose(npool_t, (1, 0, 2))[None]
    return (y_prompt, y_sample, nconv_p, new_conv_s, npool_p, new_pool_s)
```

```python
import functools

import jax
import jax.numpy as jnp
from jax import lax
from jax.experimental import pallas as pl
from jax.experimental.pallas import tpu as pltpu

F32 = jnp.float32
BF16 = jnp.bfloat16
EPS = 1e-6

LANES = 128
SUBLANES = 8
VMEM_LIMIT = 56 * 1024 * 1024

CONV_W = 31
POOL_WINDOWS = (2, 4, 8, 16)
POOL_MAX_W = 16
N_GROUPS = 4
PER_GROUP = 8
N_EXPERTS = N_GROUPS * PER_GROUP

ROUTER_GROUP_LANE0 = N_EXPERTS
TOK_CHUNK = 16
ROW_LOOP_UNROLL = 4
TAIL = 32
PTAIL = 16


def _rows(i, r):
    if isinstance(i, int):
        return pl.ds(i * r, r)
    return pl.ds(pl.multiple_of(i * r, r), r)


def _row_loop(n, body, *, static, unroll=1):
    if static:
        for i in range(n):
            body(i)
    else:
        def step(i, c):
            body(i)
            return c
        lax.fori_loop(0, n, step, 0, unroll=unroll)


def _sigmoid(x):
    return 1.0 / (1.0 + jnp.exp(-x))


def _rmsnorm_to_bf16(src_ref, g_ref, dst_ref, T, static=False):
    R = 16

    def body(i):
        rs = _rows(i, R)
        x = src_ref[rs, :]
        ms = jnp.mean(x * x, axis=-1, keepdims=True)
        dst_ref[rs, :] = (x * lax.rsqrt(ms + EPS) * g_ref[...]).astype(BF16)

    _row_loop(T // R, body, static=static, unroll=ROW_LOOP_UNROLL)


def _layernorm_swish(y_ref, lng_ref, lnb_ref, mix_ref, T, dc, static=False):
    R = 16

    def body(i):
        rs = _rows(i, R)
        y = y_ref[rs, :]
        mu = jnp.mean(y, axis=-1, keepdims=True)
        d = y - mu
        var = jnp.mean(d * d, axis=-1, keepdims=True)
        z = d * lax.rsqrt(var + EPS) * lng_ref[...] + lnb_ref[...]
        mix_ref[rs, 0:dc] = (z * _sigmoid(z)).astype(BF16)

    _row_loop(T // R, body, static=static, unroll=ROW_LOOP_UNROLL)


def _pool_project(sd_ref, wpool_ref, pscale_ref, mix_ref, dc, gd):
    for g in range(len(POOL_WINDOWS)):
        sl = slice(g * gd, (g + 1) * gd)
        o = jnp.dot(sd_ref[:, sl], wpool_ref[g], preferred_element_type=F32)
        mix_ref[:, dc + g * gd: dc + (g + 1) * gd] = (o * pscale_ref[:, sl]).astype(BF16)


def _out_proj_residual(x_ref, mix_ref, wout_ref, x2_ref):
    x2_ref[...] = x_ref[...] + jnp.dot(mix_ref[...], wout_ref[...], preferred_element_type=F32)


def _router(x2_ref, gffn_ref, wrh_ref, wrl_ref, br_ref, ltri_ref, hf_ref, ids_ref, w0_ref, w1_ref, cnt_ref,
            hhi_scr, hlo_scr, cnt_scr, T, d):
    R = TOK_CHUNK

    def body(i, c):
        rs = _rows(i, R)
        x = x2_ref[rs, :]
        ms = jnp.mean(x * x, axis=-1, keepdims=True)
        h = x * lax.rsqrt(ms + EPS) * gffn_ref[...]
        hi = h.astype(BF16)
        hf_ref[i] = hi.reshape(R, d // LANES, LANES)
        hhi_scr[rs, :] = hi
        hlo_scr[rs, :] = (h - hi.astype(F32)).astype(BF16)
        return c

    lax.fori_loop(0, T // R, body, 0, unroll=ROW_LOOP_UNROLL)

    hcat = jnp.concatenate([hhi_scr[...], hlo_scr[...]], axis=0)
    wcat = jnp.concatenate([wrh_ref[...], wrl_ref[...]], axis=1)
    prod = jnp.dot(hcat, wcat, preferred_element_type=F32)
    lg = prod[0:T, 0:LANES] + prod[T:2 * T, 0:LANES] + prod[0:T, LANES:2 * LANES] + br_ref[...]

    lane = lax.broadcasted_iota(jnp.int32, lg.shape, 1).astype(F32)
    neg = jnp.float32(-jnp.inf)
    big = jnp.float32(1e9)
    g_lo = jnp.float32(ROUTER_GROUP_LANE0)
    gmask = (lane >= g_lo) & (lane < g_lo + N_GROUPS)
    lgg = jnp.where(gmask, lg, neg)
    gmax = jnp.max(lgg, axis=-1, keepdims=True)
    gsel = jnp.min(jnp.where(lgg == gmax, lane, big), axis=-1, keepdims=True) - g_lo
    gsum = jnp.sum(jnp.where(gmask, jnp.exp(lg - gmax), 0.0), axis=-1, keepdims=True)
    p_g = 1.0 / gsum

    e_lo = gsel * PER_GROUP
    emask = (lane >= e_lo) & (lane < e_lo + PER_GROUP)
    le = jnp.where(emask, lg, neg)
    v0 = jnp.max(le, axis=-1, keepdims=True)
    i0 = jnp.min(jnp.where(le == v0, lane, big), axis=-1, keepdims=True)
    le2 = jnp.where(lane == i0, neg, le)
    v1 = jnp.max(le2, axis=-1, keepdims=True)
    i1 = jnp.min(jnp.where(le2 == v1, lane, big), axis=-1, keepdims=True)
    ex = jnp.exp(v1 - v0)
    den = 1.0 / (1.0 + ex)
    w0 = den * p_g
    w1 = ex * den * p_g

    sel0 = lane == i0
    sel1 = lane == i1
    m = jnp.where(sel0 | sel1, 1.0, 0.0)
    before = jnp.dot(ltri_ref[...], m.astype(BF16), preferred_element_type=F32) + cnt_scr[0:1, :]
    rank0 = jnp.sum(jnp.where(sel0, before, 0.0), axis=-1, keepdims=True)
    rank1 = jnp.sum(jnp.where(sel1, before, 0.0), axis=-1, keepdims=True)
    total = cnt_scr[0:1, :] + jnp.sum(m, axis=0, keepdims=True)
    cnt_scr[...] = jnp.broadcast_to(total, cnt_scr.shape)
    cnt_ref[...] = jnp.broadcast_to(total, cnt_ref.shape)

    packed = jnp.where(lane == 0.0, i0, jnp.where(lane == 1.0, i1,
                                                  jnp.where(lane == 2.0, rank0, jnp.where(lane == 3.0, rank1, 0.0))))
    ids_ref[...] = packed.astype(jnp.int32)
    w0_ref[...] = jnp.broadcast_to(w0, lg.shape)
    w1_ref[...] = jnp.broadcast_to(w1, lg.shape)


def _mixer_prompt_kernel(xp_ref, xn_ref, gmix_ref, win_ref, wdw_ref, bdw_ref, lng_ref, lnb_ref, wpool_ref,
                         pscale_ref, wout_ref, gffn_ref, wrh_ref, wrl_ref, br_ref, ltri_ref,
                         x2_ref, hf_ref, ids_ref, w0_ref, w1_ref, nconv_ref, npool_ref, xs_ref, cnt_ref,
                         h_scr, pg0, pg1, ext0, ext1, extu0, extu1, yc_scr, sd0, sd1, mix0, mix1, hhi_scr, hlo_scr,
                         zbuf, cnt_scr, zsem, *, T, d, dc, dp, n_batch):
    n_s = pl.num_programs(1)
    b = pl.program_id(0)
    s = pl.program_id(1)
    t = b * n_s + s
    gd = dp // len(POOL_WINDOWS)
    pg_scr, ext_scr, extu_scr, sd_scr, mix_scr = (pg0, pg1), (ext0, ext1), (extu0, extu1), (sd0, sd1), (mix0, mix1)

    def in_proj(x_ref, slot, static):
        _rmsnorm_to_bf16(x_ref, gmix_ref, h_scr, T, static=static)
        ext_scr[slot][TAIL:TAIL + T, :] = jnp.dot(h_scr[...], win_ref[:, 0:dc], preferred_element_type=F32)
        pg_scr[slot][...] = jnp.dot(h_scr[...], win_ref[:, dc:2 * dc], preferred_element_type=F32)
        extu_scr[slot][PTAIL:PTAIL + T, :] = jnp.dot(h_scr[...], win_ref[:, 2 * dc:2 * dc + dp],
                                                     preferred_element_type=F32)

    def zero_copies():
        zr = zbuf.shape[0]
        return [pltpu.make_async_copy(zbuf, xs_ref.at[pl.ds(j * zr, zr)], zsem.at[0])
                for j in range(xs_ref.shape[0] // zr)]

    def finish_prev(q):
        _pool_project(sd_scr[q], wpool_ref, pscale_ref, mix_scr[q], dc, gd)
        _out_proj_residual(xp_ref, mix_scr[q], wout_ref, x2_ref)

    @pl.when(t == 0)
    def _():
        ext0[0:TAIL, :] = jnp.zeros((TAIL, dc), F32)
        extu0[0:PTAIL, :] = jnp.zeros((PTAIL, dp), F32)
        sd1[...] = jnp.zeros(sd1.shape, BF16)
        mix1[...] = jnp.zeros(mix1.shape, BF16)
        zbuf[...] = jnp.zeros(zbuf.shape, BF16)
        cnt_scr[...] = jnp.zeros(cnt_scr.shape, F32)
        for c in zero_copies():
            c.start()
        in_proj(xp_ref, 0, False)

    for par in range(2):
        @pl.when((b == n_batch) & (s == 0) & (t % 2 == par))
        def _():
            finish_prev(1 - par)

    @pl.when((b == n_batch) & (s == 0))
    def _():
        for c in zero_copies():
            c.wait()

    @pl.when((b == n_batch) & (s > 0))
    def _():
        x2_ref[...] = jnp.zeros(x2_ref.shape, F32)
        hf_ref[...] = jnp.zeros(hf_ref.shape, BF16)
        ids_ref[...] = jnp.zeros(ids_ref.shape, jnp.int32)
        w0_ref[...] = jnp.zeros(w0_ref.shape, F32)
        w1_ref[...] = jnp.zeros(w1_ref.shape, F32)

    def step(p):
        q = 1 - p
        in_proj(xn_ref, q, True)

        R = 16
        for i in range(T // R):
            es = pl.ds(i * R + TAIL, R)
            ext_scr[p][es, :] = ext_scr[p][es, :] * _sigmoid(pg_scr[p][pl.ds(i * R, R), :])

        RC, LC = 32, 128
        BR = RC + TAIL
        shift0 = TAIL - (CONV_W - 1)
        chain = None
        for i in range(T // RC):
            r0 = i * RC
            for lc in range(dc // LC):
                ls = slice(lc * LC, (lc + 1) * LC)
                blk = ext_scr[p][pl.ds(r0, BR), ls]
                if chain is None:
                    acc = jnp.zeros((RC, LC), F32)
                else:
                    z = (lax.bitcast_convert_type(chain, jnp.uint32) >> 16) >> 16
                    acc = jnp.tile(lax.bitcast_convert_type(z, F32), (RC // SUBLANES, 1))
                for sft in range(SUBLANES):
                    taps = [k for k in range(CONV_W) if (k + shift0) % SUBLANES == sft]
                    if not taps:
                        continue
                    rolled = blk if sft == 0 else pltpu.roll(blk, BR - sft, 0)
                    for k in taps:
                        qq = (k + shift0) // SUBLANES
                        acc = acc + wdw_ref[k:k + 1, ls] * rolled[qq * SUBLANES:qq * SUBLANES + RC, :]
                yc_scr[pl.ds(r0, RC), ls] = acc + bdw_ref[:, ls]
                chain = acc[0:SUBLANES, :]

        _layernorm_swish(yc_scr, lng_ref, lnb_ref, mix_scr[p], T, dc, static=True)

        RP = 32
        BP = RP + PTAIL
        pos_base = s * T
        for i in range(T // RP):
            r0 = i * RP
            pos = (pos_base + r0 + lax.broadcasted_iota(jnp.int32, (RP, gd), 0)).astype(F32)
            for g, w in enumerate(POOL_WINDOWS):
                ls = slice(g * gd, (g + 1) * gd)
                blk = extu_scr[p][pl.ds(r0, BP), ls]
                run = blk
                span = 1
                while span < w:
                    run = run + pltpu.roll(run, span, 0)
                    span *= 2
                cnt = jnp.minimum(pos + 1.0, jnp.float32(w))
                mean = run[PTAIL:PTAIL + RP, :] / cnt
                sd_scr[p][pl.ds(r0, RP), ls] = (mean - blk[PTAIL:PTAIL + RP, :]).astype(BF16)

        nconv_ref[...] = ext_scr[p][TAIL + T - (CONV_W - 1):TAIL + T, :]
        npool_ref[...] = extu_scr[p][PTAIL + T - (POOL_MAX_W - 1):PTAIL + T, :]
        keep = s != n_s - 1
        ext_scr[q][0:TAIL, :] = jnp.where(keep, ext_scr[p][T:T + TAIL, :], 0.0)
        extu_scr[q][0:PTAIL, :] = jnp.where(keep, extu_scr[p][T:T + PTAIL, :], 0.0)

        finish_prev(q)

    for par in range(2):
        pl.when((b < n_batch) & (t % 2 == par))(functools.partial(step, par))

    @pl.when(((b < n_batch) & (t > 0)) | ((b == n_batch) & (s == 0)))
    def _():
        _router(x2_ref, gffn_ref, wrh_ref, wrl_ref, br_ref, ltri_ref, hf_ref, ids_ref, w0_ref, w1_ref, cnt_ref,
                hhi_scr, hlo_scr, cnt_scr, T, d)


def _mixer_sample_kernel(x_ref, sconv_ref, spool_ref, gmix_ref, win_ref, wdw_ref, bdw_ref, lng_ref,
                         lnb_ref, wpool_ref, pscale_ref, wout_ref, gffn_ref, wrh_ref, wrl_ref, br_ref,
                         ltri_ref, cnt_in_ref,
                         x2_in, hf_in, ids_in, w0_in, w1_in,
                         x2_ref, hf_ref, ids_ref, w0_ref, w1_ref, nconv_ref, npool_ref, cnt_ref,
                         xt_scr, h_scr, a_scr, pg_scr, u_scr, yc_scr, sd_scr, mix_scr, hhi_scr, hlo_scr, cnt_scr,
                         *, TS, BS, d, dc, dp):
    del x2_in, hf_in, ids_in, w0_in, w1_in

    @pl.when(pl.program_id(0) == 0)
    def _():
        cnt_scr[...] = cnt_in_ref[...]

    T = TS * BS
    gd = dp // len(POOL_WINDOWS)
    NH = CONV_W - 1
    NP = POOL_MAX_W - 1

    for t in range(TS):
        xt_scr[t * BS:(t + 1) * BS, :] = x_ref[t]

    _rmsnorm_to_bf16(xt_scr, gmix_ref, h_scr, T)
    a_scr[...] = jnp.dot(h_scr[...], win_ref[:, 0:dc], preferred_element_type=F32)
    pg_scr[...] = jnp.dot(h_scr[...], win_ref[:, dc:2 * dc], preferred_element_type=F32)
    u_scr[...] = jnp.dot(h_scr[...], win_ref[:, 2 * dc:2 * dc + dp], preferred_element_type=F32)

    R = 16

    def glu(i, c):
        rs = _rows(i, R)
        a_scr[rs, :] = a_scr[rs, :] * _sigmoid(pg_scr[rs, :])
        return c

    lax.fori_loop(0, T // R, glu, 0, unroll=ROW_LOOP_UNROLL)

    def ext_conv(j, rs, ls):
        if j < NH:
            return sconv_ref[j, rs, ls]
        return a_scr[pl.ds((j - NH) * BS + rs.start, rs.size), ls]

    def ext_pool(j, rs, ls):
        if j < NP:
            return spool_ref[j, rs, ls]
        return u_scr[pl.ds((j - NP) * BS + rs.start, rs.size), ls]

    RC, LC = 32, 256

    def conv(i, c):
        rs = _rows(i, RC)
        for t in range(TS):
            for lc in range(dc // LC):
                ls = slice(lc * LC, (lc + 1) * LC)
                acc = jnp.zeros((RC, LC), F32)
                for k in range(CONV_W):
                    acc = acc + wdw_ref[k:k + 1, ls] * ext_conv(t + k, rs, ls)
                yc_scr[pl.ds(t * BS + rs.start, RC), ls] = acc + bdw_ref[:, ls]
        return c

    lax.fori_loop(0, BS // RC, conv, 0)

    _layernorm_swish(yc_scr, lng_ref, lnb_ref, mix_scr, T, dc)

    def pool(i, c):
        rs = _rows(i, RC)
        for t in range(TS):
            for g, w in enumerate(POOL_WINDOWS):
                ls = slice(g * gd, (g + 1) * gd)
                tot = ext_pool(NP + t, rs, ls)
                cur = tot
                for back in range(1, w):
                    tot = tot + ext_pool(NP + t - back, rs, ls)
                sd_scr[pl.ds(t * BS + rs.start, RC), ls] = (tot / jnp.float32(w) - cur).astype(BF16)
        return c

    lax.fori_loop(0, BS // RC, pool, 0)

    _pool_project(sd_scr, wpool_ref, pscale_ref, mix_scr, dc, gd)

    for j in range(NH):
        src = j + TS
        nconv_ref[j] = sconv_ref[src] if src < NH else a_scr[(src - NH) * BS:(src - NH + 1) * BS, :]
    for j in range(NP):
        src = j + TS
        npool_ref[j] = spool_ref[src] if src < NP else u_scr[(src - NP) * BS:(src - NP + 1) * BS, :]

    _out_proj_residual(xt_scr, mix_scr, wout_ref, x2_ref)
    _router(x2_ref, gffn_ref, wrh_ref, wrl_ref, br_ref, ltri_ref, hf_ref, ids_ref, w0_ref, w1_ref, cnt_ref,
            hhi_scr, hlo_scr, cnt_scr, T, d)


def _strict_lower_ones(n):
    return jnp.tril(jnp.ones((n, n), BF16), -1)


def _const_spec(shape):
    nd = len(shape)
    return pl.BlockSpec(shape, lambda *a: (0,) * nd, pipeline_mode=pl.Buffered(1))


def _mixer_weight_specs(d, dc, dp, cols):
    gd = dp // len(POOL_WINDOWS)
    return [
        _const_spec((1, d)),
        _const_spec((d, cols)),
        _const_spec((CONV_W, dc)),
        _const_spec((1, dc)),
        _const_spec((1, dc)),
        _const_spec((1, dc)),
        _const_spec((len(POOL_WINDOWS), gd, gd)),
        _const_spec((1, dp)),
        _const_spec((dc + dp, d)),
        _const_spec((1, d)),
        _const_spec((d, LANES)),
        _const_spec((d, LANES)),
        _const_spec((1, LANES)),
    ]


def _mixer_prompt(x, wts, *, T, n_total, n_sorted_rows, zero_rows):
    B, S, d = x.shape
    dc = wts[2].shape[1]
    dp = wts[7].shape[1]
    cols = wts[1].shape[1]
    n_s = S // T
    N = n_total
    n_blk = N // T
    assert N % T == 0 and N - B * S <= S
    blk = lambda b, s: jnp.clip(b * n_s + s - 1, 0, n_blk - 1)
    tok = lambda b, s: (blk(b, s), 0)
    tok4 = lambda b, s: (blk(b, s), 0, 0, 0)
    bclamp = lambda b: jnp.minimum(b, B - 1)

    def nxt(b, s):
        wrap = s + 1 == n_s
        return (jnp.where(wrap, bclamp(b + 1), bclamp(b)), jnp.where(wrap, 0, s + 1), 0)

    def prv(b, s):
        tp = jnp.clip(b * n_s + s - 1, 0, B * n_s - 1)
        return (tp // n_s, tp % n_s, 0)
    out_shape = (
        jax.ShapeDtypeStruct((N, d), F32),
        jax.ShapeDtypeStruct((N // TOK_CHUNK, TOK_CHUNK, d // LANES, LANES), BF16),
        jax.ShapeDtypeStruct((N, LANES), jnp.int32),
        jax.ShapeDtypeStruct((N, LANES), F32),
        jax.ShapeDtypeStruct((N, LANES), F32),
        jax.ShapeDtypeStruct((1, B, CONV_W - 1, dc), F32),
        jax.ShapeDtypeStruct((1, B, POOL_MAX_W - 1, dp), F32),
        jax.ShapeDtypeStruct((n_sorted_rows, d // LANES, LANES), BF16),
        jax.ShapeDtypeStruct((SUBLANES, LANES), F32),
    )
    assert n_sorted_rows % zero_rows == 0
    out_specs = (
        pl.BlockSpec((T, d), tok),
        pl.BlockSpec((T // TOK_CHUNK, TOK_CHUNK, d // LANES, LANES), tok4),
        pl.BlockSpec((T, LANES), tok),
        pl.BlockSpec((T, LANES), tok),
        pl.BlockSpec((T, LANES), tok),
        pl.BlockSpec((None, None, CONV_W - 1, dc), lambda b, s: (0, bclamp(b), 0, 0)),
        pl.BlockSpec((None, None, POOL_MAX_W - 1, dp), lambda b, s: (0, bclamp(b), 0, 0)),
        pl.BlockSpec(memory_space=pl.ANY),
        pl.BlockSpec((SUBLANES, LANES), lambda b, s: (0, 0)),
    )
    scratch = [
        pltpu.VMEM((T, d), BF16),
        pltpu.VMEM((T, dc), F32), pltpu.VMEM((T, dc), F32),
        pltpu.VMEM((T + TAIL, dc), F32), pltpu.VMEM((T + TAIL, dc), F32),
        pltpu.VMEM((T + PTAIL, dp), F32), pltpu.VMEM((T + PTAIL, dp), F32),
        pltpu.VMEM((T, dc), F32),
        pltpu.VMEM((T, dp), BF16), pltpu.VMEM((T, dp), BF16),
        pltpu.VMEM((T, dc + dp), BF16), pltpu.VMEM((T, dc + dp), BF16),
        pltpu.VMEM((T, d), BF16),
        pltpu.VMEM((T, d), BF16),
        pltpu.VMEM((zero_rows, d // LANES, LANES), BF16),
        pltpu.VMEM((SUBLANES, LANES), F32),
        pltpu.SemaphoreType.DMA((1,)),
    ]
    return pl.pallas_call(
        functools.partial(_mixer_prompt_kernel, T=T, d=d, dc=dc, dp=dp, n_batch=B),
        grid=(B + 1, n_s),
        in_specs=[pl.BlockSpec((None, T, d), prv),
                  pl.BlockSpec((None, T, d), nxt)]
                 + _mixer_weight_specs(d, dc, dp, cols) + [_const_spec((T, T))],
        out_specs=out_specs,
        out_shape=out_shape,
        scratch_shapes=scratch,
        compiler_params=pltpu.CompilerParams(
            dimension_semantics=("arbitrary", "arbitrary"), vmem_limit_bytes=VMEM_LIMIT),
        name="mixer_prompt",
    )(x, x, *wts, _strict_lower_ones(T))


def _mixer_sample(x_t, sconv_t, spool_t, wts, tok_arrays, counts, *, BS, tok0):
    TS, Bd, d = x_t.shape
    dc = wts[2].shape[1]
    dp = wts[7].shape[1]
    cols = wts[1].shape[1]
    T = TS * BS
    n_b = Bd // BS
    b0 = tok0 // T
    tok = lambda i: (b0 + i, 0)
    tok4 = lambda i: (b0 + i, 0, 0, 0)
    out_shape = tuple(jax.ShapeDtypeStruct(a.shape, a.dtype) for a in tok_arrays) + (
        jax.ShapeDtypeStruct((CONV_W - 1, Bd, dc), F32),
        jax.ShapeDtypeStruct((POOL_MAX_W - 1, Bd, dp), F32),
        jax.ShapeDtypeStruct(counts.shape, F32),
    )
    out_specs = (
        pl.BlockSpec((T, d), tok),
        pl.BlockSpec((T // TOK_CHUNK, TOK_CHUNK, d // LANES, LANES), tok4),
        pl.BlockSpec((T, LANES), tok),
        pl.BlockSpec((T, LANES), tok),
        pl.BlockSpec((T, LANES), tok),
        pl.BlockSpec((CONV_W - 1, BS, dc), lambda i: (0, i, 0)),
        pl.BlockSpec((POOL_MAX_W - 1, BS, dp), lambda i: (0, i, 0)),
        pl.BlockSpec(counts.shape, lambda i: (0, 0)),
    )
    scratch = [
        pltpu.VMEM((T, d), F32),
        pltpu.VMEM((T, d), BF16),
        pltpu.VMEM((T, dc), F32),
        pltpu.VMEM((T, dc), F32),
        pltpu.VMEM((T, dp), F32),
        pltpu.VMEM((T, dc), F32),
        pltpu.VMEM((T, dp), BF16),
        pltpu.VMEM((T, dc + dp), BF16),
        pltpu.VMEM((T, d), BF16),
        pltpu.VMEM((T, d), BF16),
        pltpu.VMEM(counts.shape, F32),
    ]
    return pl.pallas_call(
        functools.partial(_mixer_sample_kernel, TS=TS, BS=BS, d=d, dc=dc, dp=dp),
        grid=(n_b,),
        in_specs=[pl.BlockSpec((TS, BS, d), lambda i: (0, i, 0)),
                  pl.BlockSpec((CONV_W - 1, BS, dc), lambda i: (0, i, 0), pipeline_mode=pl.Buffered(1)),
                  pl.BlockSpec((POOL_MAX_W - 1, BS, dp), lambda i: (0, i, 0), pipeline_mode=pl.Buffered(1))]
                 + _mixer_weight_specs(d, dc, dp, cols)
                 + [_const_spec((T, T)), _const_spec(counts.shape)]
                 + [pl.BlockSpec(memory_space=pl.ANY)] * len(tok_arrays),
        out_specs=out_specs,
        out_shape=out_shape,
        scratch_shapes=scratch,
        input_output_aliases={5 + len(wts) + j: j for j in range(len(tok_arrays))},
        compiler_params=pltpu.CompilerParams(
            dimension_semantics=("arbitrary",), vmem_limit_bytes=VMEM_LIMIT),
        name="mixer_sample",
    )(x_t, sconv_t, spool_t, *wts, _strict_lower_ones(T), counts, *tok_arrays)


def _dispatch_kernel(pos_ref, hf_ref, dst_in, xs_in_ref, xs_ref, dst_ref, sem, isem, *, R, n_tok):
    del xs_in_ref
    base = pl.program_id(0) * R

    @pl.when(pl.program_id(0) == 0)
    def _():
        init = pltpu.make_async_copy(dst_in, dst_ref, isem.at[0])
        init.start()
        init.wait()

    G = 8

    def body(g, c):
        r0 = g * G
        ps = [pos_ref[2 * (base + r0) + q] for q in range(2 * G)]
        for q in range(2 * G):
            r, k = r0 + q // 2, q % 2
            pltpu.make_async_copy(hf_ref.at[pl.ds(r, 1)], xs_ref.at[pl.ds(ps[q], 1)], sem.at[0]).start(priority=k)
            dst_ref[ps[q]] = k * n_tok + base + r
        return c

    lax.fori_loop(0, R // G, body, 0)
    for k in range(2):
        pltpu.make_async_copy(hf_ref, xs_ref.at[pl.ds(0, R)], sem.at[0]).wait()


def _dispatch(pos, hf, xs_zero, *, R, n_tiles, tm):
    n_tok = hf.shape[0]
    assert n_tok % R == 0
    p = jnp.arange((n_tiles + 1) * tm, dtype=jnp.int32)
    tile = p // tm
    buf = jnp.where(tile == n_tiles, 1, tile % 2)
    dst_init = 2 * n_tok + buf * tm + p % tm
    hbm = pl.BlockSpec(memory_space=pl.ANY)
    return pl.pallas_call(
        functools.partial(_dispatch_kernel, R=R, n_tok=n_tok),
        grid_spec=pltpu.PrefetchScalarGridSpec(
            num_scalar_prefetch=1,
            grid=(n_tok // R,),
            in_specs=[pl.BlockSpec((R,) + hf.shape[1:], lambda s, pos: (s, 0, 0)), hbm, hbm],
            out_specs=(hbm, pl.BlockSpec(memory_space=pltpu.SMEM)),
            scratch_shapes=[pltpu.SemaphoreType.DMA((1,)), pltpu.SemaphoreType.DMA((1,))],
        ),
        out_shape=(jax.ShapeDtypeStruct(xs_zero.shape, xs_zero.dtype),
                   jax.ShapeDtypeStruct(dst_init.shape, jnp.int32)),
        input_output_aliases={3: 0},
        compiler_params=pltpu.CompilerParams(dimension_semantics=("arbitrary",)),
        name="dispatch",
    )(pos, hf, dst_init, xs_zero)


def _expert_mlp_kernel(te_ref, na_ref, nx_ref, nx2_ref, dst_prev_ref, dst_cur_ref, xs_ref, wg_ref, wu_ref, wd_ref, o_ref,
                       ybuf0, ybuf1, wgf, wuf, wdf, wg_scr, wu_scr, wd_scr, wslot, ssem, wsem,
                       *, tm, n_tiles, d):
    i = pl.program_id(0)
    n_act = na_ref[0]
    active = i < n_act
    slot = i % 2
    last = n_tiles - 1
    prev = te_ref[jnp.maximum(i - 1, 0)]
    new_expert = (i == 0) | (te_ref[i] != prev)
    ybuf = (ybuf0, ybuf1)

    def scatter_start(dst_ref, b):
        for r in range(tm):
            pltpu.make_async_copy(ybuf[b].at[pl.ds(r, 1)], o_ref.at[pl.ds(dst_ref[0, 0, r], 1)],
                                  ssem.at[b]).start(priority=r % 2)

    def scatter_wait(b):
        pltpu.make_async_copy(ybuf[b], o_ref.at[pl.ds(0, tm)], ssem.at[b]).wait()

    def weight_copies(e, b):
        return [pltpu.make_async_copy(src.at[e], buf.at[b], wsem.at[b])
                for src, buf in ((wg_ref, wgf), (wu_ref, wuf), (wd_ref, wdf))]

    @pl.when(i == 0)
    def _():
        wslot[0] = 0
        for c in weight_copies(te_ref[0], 0):
            c.start()

        @pl.when(nx_ref[0] >= 0)
        def _():
            for c in weight_copies(nx_ref[0], 1):
                c.start()
        ybuf1[...] = jnp.zeros(ybuf1.shape, BF16)
        trash0 = pltpu.make_async_copy(ybuf1, o_ref.at[pl.ds(o_ref.shape[0] - 2 * tm, tm)], ssem.at[0])
        trash0.start()
        trash0.wait()

    @pl.when(active & new_expert)
    def _():
        b = wslot[0]
        after_next = nx2_ref[i]

        @pl.when(after_next >= 0)
        def _():
            for c in weight_copies(after_next, (b + 2) % N_WEIGHT_BUFS):
                c.start()

        for c in weight_copies(te_ref[i], b):
            c.wait()
        wg_scr[...] = wgf[b].astype(BF16)
        wu_scr[...] = wuf[b].astype(BF16)
        wd_scr[...] = wdf[b].astype(BF16)
        wslot[0] = (b + 1) % N_WEIGHT_BUFS

    for par in range(2):
        is_par = slot == par

        @pl.when(is_par & (i >= 1) & (i - 2 < n_act))
        def _():
            scatter_wait(par)

        @pl.when(is_par & active)
        def _():
            x = xs_ref[...].reshape(tm, d)
            g = jnp.dot(x, wg_scr[...], preferred_element_type=F32)
            u = jnp.dot(x, wu_scr[...], preferred_element_type=F32)
            act = (g * _sigmoid(g) * u).astype(BF16)
            y = jnp.dot(act, wd_scr[...], preferred_element_type=F32)
            ybuf[par][...] = y.astype(BF16).reshape(tm, d // LANES, LANES)
            scatter_start(dst_prev_ref, 1 - par)

        @pl.when(is_par & (i == n_act))
        def _():
            scatter_start(dst_prev_ref, 1 - par)

        @pl.when(is_par & (i == last))
        def _():
            @pl.when(last - 1 < n_act)
            def _():
                scatter_wait(1 - par)

            @pl.when(last < n_act)
            def _():
                scatter_start(dst_cur_ref, par)
                scatter_wait(par)


def _expert_mlp(tile_expert, n_active, next_expert, after_next_expert, dst, xs, w_gate, w_up, w_down, *, tm, n_tok):
    E, d, de = w_gate.shape
    n_tiles = xs.shape[0] // tm
    row = (tm, d // LANES, LANES)
    cur = lambda i, te, na, nx, nx2: (i, 0, 0)
    prv = lambda i, te, na, nx, nx2: (jnp.where(i == 0, n_tiles, i - 1), 0, 0)
    hbm = pl.BlockSpec(memory_space=pl.ANY)
    smem_tile = lambda imap: pl.BlockSpec((1, 1, tm), imap, memory_space=pltpu.SMEM)
    return pl.pallas_call(
        functools.partial(_expert_mlp_kernel, tm=tm, n_tiles=n_tiles, d=d),
        grid_spec=pltpu.PrefetchScalarGridSpec(
            num_scalar_prefetch=4,
            grid=(n_tiles,),
            in_specs=[
                smem_tile(prv), smem_tile(cur),
                pl.BlockSpec(row, lambda i, te, na, nx, nx2: (jnp.clip(i, 0, jnp.maximum(na[0] - 1, 0)), 0, 0)),
                hbm, hbm, hbm,
            ],
            out_specs=hbm,
            scratch_shapes=[pltpu.VMEM(row, BF16), pltpu.VMEM(row, BF16),
                            pltpu.VMEM((N_WEIGHT_BUFS, d, de), F32), pltpu.VMEM((N_WEIGHT_BUFS, d, de), F32),
                            pltpu.VMEM((N_WEIGHT_BUFS, de, d), F32),
                            pltpu.VMEM((d, de), BF16), pltpu.VMEM((d, de), BF16), pltpu.VMEM((de, d), BF16),
                            pltpu.SMEM((1,), jnp.int32),
                            pltpu.SemaphoreType.DMA((2,)), pltpu.SemaphoreType.DMA((N_WEIGHT_BUFS,))],
        ),
        out_shape=jax.ShapeDtypeStruct((2 * n_tok + 2 * tm, d // LANES, LANES), BF16),
        compiler_params=pltpu.CompilerParams(
            dimension_semantics=("arbitrary",), vmem_limit_bytes=VMEM_LIMIT),
        name="expert_mlp",
    )(tile_expert, n_active, next_expert, after_next_expert, dst, dst, xs, w_gate, w_up, w_down)


def _combine_kernel(x2_ref, o0_ref, o1_ref, w0_ref, w1_ref, gfin_ref, y_ref, *, R, d):
    C = TOK_CHUNK

    def body(i, c):
        rs = _rows(i, C)
        w0 = w0_ref[rs, :]
        w1 = w1_ref[rs, :]
        o0 = o0_ref[i].reshape(C, d).astype(F32)
        o1 = o1_ref[i].reshape(C, d).astype(F32)
        parts = []
        ssq = jnp.zeros((C, LANES), F32)
        for j in range(d // LANES):
            ls = slice(j * LANES, (j + 1) * LANES)
            m = w0 * o0[:, ls] + w1 * o1[:, ls]
            v = x2_ref[rs, ls] + m
            ssq = ssq + v * v
            parts.append(v)
        ms = jnp.sum(ssq, axis=-1, keepdims=True) * jnp.float32(1.0 / d)
        inv = lax.rsqrt(ms + EPS)
        for j in range(d // LANES):
            ls = slice(j * LANES, (j + 1) * LANES)
            y_ref[rs, ls] = parts[j] * inv * gfin_ref[:, ls]
        return c

    lax.fori_loop(0, R // C, body, 0)


def _combine(x2, o, w0, w1, g_final, *, tok0, n_out):
    n_tok, d = x2.shape
    R = max(r for r in R_COMBINE_CHOICES if n_out % r == 0 and tok0 % r == 0 and n_tok % r == 0)
    n_steps = n_out // R
    b0 = tok0 // R
    b1 = n_tok // R
    tok = lambda s: (b0 + s, 0)
    orow = (R // TOK_CHUNK, TOK_CHUNK, d // LANES, LANES)
    o4 = o.reshape(o.shape[0] // TOK_CHUNK, TOK_CHUNK, d // LANES, LANES)
    return pl.pallas_call(
        functools.partial(_combine_kernel, R=R, d=d),
        grid=(n_steps,),
        in_specs=[
            pl.BlockSpec((R, d), tok),
            pl.BlockSpec(orow, lambda s: (b0 + s, 0, 0, 0)),
            pl.BlockSpec(orow, lambda s: (b1 + b0 + s, 0, 0, 0)),
            pl.BlockSpec((R, LANES), tok),
            pl.BlockSpec((R, LANES), tok),
            pl.BlockSpec((1, d), lambda s: (0, 0)),
        ],
        out_specs=pl.BlockSpec((R, d), lambda s: (s, 0)),
        out_shape=jax.ShapeDtypeStruct((n_out, d), F32),
        compiler_params=pltpu.CompilerParams(
            dimension_semantics=("arbitrary",), vmem_limit_bytes=VMEM_LIMIT),
        name="combine",
    )(x2, o4, o4, w0, w1, g_final)


def _routing_plan(ids, counts, tm, n_tiles):
    e0, e1, r0, r1 = ids[:, 0], ids[:, 1], ids[:, 2], ids[:, 3]
    ar = jnp.arange(N_EXPERTS, dtype=jnp.int32)
    tiles_e = (counts + tm - 1) // tm
    tile_end = jnp.cumsum(tiles_e)
    offs = (tile_end - tiles_e) * tm
    pos0 = jnp.sum(jnp.where(e0[:, None] == ar, offs[None, :], 0), axis=1) + r0
    pos1 = jnp.sum(jnp.where(e1[:, None] == ar, offs[None, :], 0), axis=1) + r1
    pos = jnp.stack([pos0, pos1], axis=1).reshape(-1).astype(jnp.int32)
    n_active = tile_end[-1].astype(jnp.int32)
    t = jnp.arange(n_tiles, dtype=jnp.int32)
    tq = jnp.minimum(t, n_active - 1)
    te = jnp.sum((tile_end[None, :] <= tq[:, None]).astype(jnp.int32), axis=1)
    te = jnp.minimum(te, N_EXPERTS - 1)
    later = (ar[None, :] > ar[:, None]) & (tiles_e[None, :] > 0)
    nxt_e = jnp.min(jnp.where(later, ar[None, :], N_EXPERTS), axis=1)
    nxt_e = jnp.where(nxt_e == N_EXPERTS, -1, nxt_e).astype(jnp.int32)
    nxt2_e = jnp.where(nxt_e >= 0, nxt_e[jnp.maximum(nxt_e, 0)], -1)
    return pos, te, n_active.reshape(1), nxt_e[te], nxt2_e[te]


T_PROMPT = 256
BS_SAMPLE = 32
TM_EXPERT = 256
N_WEIGHT_BUFS = 3
R_DISPATCH_CHOICES = (2176, 1088, 512, 256, 128)
R_COMBINE_CHOICES = (512, 256, 128)


def kernel(x_prompt, x_sample, state_conv, state_pool, g_mix, w_in, w_dw, b_dw, ln_g, ln_b, w_pool, pool_scale, w_out, g_ffn, w_rg, b_rg, w_re, b_re, w_gate, w_up, w_down, g_final):
    depth = g_mix.shape[0]
    assert depth == 1
    B, S, d = x_prompt.shape
    Bd, TS, _ = x_sample.shape
    n_p = B * S
    n_s = Bd * TS
    N = n_p + n_s

    assert ROUTER_GROUP_LANE0 == N_EXPERTS
    n_pad = LANES - N_EXPERTS - N_GROUPS
    w_r = jnp.concatenate([w_re[0], w_rg[0], jnp.zeros((d, n_pad), F32)], axis=1)
    w_r_hi = w_r.astype(BF16)
    w_r_lo = (w_r - w_r_hi.astype(F32)).astype(BF16)
    b_r = jnp.concatenate([b_re[0], b_rg[0], jnp.zeros((n_pad,), F32)])[None]
    wts = (g_mix[0][None], w_in[0].astype(BF16), w_dw[0], b_dw[0][None], ln_g[0][None], ln_b[0][None],
           w_pool[0].astype(BF16), pool_scale[0][None], w_out[0].astype(BF16), g_ffn[0][None],
           w_r_hi, w_r_lo, b_r)

    tm = TM_EXPERT
    n_tiles = (2 * N + N_EXPERTS * (tm - 1) + tm - 1) // tm
    *tok_arrays, nconv_p, npool_p, xs, counts_p = _mixer_prompt(x_prompt, wts, T=T_PROMPT, n_total=N,
                                                                n_sorted_rows=n_tiles * tm, zero_rows=tm)

    x_t = jnp.transpose(x_sample, (1, 0, 2))
    sconv_t = jnp.transpose(state_conv[0], (1, 0, 2))
    spool_t = jnp.transpose(state_pool[0], (1, 0, 2))
    x2, hf, ids, w0, w1, nconv_t, npool_t, counts = _mixer_sample(x_t, sconv_t, spool_t, wts, tok_arrays, counts_p,
                                                                  BS=BS_SAMPLE, tok0=n_p)

    pos, tile_expert, n_active, next_expert, after_next = _routing_plan(ids[:, 0:4], counts[0, 0:N_EXPERTS].astype(jnp.int32),
                                                            tm, n_tiles)
    hf = hf.reshape(N, d // LANES, LANES)
    r_disp = max(r for r in R_DISPATCH_CHOICES if N % r == 0)
    xs, dst = _dispatch(pos, hf, xs, R=r_disp, n_tiles=n_tiles, tm=tm)
    dst = dst.reshape(n_tiles + 1, 1, tm)
    o = _expert_mlp(tile_expert, n_active, next_expert, after_next, dst, xs, w_gate[0], w_up[0], w_down[0],
                    tm=tm, n_tok=N)

    gfin = g_final[None]
    y_p = _combine(x2, o, w0, w1, gfin, tok0=0, n_out=n_p)
    y_s = _combine(x2, o, w0, w1, gfin, tok0=n_p, n_out=n_s)

    y_prompt = y_p.reshape(B, S, d)
    y_sample = y_s.reshape(Bd // BS_SAMPLE, TS, BS_SAMPLE, d).transpose(0, 2, 1, 3).reshape(Bd, TS, d)
    new_conv_s = jnp.transpose(nconv_t, (1, 0, 2))[None]
    new_pool_s = jnp.transpose(npool_t, (1, 0, 2))[None]
    return (y_prompt, y_sample, nconv_p, new_conv_s, npool_p, new_pool_s)
```

```python
import functools

import jax
import jax.numpy as jnp
from jax import lax
from jax.experimental import pallas as pl
from jax.experimental.pallas import tpu as pltpu

F32 = jnp.float32
BF16 = jnp.bfloat16
EPS = 1e-6

LANES = 128
SUBLANES = 8
VMEM_LIMIT = 56 * 1024 * 1024

CONV_W = 31
POOL_WINDOWS = (2, 4, 8, 16)
POOL_MAX_W = 16
N_GROUPS = 4
PER_GROUP = 8
N_EXPERTS = N_GROUPS * PER_GROUP

ROUTER_GROUP_LANE0 = N_EXPERTS
TOK_CHUNK = 16
ROW_LOOP_UNROLL = 8
TAIL = 32
PTAIL = 16


def _rows(i, r):
    if isinstance(i, int):
        return pl.ds(i * r, r)
    return pl.ds(pl.multiple_of(i * r, r), r)


def _row_loop(n, body, *, static, unroll=1):
    if static:
        for i in range(n):
            body(i)
    else:
        def step(i, c):
            body(i)
            return c
        lax.fori_loop(0, n, step, 0, unroll=unroll)


def _sigmoid(x):
    return 1.0 / (1.0 + jnp.exp(-x))


def _rmsnorm_to_bf16(src_ref, g_ref, dst_ref, T, static=False):
    R = 16

    def body(i):
        rs = _rows(i, R)
        x = src_ref[rs, :]
        ms = jnp.mean(x * x, axis=-1, keepdims=True)
        dst_ref[rs, :] = (x * lax.rsqrt(ms + EPS) * g_ref[...]).astype(BF16)

    _row_loop(T // R, body, static=static, unroll=ROW_LOOP_UNROLL)


def _layernorm_swish(y_ref, lng_ref, lnb_ref, mix_ref, T, dc, static=False):
    R = 16

    def body(i):
        rs = _rows(i, R)
        y = y_ref[rs, :]
        mu = jnp.mean(y, axis=-1, keepdims=True)
        d = y - mu
        var = jnp.mean(d * d, axis=-1, keepdims=True)
        z = d * lax.rsqrt(var + EPS) * lng_ref[...] + lnb_ref[...]
        mix_ref[rs, 0:dc] = (z * _sigmoid(z)).astype(BF16)

    _row_loop(T // R, body, static=static, unroll=ROW_LOOP_UNROLL)


def _pool_project(sd_ref, wpool_ref, pscale_ref, mix_ref, dc, gd):
    for g in range(len(POOL_WINDOWS)):
        sl = slice(g * gd, (g + 1) * gd)
        o = jnp.dot(sd_ref[:, sl], wpool_ref[g], preferred_element_type=F32)
        mix_ref[:, dc + g * gd: dc + (g + 1) * gd] = (o * pscale_ref[:, sl]).astype(BF16)


def _out_proj_residual(x_ref, mix_ref, wout_ref, x2_ref):
    x2_ref[...] = x_ref[...] + jnp.dot(mix_ref[...], wout_ref[...], preferred_element_type=F32)


def _router(x2_ref, gffn_ref, wrh_ref, wrl_ref, br_ref, ltri_ref, hf_ref, ids_ref, w0_ref, w1_ref, cnt_ref,
            hhi_scr, hlo_scr, cnt_scr, T, d):
    R = TOK_CHUNK

    def body(i, c):
        rs = _rows(i, R)
        x = x2_ref[rs, :]
        ms = jnp.mean(x * x, axis=-1, keepdims=True)
        h = x * lax.rsqrt(ms + EPS) * gffn_ref[...]
        hi = h.astype(BF16)
        hf_ref[i] = hi.reshape(R, d // LANES, LANES)
        hhi_scr[rs, :] = hi
        hlo_scr[rs, :] = (h - hi.astype(F32)).astype(BF16)
        return c

    lax.fori_loop(0, T // R, body, 0, unroll=ROW_LOOP_UNROLL)

    hcat = jnp.concatenate([hhi_scr[...], hlo_scr[...]], axis=0)
    wcat = jnp.concatenate([wrh_ref[...], wrl_ref[...]], axis=1)
    prod = jnp.dot(hcat, wcat, preferred_element_type=F32)
    lg = prod[0:T, 0:LANES] + prod[T:2 * T, 0:LANES] + prod[0:T, LANES:2 * LANES] + br_ref[...]

    lane = lax.broadcasted_iota(jnp.int32, lg.shape, 1).astype(F32)
    neg = jnp.float32(-jnp.inf)
    big = jnp.float32(1e9)
    g_lo = jnp.float32(ROUTER_GROUP_LANE0)
    gmask = (lane >= g_lo) & (lane < g_lo + N_GROUPS)
    lgg = jnp.where(gmask, lg, neg)
    gmax = jnp.max(lgg, axis=-1, keepdims=True)
    gsel = jnp.min(jnp.where(lgg == gmax, lane, big), axis=-1, keepdims=True) - g_lo
    gsum = jnp.sum(jnp.where(gmask, jnp.exp(lg - gmax), 0.0), axis=-1, keepdims=True)
    p_g = 1.0 / gsum

    e_lo = gsel * PER_GROUP
    emask = (lane >= e_lo) & (lane < e_lo + PER_GROUP)
    le = jnp.where(emask, lg, neg)
    v0 = jnp.max(le, axis=-1, keepdims=True)
    i0 = jnp.min(jnp.where(le == v0, lane, big), axis=-1, keepdims=True)
    le2 = jnp.where(lane == i0, neg, le)
    v1 = jnp.max(le2, axis=-1, keepdims=True)
    i1 = jnp.min(jnp.where(le2 == v1, lane, big), axis=-1, keepdims=True)
    ex = jnp.exp(v1 - v0)
    den = 1.0 / (1.0 + ex)
    w0 = den * p_g
    w1 = ex * den * p_g

    sel0 = lane == i0
    sel1 = lane == i1
    m = jnp.where(sel0 | sel1, 1.0, 0.0)
    before = jnp.dot(ltri_ref[...], m.astype(BF16), preferred_element_type=F32) + cnt_scr[0:1, :]
    rank0 = jnp.sum(jnp.where(sel0, before, 0.0), axis=-1, keepdims=True)
    rank1 = jnp.sum(jnp.where(sel1, before, 0.0), axis=-1, keepdims=True)
    total = cnt_scr[0:1, :] + jnp.sum(m, axis=0, keepdims=True)
    cnt_scr[...] = jnp.broadcast_to(total, cnt_scr.shape)
    cnt_ref[...] = jnp.broadcast_to(total, cnt_ref.shape)

    packed = jnp.where(lane == 0.0, i0, jnp.where(lane == 1.0, i1,
                                                  jnp.where(lane == 2.0, rank0, jnp.where(lane == 3.0, rank1, 0.0))))
    ids_ref[...] = packed.astype(jnp.int32)
    w0_ref[...] = jnp.broadcast_to(w0, lg.shape)
    w1_ref[...] = jnp.broadcast_to(w1, lg.shape)


def _mixer_prompt_kernel(xp_ref, xn_ref, gmix_ref, win_ref, wdw_ref, bdw_ref, lng_ref, lnb_ref, wpool_ref,
                         pscale_ref, wout_ref, gffn_ref, wrh_ref, wrl_ref, br_ref, ltri_ref,
                         x2_ref, hf_ref, ids_ref, w0_ref, w1_ref, nconv_ref, npool_ref, xs_ref, cnt_ref,
                         h_scr, pg0, pg1, ext0, ext1, extu0, extu1, yc_scr, sd0, sd1, mix0, mix1, hhi_scr, hlo_scr,
                         zbuf, cnt_scr, zsem, *, T, d, dc, dp, n_batch):
    n_s = pl.num_programs(1)
    b = pl.program_id(0)
    s = pl.program_id(1)
    t = b * n_s + s
    gd = dp // len(POOL_WINDOWS)
    pg_scr, ext_scr, extu_scr, sd_scr, mix_scr = (pg0, pg1), (ext0, ext1), (extu0, extu1), (sd0, sd1), (mix0, mix1)

    def in_proj(x_ref, slot, static):
        _rmsnorm_to_bf16(x_ref, gmix_ref, h_scr, T, static=static)
        ext_scr[slot][TAIL:TAIL + T, :] = jnp.dot(h_scr[...], win_ref[:, 0:dc], preferred_element_type=F32)
        pg_scr[slot][...] = jnp.dot(h_scr[...], win_ref[:, dc:2 * dc], preferred_element_type=F32)
        extu_scr[slot][PTAIL:PTAIL + T, :] = jnp.dot(h_scr[...], win_ref[:, 2 * dc:2 * dc + dp],
                                                     preferred_element_type=F32)

    def zero_copies():
        zr = zbuf.shape[0]
        return [pltpu.make_async_copy(zbuf, xs_ref.at[pl.ds(j * zr, zr)], zsem.at[0])
                for j in range(xs_ref.shape[0] // zr)]

    def finish_prev(q):
        _pool_project(sd_scr[q], wpool_ref, pscale_ref, mix_scr[q], dc, gd)
        _out_proj_residual(xp_ref, mix_scr[q], wout_ref, x2_ref)

    @pl.when(t == 0)
    def _():
        ext0[0:TAIL, :] = jnp.zeros((TAIL, dc), F32)
        extu0[0:PTAIL, :] = jnp.zeros((PTAIL, dp), F32)
        sd1[...] = jnp.zeros(sd1.shape, BF16)
        mix1[...] = jnp.zeros(mix1.shape, BF16)
        zbuf[...] = jnp.zeros(zbuf.shape, BF16)
        cnt_scr[...] = jnp.zeros(cnt_scr.shape, F32)
        for c in zero_copies():
            c.start()
        in_proj(xp_ref, 0, False)

    for par in range(2):
        @pl.when((b == n_batch) & (s == 0) & (t % 2 == par))
        def _():
            finish_prev(1 - par)

    @pl.when((b == n_batch) & (s == 0))
    def _():
        for c in zero_copies():
            c.wait()

    @pl.when((b == n_batch) & (s > 0))
    def _():
        x2_ref[...] = jnp.zeros(x2_ref.shape, F32)
        hf_ref[...] = jnp.zeros(hf_ref.shape, BF16)
        ids_ref[...] = jnp.zeros(ids_ref.shape, jnp.int32)
        w0_ref[...] = jnp.zeros(w0_ref.shape, F32)
        w1_ref[...] = jnp.zeros(w1_ref.shape, F32)

    def step(p):
        q = 1 - p
        in_proj(xn_ref, q, True)

        R = 16
        for i in range(T // R):
            es = pl.ds(i * R + TAIL, R)
            ext_scr[p][es, :] = ext_scr[p][es, :] * _sigmoid(pg_scr[p][pl.ds(i * R, R), :])

        RC, LC = 32, 128
        BR = RC + TAIL
        shift0 = TAIL - (CONV_W - 1)
        chain = None
        for i in range(T // RC):
            r0 = i * RC
            for lc in range(dc // LC):
                ls = slice(lc * LC, (lc + 1) * LC)
                blk = ext_scr[p][pl.ds(r0, BR), ls]
                if chain is None:
                    acc = jnp.zeros((RC, LC), F32)
                else:
                    z = (lax.bitcast_convert_type(chain, jnp.uint32) >> 16) >> 16
                    acc = jnp.tile(lax.bitcast_convert_type(z, F32), (RC // SUBLANES, 1))
                for sft in range(SUBLANES):
                    taps = [k for k in range(CONV_W) if (k + shift0) % SUBLANES == sft]
                    if not taps:
                        continue
                    rolled = blk if sft == 0 else pltpu.roll(blk, BR - sft, 0)
                    for k in taps:
                        qq = (k + shift0) // SUBLANES
                        acc = acc + wdw_ref[k:k + 1, ls] * rolled[qq * SUBLANES:qq * SUBLANES + RC, :]
                yc_scr[pl.ds(r0, RC), ls] = acc + bdw_ref[:, ls]
                chain = acc[0:SUBLANES, :]

        _layernorm_swish(yc_scr, lng_ref, lnb_ref, mix_scr[p], T, dc, static=True)

        RP = 32
        BP = RP + PTAIL
        pos_base = s * T
        for i in range(T // RP):
            r0 = i * RP
            pos = (pos_base + r0 + lax.broadcasted_iota(jnp.int32, (RP, gd), 0)).astype(F32)
            for g, w in enumerate(POOL_WINDOWS):
                ls = slice(g * gd, (g + 1) * gd)
                blk = extu_scr[p][pl.ds(r0, BP), ls]
                run = blk
                span = 1
                while span < w:
                    run = run + pltpu.roll(run, span, 0)
                    span *= 2
                cnt = jnp.minimum(pos + 1.0, jnp.float32(w))
                mean = run[PTAIL:PTAIL + RP, :] / cnt
                sd_scr[p][pl.ds(r0, RP), ls] = (mean - blk[PTAIL:PTAIL + RP, :]).astype(BF16)

        nconv_ref[...] = ext_scr[p][TAIL + T - (CONV_W - 1):TAIL + T, :]
        npool_ref[...] = extu_scr[p][PTAIL + T - (POOL_MAX_W - 1):PTAIL + T, :]
        keep = s != n_s - 1
        ext_scr[q][0:TAIL, :] = jnp.where(keep, ext_scr[p][T:T + TAIL, :], 0.0)
        extu_scr[q][0:PTAIL, :] = jnp.where(keep, extu_scr[p][T:T + PTAIL, :], 0.0)

        finish_prev(q)

    for par in range(2):
        pl.when((b < n_batch) & (t % 2 == par))(functools.partial(step, par))

    @pl.when(((b < n_batch) & (t > 0)) | ((b == n_batch) & (s == 0)))
    def _():
        _router(x2_ref, gffn_ref, wrh_ref, wrl_ref, br_ref, ltri_ref, hf_ref, ids_ref, w0_ref, w1_ref, cnt_ref,
                hhi_scr, hlo_scr, cnt_scr, T, d)


def _mixer_sample_kernel(x_ref, sconv_ref, spool_ref, gmix_ref, win_ref, wdw_ref, bdw_ref, lng_ref,
                         lnb_ref, wpool_ref, pscale_ref, wout_ref, gffn_ref, wrh_ref, wrl_ref, br_ref,
                         ltri_ref, cnt_in_ref,
                         x2_in, hf_in, ids_in, w0_in, w1_in,
                         x2_ref, hf_ref, ids_ref, w0_ref, w1_ref, nconv_ref, npool_ref, cnt_ref,
                         xt_scr, h_scr, a_scr, pg_scr, u_scr, yc_scr, sd_scr, mix_scr, hhi_scr, hlo_scr, cnt_scr,
                         *, TS, BS, d, dc, dp):
    del x2_in, hf_in, ids_in, w0_in, w1_in

    @pl.when(pl.program_id(0) == 0)
    def _():
        cnt_scr[...] = cnt_in_ref[...]

    T = TS * BS
    gd = dp // len(POOL_WINDOWS)
    NH = CONV_W - 1
    NP = POOL_MAX_W - 1

    for t in range(TS):
        xt_scr[t * BS:(t + 1) * BS, :] = x_ref[t]

    _rmsnorm_to_bf16(xt_scr, gmix_ref, h_scr, T)
    a_scr[...] = jnp.dot(h_scr[...], win_ref[:, 0:dc], preferred_element_type=F32)
    pg_scr[...] = jnp.dot(h_scr[...], win_ref[:, dc:2 * dc], preferred_element_type=F32)
    u_scr[...] = jnp.dot(h_scr[...], win_ref[:, 2 * dc:2 * dc + dp], preferred_element_type=F32)

    R = 16

    def glu(i, c):
        rs = _rows(i, R)
        a_scr[rs, :] = a_scr[rs, :] * _sigmoid(pg_scr[rs, :])
        return c

    lax.fori_loop(0, T // R, glu, 0, unroll=ROW_LOOP_UNROLL)

    def ext_conv(j, rs, ls):
        if j < NH:
            return sconv_ref[j, rs, ls]
        return a_scr[pl.ds((j - NH) * BS + rs.start, rs.size), ls]

    def ext_pool(j, rs, ls):
        if j < NP:
            return spool_ref[j, rs, ls]
        return u_scr[pl.ds((j - NP) * BS + rs.start, rs.size), ls]

    RC, LC = 32, 256

    def conv(i, c):
        rs = _rows(i, RC)
        for t in range(TS):
            for lc in range(dc // LC):
                ls = slice(lc * LC, (lc + 1) * LC)
                acc = jnp.zeros((RC, LC), F32)
                for k in range(CONV_W):
                    acc = acc + wdw_ref[k:k + 1, ls] * ext_conv(t + k, rs, ls)
                yc_scr[pl.ds(t * BS + rs.start, RC), ls] = acc + bdw_ref[:, ls]
        return c

    lax.fori_loop(0, BS // RC, conv, 0)

    _layernorm_swish(yc_scr, lng_ref, lnb_ref, mix_scr, T, dc)

    def pool(i, c):
        rs = _rows(i, RC)
        for t in range(TS):
            for g, w in enumerate(POOL_WINDOWS):
                ls = slice(g * gd, (g + 1) * gd)
                tot = ext_pool(NP + t, rs, ls)
                cur = tot
                for back in range(1, w):
                    tot = tot + ext_pool(NP + t - back, rs, ls)
                sd_scr[pl.ds(t * BS + rs.start, RC), ls] = (tot / jnp.float32(w) - cur).astype(BF16)
        return c

    lax.fori_loop(0, BS // RC, pool, 0)

    _pool_project(sd_scr, wpool_ref, pscale_ref, mix_scr, dc, gd)

    for j in range(NH):
        src = j + TS
        nconv_ref[j] = sconv_ref[src] if src < NH else a_scr[(src - NH) * BS:(src - NH + 1) * BS, :]
    for j in range(NP):
        src = j + TS
        npool_ref[j] = spool_ref[src] if src < NP else u_scr[(src - NP) * BS:(src - NP + 1) * BS, :]

    _out_proj_residual(xt_scr, mix_scr, wout_ref, x2_ref)
    _router(x2_ref, gffn_ref, wrh_ref, wrl_ref, br_ref, ltri_ref, hf_ref, ids_ref, w0_ref, w1_ref, cnt_ref,
            hhi_scr, hlo_scr, cnt_scr, T, d)


def _strict_lower_ones(n):
    return jnp.tril(jnp.ones((n, n), BF16), -1)


def _const_spec(shape):
    nd = len(shape)
    return pl.BlockSpec(shape, lambda *a: (0,) * nd, pipeline_mode=pl.Buffered(1))


def _mixer_weight_specs(d, dc, dp, cols):
    gd = dp // len(POOL_WINDOWS)
    return [
        _const_spec((1, d)),
        _const_spec((d, cols)),
        _const_spec((CONV_W, dc)),
        _const_spec((1, dc)),
        _const_spec((1, dc)),
        _const_spec((1, dc)),
        _const_spec((len(POOL_WINDOWS), gd, gd)),
        _const_spec((1, dp)),
        _const_spec((dc + dp, d)),
        _const_spec((1, d)),
        _const_spec((d, LANES)),
        _const_spec((d, LANES)),
        _const_spec((1, LANES)),
    ]


def _mixer_prompt(x, wts, *, T, n_total, n_sorted_rows, zero_rows):
    B, S, d = x.shape
    dc = wts[2].shape[1]
    dp = wts[7].shape[1]
    cols = wts[1].shape[1]
    n_s = S // T
    N = n_total
    n_blk = N // T
    assert N % T == 0 and N - B * S <= S
    blk = lambda b, s: jnp.clip(b * n_s + s - 1, 0, n_blk - 1)
    tok = lambda b, s: (blk(b, s), 0)
    tok4 = lambda b, s: (blk(b, s), 0, 0, 0)
    bclamp = lambda b: jnp.minimum(b, B - 1)

    def nxt(b, s):
        wrap = s + 1 == n_s
        return (jnp.where(wrap, bclamp(b + 1), bclamp(b)), jnp.where(wrap, 0, s + 1), 0)

    def prv(b, s):
        tp = jnp.clip(b * n_s + s - 1, 0, B * n_s - 1)
        return (tp // n_s, tp % n_s, 0)
    out_shape = (
        jax.ShapeDtypeStruct((N, d), F32),
        jax.ShapeDtypeStruct((N // TOK_CHUNK, TOK_CHUNK, d // LANES, LANES), BF16),
        jax.ShapeDtypeStruct((N, LANES), jnp.int32),
        jax.ShapeDtypeStruct((N, LANES), F32),
        jax.ShapeDtypeStruct((N, LANES), F32),
        jax.ShapeDtypeStruct((1, B, CONV_W - 1, dc), F32),
        jax.ShapeDtypeStruct((1, B, POOL_MAX_W - 1, dp), F32),
        jax.ShapeDtypeStruct((n_sorted_rows, d // LANES, LANES), BF16),
        jax.ShapeDtypeStruct((SUBLANES, LANES), F32),
    )
    assert n_sorted_rows % zero_rows == 0
    out_specs = (
        pl.BlockSpec((T, d), tok),
        pl.BlockSpec((T // TOK_CHUNK, TOK_CHUNK, d // LANES, LANES), tok4),
        pl.BlockSpec((T, LANES), tok),
        pl.BlockSpec((T, LANES), tok),
        pl.BlockSpec((T, LANES), tok),
        pl.BlockSpec((None, None, CONV_W - 1, dc), lambda b, s: (0, bclamp(b), 0, 0)),
        pl.BlockSpec((None, None, POOL_MAX_W - 1, dp), lambda b, s: (0, bclamp(b), 0, 0)),
        pl.BlockSpec(memory_space=pl.ANY),
        pl.BlockSpec((SUBLANES, LANES), lambda b, s: (0, 0)),
    )
    scratch = [
        pltpu.VMEM((T, d), BF16),
        pltpu.VMEM((T, dc), F32), pltpu.VMEM((T, dc), F32),
        pltpu.VMEM((T + TAIL, dc), F32), pltpu.VMEM((T + TAIL, dc), F32),
        pltpu.VMEM((T + PTAIL, dp), F32), pltpu.VMEM((T + PTAIL, dp), F32),
        pltpu.VMEM((T, dc), F32),
        pltpu.VMEM((T, dp), BF16), pltpu.VMEM((T, dp), BF16),
        pltpu.VMEM((T, dc + dp), BF16), pltpu.VMEM((T, dc + dp), BF16),
        pltpu.VMEM((T, d), BF16),
        pltpu.VMEM((T, d), BF16),
        pltpu.VMEM((zero_rows, d // LANES, LANES), BF16),
        pltpu.VMEM((SUBLANES, LANES), F32),
        pltpu.SemaphoreType.DMA((1,)),
    ]
    return pl.pallas_call(
        functools.partial(_mixer_prompt_kernel, T=T, d=d, dc=dc, dp=dp, n_batch=B),
        grid=(B + 1, n_s),
        in_specs=[pl.BlockSpec((None, T, d), prv),
                  pl.BlockSpec((None, T, d), nxt)]
                 + _mixer_weight_specs(d, dc, dp, cols) + [_const_spec((T, T))],
        out_specs=out_specs,
        out_shape=out_shape,
        scratch_shapes=scratch,
        compiler_params=pltpu.CompilerParams(
            dimension_semantics=("arbitrary", "arbitrary"), vmem_limit_bytes=VMEM_LIMIT),
        name="mixer_prompt",
    )(x, x, *wts, _strict_lower_ones(T))


def _mixer_sample(x_t, sconv_t, spool_t, wts, tok_arrays, counts, *, BS, tok0):
    TS, Bd, d = x_t.shape
    dc = wts[2].shape[1]
    dp = wts[7].shape[1]
    cols = wts[1].shape[1]
    T = TS * BS
    n_b = Bd // BS
    b0 = tok0 // T
    tok = lambda i: (b0 + i, 0)
    tok4 = lambda i: (b0 + i, 0, 0, 0)
    out_shape = tuple(jax.ShapeDtypeStruct(a.shape, a.dtype) for a in tok_arrays) + (
        jax.ShapeDtypeStruct((CONV_W - 1, Bd, dc), F32),
        jax.ShapeDtypeStruct((POOL_MAX_W - 1, Bd, dp), F32),
        jax.ShapeDtypeStruct(counts.shape, F32),
    )
    out_specs = (
        pl.BlockSpec((T, d), tok),
        pl.BlockSpec((T // TOK_CHUNK, TOK_CHUNK, d // LANES, LANES), tok4),
        pl.BlockSpec((T, LANES), tok),
        pl.BlockSpec((T, LANES), tok),
        pl.BlockSpec((T, LANES), tok),
        pl.BlockSpec((CONV_W - 1, BS, dc), lambda i: (0, i, 0)),
        pl.BlockSpec((POOL_MAX_W - 1, BS, dp), lambda i: (0, i, 0)),
        pl.BlockSpec(counts.shape, lambda i: (0, 0)),
    )
    scratch = [
        pltpu.VMEM((T, d), F32),
        pltpu.VMEM((T, d), BF16),
        pltpu.VMEM((T, dc), F32),
        pltpu.VMEM((T, dc), F32),
        pltpu.VMEM((T, dp), F32),
        pltpu.VMEM((T, dc), F32),
        pltpu.VMEM((T, dp), BF16),
        pltpu.VMEM((T, dc + dp), BF16),
        pltpu.VMEM((T, d), BF16),
        pltpu.VMEM((T, d), BF16),
        pltpu.VMEM(counts.shape, F32),
    ]
    return pl.pallas_call(
        functools.partial(_mixer_sample_kernel, TS=TS, BS=BS, d=d, dc=dc, dp=dp),
        grid=(n_b,),
        in_specs=[pl.BlockSpec((TS, BS, d), lambda i: (0, i, 0)),
                  pl.BlockSpec((CONV_W - 1, BS, dc), lambda i: (0, i, 0), pipeline_mode=pl.Buffered(1)),
                  pl.BlockSpec((POOL_MAX_W - 1, BS, dp), lambda i: (0, i, 0), pipeline_mode=pl.Buffered(1))]
                 + _mixer_weight_specs(d, dc, dp, cols)
                 + [_const_spec((T, T)), _const_spec(counts.shape)]
                 + [pl.BlockSpec(memory_space=pl.ANY)] * len(tok_arrays),
        out_specs=out_specs,
        out_shape=out_shape,
        scratch_shapes=scratch,
        input_output_aliases={5 + len(wts) + j: j for j in range(len(tok_arrays))},
        compiler_params=pltpu.CompilerParams(
            dimension_semantics=("arbitrary",), vmem_limit_bytes=VMEM_LIMIT),
        name="mixer_sample",
    )(x_t, sconv_t, spool_t, *wts, _strict_lower_ones(T), counts, *tok_arrays)


def _dispatch_kernel(pos_ref, hf_ref, dst_in, xs_in_ref, xs_ref, dst_ref, sem, isem, *, R, n_tok):
    del xs_in_ref
    base = pl.program_id(0) * R

    @pl.when(pl.program_id(0) == 0)
    def _():
        init = pltpu.make_async_copy(dst_in, dst_ref, isem.at[0])
        init.start()
        init.wait()

    G = 8

    def body(g, c):
        r0 = g * G
        ps = [pos_ref[2 * (base + r0) + q] for q in range(2 * G)]
        for q in range(2 * G):
            r, k = r0 + q // 2, q % 2
            pltpu.make_async_copy(hf_ref.at[pl.ds(r, 1)], xs_ref.at[pl.ds(ps[q], 1)], sem.at[0]).start(priority=k)
            dst_ref[ps[q]] = k * n_tok + base + r
        return c

    lax.fori_loop(0, R // G, body, 0)
    for k in range(2):
        pltpu.make_async_copy(hf_ref, xs_ref.at[pl.ds(0, R)], sem.at[0]).wait()


def _dispatch(pos, hf, xs_zero, *, R, n_tiles, tm):
    n_tok = hf.shape[0]
    assert n_tok % R == 0
    p = jnp.arange((n_tiles + 1) * tm, dtype=jnp.int32)
    tile = p // tm
    buf = jnp.where(tile == n_tiles, 1, tile % 2)
    dst_init = 2 * n_tok + buf * tm + p % tm
    hbm = pl.BlockSpec(memory_space=pl.ANY)
    return pl.pallas_call(
        functools.partial(_dispatch_kernel, R=R, n_tok=n_tok),
        grid_spec=pltpu.PrefetchScalarGridSpec(
            num_scalar_prefetch=1,
            grid=(n_tok // R,),
            in_specs=[pl.BlockSpec((R,) + hf.shape[1:], lambda s, pos: (s, 0, 0)), hbm, hbm],
            out_specs=(hbm, pl.BlockSpec(memory_space=pltpu.SMEM)),
            scratch_shapes=[pltpu.SemaphoreType.DMA((1,)), pltpu.SemaphoreType.DMA((1,))],
        ),
        out_shape=(jax.ShapeDtypeStruct(xs_zero.shape, xs_zero.dtype),
                   jax.ShapeDtypeStruct(dst_init.shape, jnp.int32)),
        input_output_aliases={3: 0},
        compiler_params=pltpu.CompilerParams(dimension_semantics=("arbitrary",)),
        name="dispatch",
    )(pos, hf, dst_init, xs_zero)


def _expert_mlp_kernel(te_ref, na_ref, nx_ref, nx2_ref, dst_prev_ref, dst_cur_ref, xs_ref, wg_ref, wu_ref, wd_ref, o_ref,
                       ybuf0, ybuf1, wgf, wuf, wdf, wg_scr, wu_scr, wd_scr, wslot, ssem, wsem,
                       *, tm, n_tiles, d):
    i = pl.program_id(0)
    n_act = na_ref[0]
    active = i < n_act
    slot = i % 2
    last = n_tiles - 1
    prev = te_ref[jnp.maximum(i - 1, 0)]
    new_expert = (i == 0) | (te_ref[i] != prev)
    ybuf = (ybuf0, ybuf1)

    def scatter_start(dst_ref, b):
        for r in range(tm):
            pltpu.make_async_copy(ybuf[b].at[pl.ds(r, 1)], o_ref.at[pl.ds(dst_ref[0, 0, r], 1)],
                                  ssem.at[b]).start(priority=r % 2)

    def scatter_wait(b):
        pltpu.make_async_copy(ybuf[b], o_ref.at[pl.ds(0, tm)], ssem.at[b]).wait()

    def weight_copies(e, b):
        return [pltpu.make_async_copy(src.at[e], buf.at[b], wsem.at[b])
                for src, buf in ((wg_ref, wgf), (wu_ref, wuf), (wd_ref, wdf))]

    @pl.when(i == 0)
    def _():
        wslot[0] = 0
        for c in weight_copies(te_ref[0], 0):
            c.start()

        @pl.when(nx_ref[0] >= 0)
        def _():
            for c in weight_copies(nx_ref[0], 1):
                c.start()
        ybuf1[...] = jnp.zeros(ybuf1.shape, BF16)
        trash0 = pltpu.make_async_copy(ybuf1, o_ref.at[pl.ds(o_ref.shape[0] - 2 * tm, tm)], ssem.at[0])
        trash0.start()
        trash0.wait()

    @pl.when(active & new_expert)
    def _():
        b = wslot[0]
        after_next = nx2_ref[i]

        @pl.when(after_next >= 0)
        def _():
            for c in weight_copies(after_next, (b + 2) % N_WEIGHT_BUFS):
                c.start()

        for c in weight_copies(te_ref[i], b):
            c.wait()
        wg_scr[...] = wgf[b].astype(BF16)
        wu_scr[...] = wuf[b].astype(BF16)
        wd_scr[...] = wdf[b].astype(BF16)
        wslot[0] = (b + 1) % N_WEIGHT_BUFS

    for par in range(2):
        is_par = slot == par

        @pl.when(is_par & (i >= 1) & (i - 2 < n_act))
        def _():
            scatter_wait(par)

        @pl.when(is_par & active)
        def _():
            x = xs_ref[...].reshape(tm, d)
            g = jnp.dot(x, wg_scr[...], preferred_element_type=F32)
            u = jnp.dot(x, wu_scr[...], preferred_element_type=F32)
            act = (g * _sigmoid(g) * u).astype(BF16)
            y = jnp.dot(act, wd_scr[...], preferred_element_type=F32)
            ybuf[par][...] = y.astype(BF16).reshape(tm, d // LANES, LANES)
            scatter_start(dst_prev_ref, 1 - par)

        @pl.when(is_par & (i == n_act))
        def _():
            scatter_start(dst_prev_ref, 1 - par)

        @pl.when(is_par & (i == last))
        def _():
            @pl.when(last - 1 < n_act)
            def _():
                scatter_wait(1 - par)

            @pl.when(last < n_act)
            def _():
                scatter_start(dst_cur_ref, par)
                scatter_wait(par)


def _expert_mlp(tile_expert, n_active, next_expert, after_next_expert, dst, xs, w_gate, w_up, w_down, *, tm, n_tok):
    E, d, de = w_gate.shape
    n_tiles = xs.shape[0] // tm
    row = (tm, d // LANES, LANES)
    cur = lambda i, te, na, nx, nx2: (i, 0, 0)
    prv = lambda i, te, na, nx, nx2: (jnp.where(i == 0, n_tiles, i - 1), 0, 0)
    hbm = pl.BlockSpec(memory_space=pl.ANY)
    smem_tile = lambda imap: pl.BlockSpec((1, 1, tm), imap, memory_space=pltpu.SMEM)
    return pl.pallas_call(
        functools.partial(_expert_mlp_kernel, tm=tm, n_tiles=n_tiles, d=d),
        grid_spec=pltpu.PrefetchScalarGridSpec(
            num_scalar_prefetch=4,
            grid=(n_tiles,),
            in_specs=[
                smem_tile(prv), smem_tile(cur),
                pl.BlockSpec(row, lambda i, te, na, nx, nx2: (jnp.clip(i, 0, jnp.maximum(na[0] - 1, 0)), 0, 0)),
                hbm, hbm, hbm,
            ],
            out_specs=hbm,
            scratch_shapes=[pltpu.VMEM(row, BF16), pltpu.VMEM(row, BF16),
                            pltpu.VMEM((N_WEIGHT_BUFS, d, de), F32), pltpu.VMEM((N_WEIGHT_BUFS, d, de), F32),
                            pltpu.VMEM((N_WEIGHT_BUFS, de, d), F32),
                            pltpu.VMEM((d, de), BF16), pltpu.VMEM((d, de), BF16), pltpu.VMEM((de, d), BF16),
                            pltpu.SMEM((1,), jnp.int32),
                            pltpu.SemaphoreType.DMA((2,)), pltpu.SemaphoreType.DMA((N_WEIGHT_BUFS,))],
        ),
        out_shape=jax.ShapeDtypeStruct((2 * n_tok + 2 * tm, d // LANES, LANES), BF16),
        compiler_params=pltpu.CompilerParams(
            dimension_semantics=("arbitrary",), vmem_limit_bytes=VMEM_LIMIT),
        name="expert_mlp",
    )(tile_expert, n_active, next_expert, after_next_expert, dst, dst, xs, w_gate, w_up, w_down)


def _combine_kernel(x2_ref, o0_ref, o1_ref, w0_ref, w1_ref, gfin_ref, y_ref, *, R, d):
    C = TOK_CHUNK

    def body(i, c):
        rs = _rows(i, C)
        w0 = w0_ref[rs, :]
        w1 = w1_ref[rs, :]
        o0 = o0_ref[i].reshape(C, d).astype(F32)
        o1 = o1_ref[i].reshape(C, d).astype(F32)
        parts = []
        ssq = jnp.zeros((C, LANES), F32)
        for j in range(d // LANES):
            ls = slice(j * LANES, (j + 1) * LANES)
            m = w0 * o0[:, ls] + w1 * o1[:, ls]
            v = x2_ref[rs, ls] + m
            ssq = ssq + v * v
            parts.append(v)
        ms = jnp.sum(ssq, axis=-1, keepdims=True) * jnp.float32(1.0 / d)
        inv = lax.rsqrt(ms + EPS)
        for j in range(d // LANES):
            ls = slice(j * LANES, (j + 1) * LANES)
            y_ref[rs, ls] = parts[j] * inv * gfin_ref[:, ls]
        return c

    lax.fori_loop(0, R // C, body, 0)


def _combine(x2, o, w0, w1, g_final, *, tok0, n_out):
    n_tok, d = x2.shape
    R = max(r for r in R_COMBINE_CHOICES if n_out % r == 0 and tok0 % r == 0 and n_tok % r == 0)
    n_steps = n_out // R
    b0 = tok0 // R
    b1 = n_tok // R
    tok = lambda s: (b0 + s, 0)
    orow = (R // TOK_CHUNK, TOK_CHUNK, d // LANES, LANES)
    o4 = o.reshape(o.shape[0] // TOK_CHUNK, TOK_CHUNK, d // LANES, LANES)
    return pl.pallas_call(
        functools.partial(_combine_kernel, R=R, d=d),
        grid=(n_steps,),
        in_specs=[
            pl.BlockSpec((R, d), tok),
            pl.BlockSpec(orow, lambda s: (b0 + s, 0, 0, 0)),
            pl.BlockSpec(orow, lambda s: (b1 + b0 + s, 0, 0, 0)),
            pl.BlockSpec((R, LANES), tok),
            pl.BlockSpec((R, LANES), tok),
            pl.BlockSpec((1, d), lambda s: (0, 0)),
        ],
        out_specs=pl.BlockSpec((R, d), lambda s: (s, 0)),
        out_shape=jax.ShapeDtypeStruct((n_out, d), F32),
        compiler_params=pltpu.CompilerParams(
            dimension_semantics=("arbitrary",), vmem_limit_bytes=VMEM_LIMIT),
        name="combine",
    )(x2, o4, o4, w0, w1, g_final)


def _routing_plan(ids, counts, tm, n_tiles):
    e0, e1, r0, r1 = ids[:, 0], ids[:, 1], ids[:, 2], ids[:, 3]
    ar = jnp.arange(N_EXPERTS, dtype=jnp.int32)
    tiles_e = (counts + tm - 1) // tm
    tile_end = jnp.cumsum(tiles_e)
    offs = (tile_end - tiles_e) * tm
    pos0 = jnp.sum(jnp.where(e0[:, None] == ar, offs[None, :], 0), axis=1) + r0
    pos1 = jnp.sum(jnp.where(e1[:, None] == ar, offs[None, :], 0), axis=1) + r1
    pos = jnp.stack([pos0, pos1], axis=1).reshape(-1).astype(jnp.int32)
    n_active = tile_end[-1].astype(jnp.int32)
    t = jnp.arange(n_tiles, dtype=jnp.int32)
    tq = jnp.minimum(t, n_active - 1)
    te = jnp.sum((tile_end[None, :] <= tq[:, None]).astype(jnp.int32), axis=1)
    te = jnp.minimum(te, N_EXPERTS - 1)
    later = (ar[None, :] > ar[:, None]) & (tiles_e[None, :] > 0)
    nxt_e = jnp.min(jnp.where(later, ar[None, :], N_EXPERTS), axis=1)
    nxt_e = jnp.where(nxt_e == N_EXPERTS, -1, nxt_e).astype(jnp.int32)
    nxt2_e = jnp.where(nxt_e >= 0, nxt_e[jnp.maximum(nxt_e, 0)], -1)
    return pos, te, n_active.reshape(1), nxt_e[te], nxt2_e[te]


T_PROMPT = 256
BS_SAMPLE = 32
TM_EXPERT = 256
N_WEIGHT_BUFS = 3
R_DISPATCH_CHOICES = (2176, 1088, 512, 256, 128)
R_COMBINE_CHOICES = (512, 256, 128)


def kernel(x_prompt, x_sample, state_conv, state_pool, g_mix, w_in, w_dw, b_dw, ln_g, ln_b, w_pool, pool_scale, w_out, g_ffn, w_rg, b_rg, w_re, b_re, w_gate, w_up, w_down, g_final):
    depth = g_mix.shape[0]
    assert depth == 1
    B, S, d = x_prompt.shape
    Bd, TS, _ = x_sample.shape
    n_p = B * S
    n_s = Bd * TS
    N = n_p + n_s

    assert ROUTER_GROUP_LANE0 == N_EXPERTS
    n_pad = LANES - N_EXPERTS - N_GROUPS
    w_r = jnp.concatenate([w_re[0], w_rg[0], jnp.zeros((d, n_pad), F32)], axis=1)
    w_r_hi = w_r.astype(BF16)
    w_r_lo = (w_r - w_r_hi.astype(F32)).astype(BF16)
    b_r = jnp.concatenate([b_re[0], b_rg[0], jnp.zeros((n_pad,), F32)])[None]
    wts = (g_mix[0][None], w_in[0].astype(BF16), w_dw[0], b_dw[0][None], ln_g[0][None], ln_b[0][None],
           w_pool[0].astype(BF16), pool_scale[0][None], w_out[0].astype(BF16), g_ffn[0][None],
           w_r_hi, w_r_lo, b_r)

    tm = TM_EXPERT
    n_tiles = (2 * N + N_EXPERTS * (tm - 1) + tm - 1) // tm
    *tok_arrays, nconv_p, npool_p, xs, counts_p = _mixer_prompt(x_prompt, wts, T=T_PROMPT, n_total=N,
                                                                n_sorted_rows=n_tiles * tm, zero_rows=tm)

    x_t = jnp.transpose(x_sample, (1, 0, 2))
    sconv_t = jnp.transpose(state_conv[0], (1, 0, 2))
    spool_t = jnp.transpose(state_pool[0], (1, 0, 2))
    x2, hf, ids, w0, w1, nconv_t, npool_t, counts = _mixer_sample(x_t, sconv_t, spool_t, wts, tok_arrays, counts_p,
                                                                  BS=BS_SAMPLE, tok0=n_p)

    pos, tile_expert, n_active, next_expert, after_next = _routing_plan(ids[:, 0:4], counts[0, 0:N_EXPERTS].astype(jnp.int32),
                                                            tm, n_tiles)
    hf = hf.reshape(N, d // LANES, LANES)
    r_disp = max(r for r in R_DISPATCH_CHOICES if N % r == 0)
    xs, dst = _dispatch(pos, hf, xs, R=r_disp, n_tiles=n_tiles, tm=tm)
    dst = dst.reshape(n_tiles + 1, 1, tm)
    o = _expert_mlp(tile_expert, n_active, next_expert, after_next, dst, xs, w_gate[0], w_up[0], w_down[0],
                    tm=tm, n_tok=N)

    gfin = g_final[None]
    y_p = _combine(x2, o, w0, w1, gfin, tok0=0, n_out=n_p)
    y_s = _combine(x2, o, w0, w1, gfin, tok0=n_p, n_out=n_s)

    y_prompt = y_p.reshape(B, S, d)
    y_sample = y_s.reshape(Bd // BS_SAMPLE, TS, BS_SAMPLE, d).transpose(0, 2, 1, 3).reshape(Bd, TS, d)
    new_conv_s = jnp.transpose(nconv_t, (1, 0, 2))[None]
    new_pool_s = jnp.transpose(npool_t, (1, 0, 2))[None]
    return (y_prompt, y_sample, nconv_p, new_conv_s, npool_p, new_pool_s)
```

```python
import functools

import jax
import jax.numpy as jnp
from jax import lax
from jax.experimental import pallas as pl
from jax.experimental.pallas import tpu as pltpu

F32 = jnp.float32
BF16 = jnp.bfloat16
EPS = 1e-6

LANES = 128
SUBLANES = 8
VMEM_LIMIT = 56 * 1024 * 1024

CONV_W = 31
POOL_WINDOWS = (2, 4, 8, 16)
POOL_MAX_W = 16
N_GROUPS = 4
PER_GROUP = 8
N_EXPERTS = N_GROUPS * PER_GROUP

ROUTER_GROUP_LANE0 = N_EXPERTS
TOK_CHUNK = 16
ROW_LOOP_UNROLL = 8
TAIL = 32
PTAIL = 16


def _rows(i, r):
    if isinstance(i, int):
        return pl.ds(i * r, r)
    return pl.ds(pl.multiple_of(i * r, r), r)


def _row_loop(n, body, *, static, unroll=1):
    if static:
        for i in range(n):
            body(i)
    else:
        def step(i, c):
            body(i)
            return c
        lax.fori_loop(0, n, step, 0, unroll=unroll)


def _sigmoid(x):
    return 1.0 / (1.0 + jnp.exp(-x))


def _rmsnorm_to_bf16(src_ref, g_ref, dst_ref, T, static=False):
    R = 16

    def body(i):
        rs = _rows(i, R)
        x = src_ref[rs, :]
        ms = jnp.mean(x * x, axis=-1, keepdims=True)
        dst_ref[rs, :] = (x * lax.rsqrt(ms + EPS) * g_ref[...]).astype(BF16)

    _row_loop(T // R, body, static=static, unroll=ROW_LOOP_UNROLL)


def _layernorm_swish(y_ref, lng_ref, lnb_ref, mix_ref, T, dc, static=False):
    R = 16

    def body(i):
        rs = _rows(i, R)
        y = y_ref[rs, :]
        mu = jnp.mean(y, axis=-1, keepdims=True)
        d = y - mu
        var = jnp.mean(d * d, axis=-1, keepdims=True)
        z = d * lax.rsqrt(var + EPS) * lng_ref[...] + lnb_ref[...]
        mix_ref[rs, 0:dc] = (z * _sigmoid(z)).astype(BF16)

    _row_loop(T // R, body, static=static, unroll=ROW_LOOP_UNROLL)


def _pool_project(sd_ref, wpool_ref, pscale_ref, mix_ref, dc, gd):
    for g in range(len(POOL_WINDOWS)):
        sl = slice(g * gd, (g + 1) * gd)
        o = jnp.dot(sd_ref[:, sl], wpool_ref[g], preferred_element_type=F32)
        mix_ref[:, dc + g * gd: dc + (g + 1) * gd] = (o * pscale_ref[:, sl]).astype(BF16)


def _out_proj_residual(x_ref, mix_ref, wout_ref, x2_ref):
    x2_ref[...] = x_ref[...] + jnp.dot(mix_ref[...], wout_ref[...], preferred_element_type=F32)


def _router(x2_ref, gffn_ref, wrh_ref, wrl_ref, br_ref, ltri_ref, hf_ref, ids_ref, w0_ref, w1_ref, cnt_ref,
            hhi_scr, hlo_scr, cnt_scr, T, d):
    R = TOK_CHUNK

    def body(i, c):
        rs = _rows(i, R)
        x = x2_ref[rs, :]
        ms = jnp.mean(x * x, axis=-1, keepdims=True)
        h = x * lax.rsqrt(ms + EPS) * gffn_ref[...]
        hi = h.astype(BF16)
        hf_ref[i] = hi.reshape(R, d // LANES, LANES)
        hhi_scr[rs, :] = hi
        hlo_scr[rs, :] = (h - hi.astype(F32)).astype(BF16)
        return c

    lax.fori_loop(0, T // R, body, 0, unroll=ROW_LOOP_UNROLL)

    hcat = jnp.concatenate([hhi_scr[...], hlo_scr[...]], axis=0)
    wcat = jnp.concatenate([wrh_ref[...], wrl_ref[...]], axis=1)
    prod = jnp.dot(hcat, wcat, preferred_element_type=F32)
    lg = prod[0:T, 0:LANES] + prod[T:2 * T, 0:LANES] + prod[0:T, LANES:2 * LANES] + br_ref[...]

    lane = lax.broadcasted_iota(jnp.int32, lg.shape, 1).astype(F32)
    neg = jnp.float32(-jnp.inf)
    big = jnp.float32(1e9)
    g_lo = jnp.float32(ROUTER_GROUP_LANE0)
    gmask = (lane >= g_lo) & (lane < g_lo + N_GROUPS)
    lgg = jnp.where(gmask, lg, neg)
    gmax = jnp.max(lgg, axis=-1, keepdims=True)
    gsel = jnp.min(jnp.where(lgg == gmax, lane, big), axis=-1, keepdims=True) - g_lo
    gsum = jnp.sum(jnp.where(gmask, jnp.exp(lg - gmax), 0.0), axis=-1, keepdims=True)
    p_g = 1.0 / gsum

    e_lo = gsel * PER_GROUP
    emask = (lane >= e_lo) & (lane < e_lo + PER_GROUP)
    le = jnp.where(emask, lg, neg)
    v0 = jnp.max(le, axis=-1, keepdims=True)
    i0 = jnp.min(jnp.where(le == v0, lane, big), axis=-1, keepdims=True)
    le2 = jnp.where(lane == i0, neg, le)
    v1 = jnp.max(le2, axis=-1, keepdims=True)
    i1 = jnp.min(jnp.where(le2 == v1, lane, big), axis=-1, keepdims=True)
    ex = jnp.exp(v1 - v0)
    den = 1.0 / (1.0 + ex)
    w0 = den * p_g
    w1 = ex * den * p_g

    sel0 = lane == i0
    sel1 = lane == i1
    m = jnp.where(sel0 | sel1, 1.0, 0.0)
    before = jnp.dot(ltri_ref[...], m.astype(BF16), preferred_element_type=F32) + cnt_scr[0:1, :]
    rank0 = jnp.sum(jnp.where(sel0, before, 0.0), axis=-1, keepdims=True)
    rank1 = jnp.sum(jnp.where(sel1, before, 0.0), axis=-1, keepdims=True)
    total = cnt_scr[0:1, :] + jnp.sum(m, axis=0, keepdims=True)
    cnt_scr[...] = jnp.broadcast_to(total, cnt_scr.shape)
    cnt_ref[...] = jnp.broadcast_to(total, cnt_ref.shape)

    packed = jnp.where(lane == 0.0, i0, jnp.where(lane == 1.0, i1,
                                                  jnp.where(lane == 2.0, rank0, jnp.where(lane == 3.0, rank1, 0.0))))
    ids_ref[...] = packed.astype(jnp.int32)
    w0_ref[...] = jnp.broadcast_to(w0, lg.shape)
    w1_ref[...] = jnp.broadcast_to(w1, lg.shape)


def _mixer_prompt_kernel(xp_ref, xn_ref, gmix_ref, win_ref, wdw_ref, bdw_ref, lng_ref, lnb_ref, wpool_ref,
                         pscale_ref, wout_ref, gffn_ref, wrh_ref, wrl_ref, br_ref, ltri_ref,
                         x2_ref, hf_ref, ids_ref, w0_ref, w1_ref, nconv_ref, npool_ref, xs_ref, cnt_ref,
                         h_scr, pg0, pg1, ext0, ext1, extu0, extu1, yc_scr, sd0, sd1, mix0, mix1, hhi_scr, hlo_scr,
                         zbuf, cnt_scr, zsem, *, T, d, dc, dp, n_batch):
    n_s = pl.num_programs(1)
    b = pl.program_id(0)
    s = pl.program_id(1)
    t = b * n_s + s
    gd = dp // len(POOL_WINDOWS)
    pg_scr, ext_scr, extu_scr, sd_scr, mix_scr = (pg0, pg1), (ext0, ext1), (extu0, extu1), (sd0, sd1), (mix0, mix1)

    def in_proj(x_ref, slot, static):
        _rmsnorm_to_bf16(x_ref, gmix_ref, h_scr, T, static=static)
        ext_scr[slot][TAIL:TAIL + T, :] = jnp.dot(h_scr[...], win_ref[:, 0:dc], preferred_element_type=F32)
        pg_scr[slot][...] = jnp.dot(h_scr[...], win_ref[:, dc:2 * dc], preferred_element_type=F32)
        extu_scr[slot][PTAIL:PTAIL + T, :] = jnp.dot(h_scr[...], win_ref[:, 2 * dc:2 * dc + dp],
                                                     preferred_element_type=F32)

    def zero_copies():
        zr = zbuf.shape[0]
        return [pltpu.make_async_copy(zbuf, xs_ref.at[pl.ds(j * zr, zr)], zsem.at[0])
                for j in range(xs_ref.shape[0] // zr)]

    def finish_prev(q):
        _pool_project(sd_scr[q], wpool_ref, pscale_ref, mix_scr[q], dc, gd)
        _out_proj_residual(xp_ref, mix_scr[q], wout_ref, x2_ref)

    @pl.when(t == 0)
    def _():
        ext0[0:TAIL, :] = jnp.zeros((TAIL, dc), F32)
        extu0[0:PTAIL, :] = jnp.zeros((PTAIL, dp), F32)
        sd1[...] = jnp.zeros(sd1.shape, BF16)
        mix1[...] = jnp.zeros(mix1.shape, BF16)
        zbuf[...] = jnp.zeros(zbuf.shape, BF16)
        cnt_scr[...] = jnp.zeros(cnt_scr.shape, F32)
        for c in zero_copies():
            c.start()
        in_proj(xp_ref, 0, False)

    for par in range(2):
        @pl.when((b == n_batch) & (s == 0) & (t % 2 == par))
        def _():
            finish_prev(1 - par)

    @pl.when((b == n_batch) & (s == 0))
    def _():
        for c in zero_copies():
            c.wait()

    @pl.when((b == n_batch) & (s > 0))
    def _():
        x2_ref[...] = jnp.zeros(x2_ref.shape, F32)
        hf_ref[...] = jnp.zeros(hf_ref.shape, BF16)
        ids_ref[...] = jnp.zeros(ids_ref.shape, jnp.int32)
        w0_ref[...] = jnp.zeros(w0_ref.shape, F32)
        w1_ref[...] = jnp.zeros(w1_ref.shape, F32)

    def step(p):
        q = 1 - p
        in_proj(xn_ref, q, True)

        R = 16
        for i in range(T // R):
            es = pl.ds(i * R + TAIL, R)
            ext_scr[p][es, :] = ext_scr[p][es, :] * _sigmoid(pg_scr[p][pl.ds(i * R, R), :])

        RC, LC = 32, 128
        BR = RC + TAIL
        shift0 = TAIL - (CONV_W - 1)
        chain = None
        for i in range(T // RC):
            r0 = i * RC
            for lc in range(dc // LC):
                ls = slice(lc * LC, (lc + 1) * LC)
                blk = ext_scr[p][pl.ds(r0, BR), ls]
                if chain is None:
                    acc = jnp.zeros((RC, LC), F32)
                else:
                    z = (lax.bitcast_convert_type(chain, jnp.uint32) >> 16) >> 16
                    acc = jnp.tile(lax.bitcast_convert_type(z, F32), (RC // SUBLANES, 1))
                for sft in range(SUBLANES):
                    taps = [k for k in range(CONV_W) if (k + shift0) % SUBLANES == sft]
                    if not taps:
                        continue
                    rolled = blk if sft == 0 else pltpu.roll(blk, BR - sft, 0)
                    for k in taps:
                        qq = (k + shift0) // SUBLANES
                        acc = acc + wdw_ref[k:k + 1, ls] * rolled[qq * SUBLANES:qq * SUBLANES + RC, :]
                yc_scr[pl.ds(r0, RC), ls] = acc + bdw_ref[:, ls]
                chain = acc[0:SUBLANES, :]

        _layernorm_swish(yc_scr, lng_ref, lnb_ref, mix_scr[p], T, dc, static=True)

        RP = 32
        BP = RP + PTAIL
        pos_base = s * T
        for i in range(T // RP):
            r0 = i * RP
            pos = (pos_base + r0 + lax.broadcasted_iota(jnp.int32, (RP, gd), 0)).astype(F32)
            for g, w in enumerate(POOL_WINDOWS):
                ls = slice(g * gd, (g + 1) * gd)
                blk = extu_scr[p][pl.ds(r0, BP), ls]
                run = blk
                span = 1
                while span < w:
                    run = run + pltpu.roll(run, span, 0)
                    span *= 2
                cnt = jnp.minimum(pos + 1.0, jnp.float32(w))
                mean = run[PTAIL:PTAIL + RP, :] / cnt
                sd_scr[p][pl.ds(r0, RP), ls] = (mean - blk[PTAIL:PTAIL + RP, :]).astype(BF16)

        nconv_ref[...] = ext_scr[p][TAIL + T - (CONV_W - 1):TAIL + T, :]
        npool_ref[...] = extu_scr[p][PTAIL + T - (POOL_MAX_W - 1):PTAIL + T, :]
        keep = s != n_s - 1
        ext_scr[q][0:TAIL, :] = jnp.where(keep, ext_scr[p][T:T + TAIL, :], 0.0)
        extu_scr[q][0:PTAIL, :] = jnp.where(keep, extu_scr[p][T:T + PTAIL, :], 0.0)

        finish_prev(q)

    for par in range(2):
        pl.when((b < n_batch) & (t % 2 == par))(functools.partial(step, par))

    @pl.when(((b < n_batch) & (t > 0)) | ((b == n_batch) & (s == 0)))
    def _():
        _router(x2_ref, gffn_ref, wrh_ref, wrl_ref, br_ref, ltri_ref, hf_ref, ids_ref, w0_ref, w1_ref, cnt_ref,
                hhi_scr, hlo_scr, cnt_scr, T, d)


def _mixer_sample_kernel(x_ref, sconv_ref, spool_ref, gmix_ref, win_ref, wdw_ref, bdw_ref, lng_ref,
                         lnb_ref, wpool_ref, pscale_ref, wout_ref, gffn_ref, wrh_ref, wrl_ref, br_ref,
                         ltri_ref, cnt_in_ref,
                         x2_in, hf_in, ids_in, w0_in, w1_in,
                         x2_ref, hf_ref, ids_ref, w0_ref, w1_ref, nconv_ref, npool_ref, cnt_ref,
                         xt_scr, h_scr, a_scr, pg_scr, u_scr, yc_scr, sd_scr, mix_scr, hhi_scr, hlo_scr, cnt_scr,
                         *, TS, BS, d, dc, dp):
    del x2_in, hf_in, ids_in, w0_in, w1_in

    @pl.when(pl.program_id(0) == 0)
    def _():
        cnt_scr[...] = cnt_in_ref[...]

    T = TS * BS
    gd = dp // len(POOL_WINDOWS)
    NH = CONV_W - 1
    NP = POOL_MAX_W - 1

    for t in range(TS):
        xt_scr[t * BS:(t + 1) * BS, :] = x_ref[t]

    _rmsnorm_to_bf16(xt_scr, gmix_ref, h_scr, T)
    a_scr[...] = jnp.dot(h_scr[...], win_ref[:, 0:dc], preferred_element_type=F32)
    pg_scr[...] = jnp.dot(h_scr[...], win_ref[:, dc:2 * dc], preferred_element_type=F32)
    u_scr[...] = jnp.dot(h_scr[...], win_ref[:, 2 * dc:2 * dc + dp], preferred_element_type=F32)

    R = 16

    def glu(i, c):
        rs = _rows(i, R)
        a_scr[rs, :] = a_scr[rs, :] * _sigmoid(pg_scr[rs, :])
        return c

    lax.fori_loop(0, T // R, glu, 0, unroll=ROW_LOOP_UNROLL)

    def ext_conv(j, rs, ls):
        if j < NH:
            return sconv_ref[j, rs, ls]
        return a_scr[pl.ds((j - NH) * BS + rs.start, rs.size), ls]

    def ext_pool(j, rs, ls):
        if j < NP:
            return spool_ref[j, rs, ls]
        return u_scr[pl.ds((j - NP) * BS + rs.start, rs.size), ls]

    RC, LC = 32, 256

    def conv(i, c):
        rs = _rows(i, RC)
        for t in range(TS):
            for lc in range(dc // LC):
                ls = slice(lc * LC, (lc + 1) * LC)
                acc = jnp.zeros((RC, LC), F32)
                for k in range(CONV_W):
                    acc = acc + wdw_ref[k:k + 1, ls] * ext_conv(t + k, rs, ls)
                yc_scr[pl.ds(t * BS + rs.start, RC), ls] = acc + bdw_ref[:, ls]
        return c

    lax.fori_loop(0, BS // RC, conv, 0)

    _layernorm_swish(yc_scr, lng_ref, lnb_ref, mix_scr, T, dc)

    def pool(i, c):
        rs = _rows(i, RC)
        for t in range(TS):
            for g, w in enumerate(POOL_WINDOWS):
                ls = slice(g * gd, (g + 1) * gd)
                tot = ext_pool(NP + t, rs, ls)
                cur = tot
                for back in range(1, w):
                    tot = tot + ext_pool(NP + t - back, rs, ls)
                sd_scr[pl.ds(t * BS + rs.start, RC), ls] = (tot / jnp.float32(w) - cur).astype(BF16)
        return c

    lax.fori_loop(0, BS // RC, pool, 0)

    _pool_project(sd_scr, wpool_ref, pscale_ref, mix_scr, dc, gd)

    for j in range(NH):
        src = j + TS
        nconv_ref[j] = sconv_ref[src] if src < NH else a_scr[(src - NH) * BS:(src - NH + 1) * BS, :]
    for j in range(NP):
        src = j + TS
        npool_ref[j] = spool_ref[src] if src < NP else u_scr[(src - NP) * BS:(src - NP + 1) * BS, :]

    _out_proj_residual(xt_scr, mix_scr, wout_ref, x2_ref)
    _router(x2_ref, gffn_ref, wrh_ref, wrl_ref, br_ref, ltri_ref, hf_ref, ids_ref, w0_ref, w1_ref, cnt_ref,
            hhi_scr, hlo_scr, cnt_scr, T, d)


def _strict_lower_ones(n):
    return jnp.tril(jnp.ones((n, n), BF16), -1)


def _const_spec(shape):
    nd = len(shape)
    return pl.BlockSpec(shape, lambda *a: (0,) * nd, pipeline_mode=pl.Buffered(1))


def _mixer_weight_specs(d, dc, dp, cols):
    gd = dp // len(POOL_WINDOWS)
    return [
        _const_spec((1, d)),
        _const_spec((d, cols)),
        _const_spec((CONV_W, dc)),
        _const_spec((1, dc)),
        _const_spec((1, dc)),
        _const_spec((1, dc)),
        _const_spec((len(POOL_WINDOWS), gd, gd)),
        _const_spec((1, dp)),
        _const_spec((dc + dp, d)),
        _const_spec((1, d)),
        _const_spec((d, LANES)),
        _const_spec((d, LANES)),
        _const_spec((1, LANES)),
    ]


def _mixer_prompt(x, wts, *, T, n_total, n_sorted_rows, zero_rows):
    B, S, d = x.shape
    dc = wts[2].shape[1]
    dp = wts[7].shape[1]
    cols = wts[1].shape[1]
    n_s = S // T
    N = n_total
    n_blk = N // T
    assert N % T == 0 and N - B * S <= S
    blk = lambda b, s: jnp.clip(b * n_s + s - 1, 0, n_blk - 1)
    tok = lambda b, s: (blk(b, s), 0)
    tok4 = lambda b, s: (blk(b, s), 0, 0, 0)
    bclamp = lambda b: jnp.minimum(b, B - 1)

    def nxt(b, s):
        wrap = s + 1 == n_s
        return (jnp.where(wrap, bclamp(b + 1), bclamp(b)), jnp.where(wrap, 0, s + 1), 0)

    def prv(b, s):
        tp = jnp.clip(b * n_s + s - 1, 0, B * n_s - 1)
        return (tp // n_s, tp % n_s, 0)
    out_shape = (
        jax.ShapeDtypeStruct((N, d), F32),
        jax.ShapeDtypeStruct((N // TOK_CHUNK, TOK_CHUNK, d // LANES, LANES), BF16),
        jax.ShapeDtypeStruct((N, LANES), jnp.int32),
        jax.ShapeDtypeStruct((N, LANES), F32),
        jax.ShapeDtypeStruct((N, LANES), F32),
        jax.ShapeDtypeStruct((1, B, CONV_W - 1, dc), F32),
        jax.ShapeDtypeStruct((1, B, POOL_MAX_W - 1, dp), F32),
        jax.ShapeDtypeStruct((n_sorted_rows, d // LANES, LANES), BF16),
        jax.ShapeDtypeStruct((SUBLANES, LANES), F32),
    )
    assert n_sorted_rows % zero_rows == 0
    out_specs = (
        pl.BlockSpec((T, d), tok),
        pl.BlockSpec((T // TOK_CHUNK, TOK_CHUNK, d // LANES, LANES), tok4),
        pl.BlockSpec((T, LANES), tok),
        pl.BlockSpec((T, LANES), tok),
        pl.BlockSpec((T, LANES), tok),
        pl.BlockSpec((None, None, CONV_W - 1, dc), lambda b, s: (0, bclamp(b), 0, 0)),
        pl.BlockSpec((None, None, POOL_MAX_W - 1, dp), lambda b, s: (0, bclamp(b), 0, 0)),
        pl.BlockSpec(memory_space=pl.ANY),
        pl.BlockSpec((SUBLANES, LANES), lambda b, s: (0, 0)),
    )
    scratch = [
        pltpu.VMEM((T, d), BF16),
        pltpu.VMEM((T, dc), F32), pltpu.VMEM((T, dc), F32),
        pltpu.VMEM((T + TAIL, dc), F32), pltpu.VMEM((T + TAIL, dc), F32),
        pltpu.VMEM((T + PTAIL, dp), F32), pltpu.VMEM((T + PTAIL, dp), F32),
        pltpu.VMEM((T, dc), F32),
        pltpu.VMEM((T, dp), BF16), pltpu.VMEM((T, dp), BF16),
        pltpu.VMEM((T, dc + dp), BF16), pltpu.VMEM((T, dc + dp), BF16),
        pltpu.VMEM((T, d), BF16),
        pltpu.VMEM((T, d), BF16),
        pltpu.VMEM((zero_rows, d // LANES, LANES), BF16),
        pltpu.VMEM((SUBLANES, LANES), F32),
        pltpu.SemaphoreType.DMA((1,)),
    ]
    return pl.pallas_call(
        functools.partial(_mixer_prompt_kernel, T=T, d=d, dc=dc, dp=dp, n_batch=B),
        grid=(B + 1, n_s),
        in_specs=[pl.BlockSpec((None, T, d), prv),
                  pl.BlockSpec((None, T, d), nxt)]
                 + _mixer_weight_specs(d, dc, dp, cols) + [_const_spec((T, T))],
        out_specs=out_specs,
        out_shape=out_shape,
        scratch_shapes=scratch,
        compiler_params=pltpu.CompilerParams(
            dimension_semantics=("arbitrary", "arbitrary"), vmem_limit_bytes=VMEM_LIMIT),
        name="mixer_prompt",
    )(x, x, *wts, _strict_lower_ones(T))


def _mixer_sample(x_t, sconv_t, spool_t, wts, tok_arrays, counts, *, BS, tok0):
    TS, Bd, d = x_t.shape
    dc = wts[2].shape[1]
    dp = wts[7].shape[1]
    cols = wts[1].shape[1]
    T = TS * BS
    n_b = Bd // BS
    b0 = tok0 // T
    tok = lambda i: (b0 + i, 0)
    tok4 = lambda i: (b0 + i, 0, 0, 0)
    out_shape = tuple(jax.ShapeDtypeStruct(a.shape, a.dtype) for a in tok_arrays) + (
        jax.ShapeDtypeStruct((CONV_W - 1, Bd, dc), F32),
        jax.ShapeDtypeStruct((POOL_MAX_W - 1, Bd, dp), F32),
        jax.ShapeDtypeStruct(counts.shape, F32),
    )
    out_specs = (
        pl.BlockSpec((T, d), tok),
        pl.BlockSpec((T // TOK_CHUNK, TOK_CHUNK, d // LANES, LANES), tok4),
        pl.BlockSpec((T, LANES), tok),
        pl.BlockSpec((T, LANES), tok),
        pl.BlockSpec((T, LANES), tok),
        pl.BlockSpec((CONV_W - 1, BS, dc), lambda i: (0, i, 0)),
        pl.BlockSpec((POOL_MAX_W - 1, BS, dp), lambda i: (0, i, 0)),
        pl.BlockSpec(counts.shape, lambda i: (0, 0)),
    )
    scratch = [
        pltpu.VMEM((T, d), F32),
        pltpu.VMEM((T, d), BF16),
        pltpu.VMEM((T, dc), F32),
        pltpu.VMEM((T, dc), F32),
        pltpu.VMEM((T, dp), F32),
        pltpu.VMEM((T, dc), F32),
        pltpu.VMEM((T, dp), BF16),
        pltpu.VMEM((T, dc + dp), BF16),
        pltpu.VMEM((T, d), BF16),
        pltpu.VMEM((T, d), BF16),
        pltpu.VMEM(counts.shape, F32),
    ]
    return pl.pallas_call(
        functools.partial(_mixer_sample_kernel, TS=TS, BS=BS, d=d, dc=dc, dp=dp),
        grid=(n_b,),
        in_specs=[pl.BlockSpec((TS, BS, d), lambda i: (0, i, 0)),
                  pl.BlockSpec((CONV_W - 1, BS, dc), lambda i: (0, i, 0), pipeline_mode=pl.Buffered(1)),
                  pl.BlockSpec((POOL_MAX_W - 1, BS, dp), lambda i: (0, i, 0), pipeline_mode=pl.Buffered(1))]
                 + _mixer_weight_specs(d, dc, dp, cols)
                 + [_const_spec((T, T)), _const_spec(counts.shape)]
                 + [pl.BlockSpec(memory_space=pl.ANY)] * len(tok_arrays),
        out_specs=out_specs,
        out_shape=out_shape,
        scratch_shapes=scratch,
        input_output_aliases={5 + len(wts) + j: j for j in range(len(tok_arrays))},
        compiler_params=pltpu.CompilerParams(
            dimension_semantics=("arbitrary",), vmem_limit_bytes=VMEM_LIMIT),
        name="mixer_sample",
    )(x_t, sconv_t, spool_t, *wts, _strict_lower_ones(T), counts, *tok_arrays)


def _dispatch_kernel(pos_ref, hf_ref, dst_in, xs_in_ref, xs_ref, dst_ref, sem, isem, *, R, n_tok):
    del xs_in_ref
    base = pl.program_id(0) * R

    @pl.when(pl.program_id(0) == 0)
    def _():
        init = pltpu.make_async_copy(dst_in, dst_ref, isem.at[0])
        init.start()
        init.wait()

    G = 8

    def body(g, c):
        r0 = g * G
        ps = [pos_ref[2 * (base + r0) + q] for q in range(2 * G)]
        for q in range(2 * G):
            r, k = r0 + q // 2, q % 2
            pltpu.make_async_copy(hf_ref.at[pl.ds(r, 1)], xs_ref.at[pl.ds(ps[q], 1)], sem.at[0]).start(priority=k)
            dst_ref[ps[q]] = k * n_tok + base + r
        return c

    lax.fori_loop(0, R // G, body, 0)
    for k in range(2):
        pltpu.make_async_copy(hf_ref, xs_ref.at[pl.ds(0, R)], sem.at[0]).wait()


def _dispatch(pos, hf, xs_zero, *, R, n_tiles, tm):
    n_tok = hf.shape[0]
    assert n_tok % R == 0
    p = jnp.arange((n_tiles + 1) * tm, dtype=jnp.int32)
    tile = p // tm
    buf = jnp.where(tile == n_tiles, 1, tile % 2)
    dst_init = 2 * n_tok + buf * tm + p % tm
    hbm = pl.BlockSpec(memory_space=pl.ANY)
    return pl.pallas_call(
        functools.partial(_dispatch_kernel, R=R, n_tok=n_tok),
        grid_spec=pltpu.PrefetchScalarGridSpec(
            num_scalar_prefetch=1,
            grid=(n_tok // R,),
            in_specs=[pl.BlockSpec((R,) + hf.shape[1:], lambda s, pos: (s, 0, 0)), hbm, hbm],
            out_specs=(hbm, pl.BlockSpec(memory_space=pltpu.SMEM)),
            scratch_shapes=[pltpu.SemaphoreType.DMA((1,)), pltpu.SemaphoreType.DMA((1,))],
        ),
        out_shape=(jax.ShapeDtypeStruct(xs_zero.shape, xs_zero.dtype),
                   jax.ShapeDtypeStruct(dst_init.shape, jnp.int32)),
        input_output_aliases={3: 0},
        compiler_params=pltpu.CompilerParams(dimension_semantics=("arbitrary",)),
        name="dispatch",
    )(pos, hf, dst_init, xs_zero)


def _expert_mlp_kernel(te_ref, na_ref, nx_ref, nx2_ref, dst_prev_ref, dst_cur_ref, xs_ref, wg_ref, wu_ref, wd_ref, o_ref,
                       ybuf0, ybuf1, wgf, wuf, wdf, wg_scr, wu_scr, wd_scr, wslot, ssem, wsem,
                       *, tm, n_tiles, d):
    i = pl.program_id(0)
    n_act = na_ref[0]
    active = i < n_act
    slot = i % 2
    last = n_tiles - 1
    prev = te_ref[jnp.maximum(i - 1, 0)]
    new_expert = (i == 0) | (te_ref[i] != prev)
    ybuf = (ybuf0, ybuf1)

    def scatter_start(dst_ref, b):
        for r in range(tm):
            pltpu.make_async_copy(ybuf[b].at[pl.ds(r, 1)], o_ref.at[pl.ds(dst_ref[0, 0, r], 1)],
                                  ssem.at[b]).start(priority=r % 2)

    def scatter_wait(b):
        pltpu.make_async_copy(ybuf[b], o_ref.at[pl.ds(0, tm)], ssem.at[b]).wait()

    def weight_copies(e, b):
        return [pltpu.make_async_copy(src.at[e], buf.at[b], wsem.at[b])
                for src, buf in ((wg_ref, wgf), (wu_ref, wuf), (wd_ref, wdf))]

    @pl.when(i == 0)
    def _():
        wslot[0] = 0

        @pl.when(n_act > 0)
        def _():
            for c in weight_copies(te_ref[0], 0):
                c.start()

        @pl.when((n_act > 0) & (nx_ref[0] >= 0))
        def _():
            for c in weight_copies(nx_ref[0], 1):
                c.start()
        ybuf1[...] = jnp.zeros(ybuf1.shape, BF16)
        trash0 = pltpu.make_async_copy(ybuf1, o_ref.at[pl.ds(o_ref.shape[0] - 2 * tm, tm)], ssem.at[0])
        trash0.start()
        trash0.wait()

    @pl.when(active & new_expert)
    def _():
        b = wslot[0]
        after_next = nx2_ref[i]

        @pl.when(after_next >= 0)
        def _():
            for c in weight_copies(after_next, (b + 2) % N_WEIGHT_BUFS):
                c.start()

        for c in weight_copies(te_ref[i], b):
            c.wait()
        wg_scr[...] = wgf[b].astype(BF16)
        wu_scr[...] = wuf[b].astype(BF16)
        wd_scr[...] = wdf[b].astype(BF16)
        wslot[0] = (b + 1) % N_WEIGHT_BUFS

    for par in range(2):
        is_par = slot == par

        @pl.when(is_par & (i >= 1) & (i - 2 < n_act))
        def _():
            scatter_wait(par)

        @pl.when(is_par & active)
        def _():
            x = xs_ref[...].reshape(tm, d)
            g = jnp.dot(x, wg_scr[...], preferred_element_type=F32)
            u = jnp.dot(x, wu_scr[...], preferred_element_type=F32)
            act = (g * _sigmoid(g) * u).astype(BF16)
            y = jnp.dot(act, wd_scr[...], preferred_element_type=F32)
            ybuf[par][...] = y.astype(BF16).reshape(tm, d // LANES, LANES)
            scatter_start(dst_prev_ref, 1 - par)

        @pl.when(is_par & (i == n_act))
        def _():
            scatter_start(dst_prev_ref, 1 - par)

        @pl.when(is_par & (i == last))
        def _():
            @pl.when(last - 1 < n_act)
            def _():
                scatter_wait(1 - par)

            @pl.when(last < n_act)
            def _():
                scatter_start(dst_cur_ref, par)
                scatter_wait(par)


def _expert_mlp(tile_expert, n_active, next_expert, after_next_expert, dst, xs, w_gate, w_up, w_down, *, tm, n_tok):
    E, d, de = w_gate.shape
    n_tiles = xs.shape[0] // tm
    row = (tm, d // LANES, LANES)
    cur = lambda i, te, na, nx, nx2: (i, 0, 0)
    prv = lambda i, te, na, nx, nx2: (jnp.where(i == 0, n_tiles, i - 1), 0, 0)
    hbm = pl.BlockSpec(memory_space=pl.ANY)
    smem_tile = lambda imap: pl.BlockSpec((1, 1, tm), imap, memory_space=pltpu.SMEM)
    return pl.pallas_call(
        functools.partial(_expert_mlp_kernel, tm=tm, n_tiles=n_tiles, d=d),
        grid_spec=pltpu.PrefetchScalarGridSpec(
            num_scalar_prefetch=4,
            grid=(n_tiles,),
            in_specs=[
                smem_tile(prv), smem_tile(cur),
                pl.BlockSpec(row, lambda i, te, na, nx, nx2: (jnp.clip(i, 0, jnp.maximum(na[0] - 1, 0)), 0, 0)),
                hbm, hbm, hbm,
            ],
            out_specs=hbm,
            scratch_shapes=[pltpu.VMEM(row, BF16), pltpu.VMEM(row, BF16),
                            pltpu.VMEM((N_WEIGHT_BUFS, d, de), F32), pltpu.VMEM((N_WEIGHT_BUFS, d, de), F32),
                            pltpu.VMEM((N_WEIGHT_BUFS, de, d), F32),
                            pltpu.VMEM((d, de), BF16), pltpu.VMEM((d, de), BF16), pltpu.VMEM((de, d), BF16),
                            pltpu.SMEM((1,), jnp.int32),
                            pltpu.SemaphoreType.DMA((2,)), pltpu.SemaphoreType.DMA((N_WEIGHT_BUFS,))],
        ),
        out_shape=jax.ShapeDtypeStruct((2 * n_tok + 2 * tm, d // LANES, LANES), BF16),
        compiler_params=pltpu.CompilerParams(
            dimension_semantics=("arbitrary",), vmem_limit_bytes=VMEM_LIMIT),
        name="expert_mlp",
    )(tile_expert, n_active, next_expert, after_next_expert, dst, dst, xs, w_gate, w_up, w_down)


def _combine_kernel(x2_ref, o0_ref, o1_ref, w0_ref, w1_ref, gfin_ref, y_ref, *, R, d):
    C = TOK_CHUNK

    def body(i, c):
        rs = _rows(i, C)
        w0 = w0_ref[rs, :]
        w1 = w1_ref[rs, :]
        o0 = o0_ref[i].reshape(C, d).astype(F32)
        o1 = o1_ref[i].reshape(C, d).astype(F32)
        parts = []
        ssq = jnp.zeros((C, LANES), F32)
        for j in range(d // LANES):
            ls = slice(j * LANES, (j + 1) * LANES)
            m = w0 * o0[:, ls] + w1 * o1[:, ls]
            v = x2_ref[rs, ls] + m
            ssq = ssq + v * v
            parts.append(v)
        ms = jnp.sum(ssq, axis=-1, keepdims=True) * jnp.float32(1.0 / d)
        inv = lax.rsqrt(ms + EPS)
        for j in range(d // LANES):
            ls = slice(j * LANES, (j + 1) * LANES)
            y_ref[rs, ls] = parts[j] * inv * gfin_ref[:, ls]
        return c

    lax.fori_loop(0, R // C, body, 0)


def _combine(x2, o, w0, w1, g_final, *, tok0, n_out):
    n_tok, d = x2.shape
    R = max(r for r in R_COMBINE_CHOICES if n_out % r == 0 and tok0 % r == 0 and n_tok % r == 0)
    n_steps = n_out // R
    b0 = tok0 // R
    b1 = n_tok // R
    tok = lambda s: (b0 + s, 0)
    orow = (R // TOK_CHUNK, TOK_CHUNK, d // LANES, LANES)
    o4 = o.reshape(o.shape[0] // TOK_CHUNK, TOK_CHUNK, d // LANES, LANES)
    return pl.pallas_call(
        functools.partial(_combine_kernel, R=R, d=d),
        grid=(n_steps,),
        in_specs=[
            pl.BlockSpec((R, d), tok),
            pl.BlockSpec(orow, lambda s: (b0 + s, 0, 0, 0)),
            pl.BlockSpec(orow, lambda s: (b1 + b0 + s, 0, 0, 0)),
            pl.BlockSpec((R, LANES), tok),
            pl.BlockSpec((R, LANES), tok),
            pl.BlockSpec((1, d), lambda s: (0, 0)),
        ],
        out_specs=pl.BlockSpec((R, d), lambda s: (s, 0)),
        out_shape=jax.ShapeDtypeStruct((n_out, d), F32),
        compiler_params=pltpu.CompilerParams(
            dimension_semantics=("arbitrary",), vmem_limit_bytes=VMEM_LIMIT),
        name="combine",
    )(x2, o4, o4, w0, w1, g_final)


def _routing_plan(ids, counts, tm, n_tiles):
    e0, e1, r0, r1 = ids[:, 0], ids[:, 1], ids[:, 2], ids[:, 3]
    ar = jnp.arange(N_EXPERTS, dtype=jnp.int32)
    tiles_e = (counts + tm - 1) // tm
    tile_end = jnp.cumsum(tiles_e)
    offs = (tile_end - tiles_e) * tm
    pos0 = jnp.sum(jnp.where(e0[:, None] == ar, offs[None, :], 0), axis=1) + r0
    pos1 = jnp.sum(jnp.where(e1[:, None] == ar, offs[None, :], 0), axis=1) + r1
    pos = jnp.stack([pos0, pos1], axis=1).reshape(-1).astype(jnp.int32)
    n_active = tile_end[-1].astype(jnp.int32)
    t = jnp.arange(n_tiles, dtype=jnp.int32)
    tq = jnp.minimum(t, n_active - 1)
    te = jnp.sum((tile_end[None, :] <= tq[:, None]).astype(jnp.int32), axis=1)
    te = jnp.minimum(te, N_EXPERTS - 1)
    later = (ar[None, :] > ar[:, None]) & (tiles_e[None, :] > 0)
    nxt_e = jnp.min(jnp.where(later, ar[None, :], N_EXPERTS), axis=1)
    nxt_e = jnp.where(nxt_e == N_EXPERTS, -1, nxt_e).astype(jnp.int32)
    nxt2_e = jnp.where(nxt_e >= 0, nxt_e[jnp.maximum(nxt_e, 0)], -1)
    return pos, te, n_active.reshape(1), nxt_e[te], nxt2_e[te]


T_PROMPT = 256
BS_SAMPLE = 32
TM_EXPERT = 256
N_WEIGHT_BUFS = 3
R_DISPATCH_CHOICES = (2176, 1088, 512, 256, 128)
R_COMBINE_CHOICES = (512, 256, 128)


def kernel(x_prompt, x_sample, state_conv, state_pool, g_mix, w_in, w_dw, b_dw, ln_g, ln_b, w_pool, pool_scale, w_out, g_ffn, w_rg, b_rg, w_re, b_re, w_gate, w_up, w_down, g_final):
    depth = g_mix.shape[0]
    assert depth == 1
    B, S, d = x_prompt.shape
    Bd, TS, _ = x_sample.shape
    n_p = B * S
    n_s = Bd * TS
    N = n_p + n_s

    assert ROUTER_GROUP_LANE0 == N_EXPERTS
    n_pad = LANES - N_EXPERTS - N_GROUPS
    w_r = jnp.concatenate([w_re[0], w_rg[0], jnp.zeros((d, n_pad), F32)], axis=1)
    w_r_hi = w_r.astype(BF16)
    w_r_lo = (w_r - w_r_hi.astype(F32)).astype(BF16)
    b_r = jnp.concatenate([b_re[0], b_rg[0], jnp.zeros((n_pad,), F32)])[None]
    wts = (g_mix[0][None], w_in[0].astype(BF16), w_dw[0], b_dw[0][None], ln_g[0][None], ln_b[0][None],
           w_pool[0].astype(BF16), pool_scale[0][None], w_out[0].astype(BF16), g_ffn[0][None],
           w_r_hi, w_r_lo, b_r)

    tm = TM_EXPERT
    n_tiles = (2 * N + N_EXPERTS * (tm - 1) + tm - 1) // tm
    *tok_arrays, nconv_p, npool_p, xs, counts_p = _mixer_prompt(x_prompt, wts, T=T_PROMPT, n_total=N,
                                                                n_sorted_rows=n_tiles * tm, zero_rows=tm)

    x_t = jnp.transpose(x_sample, (1, 0, 2))
    sconv_t = jnp.transpose(state_conv[0], (1, 0, 2))
    spool_t = jnp.transpose(state_pool[0], (1, 0, 2))
    x2, hf, ids, w0, w1, nconv_t, npool_t, counts = _mixer_sample(x_t, sconv_t, spool_t, wts, tok_arrays, counts_p,
                                                                  BS=BS_SAMPLE, tok0=n_p)

    pos, tile_expert, n_active, next_expert, after_next = _routing_plan(ids[:, 0:4], counts[0, 0:N_EXPERTS].astype(jnp.int32),
                                                            tm, n_tiles)
    hf = hf.reshape(N, d // LANES, LANES)
    r_disp = max(r for r in R_DISPATCH_CHOICES if N % r == 0)
    xs, dst = _dispatch(pos, hf, xs, R=r_disp, n_tiles=n_tiles, tm=tm)
    dst = dst.reshape(n_tiles + 1, 1, tm)
    o = _expert_mlp(tile_expert, n_active, next_expert, after_next, dst, xs, w_gate[0], w_up[0], w_down[0],
                    tm=tm, n_tok=N)

    gfin = g_final[None]
    y_p = _combine(x2, o, w0, w1, gfin, tok0=0, n_out=n_p)
    y_s = _combine(x2, o, w0, w1, gfin, tok0=n_p, n_out=n_s)

    y_prompt = y_p.reshape(B, S, d)
    y_sample = y_s.reshape(Bd // BS_SAMPLE, TS, BS_SAMPLE, d).transpose(0, 2, 1, 3).reshape(Bd, TS, d)
    new_conv_s = jnp.transpose(nconv_t, (1, 0, 2))[None]
    new_pool_s = jnp.transpose(npool_t, (1, 0, 2))[None]
    return (y_prompt, y_sample, nconv_p, new_conv_s, npool_p, new_pool_s)
```

```python
import functools

import jax
import jax.numpy as jnp
from jax import lax
from jax.experimental import pallas as pl
from jax.experimental.pallas import tpu as pltpu

F32 = jnp.float32
BF16 = jnp.bfloat16
EPS = 1e-6

LANES = 128
SUBLANES = 8
VMEM_LIMIT = 56 * 1024 * 1024

CONV_W = 31
POOL_WINDOWS = (2, 4, 8, 16)
POOL_MAX_W = 16
N_GROUPS = 4
PER_GROUP = 8
N_EXPERTS = N_GROUPS * PER_GROUP

ROUTER_GROUP_LANE0 = N_EXPERTS
TOK_CHUNK = 16
ROW_LOOP_UNROLL = 16
TAIL = 32
PTAIL = 16


def _rows(i, r):
    if isinstance(i, int):
        return pl.ds(i * r, r)
    return pl.ds(pl.multiple_of(i * r, r), r)


def _row_loop(n, body, *, static, unroll=1):
    if static:
        for i in range(n):
            body(i)
    else:
        def step(i, c):
            body(i)
            return c
        lax.fori_loop(0, n, step, 0, unroll=unroll)


def _sigmoid(x):
    return 1.0 / (1.0 + jnp.exp(-x))


def _rmsnorm_to_bf16(src_ref, g_ref, dst_ref, T, static=False):
    R = 16

    def body(i):
        rs = _rows(i, R)
        x = src_ref[rs, :]
        ms = jnp.mean(x * x, axis=-1, keepdims=True)
        dst_ref[rs, :] = (x * lax.rsqrt(ms + EPS) * g_ref[...]).astype(BF16)

    _row_loop(T // R, body, static=static, unroll=ROW_LOOP_UNROLL)


def _layernorm_swish(y_ref, lng_ref, lnb_ref, mix_ref, T, dc, static=False):
    R = 16

    def body(i):
        rs = _rows(i, R)
        y = y_ref[rs, :]
        mu = jnp.mean(y, axis=-1, keepdims=True)
        d = y - mu
        var = jnp.mean(d * d, axis=-1, keepdims=True)
        z = d * lax.rsqrt(var + EPS) * lng_ref[...] + lnb_ref[...]
        mix_ref[rs, 0:dc] = (z * _sigmoid(z)).astype(BF16)

    _row_loop(T // R, body, static=static, unroll=ROW_LOOP_UNROLL)


def _pool_project(sd_ref, wpool_ref, pscale_ref, mix_ref, dc, gd):
    for g in range(len(POOL_WINDOWS)):
        sl = slice(g * gd, (g + 1) * gd)
        o = jnp.dot(sd_ref[:, sl], wpool_ref[g], preferred_element_type=F32)
        mix_ref[:, dc + g * gd: dc + (g + 1) * gd] = (o * pscale_ref[:, sl]).astype(BF16)


def _out_proj_residual(x_ref, mix_ref, wout_ref, x2_ref):
    x2_ref[...] = x_ref[...] + jnp.dot(mix_ref[...], wout_ref[...], preferred_element_type=F32)


def _router(x2_ref, gffn_ref, wrh_ref, wrl_ref, br_ref, ltri_ref, hf_ref, ids_ref, w0_ref, w1_ref, cnt_ref,
            hhi_scr, hlo_scr, cnt_scr, T, d):
    R = TOK_CHUNK

    def body(i, c):
        rs = _rows(i, R)
        x = x2_ref[rs, :]
        ms = jnp.mean(x * x, axis=-1, keepdims=True)
        h = x * lax.rsqrt(ms + EPS) * gffn_ref[...]
        hi = h.astype(BF16)
        hf_ref[i] = hi.reshape(R, d // LANES, LANES)
        hhi_scr[rs, :] = hi
        hlo_scr[rs, :] = (h - hi.astype(F32)).astype(BF16)
        return c

    lax.fori_loop(0, T // R, body, 0, unroll=ROW_LOOP_UNROLL)

    hcat = jnp.concatenate([hhi_scr[...], hlo_scr[...]], axis=0)
    wcat = jnp.concatenate([wrh_ref[...], wrl_ref[...]], axis=1)
    prod = jnp.dot(hcat, wcat, preferred_element_type=F32)
    lg = prod[0:T, 0:LANES] + prod[T:2 * T, 0:LANES] + prod[0:T, LANES:2 * LANES] + br_ref[...]

    lane = lax.broadcasted_iota(jnp.int32, lg.shape, 1).astype(F32)
    neg = jnp.float32(-jnp.inf)
    big = jnp.float32(1e9)
    g_lo = jnp.float32(ROUTER_GROUP_LANE0)
    gmask = (lane >= g_lo) & (lane < g_lo + N_GROUPS)
    lgg = jnp.where(gmask, lg, neg)
    gmax = jnp.max(lgg, axis=-1, keepdims=True)
    gsel = jnp.min(jnp.where(lgg == gmax, lane, big), axis=-1, keepdims=True) - g_lo
    gsum = jnp.sum(jnp.where(gmask, jnp.exp(lg - gmax), 0.0), axis=-1, keepdims=True)
    p_g = 1.0 / gsum

    e_lo = gsel * PER_GROUP
    emask = (lane >= e_lo) & (lane < e_lo + PER_GROUP)
    le = jnp.where(emask, lg, neg)
    v0 = jnp.max(le, axis=-1, keepdims=True)
    i0 = jnp.min(jnp.where(le == v0, lane, big), axis=-1, keepdims=True)
    le2 = jnp.where(lane == i0, neg, le)
    v1 = jnp.max(le2, axis=-1, keepdims=True)
    i1 = jnp.min(jnp.where(le2 == v1, lane, big), axis=-1, keepdims=True)
    ex = jnp.exp(v1 - v0)
    den = 1.0 / (1.0 + ex)
    w0 = den * p_g
    w1 = ex * den * p_g

    sel0 = lane == i0
    sel1 = lane == i1
    m = jnp.where(sel0 | sel1, 1.0, 0.0)
    before = jnp.dot(ltri_ref[...], m.astype(BF16), preferred_element_type=F32) + cnt_scr[0:1, :]
    rank0 = jnp.sum(jnp.where(sel0, before, 0.0), axis=-1, keepdims=True)
    rank1 = jnp.sum(jnp.where(sel1, before, 0.0), axis=-1, keepdims=True)
    total = cnt_scr[0:1, :] + jnp.sum(m, axis=0, keepdims=True)
    cnt_scr[...] = jnp.broadcast_to(total, cnt_scr.shape)
    cnt_ref[...] = jnp.broadcast_to(total, cnt_ref.shape)

    packed = jnp.where(lane == 0.0, i0, jnp.where(lane == 1.0, i1,
                                                  jnp.where(lane == 2.0, rank0, jnp.where(lane == 3.0, rank1, 0.0))))
    ids_ref[...] = packed.astype(jnp.int32)
    w0_ref[...] = jnp.broadcast_to(w0, lg.shape)
    w1_ref[...] = jnp.broadcast_to(w1, lg.shape)


def _mixer_prompt_kernel(xp_ref, xn_ref, gmix_ref, win_ref, wdw_ref, bdw_ref, lng_ref, lnb_ref, wpool_ref,
                         pscale_ref, wout_ref, gffn_ref, wrh_ref, wrl_ref, br_ref, ltri_ref,
                         x2_ref, hf_ref, ids_ref, w0_ref, w1_ref, nconv_ref, npool_ref, xs_ref, cnt_ref,
                         h_scr, pg0, pg1, ext0, ext1, extu0, extu1, yc_scr, sd0, sd1, mix0, mix1, hhi_scr, hlo_scr,
                         zbuf, cnt_scr, zsem, *, T, d, dc, dp, n_batch):
    n_s = pl.num_programs(1)
    b = pl.program_id(0)
    s = pl.program_id(1)
    t = b * n_s + s
    gd = dp // len(POOL_WINDOWS)
    pg_scr, ext_scr, extu_scr, sd_scr, mix_scr = (pg0, pg1), (ext0, ext1), (extu0, extu1), (sd0, sd1), (mix0, mix1)

    def in_proj(x_ref, slot, static):
        _rmsnorm_to_bf16(x_ref, gmix_ref, h_scr, T, static=static)
        ext_scr[slot][TAIL:TAIL + T, :] = jnp.dot(h_scr[...], win_ref[:, 0:dc], preferred_element_type=F32)
        pg_scr[slot][...] = jnp.dot(h_scr[...], win_ref[:, dc:2 * dc], preferred_element_type=F32)
        extu_scr[slot][PTAIL:PTAIL + T, :] = jnp.dot(h_scr[...], win_ref[:, 2 * dc:2 * dc + dp],
                                                     preferred_element_type=F32)

    def zero_copies():
        zr = zbuf.shape[0]
        return [pltpu.make_async_copy(zbuf, xs_ref.at[pl.ds(j * zr, zr)], zsem.at[0])
                for j in range(xs_ref.shape[0] // zr)]

    def finish_prev(q):
        _pool_project(sd_scr[q], wpool_ref, pscale_ref, mix_scr[q], dc, gd)
        _out_proj_residual(xp_ref, mix_scr[q], wout_ref, x2_ref)

    @pl.when(t == 0)
    def _():
        ext0[0:TAIL, :] = jnp.zeros((TAIL, dc), F32)
        extu0[0:PTAIL, :] = jnp.zeros((PTAIL, dp), F32)
        sd1[...] = jnp.zeros(sd1.shape, BF16)
        mix1[...] = jnp.zeros(mix1.shape, BF16)
        zbuf[...] = jnp.zeros(zbuf.shape, BF16)
        cnt_scr[...] = jnp.zeros(cnt_scr.shape, F32)
        for c in zero_copies():
            c.start()
        in_proj(xp_ref, 0, False)

    for par in range(2):
        @pl.when((b == n_batch) & (s == 0) & (t % 2 == par))
        def _():
            finish_prev(1 - par)

    @pl.when((b == n_batch) & (s == 0))
    def _():
        for c in zero_copies():
            c.wait()

    @pl.when((b == n_batch) & (s > 0))
    def _():
        x2_ref[...] = jnp.zeros(x2_ref.shape, F32)
        hf_ref[...] = jnp.zeros(hf_ref.shape, BF16)
        ids_ref[...] = jnp.zeros(ids_ref.shape, jnp.int32)
        w0_ref[...] = jnp.zeros(w0_ref.shape, F32)
        w1_ref[...] = jnp.zeros(w1_ref.shape, F32)

    def step(p):
        q = 1 - p
        in_proj(xn_ref, q, True)

        R = 16
        for i in range(T // R):
            es = pl.ds(i * R + TAIL, R)
            ext_scr[p][es, :] = ext_scr[p][es, :] * _sigmoid(pg_scr[p][pl.ds(i * R, R), :])

        RC, LC = 32, 128
        BR = RC + TAIL
        shift0 = TAIL - (CONV_W - 1)
        chain = None
        for i in range(T // RC):
            r0 = i * RC
            for lc in range(dc // LC):
                ls = slice(lc * LC, (lc + 1) * LC)
                blk = ext_scr[p][pl.ds(r0, BR), ls]
                if chain is None:
                    acc = jnp.zeros((RC, LC), F32)
                else:
                    z = (lax.bitcast_convert_type(chain, jnp.uint32) >> 16) >> 16
                    acc = jnp.tile(lax.bitcast_convert_type(z, F32), (RC // SUBLANES, 1))
                for sft in range(SUBLANES):
                    taps = [k for k in range(CONV_W) if (k + shift0) % SUBLANES == sft]
                    if not taps:
                        continue
                    rolled = blk if sft == 0 else pltpu.roll(blk, BR - sft, 0)
                    for k in taps:
                        qq = (k + shift0) // SUBLANES
                        acc = acc + wdw_ref[k:k + 1, ls] * rolled[qq * SUBLANES:qq * SUBLANES + RC, :]
                yc_scr[pl.ds(r0, RC), ls] = acc + bdw_ref[:, ls]
                chain = acc[0:SUBLANES, :]

        _layernorm_swish(yc_scr, lng_ref, lnb_ref, mix_scr[p], T, dc, static=True)

        RP = 32
        BP = RP + PTAIL
        pos_base = s * T
        for i in range(T // RP):
            r0 = i * RP
            pos = (pos_base + r0 + lax.broadcasted_iota(jnp.int32, (RP, gd), 0)).astype(F32)
            for g, w in enumerate(POOL_WINDOWS):
                ls = slice(g * gd, (g + 1) * gd)
                blk = extu_scr[p][pl.ds(r0, BP), ls]
                run = blk
                span = 1
                while span < w:
                    run = run + pltpu.roll(run, span, 0)
                    span *= 2
                cnt = jnp.minimum(pos + 1.0, jnp.float32(w))
                mean = run[PTAIL:PTAIL + RP, :] / cnt
                sd_scr[p][pl.ds(r0, RP), ls] = (mean - blk[PTAIL:PTAIL + RP, :]).astype(BF16)

        nconv_ref[...] = ext_scr[p][TAIL + T - (CONV_W - 1):TAIL + T, :]
        npool_ref[...] = extu_scr[p][PTAIL + T - (POOL_MAX_W - 1):PTAIL + T, :]
        keep = s != n_s - 1
        ext_scr[q][0:TAIL, :] = jnp.where(keep, ext_scr[p][T:T + TAIL, :], 0.0)
        extu_scr[q][0:PTAIL, :] = jnp.where(keep, extu_scr[p][T:T + PTAIL, :], 0.0)

        finish_prev(q)

    for par in range(2):
        pl.when((b < n_batch) & (t % 2 == par))(functools.partial(step, par))

    @pl.when(((b < n_batch) & (t > 0)) | ((b == n_batch) & (s == 0)))
    def _():
        _router(x2_ref, gffn_ref, wrh_ref, wrl_ref, br_ref, ltri_ref, hf_ref, ids_ref, w0_ref, w1_ref, cnt_ref,
                hhi_scr, hlo_scr, cnt_scr, T, d)


def _mixer_sample_kernel(x_ref, sconv_ref, spool_ref, gmix_ref, win_ref, wdw_ref, bdw_ref, lng_ref,
                         lnb_ref, wpool_ref, pscale_ref, wout_ref, gffn_ref, wrh_ref, wrl_ref, br_ref,
                         ltri_ref, cnt_in_ref,
                         x2_in, hf_in, ids_in, w0_in, w1_in,
                         x2_ref, hf_ref, ids_ref, w0_ref, w1_ref, nconv_ref, npool_ref, cnt_ref,
                         xt_scr, h_scr, a_scr, pg_scr, u_scr, yc_scr, sd_scr, mix_scr, hhi_scr, hlo_scr, cnt_scr,
                         *, TS, BS, d, dc, dp):
    del x2_in, hf_in, ids_in, w0_in, w1_in

    @pl.when(pl.program_id(0) == 0)
    def _():
        cnt_scr[...] = cnt_in_ref[...]

    T = TS * BS
    gd = dp // len(POOL_WINDOWS)
    NH = CONV_W - 1
    NP = POOL_MAX_W - 1

    for t in range(TS):
        xt_scr[t * BS:(t + 1) * BS, :] = x_ref[t]

    _rmsnorm_to_bf16(xt_scr, gmix_ref, h_scr, T)
    a_scr[...] = jnp.dot(h_scr[...], win_ref[:, 0:dc], preferred_element_type=F32)
    pg_scr[...] = jnp.dot(h_scr[...], win_ref[:, dc:2 * dc], preferred_element_type=F32)
    u_scr[...] = jnp.dot(h_scr[...], win_ref[:, 2 * dc:2 * dc + dp], preferred_element_type=F32)

    R = 16

    def glu(i, c):
        rs = _rows(i, R)
        a_scr[rs, :] = a_scr[rs, :] * _sigmoid(pg_scr[rs, :])
        return c

    lax.fori_loop(0, T // R, glu, 0, unroll=ROW_LOOP_UNROLL)

    def ext_conv(j, rs, ls):
        if j < NH:
            return sconv_ref[j, rs, ls]
        return a_scr[pl.ds((j - NH) * BS + rs.start, rs.size), ls]

    def ext_pool(j, rs, ls):
        if j < NP:
            return spool_ref[j, rs, ls]
        return u_scr[pl.ds((j - NP) * BS + rs.start, rs.size), ls]

    RC, LC = 32, 256

    def conv(i, c):
        rs = _rows(i, RC)
        for t in range(TS):
            for lc in range(dc // LC):
                ls = slice(lc * LC, (lc + 1) * LC)
                acc = jnp.zeros((RC, LC), F32)
                for k in range(CONV_W):
                    acc = acc + wdw_ref[k:k + 1, ls] * ext_conv(t + k, rs, ls)
                yc_scr[pl.ds(t * BS + rs.start, RC), ls] = acc + bdw_ref[:, ls]
        return c

    lax.fori_loop(0, BS // RC, conv, 0)

    _layernorm_swish(yc_scr, lng_ref, lnb_ref, mix_scr, T, dc)

    def pool(i, c):
        rs = _rows(i, RC)
        for t in range(TS):
            for g, w in enumerate(POOL_WINDOWS):
                ls = slice(g * gd, (g + 1) * gd)
                tot = ext_pool(NP + t, rs, ls)
                cur = tot
                for back in range(1, w):
                    tot = tot + ext_pool(NP + t - back, rs, ls)
                sd_scr[pl.ds(t * BS + rs.start, RC), ls] = (tot / jnp.float32(w) - cur).astype(BF16)
        return c

    lax.fori_loop(0, BS // RC, pool, 0)

    _pool_project(sd_scr, wpool_ref, pscale_ref, mix_scr, dc, gd)

    for j in range(NH):
        src = j + TS
        nconv_ref[j] = sconv_ref[src] if src < NH else a_scr[(src - NH) * BS:(src - NH + 1) * BS, :]
    for j in range(NP):
        src = j + TS
        npool_ref[j] = spool_ref[src] if src < NP else u_scr[(src - NP) * BS:(src - NP + 1) * BS, :]

    _out_proj_residual(xt_scr, mix_scr, wout_ref, x2_ref)
    _router(x2_ref, gffn_ref, wrh_ref, wrl_ref, br_ref, ltri_ref, hf_ref, ids_ref, w0_ref, w1_ref, cnt_ref,
            hhi_scr, hlo_scr, cnt_scr, T, d)


def _strict_lower_ones(n):
    return jnp.tril(jnp.ones((n, n), BF16), -1)


def _const_spec(shape):
    nd = len(shape)
    return pl.BlockSpec(shape, lambda *a: (0,) * nd, pipeline_mode=pl.Buffered(1))


def _mixer_weight_specs(d, dc, dp, cols):
    gd = dp // len(POOL_WINDOWS)
    return [
        _const_spec((1, d)),
        _const_spec((d, cols)),
        _const_spec((CONV_W, dc)),
        _const_spec((1, dc)),
        _const_spec((1, dc)),
        _const_spec((1, dc)),
        _const_spec((len(POOL_WINDOWS), gd, gd)),
        _const_spec((1, dp)),
        _const_spec((dc + dp, d)),
        _const_spec((1, d)),
        _const_spec((d, LANES)),
        _const_spec((d, LANES)),
        _const_spec((1, LANES)),
    ]


def _mixer_prompt(x, wts, *, T, n_total, n_sorted_rows, zero_rows):
    B, S, d = x.shape
    dc = wts[2].shape[1]
    dp = wts[7].shape[1]
    cols = wts[1].shape[1]
    n_s = S // T
    N = n_total
    n_blk = N // T
    assert N % T == 0 and N - B * S <= S
    blk = lambda b, s: jnp.clip(b * n_s + s - 1, 0, n_blk - 1)
    tok = lambda b, s: (blk(b, s), 0)
    tok4 = lambda b, s: (blk(b, s), 0, 0, 0)
    bclamp = lambda b: jnp.minimum(b, B - 1)

    def nxt(b, s):
        wrap = s + 1 == n_s
        return (jnp.where(wrap, bclamp(b + 1), bclamp(b)), jnp.where(wrap, 0, s + 1), 0)

    def prv(b, s):
        tp = jnp.clip(b * n_s + s - 1, 0, B * n_s - 1)
        return (tp // n_s, tp % n_s, 0)
    out_shape = (
        jax.ShapeDtypeStruct((N, d), F32),
        jax.ShapeDtypeStruct((N // TOK_CHUNK, TOK_CHUNK, d // LANES, LANES), BF16),
        jax.ShapeDtypeStruct((N, LANES), jnp.int32),
        jax.ShapeDtypeStruct((N, LANES), F32),
        jax.ShapeDtypeStruct((N, LANES), F32),
        jax.ShapeDtypeStruct((1, B, CONV_W - 1, dc), F32),
        jax.ShapeDtypeStruct((1, B, POOL_MAX_W - 1, dp), F32),
        jax.ShapeDtypeStruct((n_sorted_rows, d // LANES, LANES), BF16),
        jax.ShapeDtypeStruct((SUBLANES, LANES), F32),
    )
    assert n_sorted_rows % zero_rows == 0
    out_specs = (
        pl.BlockSpec((T, d), tok),
        pl.BlockSpec((T // TOK_CHUNK, TOK_CHUNK, d // LANES, LANES), tok4),
        pl.BlockSpec((T, LANES), tok),
        pl.BlockSpec((T, LANES), tok),
        pl.BlockSpec((T, LANES), tok),
        pl.BlockSpec((None, None, CONV_W - 1, dc), lambda b, s: (0, bclamp(b), 0, 0)),
        pl.BlockSpec((None, None, POOL_MAX_W - 1, dp), lambda b, s: (0, bclamp(b), 0, 0)),
        pl.BlockSpec(memory_space=pl.ANY),
        pl.BlockSpec((SUBLANES, LANES), lambda b, s: (0, 0)),
    )
    scratch = [
        pltpu.VMEM((T, d), BF16),
        pltpu.VMEM((T, dc), F32), pltpu.VMEM((T, dc), F32),
        pltpu.VMEM((T + TAIL, dc), F32), pltpu.VMEM((T + TAIL, dc), F32),
        pltpu.VMEM((T + PTAIL, dp), F32), pltpu.VMEM((T + PTAIL, dp), F32),
        pltpu.VMEM((T, dc), F32),
        pltpu.VMEM((T, dp), BF16), pltpu.VMEM((T, dp), BF16),
        pltpu.VMEM((T, dc + dp), BF16), pltpu.VMEM((T, dc + dp), BF16),
        pltpu.VMEM((T, d), BF16),
        pltpu.VMEM((T, d), BF16),
        pltpu.VMEM((zero_rows, d // LANES, LANES), BF16),
        pltpu.VMEM((SUBLANES, LANES), F32),
        pltpu.SemaphoreType.DMA((1,)),
    ]
    return pl.pallas_call(
        functools.partial(_mixer_prompt_kernel, T=T, d=d, dc=dc, dp=dp, n_batch=B),
        grid=(B + 1, n_s),
        in_specs=[pl.BlockSpec((None, T, d), prv),
                  pl.BlockSpec((None, T, d), nxt)]
                 + _mixer_weight_specs(d, dc, dp, cols) + [_const_spec((T, T))],
        out_specs=out_specs,
        out_shape=out_shape,
        scratch_shapes=scratch,
        compiler_params=pltpu.CompilerParams(
            dimension_semantics=("arbitrary", "arbitrary"), vmem_limit_bytes=VMEM_LIMIT),
        name="mixer_prompt",
    )(x, x, *wts, _strict_lower_ones(T))


def _mixer_sample(x_t, sconv_t, spool_t, wts, tok_arrays, counts, *, BS, tok0):
    TS, Bd, d = x_t.shape
    dc = wts[2].shape[1]
    dp = wts[7].shape[1]
    cols = wts[1].shape[1]
    T = TS * BS
    n_b = Bd // BS
    b0 = tok0 // T
    tok = lambda i: (b0 + i, 0)
    tok4 = lambda i: (b0 + i, 0, 0, 0)
    out_shape = tuple(jax.ShapeDtypeStruct(a.shape, a.dtype) for a in tok_arrays) + (
        jax.ShapeDtypeStruct((CONV_W - 1, Bd, dc), F32),
        jax.ShapeDtypeStruct((POOL_MAX_W - 1, Bd, dp), F32),
        jax.ShapeDtypeStruct(counts.shape, F32),
    )
    out_specs = (
        pl.BlockSpec((T, d), tok),
        pl.BlockSpec((T // TOK_CHUNK, TOK_CHUNK, d // LANES, LANES), tok4),
        pl.BlockSpec((T, LANES), tok),
        pl.BlockSpec((T, LANES), tok),
        pl.BlockSpec((T, LANES), tok),
        pl.BlockSpec((CONV_W - 1, BS, dc), lambda i: (0, i, 0)),
        pl.BlockSpec((POOL_MAX_W - 1, BS, dp), lambda i: (0, i, 0)),
        pl.BlockSpec(counts.shape, lambda i: (0, 0)),
    )
    scratch = [
        pltpu.VMEM((T, d), F32),
        pltpu.VMEM((T, d), BF16),
        pltpu.VMEM((T, dc), F32),
        pltpu.VMEM((T, dc), F32),
        pltpu.VMEM((T, dp), F32),
        pltpu.VMEM((T, dc), F32),
        pltpu.VMEM((T, dp), BF16),
        pltpu.VMEM((T, dc + dp), BF16),
        pltpu.VMEM((T, d), BF16),
        pltpu.VMEM((T, d), BF16),
        pltpu.VMEM(counts.shape, F32),
    ]
    return pl.pallas_call(
        functools.partial(_mixer_sample_kernel, TS=TS, BS=BS, d=d, dc=dc, dp=dp),
        grid=(n_b,),
        in_specs=[pl.BlockSpec((TS, BS, d), lambda i: (0, i, 0)),
                  pl.BlockSpec((CONV_W - 1, BS, dc), lambda i: (0, i, 0), pipeline_mode=pl.Buffered(1)),
                  pl.BlockSpec((POOL_MAX_W - 1, BS, dp), lambda i: (0, i, 0), pipeline_mode=pl.Buffered(1))]
                 + _mixer_weight_specs(d, dc, dp, cols)
                 + [_const_spec((T, T)), _const_spec(counts.shape)]
                 + [pl.BlockSpec(memory_space=pl.ANY)] * len(tok_arrays),
        out_specs=out_specs,
        out_shape=out_shape,
        scratch_shapes=scratch,
        input_output_aliases={5 + len(wts) + j: j for j in range(len(tok_arrays))},
        compiler_params=pltpu.CompilerParams(
            dimension_semantics=("arbitrary",), vmem_limit_bytes=VMEM_LIMIT),
        name="mixer_sample",
    )(x_t, sconv_t, spool_t, *wts, _strict_lower_ones(T), counts, *tok_arrays)


def _dispatch_kernel(pos_ref, hf_ref, dst_in, xs_in_ref, xs_ref, dst_ref, sem, isem, *, R, n_tok):
    del xs_in_ref
    base = pl.program_id(0) * R

    @pl.when(pl.program_id(0) == 0)
    def _():
        init = pltpu.make_async_copy(dst_in, dst_ref, isem.at[0])
        init.start()
        init.wait()

    G = 8

    def body(g, c):
        r0 = g * G
        ps = [pos_ref[2 * (base + r0) + q] for q in range(2 * G)]
        for q in range(2 * G):
            r, k = r0 + q // 2, q % 2
            pltpu.make_async_copy(hf_ref.at[pl.ds(r, 1)], xs_ref.at[pl.ds(ps[q], 1)], sem.at[0]).start(priority=k)
            dst_ref[ps[q]] = k * n_tok + base + r
        return c

    lax.fori_loop(0, R // G, body, 0)
    for k in range(2):
        pltpu.make_async_copy(hf_ref, xs_ref.at[pl.ds(0, R)], sem.at[0]).wait()


def _dispatch(pos, hf, xs_zero, *, R, n_tiles, tm):
    n_tok = hf.shape[0]
    assert n_tok % R == 0
    p = jnp.arange((n_tiles + 1) * tm, dtype=jnp.int32)
    tile = p // tm
    buf = jnp.where(tile == n_tiles, 1, tile % 2)
    dst_init = 2 * n_tok + buf * tm + p % tm
    hbm = pl.BlockSpec(memory_space=pl.ANY)
    return pl.pallas_call(
        functools.partial(_dispatch_kernel, R=R, n_tok=n_tok),
        grid_spec=pltpu.PrefetchScalarGridSpec(
            num_scalar_prefetch=1,
            grid=(n_tok // R,),
            in_specs=[pl.BlockSpec((R,) + hf.shape[1:], lambda s, pos: (s, 0, 0)), hbm, hbm],
            out_specs=(hbm, pl.BlockSpec(memory_space=pltpu.SMEM)),
            scratch_shapes=[pltpu.SemaphoreType.DMA((1,)), pltpu.SemaphoreType.DMA((1,))],
        ),
        out_shape=(jax.ShapeDtypeStruct(xs_zero.shape, xs_zero.dtype),
                   jax.ShapeDtypeStruct(dst_init.shape, jnp.int32)),
        input_output_aliases={3: 0},
        compiler_params=pltpu.CompilerParams(dimension_semantics=("arbitrary",)),
        name="dispatch",
    )(pos, hf, dst_init, xs_zero)


def _expert_mlp_kernel(te_ref, na_ref, nx_ref, nx2_ref, dst_prev_ref, dst_cur_ref, xs_ref, wg_ref, wu_ref, wd_ref, o_ref,
                       ybuf0, ybuf1, wgf, wuf, wdf, wg_scr, wu_scr, wd_scr, wslot, ssem, wsem,
                       *, tm, n_tiles, d):
    i = pl.program_id(0)
    n_act = na_ref[0]
    active = i < n_act
    slot = i % 2
    last = n_tiles - 1
    prev = te_ref[jnp.maximum(i - 1, 0)]
    new_expert = (i == 0) | (te_ref[i] != prev)
    ybuf = (ybuf0, ybuf1)

    def scatter_start(dst_ref, b):
        for r in range(tm):
            pltpu.make_async_copy(ybuf[b].at[pl.ds(r, 1)], o_ref.at[pl.ds(dst_ref[0, 0, r], 1)],
                                  ssem.at[b]).start(priority=r % 2)

    def scatter_wait(b):
        pltpu.make_async_copy(ybuf[b], o_ref.at[pl.ds(0, tm)], ssem.at[b]).wait()

    def weight_copies(e, b):
        return [pltpu.make_async_copy(src.at[e], buf.at[b], wsem.at[b])
                for src, buf in ((wg_ref, wgf), (wu_ref, wuf), (wd_ref, wdf))]

    @pl.when(i == 0)
    def _():
        wslot[0] = 0

        @pl.when(n_act > 0)
        def _():
            for c in weight_copies(te_ref[0], 0):
                c.start()

        @pl.when((n_act > 0) & (nx_ref[0] >= 0))
        def _():
            for c in weight_copies(nx_ref[0], 1):
                c.start()
        ybuf1[...] = jnp.zeros(ybuf1.shape, BF16)
        trash0 = pltpu.make_async_copy(ybuf1, o_ref.at[pl.ds(o_ref.shape[0] - 2 * tm, tm)], ssem.at[0])
        trash0.start()
        trash0.wait()

    @pl.when(active & new_expert)
    def _():
        b = wslot[0]
        after_next = nx2_ref[i]

        @pl.when(after_next >= 0)
        def _():
            for c in weight_copies(after_next, (b + 2) % N_WEIGHT_BUFS):
                c.start()

        for c in weight_copies(te_ref[i], b):
            c.wait()
        wg_scr[...] = wgf[b].astype(BF16)
        wu_scr[...] = wuf[b].astype(BF16)
        wd_scr[...] = wdf[b].astype(BF16)
        wslot[0] = (b + 1) % N_WEIGHT_BUFS

    for par in range(2):
        is_par = slot == par

        @pl.when(is_par & (i >= 1) & (i - 2 < n_act))
        def _():
            scatter_wait(par)

        @pl.when(is_par & active)
        def _():
            x = xs_ref[...].reshape(tm, d)
            g = jnp.dot(x, wg_scr[...], preferred_element_type=F32)
            u = jnp.dot(x, wu_scr[...], preferred_element_type=F32)
            act = (g * _sigmoid(g) * u).astype(BF16)
            y = jnp.dot(act, wd_scr[...], preferred_element_type=F32)
            ybuf[par][...] = y.astype(BF16).reshape(tm, d // LANES, LANES)
            scatter_start(dst_prev_ref, 1 - par)

        @pl.when(is_par & (i == n_act))
        def _():
            scatter_start(dst_prev_ref, 1 - par)

        @pl.when(is_par & (i == last))
        def _():
            @pl.when(last - 1 < n_act)
            def _():
                scatter_wait(1 - par)

            @pl.when(last < n_act)
            def _():
                scatter_start(dst_cur_ref, par)
                scatter_wait(par)


def _expert_mlp(tile_expert, n_active, next_expert, after_next_expert, dst, xs, w_gate, w_up, w_down, *, tm, n_tok):
    E, d, de = w_gate.shape
    n_tiles = xs.shape[0] // tm
    row = (tm, d // LANES, LANES)
    cur = lambda i, te, na, nx, nx2: (i, 0, 0)
    prv = lambda i, te, na, nx, nx2: (jnp.where(i == 0, n_tiles, i - 1), 0, 0)
    hbm = pl.BlockSpec(memory_space=pl.ANY)
    smem_tile = lambda imap: pl.BlockSpec((1, 1, tm), imap, memory_space=pltpu.SMEM)
    return pl.pallas_call(
        functools.partial(_expert_mlp_kernel, tm=tm, n_tiles=n_tiles, d=d),
        grid_spec=pltpu.PrefetchScalarGridSpec(
            num_scalar_prefetch=4,
            grid=(n_tiles,),
            in_specs=[
                smem_tile(prv), smem_tile(cur),
                pl.BlockSpec(row, lambda i, te, na, nx, nx2: (jnp.clip(i, 0, jnp.maximum(na[0] - 1, 0)), 0, 0)),
                hbm, hbm, hbm,
            ],
            out_specs=hbm,
            scratch_shapes=[pltpu.VMEM(row, BF16), pltpu.VMEM(row, BF16),
                            pltpu.VMEM((N_WEIGHT_BUFS, d, de), F32), pltpu.VMEM((N_WEIGHT_BUFS, d, de), F32),
                            pltpu.VMEM((N_WEIGHT_BUFS, de, d), F32),
                            pltpu.VMEM((d, de), BF16), pltpu.VMEM((d, de), BF16), pltpu.VMEM((de, d), BF16),
                            pltpu.SMEM((1,), jnp.int32),
                            pltpu.SemaphoreType.DMA((2,)), pltpu.SemaphoreType.DMA((N_WEIGHT_BUFS,))],
        ),
        out_shape=jax.ShapeDtypeStruct((2 * n_tok + 2 * tm, d // LANES, LANES), BF16),
        compiler_params=pltpu.CompilerParams(
            dimension_semantics=("arbitrary",), vmem_limit_bytes=VMEM_LIMIT),
        name="expert_mlp",
    )(tile_expert, n_active, next_expert, after_next_expert, dst, dst, xs, w_gate, w_up, w_down)


def _combine_kernel(x2_ref, o0_ref, o1_ref, w0_ref, w1_ref, gfin_ref, y_ref, *, R, d):
    C = TOK_CHUNK

    def body(i, c):
        rs = _rows(i, C)
        w0 = w0_ref[rs, :]
        w1 = w1_ref[rs, :]
        o0 = o0_ref[i].reshape(C, d).astype(F32)
        o1 = o1_ref[i].reshape(C, d).astype(F32)
        parts = []
        ssq = jnp.zeros((C, LANES), F32)
        for j in range(d // LANES):
            ls = slice(j * LANES, (j + 1) * LANES)
            m = w0 * o0[:, ls] + w1 * o1[:, ls]
            v = x2_ref[rs, ls] + m
            ssq = ssq + v * v
            parts.append(v)
        ms = jnp.sum(ssq, axis=-1, keepdims=True) * jnp.float32(1.0 / d)
        inv = lax.rsqrt(ms + EPS)
        for j in range(d // LANES):
            ls = slice(j * LANES, (j + 1) * LANES)
            y_ref[rs, ls] = parts[j] * inv * gfin_ref[:, ls]
        return c

    lax.fori_loop(0, R // C, body, 0)


def _combine(x2, o, w0, w1, g_final, *, tok0, n_out):
    n_tok, d = x2.shape
    R = max(r for r in R_COMBINE_CHOICES if n_out % r == 0 and tok0 % r == 0 and n_tok % r == 0)
    n_steps = n_out // R
    b0 = tok0 // R
    b1 = n_tok // R
    tok = lambda s: (b0 + s, 0)
    orow = (R // TOK_CHUNK, TOK_CHUNK, d // LANES, LANES)
    o4 = o.reshape(o.shape[0] // TOK_CHUNK, TOK_CHUNK, d // LANES, LANES)
    return pl.pallas_call(
        functools.partial(_combine_kernel, R=R, d=d),
        grid=(n_steps,),
        in_specs=[
            pl.BlockSpec((R, d), tok),
            pl.BlockSpec(orow, lambda s: (b0 + s, 0, 0, 0)),
            pl.BlockSpec(orow, lambda s: (b1 + b0 + s, 0, 0, 0)),
            pl.BlockSpec((R, LANES), tok),
            pl.BlockSpec((R, LANES), tok),
            pl.BlockSpec((1, d), lambda s: (0, 0)),
        ],
        out_specs=pl.BlockSpec((R, d), lambda s: (s, 0)),
        out_shape=jax.ShapeDtypeStruct((n_out, d), F32),
        compiler_params=pltpu.CompilerParams(
            dimension_semantics=("arbitrary",), vmem_limit_bytes=VMEM_LIMIT),
        name="combine",
    )(x2, o4, o4, w0, w1, g_final)


def _routing_plan(ids, counts, tm, n_tiles):
    e0, e1, r0, r1 = ids[:, 0], ids[:, 1], ids[:, 2], ids[:, 3]
    ar = jnp.arange(N_EXPERTS, dtype=jnp.int32)
    tiles_e = (counts + tm - 1) // tm
    tile_end = jnp.cumsum(tiles_e)
    offs = (tile_end - tiles_e) * tm
    pos0 = jnp.sum(jnp.where(e0[:, None] == ar, offs[None, :], 0), axis=1) + r0
    pos1 = jnp.sum(jnp.where(e1[:, None] == ar, offs[None, :], 0), axis=1) + r1
    pos = jnp.stack([pos0, pos1], axis=1).reshape(-1).astype(jnp.int32)
    n_active = tile_end[-1].astype(jnp.int32)
    t = jnp.arange(n_tiles, dtype=jnp.int32)
    tq = jnp.minimum(t, n_active - 1)
    te = jnp.sum((tile_end[None, :] <= tq[:, None]).astype(jnp.int32), axis=1)
    te = jnp.minimum(te, N_EXPERTS - 1)
    later = (ar[None, :] > ar[:, None]) & (tiles_e[None, :] > 0)
    nxt_e = jnp.min(jnp.where(later, ar[None, :], N_EXPERTS), axis=1)
    nxt_e = jnp.where(nxt_e == N_EXPERTS, -1, nxt_e).astype(jnp.int32)
    nxt2_e = jnp.where(nxt_e >= 0, nxt_e[jnp.maximum(nxt_e, 0)], -1)
    return pos, te, n_active.reshape(1), nxt_e[te], nxt2_e[te]


T_PROMPT = 256
BS_SAMPLE = 32
TM_EXPERT = 256
N_WEIGHT_BUFS = 3
R_DISPATCH_CHOICES = (2176, 1088, 512, 256, 128)
R_COMBINE_CHOICES = (512, 256, 128)


def kernel(x_prompt, x_sample, state_conv, state_pool, g_mix, w_in, w_dw, b_dw, ln_g, ln_b, w_pool, pool_scale, w_out, g_ffn, w_rg, b_rg, w_re, b_re, w_gate, w_up, w_down, g_final):
    depth = g_mix.shape[0]
    assert depth == 1
    B, S, d = x_prompt.shape
    Bd, TS, _ = x_sample.shape
    n_p = B * S
    n_s = Bd * TS
    N = n_p + n_s

    assert ROUTER_GROUP_LANE0 == N_EXPERTS
    n_pad = LANES - N_EXPERTS - N_GROUPS
    w_r = jnp.concatenate([w_re[0], w_rg[0], jnp.zeros((d, n_pad), F32)], axis=1)
    w_r_hi = w_r.astype(BF16)
    w_r_lo = (w_r - w_r_hi.astype(F32)).astype(BF16)
    b_r = jnp.concatenate([b_re[0], b_rg[0], jnp.zeros((n_pad,), F32)])[None]
    wts = (g_mix[0][None], w_in[0].astype(BF16), w_dw[0], b_dw[0][None], ln_g[0][None], ln_b[0][None],
           w_pool[0].astype(BF16), pool_scale[0][None], w_out[0].astype(BF16), g_ffn[0][None],
           w_r_hi, w_r_lo, b_r)

    tm = TM_EXPERT
    n_tiles = (2 * N + N_EXPERTS * (tm - 1) + tm - 1) // tm
    *tok_arrays, nconv_p, npool_p, xs, counts_p = _mixer_prompt(x_prompt, wts, T=T_PROMPT, n_total=N,
                                                                n_sorted_rows=n_tiles * tm, zero_rows=tm)

    x_t = jnp.transpose(x_sample, (1, 0, 2))
    sconv_t = jnp.transpose(state_conv[0], (1, 0, 2))
    spool_t = jnp.transpose(state_pool[0], (1, 0, 2))
    x2, hf, ids, w0, w1, nconv_t, npool_t, counts = _mixer_sample(x_t, sconv_t, spool_t, wts, tok_arrays, counts_p,
                                                                  BS=BS_SAMPLE, tok0=n_p)

    pos, tile_expert, n_active, next_expert, after_next = _routing_plan(ids[:, 0:4], counts[0, 0:N_EXPERTS].astype(jnp.int32),
                                                            tm, n_tiles)
    hf = hf.reshape(N, d // LANES, LANES)
    r_disp = max(r for r in R_DISPATCH_CHOICES if N % r == 0)
    xs, dst = _dispatch(pos, hf, xs, R=r_disp, n_tiles=n_tiles, tm=tm)
    dst = dst.reshape(n_tiles + 1, 1, tm)
    o = _expert_mlp(tile_expert, n_active, next_expert, after_next, dst, xs, w_gate[0], w_up[0], w_down[0],
                    tm=tm, n_tok=N)

    gfin = g_final[None]
    y_p = _combine(x2, o, w0, w1, gfin, tok0=0, n_out=n_p)
    y_s = _combine(x2, o, w0, w1, gfin, tok0=n_p, n_out=n_s)

    y_prompt = y_p.reshape(B, S, d)
    y_sample = y_s.reshape(Bd // BS_SAMPLE, TS, BS_SAMPLE, d).transpose(0, 2, 1, 3).reshape(Bd, TS, d)
    new_conv_s = jnp.transpose(nconv_t, (1, 0, 2))[None]
    new_pool_s = jnp.transpose(npool_t, (1, 0, 2))[None]
    return (y_prompt, y_sample, nconv_p, new_conv_s, npool_p, new_pool_s)
```
